```python
import math
import jax, jax.numpy as jnp
from jax import lax
import numpy as np

D_MODEL = 1024
BATCH = 2
SEQ = 8192
DEPTH = 2

GRID_W = 64
CTX_LEN = 256
EPS = 1e-6

D_LRU = 384
LRU_BLOCKS = 6
LRU_BLOCK_DIM = D_LRU // LRU_BLOCKS
LRU_CONV = 4
LRU_C = 8.0

MLA_HEADS = 6
MLA_NOPE = 64
MLA_ROPE = 32
MLA_V = 64
MLA_Q_RANK = 384
MLA_KV_RANK = 256
MLA_SCALE = (MLA_NOPE + MLA_ROPE) ** -0.5
ROPE_PAIRS = MLA_ROPE // 4
ROPE_THETA = 10000.0
ATTN_BLOCK = 128

D_HY = 256
HY_CONV = 3
HY_ORDER = 2
HY_BANDS = 8
HY_EMB = 1 + 2 * HY_BANDS
HY_FFN = 64

D_MIX = D_LRU + MLA_HEADS * MLA_V + D_HY
D_IN = 2 * D_LRU + MLA_Q_RANK + MLA_KV_RANK + MLA_ROPE + 3 * D_HY
IN_SPLITS = [D_LRU, 2 * D_LRU, 2 * D_LRU + MLA_Q_RANK,
             2 * D_LRU + MLA_Q_RANK + MLA_KV_RANK,
             2 * D_LRU + MLA_Q_RANK + MLA_KV_RANK + MLA_ROPE]

PEER_HEADS = 8
PEER_NKEYS = 128
PEER_EXPERTS = PEER_NKEYS * PEER_NKEYS
PEER_DQ = 256
PEER_TOPK = 16
PEER_BLOCK = 128

kernel_name = "hybrid_rglru_mla_hyena_peer_prefix_dit"

F32 = jnp.float32


def rmsnorm(x, g):
    xf = x.astype(F32)
    y = xf * lax.rsqrt(jnp.mean(xf * xf, axis=-1, keepdims=True) + EPS)
    return (y * g.astype(F32)).astype(x.dtype)


def depthwise_conv(x, w, b, pad_l, pad_r):
    y = lax.conv_general_dilated(
        x, w[:, None, :].astype(x.dtype), window_strides=(1,),
        padding=[(pad_l, pad_r)], dimension_numbers=('NWC', 'WIO', 'NWC'),
        feature_group_count=x.shape[-1])
    return y + b.astype(x.dtype)


def linear_scan(a, b, h0):
    b = b.at[:, 0].add(a[:, 0] * h0)

    def combine(e1, e2):
        a1, b1 = e1
        a2, b2 = e2
        return a1 * a2, a2 * b1 + b2

    _, h = lax.associative_scan(combine, (a, b), axis=1)
    return h


def rglru_coeffs(x, wr, br, wi, bi, lam):
    xb = x.reshape(x.shape[:-1] + (LRU_BLOCKS, LRU_BLOCK_DIM))
    r = jax.nn.sigmoid(jnp.einsum('blnd,nde->blne', xb, wr.astype(F32)).reshape(x.shape) + br.astype(F32))
    i = jax.nn.sigmoid(jnp.einsum('blnd,nde->blne', xb, wi.astype(F32)).reshape(x.shape) + bi.astype(F32))
    log_a = -LRU_C * r * jax.nn.softplus(-lam.astype(F32))
    a = jnp.exp(log_a)
    b = jnp.sqrt(-jnp.expm1(2.0 * log_a)) * (i * x)
    return a, b


def rglru_bidir(xx, xc, wr, br, wi, bi, lam):
    xx32, xc32 = xx.astype(F32), xc.astype(F32)
    h0 = jnp.zeros((xx.shape[0], D_LRU), F32)
    ys_x, ys_c = [], []
    for d in range(2):
        flip = (lambda t: t[:, ::-1]) if d == 1 else (lambda t: t)
        ac, bc = rglru_coeffs(flip(xc32), wr[d], br[d], wi[d], bi[d], lam[d])
        ax, bx = rglru_coeffs(flip(xx32), wr[d], br[d], wi[d], bi[d], lam[d])
        hc = linear_scan(ac, bc, h0)
        hx = linear_scan(ax, bx, hc[:, -1])
        ys_c.append(flip(hc))
        ys_x.append(flip(hx))
    return (ys_x[0] + ys_x[1]).astype(xx.dtype), (ys_c[0] + ys_c[1]).astype(xc.dtype)


def rope_axis(x, ang):
    n = x.shape[-1] // 2
    x1, x2 = x[..., :n], x[..., n:]
    cos, sin = jnp.cos(ang).astype(x.dtype), jnp.sin(ang).astype(x.dtype)
    return jnp.concatenate([x1 * cos - x2 * sin, x2 * cos + x1 * sin], axis=-1)


def rope_2d(x, ang_row, ang_col):
    h = MLA_ROPE // 2
    return jnp.concatenate([rope_axis(x[..., :h], ang_row), rope_axis(x[..., h:], ang_col)], axis=-1)


def mla_project(cq, ckv, kr, q_g, wqb, kv_g, wkvb, angles):
    B, L = cq.shape[:2]
    q = (rmsnorm(cq, q_g) @ wqb).reshape(B, L, MLA_HEADS, MLA_NOPE + MLA_ROPE)
    kv = (rmsnorm(ckv, kv_g) @ wkvb).reshape(B, L, MLA_HEADS, MLA_NOPE + MLA_V)
    q_nope, q_rope = q[..., :MLA_NOPE], q[..., MLA_NOPE:]
    k_nope, v = kv[..., :MLA_NOPE], kv[..., MLA_NOPE:]
    if angles is not None:
        ang_r, ang_c = angles
        q_rope = rope_2d(q_rope, ang_r[:, None, :], ang_c[:, None, :])
        kr = rope_2d(kr, ang_r, ang_c)
    k_rope = jnp.broadcast_to(kr[:, :, None, :], (B, L, MLA_HEADS, MLA_ROPE))
    q = jnp.concatenate([q_nope, q_rope], axis=-1)
    k = jnp.concatenate([k_nope, k_rope], axis=-1)
    return q, k, v


def attend(q, k, v):
    s = jnp.einsum('bqhd,bkhd->bhqk', q, k, preferred_element_type=F32) * MLA_SCALE
    p = jax.nn.softmax(s, axis=-1).astype(v.dtype)
    return jnp.einsum('bhqk,bkhd->bqhd', p, v)


def hyena_filter_spectra(L, w1, b1, w2, b2, w3, freq, decay):
    t = jnp.arange(L, dtype=F32) / L
    bands = jnp.linspace(1e-4, HY_BANDS - 1, HY_BANDS, dtype=F32)
    wpos = (2.0 * math.pi) * t[:, None] * bands[None, :]
    z = jnp.concatenate([t[:, None], jnp.cos(wpos), jnp.sin(wpos)], axis=-1)
    fr = freq.astype(F32)
    h = jnp.sin(fr * (z @ w1.astype(F32) + b1.astype(F32)))
    h = jnp.sin(fr * (h @ w2.astype(F32) + b2.astype(F32)))
    k = (h @ w3.astype(F32)).reshape(L, 2 * HY_ORDER, D_HY) * jnp.exp(-t[:, None, None] * decay.astype(F32))
    k = k * lax.rsqrt(jnp.sum(k * k, axis=0, keepdims=True) + EPS)
    k = k.reshape(L, HY_ORDER, 2, D_HY)
    kf, kb = k[:, :, 0], k[:, :, 1]
    k_circ = jnp.concatenate([kf, jnp.zeros((1, HY_ORDER, D_HY), F32), kb[:0:-1]], axis=0)
    return jnp.fft.rfft(k_circ, axis=0)


def fft_conv(z, kspec, skip):
    L = z.shape[1]
    z32 = z.astype(F32)
    zf = jnp.fft.rfft(z32, n=2 * L, axis=1)
    y = jnp.fft.irfft(zf * kspec[None], n=2 * L, axis=1)[:, :L]
    return (y + z32 * skip.astype(F32)).astype(z.dtype)


def hyena(u, conv_w, conv_b, w1, b1, w2, b2, w3, freq, decay, skip):
    L = u.shape[1]
    u = depthwise_conv(u, conv_w, conv_b, 1, 1)
    v, x1, x2 = jnp.split(u, 3, axis=-1)
    kspec = hyena_filter_spectra(L, w1, b1, w2, b2, w3, freq, decay)
    y = x1 * fft_conv(v, kspec[:, 0], skip[0])
    y = x2 * fft_conv(y, kspec[:, 1], skip[1])
    return y


def token_mixer(hx, hc, need_ctx, w_in, w_out,
                lru_conv_w, lru_conv_b, lru_wr, lru_br, lru_wi, lru_bi, lru_lambda,
                mla_q_norm_g, mla_wqb, mla_kv_norm_g, mla_wkvb,
                hy_conv_w, hy_conv_b, hy_f_w1, hy_f_b1, hy_f_w2, hy_f_b2, hy_f_w3,
                hy_f_freq, hy_decay, hy_skip):
    B, L, _ = hx.shape
    ROWS = L // GRID_W
    row = jnp.repeat(jnp.arange(ROWS, dtype=F32), GRID_W)
    col = jnp.tile(jnp.arange(GRID_W, dtype=F32), ROWS)
    inv_freq = ROPE_THETA ** (-jnp.arange(ROPE_PAIRS, dtype=F32) / ROPE_PAIRS)
    angles = (row[:, None] * inv_freq, col[:, None] * inv_freq)

    ux = jnp.split(hx @ w_in, IN_SPLITS, axis=-1)
    uc = jnp.split(hc @ w_in, IN_SPLITS, axis=-1)
    hy_args = (hy_conv_w, hy_conv_b, hy_f_w1, hy_f_b1, hy_f_w2, hy_f_b2, hy_f_w3, hy_f_freq, hy_decay, hy_skip)

    ax_in = depthwise_conv(ux[0], lru_conv_w, lru_conv_b, 2, 1)
    ac_in = depthwise_conv(uc[0], lru_conv_w, lru_conv_b, 2, 1)
    ra_x, ra_c = rglru_bidir(ax_in, ac_in, lru_wr, lru_br, lru_wi, lru_bi, lru_lambda)
    ya_x = ra_x * jax.nn.gelu(ux[1])

    qx, kx, vx = mla_project(ux[2], ux[3], ux[4], mla_q_norm_g, mla_wqb, mla_kv_norm_g, mla_wkvb, angles)
    qc, kc, vc = mla_project(uc[2], uc[3], uc[4], mla_q_norm_g, mla_wqb, mla_kv_norm_g, mla_wkvb, None)
    k_all = jnp.concatenate([kc, kx], axis=1)
    v_all = jnp.concatenate([vc, vx], axis=1)
    nb = L // ATTN_BLOCK
    q_blocks = jnp.moveaxis(qx.reshape(B, nb, ATTN_BLOCK, MLA_HEADS, MLA_NOPE + MLA_ROPE), 1, 0)
    o_blocks = lax.map(lambda qb: attend(qb, k_all, v_all), q_blocks)
    yb_x = jnp.moveaxis(o_blocks, 0, 1).reshape(B, L, MLA_HEADS * MLA_V)

    yc_x = hyena(ux[5], *hy_args)

    out_x = jnp.concatenate([ya_x, yb_x, yc_x], axis=-1) @ w_out
    if not need_ctx:
        return out_x, None

    Lc = hc.shape[1]
    ya_c = ra_c * jax.nn.gelu(uc[1])
    yb_c = attend(qc, kc, vc).reshape(B, Lc, MLA_HEADS * MLA_V)
    yc_c = hyena(uc[5], *hy_args)
    out_c = jnp.concatenate([ya_c, yb_c, yc_c], axis=-1) @ w_out
    return out_x, out_c


def peer(h, wq, keys, u_tab, v_tab):
    B, L, D = h.shape
    tokens = h.reshape(-1, PEER_BLOCK, D)
    half = PEER_DQ // 2

    def block(xb):
        T = xb.shape[0]
        q = (xb @ wq).reshape(T, PEER_HEADS, PEER_DQ)
        s1 = jnp.einsum('thd,kd->thk', q[..., :half], keys[0], preferred_element_type=F32)
        s2 = jnp.einsum('thd,kd->thk', q[..., half:], keys[1], preferred_element_type=F32)
        v1, i1 = lax.top_k(s1, PEER_TOPK)
        v2, i2 = lax.top_k(s2, PEER_TOPK)
        cand = (v1[..., :, None] + v2[..., None, :]).reshape(T, PEER_HEADS, PEER_TOPK * PEER_TOPK)
        cid = (i1[..., :, None] * PEER_NKEYS + i2[..., None, :]).reshape(T, PEER_HEADS, PEER_TOPK * PEER_TOPK)
        top_s, top_p = lax.top_k(cand, PEER_TOPK)
        eid = jnp.take_along_axis(cid, top_p, axis=-1)
        gate = jax.nn.softmax(top_s, axis=-1).astype(xb.dtype)
        act = jax.nn.gelu(jnp.einsum('td,thkd->thk', xb, u_tab[eid]))
        return jnp.einsum('thk,thkd->td', gate * act, v_tab[eid])

    return lax.map(block, tokens).reshape(B, L, D)


def setup_inputs(seed: int = 0) -> dict:
    keys = iter(jax.random.split(jax.random.key(seed), 64))

    def nrm(shape, scale):
        return jax.random.normal(next(keys), shape, F32) * scale

    D = D_MODEL
    lam_u = jax.random.uniform(next(keys), (DEPTH, 2, D_LRU), F32, minval=0.9, maxval=0.999)
    lam_a = lam_u ** (1.0 / LRU_C)
    return {
        "x": nrm((BATCH, SEQ, D), 1.0),
        "c": nrm((BATCH, D), 1.0),
        "ctx": nrm((BATCH, CTX_LEN, D), 1.0),
        "c_ctx": nrm((D,), 1.0),
        "w_mod": nrm((DEPTH, D, 6 * D), 0.5 * D ** -0.5),
        "b_mod": nrm((DEPTH, 6 * D), 0.02),
        "norm1_g": 1.0 + nrm((DEPTH, D), 0.05),
        "norm2_g": 1.0 + nrm((DEPTH, D), 0.05),
        "w_in": nrm((DEPTH, D, D_IN), D ** -0.5),
        "w_out": nrm((DEPTH, D_MIX, D), D_MIX ** -0.5),
        "lru_conv_w": nrm((DEPTH, LRU_CONV, D_LRU), LRU_CONV ** -0.5),
        "lru_conv_b": nrm((DEPTH, D_LRU), 0.02),
        "lru_wr": nrm((DEPTH, 2, LRU_BLOCKS, LRU_BLOCK_DIM, LRU_BLOCK_DIM), LRU_BLOCK_DIM ** -0.5),
        "lru_br": nrm((DEPTH, 2, D_LRU), 0.02),
        "lru_wi": nrm((DEPTH, 2, LRU_BLOCKS, LRU_BLOCK_DIM, LRU_BLOCK_DIM), LRU_BLOCK_DIM ** -0.5),
        "lru_bi": nrm((DEPTH, 2, D_LRU), 0.02),
        "lru_lambda": jnp.log(lam_a) - jnp.log1p(-lam_a),
        "mla_q_norm_g": 1.0 + nrm((DEPTH, MLA_Q_RANK), 0.05),
        "mla_wqb": nrm((DEPTH, MLA_Q_RANK, MLA_HEADS * (MLA_NOPE + MLA_ROPE)), MLA_Q_RANK ** -0.5),
        "mla_kv_norm_g": 1.0 + nrm((DEPTH, MLA_KV_RANK), 0.05),
        "mla_wkvb": nrm((DEPTH, MLA_KV_RANK, MLA_HEADS * (MLA_NOPE + MLA_V)), MLA_KV_RANK ** -0.5),
        "hy_conv_w": nrm((DEPTH, HY_CONV, 3 * D_HY), HY_CONV ** -0.5),
        "hy_conv_b": nrm((DEPTH, 3 * D_HY), 0.02),
        "hy_f_w1": nrm((DEPTH, HY_EMB, HY_FFN), HY_EMB ** -0.5),
        "hy_f_b1": nrm((DEPTH, HY_FFN), 0.1),
        "hy_f_w2": nrm((DEPTH, HY_FFN, HY_FFN), HY_FFN ** -0.5),
        "hy_f_b2": nrm((DEPTH, HY_FFN), 0.1),
        "hy_f_w3": nrm((DEPTH, HY_FFN, 2 * HY_ORDER * D_HY), HY_FFN ** -0.5),
        "hy_f_freq": 1.0 + nrm((DEPTH, HY_FFN), 0.1),
        "hy_decay": jax.random.uniform(next(keys), (DEPTH, 2 * HY_ORDER, D_HY), F32, minval=3.0, maxval=15.0),
        "hy_skip": nrm((DEPTH, HY_ORDER, D_HY), 0.5),
        "peer_wq": nrm((DEPTH, D, PEER_HEADS * PEER_DQ), D ** -0.5),
        "peer_keys": nrm((DEPTH, 2, PEER_NKEYS, PEER_DQ // 2), (PEER_DQ // 2) ** -0.5),
        "peer_u": nrm((DEPTH, PEER_EXPERTS, D), D ** -0.5),
        "peer_v": nrm((DEPTH, PEER_EXPERTS, D), 1.0),
        "final_g": 1.0 + nrm((D,), 0.05),
    }


def reference(x, c, ctx, c_ctx, w_mod, b_mod, norm1_g, norm2_g, w_in, w_out,
              lru_conv_w, lru_conv_b, lru_wr, lru_br, lru_wi, lru_bi, lru_lambda,
              mla_q_norm_g, mla_wqb, mla_kv_norm_g, mla_wkvb,
              hy_conv_w, hy_conv_b, hy_f_w1, hy_f_b1, hy_f_w2, hy_f_b2, hy_f_w3,
              hy_f_freq, hy_decay, hy_skip,
              peer_wq, peer_keys, peer_u, peer_v, final_g):
    for l in range(DEPTH):
        need_ctx = l < DEPTH - 1
        mod_x = jax.nn.silu(c) @ w_mod[l] + b_mod[l]
        mod_c = jax.nn.silu(c_ctx) @ w_mod[l] + b_mod[l]
        sh1, sc1, g1, sh2, sc2, g2 = jnp.split(mod_x[:, None, :], 6, axis=-1)
        csh1, csc1, cg1, csh2, csc2, cg2 = jnp.split(mod_c, 6, axis=-1)

        hx = rmsnorm(x, norm1_g[l]) * (1.0 + sc1) + sh1
        hc = rmsnorm(ctx, norm1_g[l]) * (1.0 + csc1) + csh1
        yx, yc = token_mixer(hx, hc, need_ctx, w_in[l], w_out[l],
                             lru_conv_w[l], lru_conv_b[l], lru_wr[l], lru_br[l], lru_wi[l], lru_bi[l], lru_lambda[l],
                             mla_q_norm_g[l], mla_wqb[l], mla_kv_norm_g[l], mla_wkvb[l],
                             hy_conv_w[l], hy_conv_b[l], hy_f_w1[l], hy_f_b1[l], hy_f_w2[l], hy_f_b2[l], hy_f_w3[l],
                             hy_f_freq[l], hy_decay[l], hy_skip[l])
        x = x + g1 * yx
        hx = rmsnorm(x, norm2_g[l]) * (1.0 + sc2) + sh2
        x = x + g2 * peer(hx, peer_wq[l], peer_keys[l], peer_u[l], peer_v[l])
        if need_ctx:
            ctx = ctx + cg1 * yc
            hc = rmsnorm(ctx, norm2_g[l]) * (1.0 + csc2) + csh2
            ctx = ctx + cg2 * peer(hc, peer_wq[l], peer_keys[l], peer_u[l], peer_v[l])
    return rmsnorm(x, final_g)
```

```python
import functools
import math

import jax
import jax.numpy as jnp
from jax import lax
from jax.experimental import pallas as pl
from jax.experimental.pallas import tpu as pltpu

D_MODEL = 1024
BATCH = 2
SEQ = 8192
DEPTH = 2

GRID_W = 64
CTX_LEN = 256
EPS = 1e-6

D_LRU = 384
LRU_BLOCKS = 6
LRU_BLOCK_DIM = D_LRU // LRU_BLOCKS
LRU_CONV = 4
LRU_C = 8.0

MLA_HEADS = 6
MLA_NOPE = 64
MLA_ROPE = 32
MLA_V = 64
MLA_Q_RANK = 384
MLA_KV_RANK = 256
MLA_SCALE = (MLA_NOPE + MLA_ROPE) ** -0.5
ROPE_PAIRS = MLA_ROPE // 4
ROPE_THETA = 10000.0
ATTN_BLOCK = 128

D_HY = 256
HY_CONV = 3
HY_ORDER = 2
HY_BANDS = 8
HY_EMB = 1 + 2 * HY_BANDS
HY_FFN = 64

D_MIX = D_LRU + MLA_HEADS * MLA_V + D_HY
D_IN = 2 * D_LRU + MLA_Q_RANK + MLA_KV_RANK + MLA_ROPE + 3 * D_HY
IN_SPLITS = [D_LRU, 2 * D_LRU, 2 * D_LRU + MLA_Q_RANK,
             2 * D_LRU + MLA_Q_RANK + MLA_KV_RANK,
             2 * D_LRU + MLA_Q_RANK + MLA_KV_RANK + MLA_ROPE]

PEER_HEADS = 8
PEER_NKEYS = 128
PEER_EXPERTS = PEER_NKEYS * PEER_NKEYS
PEER_DQ = 256
PEER_TOPK = 16
PEER_BLOCK = 128

F32 = jnp.float32


def rmsnorm(x, g):
    xf = x.astype(F32)
    y = xf * lax.rsqrt(jnp.mean(xf * xf, axis=-1, keepdims=True) + EPS)
    return (y * g.astype(F32)).astype(x.dtype)


def depthwise_conv(x, w, b, pad_l, pad_r):
    y = lax.conv_general_dilated(
        x, w[:, None, :].astype(x.dtype), window_strides=(1,),
        padding=[(pad_l, pad_r)], dimension_numbers=('NWC', 'WIO', 'NWC'),
        feature_group_count=x.shape[-1])
    return y + b.astype(x.dtype)


def linear_scan(a, b, h0):
    b = b.at[:, 0].add(a[:, 0] * h0)

    def combine(e1, e2):
        a1, b1 = e1
        a2, b2 = e2
        return a1 * a2, a2 * b1 + b2

    _, h = lax.associative_scan(combine, (a, b), axis=1)
    return h


def rglru_coeffs(x, wr, br, wi, bi, lam):
    xb = x.reshape(x.shape[:-1] + (LRU_BLOCKS, LRU_BLOCK_DIM))
    r = jax.nn.sigmoid(jnp.einsum('blnd,nde->blne', xb, wr.astype(F32)).reshape(x.shape) + br.astype(F32))
    i = jax.nn.sigmoid(jnp.einsum('blnd,nde->blne', xb, wi.astype(F32)).reshape(x.shape) + bi.astype(F32))
    log_a = -LRU_C * r * jax.nn.softplus(-lam.astype(F32))
    a = jnp.exp(log_a)
    b = jnp.sqrt(-jnp.expm1(2.0 * log_a)) * (i * x)
    return a, b


def rglru_bidir(xx, xc, wr, br, wi, bi, lam):
    xx32, xc32 = xx.astype(F32), xc.astype(F32)
    h0 = jnp.zeros((xx.shape[0], D_LRU), F32)
    ys_x, ys_c = [], []
    for d in range(2):
        flip = (lambda t: t[:, ::-1]) if d == 1 else (lambda t: t)
        ac, bc = rglru_coeffs(flip(xc32), wr[d], br[d], wi[d], bi[d], lam[d])
        ax, bx = rglru_coeffs(flip(xx32), wr[d], br[d], wi[d], bi[d], lam[d])
        hc = linear_scan(ac, bc, h0)
        hx = linear_scan(ax, bx, hc[:, -1])
        ys_c.append(flip(hc))
        ys_x.append(flip(hx))
    return (ys_x[0] + ys_x[1]).astype(xx.dtype), (ys_c[0] + ys_c[1]).astype(xc.dtype)


def rope_axis(x, ang):
    n = x.shape[-1] // 2
    x1, x2 = x[..., :n], x[..., n:]
    cos, sin = jnp.cos(ang).astype(x.dtype), jnp.sin(ang).astype(x.dtype)
    return jnp.concatenate([x1 * cos - x2 * sin, x2 * cos + x1 * sin], axis=-1)


def rope_2d(x, ang_row, ang_col):
    h = MLA_ROPE // 2
    return jnp.concatenate([rope_axis(x[..., :h], ang_row), rope_axis(x[..., h:], ang_col)], axis=-1)


def mla_project(cq, ckv, kr, q_g, wqb, kv_g, wkvb, angles):
    B, L = cq.shape[:2]
    q = (rmsnorm(cq, q_g) @ wqb).reshape(B, L, MLA_HEADS, MLA_NOPE + MLA_ROPE)
    kv = (rmsnorm(ckv, kv_g) @ wkvb).reshape(B, L, MLA_HEADS, MLA_NOPE + MLA_V)
    q_nope, q_rope = q[..., :MLA_NOPE], q[..., MLA_NOPE:]
    k_nope, v = kv[..., :MLA_NOPE], kv[..., MLA_NOPE:]
    if angles is not None:
        ang_r, ang_c = angles
        q_rope = rope_2d(q_rope, ang_r[:, None, :], ang_c[:, None, :])
        kr = rope_2d(kr, ang_r, ang_c)
    k_rope = jnp.broadcast_to(kr[:, :, None, :], (B, L, MLA_HEADS, MLA_ROPE))
    q = jnp.concatenate([q_nope, q_rope], axis=-1)
    k = jnp.concatenate([k_nope, k_rope], axis=-1)
    return q, k, v


def attend(q, k, v):
    s = jnp.einsum('bqhd,bkhd->bhqk', q, k, preferred_element_type=F32) * MLA_SCALE
    p = jax.nn.softmax(s, axis=-1).astype(v.dtype)
    return jnp.einsum('bhqk,bkhd->bqhd', p, v)


def hyena_filter_spectra(L, w1, b1, w2, b2, w3, freq, decay):
    t = jnp.arange(L, dtype=F32) / L
    bands = jnp.linspace(1e-4, HY_BANDS - 1, HY_BANDS, dtype=F32)
    wpos = (2.0 * math.pi) * t[:, None] * bands[None, :]
    z = jnp.concatenate([t[:, None], jnp.cos(wpos), jnp.sin(wpos)], axis=-1)
    fr = freq.astype(F32)
    h = jnp.sin(fr * (z @ w1.astype(F32) + b1.astype(F32)))
    h = jnp.sin(fr * (h @ w2.astype(F32) + b2.astype(F32)))
    k = (h @ w3.astype(F32)).reshape(L, 2 * HY_ORDER, D_HY) * jnp.exp(-t[:, None, None] * decay.astype(F32))
    k = k * lax.rsqrt(jnp.sum(k * k, axis=0, keepdims=True) + EPS)
    k = k.reshape(L, HY_ORDER, 2, D_HY)
    kf, kb = k[:, :, 0], k[:, :, 1]
    k_circ = jnp.concatenate([kf, jnp.zeros((1, HY_ORDER, D_HY), F32), kb[:0:-1]], axis=0)
    return jnp.fft.rfft(k_circ, axis=0)


def fft_conv(z, kspec, skip):
    L = z.shape[1]
    z32 = z.astype(F32)
    zf = jnp.fft.rfft(z32, n=2 * L, axis=1)
    y = jnp.fft.irfft(zf * kspec[None], n=2 * L, axis=1)[:, :L]
    return (y + z32 * skip.astype(F32)).astype(z.dtype)


def hyena(u, conv_w, conv_b, w1, b1, w2, b2, w3, freq, decay, skip):
    L = u.shape[1]
    u = depthwise_conv(u, conv_w, conv_b, 1, 1)
    v, x1, x2 = jnp.split(u, 3, axis=-1)
    kspec = hyena_filter_spectra(L, w1, b1, w2, b2, w3, freq, decay)
    y = x1 * fft_conv(v, kspec[:, 0], skip[0])
    y = x2 * fft_conv(y, kspec[:, 1], skip[1])
    return y


def token_mixer(hx, hc, need_ctx, w_in, w_out,
                lru_conv_w, lru_conv_b, lru_wr, lru_br, lru_wi, lru_bi, lru_lambda,
                mla_q_norm_g, mla_wqb, mla_kv_norm_g, mla_wkvb,
                hy_conv_w, hy_conv_b, hy_f_w1, hy_f_b1, hy_f_w2, hy_f_b2, hy_f_w3,
                hy_f_freq, hy_decay, hy_skip):
    B, L, _ = hx.shape
    ROWS = L // GRID_W
    row = jnp.repeat(jnp.arange(ROWS, dtype=F32), GRID_W)
    col = jnp.tile(jnp.arange(GRID_W, dtype=F32), ROWS)
    inv_freq = ROPE_THETA ** (-jnp.arange(ROPE_PAIRS, dtype=F32) / ROPE_PAIRS)
    angles = (row[:, None] * inv_freq, col[:, None] * inv_freq)

    ux = jnp.split(hx @ w_in, IN_SPLITS, axis=-1)
    uc = jnp.split(hc @ w_in, IN_SPLITS, axis=-1)
    hy_args = (hy_conv_w, hy_conv_b, hy_f_w1, hy_f_b1, hy_f_w2, hy_f_b2, hy_f_w3, hy_f_freq, hy_decay, hy_skip)

    ax_in = depthwise_conv(ux[0], lru_conv_w, lru_conv_b, 2, 1)
    ac_in = depthwise_conv(uc[0], lru_conv_w, lru_conv_b, 2, 1)
    ra_x, ra_c = rglru_bidir(ax_in, ac_in, lru_wr, lru_br, lru_wi, lru_bi, lru_lambda)
    ya_x = ra_x * jax.nn.gelu(ux[1])

    qx, kx, vx = mla_project(ux[2], ux[3], ux[4], mla_q_norm_g, mla_wqb, mla_kv_norm_g, mla_wkvb, angles)
    qc, kc, vc = mla_project(uc[2], uc[3], uc[4], mla_q_norm_g, mla_wqb, mla_kv_norm_g, mla_wkvb, None)
    k_all = jnp.concatenate([kc, kx], axis=1)
    v_all = jnp.concatenate([vc, vx], axis=1)
    nb = L // ATTN_BLOCK
    q_blocks = jnp.moveaxis(qx.reshape(B, nb, ATTN_BLOCK, MLA_HEADS, MLA_NOPE + MLA_ROPE), 1, 0)
    o_blocks = lax.map(lambda qb: attend(qb, k_all, v_all), q_blocks)
    yb_x = jnp.moveaxis(o_blocks, 0, 1).reshape(B, L, MLA_HEADS * MLA_V)

    yc_x = hyena(ux[5], *hy_args)

    out_x = jnp.concatenate([ya_x, yb_x, yc_x], axis=-1) @ w_out
    if not need_ctx:
        return out_x, None

    Lc = hc.shape[1]
    ya_c = ra_c * jax.nn.gelu(uc[1])
    yb_c = attend(qc, kc, vc).reshape(B, Lc, MLA_HEADS * MLA_V)
    yc_c = hyena(uc[5], *hy_args)
    out_c = jnp.concatenate([ya_c, yb_c, yc_c], axis=-1) @ w_out
    return out_x, out_c


def peer(h, wq, keys, u_tab, v_tab):
    B, L, D = h.shape
    tokens = h.reshape(-1, PEER_BLOCK, D)
    half = PEER_DQ // 2

    def block(xb):
        T = xb.shape[0]
        q = (xb @ wq).reshape(T, PEER_HEADS, PEER_DQ)
        s1 = jnp.einsum('thd,kd->thk', q[..., :half], keys[0], preferred_element_type=F32)
        s2 = jnp.einsum('thd,kd->thk', q[..., half:], keys[1], preferred_element_type=F32)
        v1, i1 = lax.top_k(s1, PEER_TOPK)
        v2, i2 = lax.top_k(s2, PEER_TOPK)
        cand = (v1[..., :, None] + v2[..., None, :]).reshape(T, PEER_HEADS, PEER_TOPK * PEER_TOPK)
        cid = (i1[..., :, None] * PEER_NKEYS + i2[..., None, :]).reshape(T, PEER_HEADS, PEER_TOPK * PEER_TOPK)
        top_s, top_p = lax.top_k(cand, PEER_TOPK)
        eid = jnp.take_along_axis(cid, top_p, axis=-1)
        gate = jax.nn.softmax(top_s, axis=-1).astype(xb.dtype)
        act = jax.nn.gelu(jnp.einsum('td,thkd->thk', xb, u_tab[eid]))
        return jnp.einsum('thk,thkd->td', gate * act, v_tab[eid])

    return lax.map(block, tokens).reshape(B, L, D)


def _final_norm_kernel(x_ref, g_ref, o_ref):
    xf = x_ref[...]
    y = xf * lax.rsqrt(jnp.mean(xf * xf, axis=-1, keepdims=True) + EPS)
    o_ref[...] = y * g_ref[...]


def final_norm(x, g):
    B, L, D = x.shape
    x2 = x.reshape(B * L, D)
    tm = 1024
    out = pl.pallas_call(
        _final_norm_kernel,
        grid=(B * L // tm,),
        in_specs=[pl.BlockSpec((tm, D), lambda i: (i, 0)),
                  pl.BlockSpec((1, D), lambda i: (0, 0))],
        out_specs=pl.BlockSpec((tm, D), lambda i: (i, 0)),
        out_shape=jax.ShapeDtypeStruct((B * L, D), F32),
        name="final_norm",
    )(x2, g.reshape(1, D))
    return out.reshape(B, L, D)


def kernel(x, c, ctx, c_ctx, w_mod, b_mod, norm1_g, norm2_g, w_in, w_out,
           lru_conv_w, lru_conv_b, lru_wr, lru_br, lru_wi, lru_bi, lru_lambda,
           mla_q_norm_g, mla_wqb, mla_kv_norm_g, mla_wkvb,
           hy_conv_w, hy_conv_b, hy_f_w1, hy_f_b1, hy_f_w2, hy_f_b2, hy_f_w3,
           hy_f_freq, hy_decay, hy_skip,
           peer_wq, peer_keys, peer_u, peer_v, final_g):
    for l in range(DEPTH):
        need_ctx = l < DEPTH - 1
        mod_x = jax.nn.silu(c) @ w_mod[l] + b_mod[l]
        mod_c = jax.nn.silu(c_ctx) @ w_mod[l] + b_mod[l]
        sh1, sc1, g1, sh2, sc2, g2 = jnp.split(mod_x[:, None, :], 6, axis=-1)
        csh1, csc1, cg1, csh2, csc2, cg2 = jnp.split(mod_c, 6, axis=-1)

        hx = rmsnorm(x, norm1_g[l]) * (1.0 + sc1) + sh1
        hc = rmsnorm(ctx, norm1_g[l]) * (1.0 + csc1) + csh1
        yx, yc = token_mixer(hx, hc, need_ctx, w_in[l], w_out[l],
                             lru_conv_w[l], lru_conv_b[l], lru_wr[l], lru_br[l], lru_wi[l], lru_bi[l], lru_lambda[l],
                             mla_q_norm_g[l], mla_wqb[l], mla_kv_norm_g[l], mla_wkvb[l],
                             hy_conv_w[l], hy_conv_b[l], hy_f_w1[l], hy_f_b1[l], hy_f_w2[l], hy_f_b2[l], hy_f_w3[l],
                             hy_f_freq[l], hy_decay[l], hy_skip[l])
        x = x + g1 * yx
        hx = rmsnorm(x, norm2_g[l]) * (1.0 + sc2) + sh2
        x = x + g2 * peer(hx, peer_wq[l], peer_keys[l], peer_u[l], peer_v[l])
        if need_ctx:
            ctx = ctx + cg1 * yc
            hc = rmsnorm(ctx, norm2_g[l]) * (1.0 + csc2) + csh2
            ctx = ctx + cg2 * peer(hc, peer_wq[l], peer_keys[l], peer_u[l], peer_v[l])
    return final_norm(x, final_g)
```

```python
import functools
import math

import jax
import jax.numpy as jnp
from jax import lax
from jax.experimental import pallas as pl
from jax.experimental.pallas import tpu as pltpu

D_MODEL = 1024
BATCH = 2
SEQ = 8192
DEPTH = 2

GRID_W = 64
CTX_LEN = 256
EPS = 1e-6

D_LRU = 384
LRU_BLOCKS = 6
LRU_BLOCK_DIM = D_LRU // LRU_BLOCKS
LRU_CONV = 4
LRU_C = 8.0

MLA_HEADS = 6
MLA_NOPE = 64
MLA_ROPE = 32
MLA_V = 64
MLA_Q_RANK = 384
MLA_KV_RANK = 256
MLA_SCALE = (MLA_NOPE + MLA_ROPE) ** -0.5
ROPE_PAIRS = MLA_ROPE // 4
ROPE_THETA = 10000.0
ATTN_BLOCK = 128

D_HY = 256
HY_CONV = 3
HY_ORDER = 2
HY_BANDS = 8
HY_EMB = 1 + 2 * HY_BANDS
HY_FFN = 64

D_MIX = D_LRU + MLA_HEADS * MLA_V + D_HY
D_IN = 2 * D_LRU + MLA_Q_RANK + MLA_KV_RANK + MLA_ROPE + 3 * D_HY
IN_SPLITS = [D_LRU, 2 * D_LRU, 2 * D_LRU + MLA_Q_RANK,
             2 * D_LRU + MLA_Q_RANK + MLA_KV_RANK,
             2 * D_LRU + MLA_Q_RANK + MLA_KV_RANK + MLA_ROPE]

PEER_HEADS = 8
PEER_NKEYS = 128
PEER_EXPERTS = PEER_NKEYS * PEER_NKEYS
PEER_DQ = 256
PEER_TOPK = 16
PEER_BLOCK = 128

F32 = jnp.float32


def rmsnorm(x, g):
    xf = x.astype(F32)
    y = xf * lax.rsqrt(jnp.mean(xf * xf, axis=-1, keepdims=True) + EPS)
    return (y * g.astype(F32)).astype(x.dtype)


def depthwise_conv(x, w, b, pad_l, pad_r):
    y = lax.conv_general_dilated(
        x, w[:, None, :].astype(x.dtype), window_strides=(1,),
        padding=[(pad_l, pad_r)], dimension_numbers=('NWC', 'WIO', 'NWC'),
        feature_group_count=x.shape[-1])
    return y + b.astype(x.dtype)


def linear_scan(a, b, h0):
    b = b.at[:, 0].add(a[:, 0] * h0)

    def combine(e1, e2):
        a1, b1 = e1
        a2, b2 = e2
        return a1 * a2, a2 * b1 + b2

    _, h = lax.associative_scan(combine, (a, b), axis=1)
    return h


def rglru_coeffs(x, wr, br, wi, bi, lam):
    xb = x.reshape(x.shape[:-1] + (LRU_BLOCKS, LRU_BLOCK_DIM))
    r = jax.nn.sigmoid(jnp.einsum('blnd,nde->blne', xb, wr.astype(F32)).reshape(x.shape) + br.astype(F32))
    i = jax.nn.sigmoid(jnp.einsum('blnd,nde->blne', xb, wi.astype(F32)).reshape(x.shape) + bi.astype(F32))
    log_a = -LRU_C * r * jax.nn.softplus(-lam.astype(F32))
    a = jnp.exp(log_a)
    b = jnp.sqrt(-jnp.expm1(2.0 * log_a)) * (i * x)
    return a, b


def rglru_bidir(xx, xc, wr, br, wi, bi, lam):
    xx32, xc32 = xx.astype(F32), xc.astype(F32)
    h0 = jnp.zeros((xx.shape[0], D_LRU), F32)
    ys_x, ys_c = [], []
    for d in range(2):
        flip = (lambda t: t[:, ::-1]) if d == 1 else (lambda t: t)
        ac, bc = rglru_coeffs(flip(xc32), wr[d], br[d], wi[d], bi[d], lam[d])
        ax, bx = rglru_coeffs(flip(xx32), wr[d], br[d], wi[d], bi[d], lam[d])
        hc = linear_scan(ac, bc, h0)
        hx = linear_scan(ax, bx, hc[:, -1])
        ys_c.append(flip(hc))
        ys_x.append(flip(hx))
    return (ys_x[0] + ys_x[1]).astype(xx.dtype), (ys_c[0] + ys_c[1]).astype(xc.dtype)


def rope_axis(x, ang):
    n = x.shape[-1] // 2
    x1, x2 = x[..., :n], x[..., n:]
    cos, sin = jnp.cos(ang).astype(x.dtype), jnp.sin(ang).astype(x.dtype)
    return jnp.concatenate([x1 * cos - x2 * sin, x2 * cos + x1 * sin], axis=-1)


def rope_2d(x, ang_row, ang_col):
    h = MLA_ROPE // 2
    return jnp.concatenate([rope_axis(x[..., :h], ang_row), rope_axis(x[..., h:], ang_col)], axis=-1)


def mla_project(cq, ckv, kr, q_g, wqb, kv_g, wkvb, angles):
    B, L = cq.shape[:2]
    q = (rmsnorm(cq, q_g) @ wqb).reshape(B, L, MLA_HEADS, MLA_NOPE + MLA_ROPE)
    kv = (rmsnorm(ckv, kv_g) @ wkvb).reshape(B, L, MLA_HEADS, MLA_NOPE + MLA_V)
    q_nope, q_rope = q[..., :MLA_NOPE], q[..., MLA_NOPE:]
    k_nope, v = kv[..., :MLA_NOPE], kv[..., MLA_NOPE:]
    if angles is not None:
        ang_r, ang_c = angles
        q_rope = rope_2d(q_rope, ang_r[:, None, :], ang_c[:, None, :])
        kr = rope_2d(kr, ang_r, ang_c)
    k_rope = jnp.broadcast_to(kr[:, :, None, :], (B, L, MLA_HEADS, MLA_ROPE))
    q = jnp.concatenate([q_nope, q_rope], axis=-1)
    k = jnp.concatenate([k_nope, k_rope], axis=-1)
    return q, k, v


def attend(q, k, v):
    s = jnp.einsum('bqhd,bkhd->bhqk', q, k, preferred_element_type=F32) * MLA_SCALE
    p = jax.nn.softmax(s, axis=-1).astype(v.dtype)
    return jnp.einsum('bhqk,bkhd->bqhd', p, v)


def hyena_filter_spectra(L, w1, b1, w2, b2, w3, freq, decay):
    t = jnp.arange(L, dtype=F32) / L
    bands = jnp.linspace(1e-4, HY_BANDS - 1, HY_BANDS, dtype=F32)
    wpos = (2.0 * math.pi) * t[:, None] * bands[None, :]
    z = jnp.concatenate([t[:, None], jnp.cos(wpos), jnp.sin(wpos)], axis=-1)
    fr = freq.astype(F32)
    h = jnp.sin(fr * (z @ w1.astype(F32) + b1.astype(F32)))
    h = jnp.sin(fr * (h @ w2.astype(F32) + b2.astype(F32)))
    k = (h @ w3.astype(F32)).reshape(L, 2 * HY_ORDER, D_HY) * jnp.exp(-t[:, None, None] * decay.astype(F32))
    k = k * lax.rsqrt(jnp.sum(k * k, axis=0, keepdims=True) + EPS)
    k = k.reshape(L, HY_ORDER, 2, D_HY)
    kf, kb = k[:, :, 0], k[:, :, 1]
    k_circ = jnp.concatenate([kf, jnp.zeros((1, HY_ORDER, D_HY), F32), kb[:0:-1]], axis=0)
    return jnp.fft.rfft(k_circ, axis=0)


def fft_conv(z, kspec, skip):
    L = z.shape[1]
    z32 = z.astype(F32)
    zf = jnp.fft.rfft(z32, n=2 * L, axis=1)
    y = jnp.fft.irfft(zf * kspec[None], n=2 * L, axis=1)[:, :L]
    return (y + z32 * skip.astype(F32)).astype(z.dtype)


def hyena(u, conv_w, conv_b, w1, b1, w2, b2, w3, freq, decay, skip):
    L = u.shape[1]
    u = depthwise_conv(u, conv_w, conv_b, 1, 1)
    v, x1, x2 = jnp.split(u, 3, axis=-1)
    kspec = hyena_filter_spectra(L, w1, b1, w2, b2, w3, freq, decay)
    y = x1 * fft_conv(v, kspec[:, 0], skip[0])
    y = x2 * fft_conv(y, kspec[:, 1], skip[1])
    return y


def token_mixer(hx, hc, need_ctx, w_in, w_out,
                lru_conv_w, lru_conv_b, lru_wr, lru_br, lru_wi, lru_bi, lru_lambda,
                mla_q_norm_g, mla_wqb, mla_kv_norm_g, mla_wkvb,
                hy_conv_w, hy_conv_b, hy_f_w1, hy_f_b1, hy_f_w2, hy_f_b2, hy_f_w3,
                hy_f_freq, hy_decay, hy_skip):
    B, L, _ = hx.shape
    ROWS = L // GRID_W
    row = jnp.repeat(jnp.arange(ROWS, dtype=F32), GRID_W)
    col = jnp.tile(jnp.arange(GRID_W, dtype=F32), ROWS)
    inv_freq = ROPE_THETA ** (-jnp.arange(ROPE_PAIRS, dtype=F32) / ROPE_PAIRS)
    angles = (row[:, None] * inv_freq, col[:, None] * inv_freq)

    ux = jnp.split(hx @ w_in, IN_SPLITS, axis=-1)
    uc = jnp.split(hc @ w_in, IN_SPLITS, axis=-1)
    hy_args = (hy_conv_w, hy_conv_b, hy_f_w1, hy_f_b1, hy_f_w2, hy_f_b2, hy_f_w3, hy_f_freq, hy_decay, hy_skip)

    ax_in = depthwise_conv(ux[0], lru_conv_w, lru_conv_b, 2, 1)
    ac_in = depthwise_conv(uc[0], lru_conv_w, lru_conv_b, 2, 1)
    ra_x, ra_c = rglru_bidir(ax_in, ac_in, lru_wr, lru_br, lru_wi, lru_bi, lru_lambda)
    ya_x = ra_x * jax.nn.gelu(ux[1])

    qx, kx, vx = mla_project(ux[2], ux[3], ux[4], mla_q_norm_g, mla_wqb, mla_kv_norm_g, mla_wkvb, angles)
    qc, kc, vc = mla_project(uc[2], uc[3], uc[4], mla_q_norm_g, mla_wqb, mla_kv_norm_g, mla_wkvb, None)
    k_all = jnp.concatenate([kc, kx], axis=1)
    v_all = jnp.concatenate([vc, vx], axis=1)
    nb = L // ATTN_BLOCK
    q_blocks = jnp.moveaxis(qx.reshape(B, nb, ATTN_BLOCK, MLA_HEADS, MLA_NOPE + MLA_ROPE), 1, 0)
    o_blocks = lax.map(lambda qb: attend(qb, k_all, v_all), q_blocks)
    yb_x = jnp.moveaxis(o_blocks, 0, 1).reshape(B, L, MLA_HEADS * MLA_V)

    yc_x = hyena(ux[5], *hy_args)

    out_x = jnp.concatenate([ya_x, yb_x, yc_x], axis=-1) @ w_out
    if not need_ctx:
        return out_x, None

    Lc = hc.shape[1]
    ya_c = ra_c * jax.nn.gelu(uc[1])
    yb_c = attend(qc, kc, vc).reshape(B, Lc, MLA_HEADS * MLA_V)
    yc_c = hyena(uc[5], *hy_args)
    out_c = jnp.concatenate([ya_c, yb_c, yc_c], axis=-1) @ w_out
    return out_x, out_c


BF16 = jnp.bfloat16
LANES = 128
VMEM_LIMIT = 48 * 1024 * 1024
PEER_TM = 512
PEER_TE = 512
NEG_BIG = -3.0e38


def _gelu_tanh(x):
    return 0.5 * x * (1.0 + jnp.tanh(0.7978845608028654 * (x + 0.044715 * (x * x * x))))


def _topk_desc(s, k):
    rank = jnp.full(s.shape, float(PEER_NKEYS), F32)
    vals = []
    for r in range(k):
        m = jnp.max(s, axis=0, keepdims=True)
        hit = s == m
        rank = jnp.where(hit, float(r), rank)
        s = jnp.where(hit, NEG_BIG, s)
        vals.append(m)
    return jnp.concatenate(vals, axis=0), rank


def _route_chunk(s1, s2):
    K = PEER_TOPK
    v1, rank1 = _topk_desc(s1, K)
    v2, rank2 = _topk_desc(s2, K)
    cands = [v1[i:i + 1] + v2 for i in range(K)]
    cmax = v1[0:1] + v2[0:1]
    c = jnp.concatenate(cands, axis=0)
    z = jnp.zeros_like(cmax)
    m = cmax
    for r in range(K):
        m = jnp.max(c, axis=0, keepdims=True)
        z = z + jnp.exp(m - cmax)
        if r + 1 < K:
            c = jnp.where(c == m, NEG_BIG, c)
    thr = m
    n = jnp.zeros_like(s1)
    for i in range(K):
        n_i = jnp.sum(jnp.where(cands[i] >= thr, 1.0, 0.0), axis=0, keepdims=True)
        n = jnp.where(rank1 == float(i), n_i, n)
    w = jnp.exp(s1 - v1[0:1]) / z
    e2 = jnp.exp(s2 - v2[0:1])
    return rank2, e2, n, w


def _peer_route_kernel(hT_ref, wqT_ref, keys_ref, rank2_ref, e2_ref, n_ref, w_ref, q_scr, s1_scr, s2_scr):
    h = pl.program_id(1)

    @pl.when(h == 0)
    def _():
        q_scr[...] = jnp.dot(wqT_ref[...], hT_ref[...], preferred_element_type=F32)

    cdt = keys_ref.dtype
    half = PEER_DQ // 2
    row = pl.multiple_of(h * PEER_DQ, PEER_DQ)
    q1 = q_scr[pl.ds(row, half), :].astype(cdt)
    q2 = q_scr[pl.ds(row + half, half), :].astype(cdt)
    s1_scr[...] = jnp.dot(keys_ref[0], q1, preferred_element_type=F32)
    s2_scr[...] = jnp.dot(keys_ref[1], q2, preferred_element_type=F32)

    def chunk(c, carry):
        sl = pl.ds(pl.multiple_of(c * LANES, LANES), LANES)
        rank2, e2, n, w = _route_chunk(s1_scr[:, sl], s2_scr[:, sl])
        rank2_ref[0, :, sl] = rank2
        e2_ref[0, :, sl] = e2
        n_ref[0, :, sl] = n
        w_ref[0, :, sl] = w
        return carry

    lax.fori_loop(0, s1_scr.shape[1] // LANES, chunk, 0)


def _peer_dense_kernel(hT_ref, rank2_ref, e2_ref, n_ref, w_ref, u_ref, vT_ref, o_ref, acc_ref, act_ref):
    j = pl.program_id(1)

    @pl.when(j == 0)
    def _():
        acc_ref[...] = jnp.zeros_like(acc_ref)

    te = u_ref.shape[0]
    groups = te // PEER_NKEYS
    pre = jnp.dot(u_ref[...], hT_ref[...], preferred_element_type=F32)
    for al in range(groups):
        a = j * groups + al
        g = None
        for h in range(PEER_HEADS):
            n_row = n_ref[h, pl.ds(a, 1), :]
            w_row = w_ref[h, pl.ds(a, 1), :]
            t = jnp.where(rank2_ref[h] < n_row, e2_ref[h] * w_row, 0.0)
            g = t if g is None else g + t
        blk = slice(al * PEER_NKEYS, (al + 1) * PEER_NKEYS)
        act_ref[blk, :] = (g * _gelu_tanh(pre[blk, :])).astype(act_ref.dtype)
    acc_ref[...] += jnp.dot(vT_ref[...], act_ref[...], preferred_element_type=F32)

    @pl.when(j == pl.num_programs(1) - 1)
    def _():
        o_ref[...] = acc_ref[...].T


def peer_tokens(h, wq, keys, u_tab, v_tab, cdt=BF16, interpret=False):
    T, D = h.shape
    tm, te = PEER_TM, PEER_TE
    assert T % tm == 0 and PEER_EXPERTS % te == 0
    hT = h.T.astype(cdt)
    wqT = wq.T.astype(cdt)
    route_shape = jax.ShapeDtypeStruct((PEER_HEADS, PEER_NKEYS, T), F32)
    route_spec = pl.BlockSpec((1, PEER_NKEYS, tm), lambda i, hh: (hh, 0, i))
    rank2, e2, n, w = pl.pallas_call(
        _peer_route_kernel,
        grid=(T // tm, PEER_HEADS),
        in_specs=[pl.BlockSpec((D, tm), lambda i, hh: (0, i)),
                  pl.BlockSpec((PEER_HEADS * PEER_DQ, D), lambda i, hh: (0, 0)),
                  pl.BlockSpec((2, PEER_NKEYS, PEER_DQ // 2), lambda i, hh: (0, 0, 0))],
        out_specs=[route_spec] * 4,
        out_shape=[route_shape] * 4,
        scratch_shapes=[pltpu.VMEM((PEER_HEADS * PEER_DQ, tm), F32),
                        pltpu.VMEM((PEER_NKEYS, tm), F32),
                        pltpu.VMEM((PEER_NKEYS, tm), F32)],
        compiler_params=pltpu.CompilerParams(
            dimension_semantics=("arbitrary", "arbitrary"), vmem_limit_bytes=VMEM_LIMIT),
        name="peer_route", interpret=interpret,
    )(hT, wqT, keys.astype(cdt))

    head_spec = pl.BlockSpec((PEER_HEADS, PEER_NKEYS, tm), lambda i, j: (0, 0, i))
    out = pl.pallas_call(
        _peer_dense_kernel,
        grid=(T // tm, PEER_EXPERTS // te),
        in_specs=[pl.BlockSpec((D, tm), lambda i, j: (0, i)),
                  head_spec, head_spec, head_spec, head_spec,
                  pl.BlockSpec((te, D), lambda i, j: (j, 0)),
                  pl.BlockSpec((D, te), lambda i, j: (0, j))],
        out_specs=pl.BlockSpec((tm, D), lambda i, j: (i, 0)),
        out_shape=jax.ShapeDtypeStruct((T, D), F32),
        scratch_shapes=[pltpu.VMEM((D, tm), F32), pltpu.VMEM((te, tm), cdt)],
        compiler_params=pltpu.CompilerParams(
            dimension_semantics=("arbitrary", "arbitrary"), vmem_limit_bytes=VMEM_LIMIT),
        name="peer_dense", interpret=interpret,
    )(hT, rank2, e2, n, w, u_tab.astype(cdt), v_tab.T.astype(cdt))
    return out


def peer(h, wq, keys, u_tab, v_tab):
    B, L, D = h.shape
    return peer_tokens(h.reshape(B * L, D), wq, keys, u_tab, v_tab).reshape(B, L, D)


def _final_norm_kernel(x_ref, g_ref, o_ref):
    xf = x_ref[...]
    y = xf * lax.rsqrt(jnp.mean(xf * xf, axis=-1, keepdims=True) + EPS)
    o_ref[...] = y * g_ref[...]


def final_norm(x, g):
    B, L, D = x.shape
    x2 = x.reshape(B * L, D)
    tm = 1024
    out = pl.pallas_call(
        _final_norm_kernel,
        grid=(B * L // tm,),
        in_specs=[pl.BlockSpec((tm, D), lambda i: (i, 0)),
                  pl.BlockSpec((1, D), lambda i: (0, 0))],
        out_specs=pl.BlockSpec((tm, D), lambda i: (i, 0)),
        out_shape=jax.ShapeDtypeStruct((B * L, D), F32),
        name="final_norm",
    )(x2, g.reshape(1, D))
    return out.reshape(B, L, D)


def kernel(x, c, ctx, c_ctx, w_mod, b_mod, norm1_g, norm2_g, w_in, w_out,
           lru_conv_w, lru_conv_b, lru_wr, lru_br, lru_wi, lru_bi, lru_lambda,
           mla_q_norm_g, mla_wqb, mla_kv_norm_g, mla_wkvb,
           hy_conv_w, hy_conv_b, hy_f_w1, hy_f_b1, hy_f_w2, hy_f_b2, hy_f_w3,
           hy_f_freq, hy_decay, hy_skip,
           peer_wq, peer_keys, peer_u, peer_v, final_g):
    for l in range(DEPTH):
        need_ctx = l < DEPTH - 1
        mod_x = jax.nn.silu(c) @ w_mod[l] + b_mod[l]
        mod_c = jax.nn.silu(c_ctx) @ w_mod[l] + b_mod[l]
        sh1, sc1, g1, sh2, sc2, g2 = jnp.split(mod_x[:, None, :], 6, axis=-1)
        csh1, csc1, cg1, csh2, csc2, cg2 = jnp.split(mod_c, 6, axis=-1)

        hx = rmsnorm(x, norm1_g[l]) * (1.0 + sc1) + sh1
        hc = rmsnorm(ctx, norm1_g[l]) * (1.0 + csc1) + csh1
        yx, yc = token_mixer(hx, hc, need_ctx, w_in[l], w_out[l],
                             lru_conv_w[l], lru_conv_b[l], lru_wr[l], lru_br[l], lru_wi[l], lru_bi[l], lru_lambda[l],
                             mla_q_norm_g[l], mla_wqb[l], mla_kv_norm_g[l], mla_wkvb[l],
                             hy_conv_w[l], hy_conv_b[l], hy_f_w1[l], hy_f_b1[l], hy_f_w2[l], hy_f_b2[l], hy_f_w3[l],
                             hy_f_freq[l], hy_decay[l], hy_skip[l])
        x = x + g1 * yx
        hx = rmsnorm(x, norm2_g[l]) * (1.0 + sc2) + sh2
        x = x + g2 * peer(hx, peer_wq[l], peer_keys[l], peer_u[l], peer_v[l])
        if need_ctx:
            ctx = ctx + cg1 * yc
            hc = rmsnorm(ctx, norm2_g[l]) * (1.0 + csc2) + csh2
            ctx = ctx + cg2 * peer(hc, peer_wq[l], peer_keys[l], peer_u[l], peer_v[l])
    return final_norm(x, final_g)
```

```python
import functools
import math

import jax
import jax.numpy as jnp
from jax import lax
from jax.experimental import pallas as pl
from jax.experimental.pallas import tpu as pltpu

D_MODEL = 1024
BATCH = 2
SEQ = 8192
DEPTH = 2

GRID_W = 64
CTX_LEN = 256
EPS = 1e-6

D_LRU = 384
LRU_BLOCKS = 6
LRU_BLOCK_DIM = D_LRU // LRU_BLOCKS
LRU_CONV = 4
LRU_C = 8.0

MLA_HEADS = 6
MLA_NOPE = 64
MLA_ROPE = 32
MLA_V = 64
MLA_Q_RANK = 384
MLA_KV_RANK = 256
MLA_SCALE = (MLA_NOPE + MLA_ROPE) ** -0.5
ROPE_PAIRS = MLA_ROPE // 4
ROPE_THETA = 10000.0
ATTN_BLOCK = 128

D_HY = 256
HY_CONV = 3
HY_ORDER = 2
HY_BANDS = 8
HY_EMB = 1 + 2 * HY_BANDS
HY_FFN = 64

D_MIX = D_LRU + MLA_HEADS * MLA_V + D_HY
D_IN = 2 * D_LRU + MLA_Q_RANK + MLA_KV_RANK + MLA_ROPE + 3 * D_HY
IN_SPLITS = [D_LRU, 2 * D_LRU, 2 * D_LRU + MLA_Q_RANK,
             2 * D_LRU + MLA_Q_RANK + MLA_KV_RANK,
             2 * D_LRU + MLA_Q_RANK + MLA_KV_RANK + MLA_ROPE]

PEER_HEADS = 8
PEER_NKEYS = 128
PEER_EXPERTS = PEER_NKEYS * PEER_NKEYS
PEER_DQ = 256
PEER_TOPK = 16
PEER_BLOCK = 128

F32 = jnp.float32


def rmsnorm(x, g):
    xf = x.astype(F32)
    y = xf * lax.rsqrt(jnp.mean(xf * xf, axis=-1, keepdims=True) + EPS)
    return (y * g.astype(F32)).astype(x.dtype)


def depthwise_conv(x, w, b, pad_l, pad_r):
    y = lax.conv_general_dilated(
        x, w[:, None, :].astype(x.dtype), window_strides=(1,),
        padding=[(pad_l, pad_r)], dimension_numbers=('NWC', 'WIO', 'NWC'),
        feature_group_count=x.shape[-1])
    return y + b.astype(x.dtype)


def linear_scan(a, b, h0):
    b = b.at[:, 0].add(a[:, 0] * h0)

    def combine(e1, e2):
        a1, b1 = e1
        a2, b2 = e2
        return a1 * a2, a2 * b1 + b2

    _, h = lax.associative_scan(combine, (a, b), axis=1)
    return h


def rglru_coeffs(x, wr, br, wi, bi, lam):
    xb = x.reshape(x.shape[:-1] + (LRU_BLOCKS, LRU_BLOCK_DIM))
    r = jax.nn.sigmoid(jnp.einsum('blnd,nde->blne', xb, wr.astype(F32)).reshape(x.shape) + br.astype(F32))
    i = jax.nn.sigmoid(jnp.einsum('blnd,nde->blne', xb, wi.astype(F32)).reshape(x.shape) + bi.astype(F32))
    log_a = -LRU_C * r * jax.nn.softplus(-lam.astype(F32))
    a = jnp.exp(log_a)
    b = jnp.sqrt(-jnp.expm1(2.0 * log_a)) * (i * x)
    return a, b


def rglru_bidir(xx, xc, wr, br, wi, bi, lam):
    xx32, xc32 = xx.astype(F32), xc.astype(F32)
    h0 = jnp.zeros((xx.shape[0], D_LRU), F32)
    ys_x, ys_c = [], []
    for d in range(2):
        flip = (lambda t: t[:, ::-1]) if d == 1 else (lambda t: t)
        ac, bc = rglru_coeffs(flip(xc32), wr[d], br[d], wi[d], bi[d], lam[d])
        ax, bx = rglru_coeffs(flip(xx32), wr[d], br[d], wi[d], bi[d], lam[d])
        hc = linear_scan(ac, bc, h0)
        hx = linear_scan(ax, bx, hc[:, -1])
        ys_c.append(flip(hc))
        ys_x.append(flip(hx))
    return (ys_x[0] + ys_x[1]).astype(xx.dtype), (ys_c[0] + ys_c[1]).astype(xc.dtype)


def rope_axis(x, ang):
    n = x.shape[-1] // 2
    x1, x2 = x[..., :n], x[..., n:]
    cos, sin = jnp.cos(ang).astype(x.dtype), jnp.sin(ang).astype(x.dtype)
    return jnp.concatenate([x1 * cos - x2 * sin, x2 * cos + x1 * sin], axis=-1)


def rope_2d(x, ang_row, ang_col):
    h = MLA_ROPE // 2
    return jnp.concatenate([rope_axis(x[..., :h], ang_row), rope_axis(x[..., h:], ang_col)], axis=-1)


def mla_project(cq, ckv, kr, q_g, wqb, kv_g, wkvb, angles):
    B, L = cq.shape[:2]
    q = (rmsnorm(cq, q_g) @ wqb).reshape(B, L, MLA_HEADS, MLA_NOPE + MLA_ROPE)
    kv = (rmsnorm(ckv, kv_g) @ wkvb).reshape(B, L, MLA_HEADS, MLA_NOPE + MLA_V)
    q_nope, q_rope = q[..., :MLA_NOPE], q[..., MLA_NOPE:]
    k_nope, v = kv[..., :MLA_NOPE], kv[..., MLA_NOPE:]
    if angles is not None:
        ang_r, ang_c = angles
        q_rope = rope_2d(q_rope, ang_r[:, None, :], ang_c[:, None, :])
        kr = rope_2d(kr, ang_r, ang_c)
    k_rope = jnp.broadcast_to(kr[:, :, None, :], (B, L, MLA_HEADS, MLA_ROPE))
    q = jnp.concatenate([q_nope, q_rope], axis=-1)
    k = jnp.concatenate([k_nope, k_rope], axis=-1)
    return q, k, v


def attend(q, k, v):
    s = jnp.einsum('bqhd,bkhd->bhqk', q, k, preferred_element_type=F32) * MLA_SCALE
    p = jax.nn.softmax(s, axis=-1).astype(v.dtype)
    return jnp.einsum('bhqk,bkhd->bqhd', p, v)


def hyena_filter_spectra(L, w1, b1, w2, b2, w3, freq, decay):
    t = jnp.arange(L, dtype=F32) / L
    bands = jnp.linspace(1e-4, HY_BANDS - 1, HY_BANDS, dtype=F32)
    wpos = (2.0 * math.pi) * t[:, None] * bands[None, :]
    z = jnp.concatenate([t[:, None], jnp.cos(wpos), jnp.sin(wpos)], axis=-1)
    fr = freq.astype(F32)
    h = jnp.sin(fr * (z @ w1.astype(F32) + b1.astype(F32)))
    h = jnp.sin(fr * (h @ w2.astype(F32) + b2.astype(F32)))
    k = (h @ w3.astype(F32)).reshape(L, 2 * HY_ORDER, D_HY) * jnp.exp(-t[:, None, None] * decay.astype(F32))
    k = k * lax.rsqrt(jnp.sum(k * k, axis=0, keepdims=True) + EPS)
    k = k.reshape(L, HY_ORDER, 2, D_HY)
    kf, kb = k[:, :, 0], k[:, :, 1]
    k_circ = jnp.concatenate([kf, jnp.zeros((1, HY_ORDER, D_HY), F32), kb[:0:-1]], axis=0)
    return jnp.fft.rfft(k_circ, axis=0)


def fft_conv(z, kspec, skip):
    L = z.shape[1]
    z32 = z.astype(F32)
    zf = jnp.fft.rfft(z32, n=2 * L, axis=1)
    y = jnp.fft.irfft(zf * kspec[None], n=2 * L, axis=1)[:, :L]
    return (y + z32 * skip.astype(F32)).astype(z.dtype)


def hyena(u, conv_w, conv_b, w1, b1, w2, b2, w3, freq, decay, skip):
    L = u.shape[1]
    u = depthwise_conv(u, conv_w, conv_b, 1, 1)
    v, x1, x2 = jnp.split(u, 3, axis=-1)
    kspec = hyena_filter_spectra(L, w1, b1, w2, b2, w3, freq, decay)
    y = x1 * fft_conv(v, kspec[:, 0], skip[0])
    y = x2 * fft_conv(y, kspec[:, 1], skip[1])
    return y


BF16 = jnp.bfloat16
LANES = 128
SUBLANES = 8
VMEM_LIMIT = 48 * 1024 * 1024
SEQ_ALL = CTX_LEN + SEQ
LRU_TT = CTX_LEN
LRU_NT = SEQ_ALL // LRU_TT


def _gelu_tanh(x):
    return 0.5 * x * (1.0 + jnp.tanh(0.7978845608028654 * (x + 0.044715 * (x * x * x))))


def _lru_coeff_kernel(u_ref, up_ref, un_ref, cw_ref, cb_ref, wg_ref, bg_ref, c_ref,
                      af_ref, bf_ref, ab_ref, bb_ref):
    i = pl.program_id(1)
    x = u_ref[0]
    tt = x.shape[0]
    use_prev = i >= 2
    use_next = jnp.logical_and(i >= 1, i < LRU_NT - 1)
    p2 = jnp.where(use_prev, up_ref[0][SUBLANES - 2:SUBLANES], 0.0)
    n1 = jnp.where(use_next, un_ref[0][0:1], 0.0)
    ext = jnp.concatenate([p2, x, n1], axis=0)
    xc = cb_ref[...] + sum(cw_ref[k:k + 1, :] * ext[k:k + tt] for k in range(LRU_CONV))
    g = jnp.dot(xc.astype(wg_ref.dtype), wg_ref[...], preferred_element_type=F32) + bg_ref[...]
    for d, (a_ref, b_ref) in enumerate(((af_ref, bf_ref), (ab_ref, bb_ref))):
        r = jax.nn.sigmoid(g[:, (2 * d) * D_LRU:(2 * d + 1) * D_LRU])
        gi = jax.nn.sigmoid(g[:, (2 * d + 1) * D_LRU:(2 * d + 2) * D_LRU])
        log_a = c_ref[d:d + 1, :] * r
        a_ref[0] = jnp.exp(log_a)
        b_ref[0] = jnp.sqrt(1.0 - jnp.exp(2.0 * log_a)) * (gi * xc)


def _lru_scan_rows(a_ref, b_ref, h_scr, emit, reverse):
    tt = a_ref.shape[1]
    nb = a_ref.shape[0]

    def body(s, hs):
        t = (tt - 1 - s) if reverse else s
        out = []
        for b in range(nb):
            h = a_ref[b, pl.ds(t, 1), :] * hs[b] + b_ref[b, pl.ds(t, 1), :]
            emit(b, t, h)
            out.append(h)
        return tuple(out)

    hs = lax.fori_loop(0, tt, body, tuple(h_scr[b:b + 1, :] for b in range(nb)), unroll=8)
    for b in range(nb):
        h_scr[b:b + 1, :] = hs[b]


def _lru_fwd_kernel(a_ref, b_ref, hf_ref, h_scr):
    @pl.when(pl.program_id(0) == 0)
    def _():
        h_scr[...] = jnp.zeros_like(h_scr)

    def emit(b, t, h):
        hf_ref[b, pl.ds(t, 1), :] = h

    _lru_scan_rows(a_ref, b_ref, h_scr, emit, reverse=False)


def _lru_bwd_kernel(a_ref, b_ref, hf_ref, gate_ref, y_ref, h_scr):
    @pl.when(pl.program_id(0) == 0)
    def _():
        h_scr[...] = jnp.zeros_like(h_scr)

    def emit(b, t, h):
        y_ref[b, pl.ds(t, 1), :] = h

    _lru_scan_rows(a_ref, b_ref, h_scr, emit, reverse=True)
    y_ref[...] = (y_ref[...] + hf_ref[...]) * _gelu_tanh(gate_ref[...])


def rglru_mixer(u_lru, u_gate, conv_w, conv_b, wr, br, wi, bi, lam, cdt=BF16, interpret=False):
    B = u_lru.shape[0]
    tt, nt = LRU_TT, LRU_NT
    hb = tt // SUBLANES

    def blockdiag(w):
        eye = jnp.eye(LRU_BLOCKS, dtype=w.dtype)
        return jnp.einsum('nde,nm->ndme', w, eye).reshape(D_LRU, D_LRU)

    wg = jnp.concatenate([blockdiag(wr[0]), blockdiag(wi[0]), blockdiag(wr[1]), blockdiag(wi[1])], axis=1).astype(cdt)
    bg = jnp.concatenate([br[0], bi[0], br[1], bi[1]]).reshape(1, 4 * D_LRU).astype(F32)
    cdec = -LRU_C * jax.nn.softplus(-lam.astype(F32))

    tile = pl.BlockSpec((1, tt, D_LRU), lambda b, i: (b, i, 0))
    full = lambda shape: pl.BlockSpec(shape, lambda b, i: (0,) * len(shape))
    coeff_shape = jax.ShapeDtypeStruct((B, SEQ_ALL, D_LRU), F32)
    a_f, b_f, a_b, b_b = pl.pallas_call(
        _lru_coeff_kernel,
        grid=(B, nt),
        in_specs=[tile,
                  pl.BlockSpec((1, SUBLANES, D_LRU), lambda b, i: (b, jnp.maximum(i * hb - 1, 0), 0)),
                  pl.BlockSpec((1, SUBLANES, D_LRU), lambda b, i: (b, jnp.minimum((i + 1) * hb, nt * hb - 1), 0)),
                  full((LRU_CONV, D_LRU)), full((1, D_LRU)), full((D_LRU, 4 * D_LRU)), full((1, 4 * D_LRU)),
                  full((2, D_LRU))],
        out_specs=[tile] * 4,
        out_shape=[coeff_shape] * 4,
        compiler_params=pltpu.CompilerParams(dimension_semantics=("arbitrary", "arbitrary")),
        name="lru_coeff", interpret=interpret,
    )(u_lru, u_lru, u_lru, conv_w.astype(F32), conv_b.reshape(1, D_LRU).astype(F32), wg, bg, cdec)

    seq = pl.BlockSpec((B, tt, D_LRU), lambda s: (0, s, 0))
    h_f = pl.pallas_call(
        _lru_fwd_kernel,
        grid=(nt,),
        in_specs=[seq, seq],
        out_specs=seq,
        out_shape=coeff_shape,
        scratch_shapes=[pltpu.VMEM((B, D_LRU), F32)],
        compiler_params=pltpu.CompilerParams(dimension_semantics=("arbitrary",)),
        name="lru_fwd", interpret=interpret,
    )(a_f, b_f)

    rseq = pl.BlockSpec((B, tt, D_LRU), lambda s: (0, jnp.where(s == 0, 0, nt - s), 0))
    return pl.pallas_call(
        _lru_bwd_kernel,
        grid=(nt,),
        in_specs=[rseq, rseq, rseq, rseq],
        out_specs=rseq,
        out_shape=coeff_shape,
        scratch_shapes=[pltpu.VMEM((B, D_LRU), F32)],
        compiler_params=pltpu.CompilerParams(dimension_semantics=("arbitrary",)),
        name="lru_bwd", interpret=interpret,
    )(a_b, b_b, h_f, u_gate)


ATT_TQ = 256


def _flash_kernel(q_ref, kT_ref, v_ref, o_ref, *, tk):
    q = q_ref[0]
    tq = q.shape[0]
    n_chunks = kT_ref.shape[2] // tk

    def body(c, carry):
        m, l, acc = carry
        off = pl.multiple_of(c * tk, LANES)
        s = jnp.dot(q, kT_ref[0, :, pl.ds(off, tk)], preferred_element_type=F32)
        m_new = jnp.maximum(m, jnp.max(s, axis=1, keepdims=True))
        alpha = jnp.exp(m - m_new)
        p = jnp.exp(s - m_new)
        l = alpha * l + jnp.sum(p, axis=1, keepdims=True)
        pv = jnp.dot(p.astype(v_ref.dtype), v_ref[0, pl.ds(off, tk), :], preferred_element_type=F32)
        return m_new, l, alpha * acc + pv

    init = (jnp.full((tq, 1), NEG_BIG, F32), jnp.zeros((tq, 1), F32), jnp.zeros((tq, v_ref.shape[2]), F32))
    _, l, acc = lax.fori_loop(0, n_chunks, body, init, unroll=True)
    o_ref[0] = acc / l


def flash_attention(q, k, v, tk, cdt=BF16, interpret=False):
    B, Lq, H, dh = q.shape
    Lk, dv = k.shape[1], v.shape[3]
    tq = min(ATT_TQ, Lq)
    assert Lq % tq == 0 and Lk % tk == 0 and tk % LANES == 0
    qh = (q * MLA_SCALE).astype(cdt).transpose(0, 2, 1, 3).reshape(B * H, Lq, dh)
    kT = k.astype(cdt).transpose(0, 2, 3, 1).reshape(B * H, dh, Lk)
    vh = v.astype(cdt).transpose(0, 2, 1, 3).reshape(B * H, Lk, dv)
    o = pl.pallas_call(
        functools.partial(_flash_kernel, tk=tk),
        grid=(B * H, Lq // tq),
        in_specs=[pl.BlockSpec((1, tq, dh), lambda g, i: (g, i, 0)),
                  pl.BlockSpec((1, dh, Lk), lambda g, i: (g, 0, 0)),
                  pl.BlockSpec((1, Lk, dv), lambda g, i: (g, 0, 0))],
        out_specs=pl.BlockSpec((1, tq, dv), lambda g, i: (g, i, 0)),
        out_shape=jax.ShapeDtypeStruct((B * H, Lq, dv), F32),
        compiler_params=pltpu.CompilerParams(
            dimension_semantics=("arbitrary", "arbitrary"), vmem_limit_bytes=VMEM_LIMIT),
        name="flash_attention", interpret=interpret,
    )(qh, kT, vh)
    return o.reshape(B, H, Lq, dv).transpose(0, 2, 1, 3).reshape(B, Lq, H * dv)


def token_mixer(hx, hc, need_ctx, w_in, w_out,
                lru_conv_w, lru_conv_b, lru_wr, lru_br, lru_wi, lru_bi, lru_lambda,
                mla_q_norm_g, mla_wqb, mla_kv_norm_g, mla_wkvb,
                hy_conv_w, hy_conv_b, hy_f_w1, hy_f_b1, hy_f_w2, hy_f_b2, hy_f_w3,
                hy_f_freq, hy_decay, hy_skip):
    B, L, _ = hx.shape
    ROWS = L // GRID_W
    row = jnp.repeat(jnp.arange(ROWS, dtype=F32), GRID_W)
    col = jnp.tile(jnp.arange(GRID_W, dtype=F32), ROWS)
    inv_freq = ROPE_THETA ** (-jnp.arange(ROPE_PAIRS, dtype=F32) / ROPE_PAIRS)
    angles = (row[:, None] * inv_freq, col[:, None] * inv_freq)

    ux = jnp.split(hx @ w_in, IN_SPLITS, axis=-1)
    uc = jnp.split(hc @ w_in, IN_SPLITS, axis=-1)
    hy_args = (hy_conv_w, hy_conv_b, hy_f_w1, hy_f_b1, hy_f_w2, hy_f_b2, hy_f_w3, hy_f_freq, hy_decay, hy_skip)

    ya = rglru_mixer(jnp.concatenate([uc[0], ux[0]], axis=1), jnp.concatenate([uc[1], ux[1]], axis=1),
                     lru_conv_w, lru_conv_b, lru_wr, lru_br, lru_wi, lru_bi, lru_lambda)
    ya_c, ya_x = ya[:, :CTX_LEN], ya[:, CTX_LEN:]

    qx, kx, vx = mla_project(ux[2], ux[3], ux[4], mla_q_norm_g, mla_wqb, mla_kv_norm_g, mla_wkvb, angles)
    qc, kc, vc = mla_project(uc[2], uc[3], uc[4], mla_q_norm_g, mla_wqb, mla_kv_norm_g, mla_wkvb, None)
    k_all = jnp.concatenate([kc, kx], axis=1)
    v_all = jnp.concatenate([vc, vx], axis=1)
    yb_x = flash_attention(qx, k_all, v_all, tk=SEQ_ALL // 6)

    yc_x = hyena(ux[5], *hy_args)

    out_x = jnp.concatenate([ya_x, yb_x, yc_x], axis=-1) @ w_out
    if not need_ctx:
        return out_x, None

    Lc = hc.shape[1]
    yb_c = flash_attention(qc, kc, vc, tk=Lc)
    yc_c = hyena(uc[5], *hy_args)
    out_c = jnp.concatenate([ya_c, yb_c, yc_c], axis=-1) @ w_out
    return out_x, out_c


PEER_TM = 512
PEER_TE = 512
NEG_BIG = -3.0e38


def _topk_desc(s, k):
    rank = jnp.full(s.shape, float(PEER_NKEYS), F32)
    vals = []
    for r in range(k):
        m = jnp.max(s, axis=0, keepdims=True)
        hit = s == m
        rank = jnp.where(hit, float(r), rank)
        s = jnp.where(hit, NEG_BIG, s)
        vals.append(m)
    return jnp.concatenate(vals, axis=0), rank


def _route_chunk(s1, s2):
    K = PEER_TOPK
    v1, rank1 = _topk_desc(s1, K)
    v2, rank2 = _topk_desc(s2, K)
    cands = [v1[i:i + 1] + v2 for i in range(K)]
    cmax = v1[0:1] + v2[0:1]
    c = jnp.concatenate(cands, axis=0)
    z = jnp.zeros_like(cmax)
    m = cmax
    for r in range(K):
        m = jnp.max(c, axis=0, keepdims=True)
        z = z + jnp.exp(m - cmax)
        if r + 1 < K:
            c = jnp.where(c == m, NEG_BIG, c)
    thr = m
    n = jnp.zeros_like(s1)
    for i in range(K):
        n_i = jnp.sum(jnp.where(cands[i] >= thr, 1.0, 0.0), axis=0, keepdims=True)
        n = jnp.where(rank1 == float(i), n_i, n)
    w = jnp.exp(s1 - v1[0:1]) / z
    e2 = jnp.exp(s2 - v2[0:1])
    return rank2, e2, n, w


def _peer_route_kernel(hT_ref, wqT_ref, keys_ref, rank2_ref, e2_ref, n_ref, w_ref, q_scr, s1_scr, s2_scr):
    h = pl.program_id(1)

    @pl.when(h == 0)
    def _():
        q_scr[...] = jnp.dot(wqT_ref[...], hT_ref[...], preferred_element_type=F32)

    cdt = keys_ref.dtype
    half = PEER_DQ // 2
    row = pl.multiple_of(h * PEER_DQ, PEER_DQ)
    q1 = q_scr[pl.ds(row, half), :].astype(cdt)
    q2 = q_scr[pl.ds(row + half, half), :].astype(cdt)
    s1_scr[...] = jnp.dot(keys_ref[0], q1, preferred_element_type=F32)
    s2_scr[...] = jnp.dot(keys_ref[1], q2, preferred_element_type=F32)

    def chunk(c, carry):
        sl = pl.ds(pl.multiple_of(c * LANES, LANES), LANES)
        rank2, e2, n, w = _route_chunk(s1_scr[:, sl], s2_scr[:, sl])
        rank2_ref[0, :, sl] = rank2
        e2_ref[0, :, sl] = e2
        n_ref[0, :, sl] = n
        w_ref[0, :, sl] = w
        return carry

    lax.fori_loop(0, s1_scr.shape[1] // LANES, chunk, 0)


def _peer_dense_kernel(hT_ref, rank2_ref, e2_ref, n_ref, w_ref, u_ref, vT_ref, o_ref, acc_ref, act_ref):
    j = pl.program_id(1)

    @pl.when(j == 0)
    def _():
        acc_ref[...] = jnp.zeros_like(acc_ref)

    te = u_ref.shape[0]
    groups = te // PEER_NKEYS
    pre = jnp.dot(u_ref[...], hT_ref[...], preferred_element_type=F32)
    for al in range(groups):
        a = j * groups + al
        g = None
        for h in range(PEER_HEADS):
            n_row = n_ref[h, pl.ds(a, 1), :]
            w_row = w_ref[h, pl.ds(a, 1), :]
            t = jnp.where(rank2_ref[h] < n_row, e2_ref[h] * w_row, 0.0)
            g = t if g is None else g + t
        blk = slice(al * PEER_NKEYS, (al + 1) * PEER_NKEYS)
        act_ref[blk, :] = (g * _gelu_tanh(pre[blk, :])).astype(act_ref.dtype)
    acc_ref[...] += jnp.dot(vT_ref[...], act_ref[...], preferred_element_type=F32)

    @pl.when(j == pl.num_programs(1) - 1)
    def _():
        o_ref[...] = acc_ref[...].T


def peer_tokens(h, wq, keys, u_tab, v_tab, cdt=BF16, interpret=False):
    T, D = h.shape
    tm, te = PEER_TM, PEER_TE
    assert T % tm == 0 and PEER_EXPERTS % te == 0
    hT = h.T.astype(cdt)
    wqT = wq.T.astype(cdt)
    route_shape = jax.ShapeDtypeStruct((PEER_HEADS, PEER_NKEYS, T), F32)
    route_spec = pl.BlockSpec((1, PEER_NKEYS, tm), lambda i, hh: (hh, 0, i))
    rank2, e2, n, w = pl.pallas_call(
        _peer_route_kernel,
        grid=(T // tm, PEER_HEADS),
        in_specs=[pl.BlockSpec((D, tm), lambda i, hh: (0, i)),
                  pl.BlockSpec((PEER_HEADS * PEER_DQ, D), lambda i, hh: (0, 0)),
                  pl.BlockSpec((2, PEER_NKEYS, PEER_DQ // 2), lambda i, hh: (0, 0, 0))],
        out_specs=[route_spec] * 4,
        out_shape=[route_shape] * 4,
        scratch_shapes=[pltpu.VMEM((PEER_HEADS * PEER_DQ, tm), F32),
                        pltpu.VMEM((PEER_NKEYS, tm), F32),
                        pltpu.VMEM((PEER_NKEYS, tm), F32)],
        compiler_params=pltpu.CompilerParams(
            dimension_semantics=("arbitrary", "arbitrary"), vmem_limit_bytes=VMEM_LIMIT),
        name="peer_route", interpret=interpret,
    )(hT, wqT, keys.astype(cdt))

    head_spec = pl.BlockSpec((PEER_HEADS, PEER_NKEYS, tm), lambda i, j: (0, 0, i))
    out = pl.pallas_call(
        _peer_dense_kernel,
        grid=(T // tm, PEER_EXPERTS // te),
        in_specs=[pl.BlockSpec((D, tm), lambda i, j: (0, i)),
                  head_spec, head_spec, head_spec, head_spec,
                  pl.BlockSpec((te, D), lambda i, j: (j, 0)),
                  pl.BlockSpec((D, te), lambda i, j: (0, j))],
        out_specs=pl.BlockSpec((tm, D), lambda i, j: (i, 0)),
        out_shape=jax.ShapeDtypeStruct((T, D), F32),
        scratch_shapes=[pltpu.VMEM((D, tm), F32), pltpu.VMEM((te, tm), cdt)],
        compiler_params=pltpu.CompilerParams(
            dimension_semantics=("arbitrary", "arbitrary"), vmem_limit_bytes=VMEM_LIMIT),
        name="peer_dense", interpret=interpret,
    )(hT, rank2, e2, n, w, u_tab.astype(cdt), v_tab.T.astype(cdt))
    return out


def peer(h, wq, keys, u_tab, v_tab):
    B, L, D = h.shape
    return peer_tokens(h.reshape(B * L, D), wq, keys, u_tab, v_tab).reshape(B, L, D)


def _final_norm_kernel(x_ref, g_ref, o_ref):
    xf = x_ref[...]
    y = xf * lax.rsqrt(jnp.mean(xf * xf, axis=-1, keepdims=True) + EPS)
    o_ref[...] = y * g_ref[...]


def final_norm(x, g):
    B, L, D = x.shape
    x2 = x.reshape(B * L, D)
    tm = 1024
    out = pl.pallas_call(
        _final_norm_kernel,
        grid=(B * L // tm,),
        in_specs=[pl.BlockSpec((tm, D), lambda i: (i, 0)),
                  pl.BlockSpec((1, D), lambda i: (0, 0))],
        out_specs=pl.BlockSpec((tm, D), lambda i: (i, 0)),
        out_shape=jax.ShapeDtypeStruct((B * L, D), F32),
        name="final_norm",
    )(x2, g.reshape(1, D))
    return out.reshape(B, L, D)


def kernel(x, c, ctx, c_ctx, w_mod, b_mod, norm1_g, norm2_g, w_in, w_out,
           lru_conv_w, lru_conv_b, lru_wr, lru_br, lru_wi, lru_bi, lru_lambda,
           mla_q_norm_g, mla_wqb, mla_kv_norm_g, mla_wkvb,
           hy_conv_w, hy_conv_b, hy_f_w1, hy_f_b1, hy_f_w2, hy_f_b2, hy_f_w3,
           hy_f_freq, hy_decay, hy_skip,
           peer_wq, peer_keys, peer_u, peer_v, final_g):
    for l in range(DEPTH):
        need_ctx = l < DEPTH - 1
        mod_x = jax.nn.silu(c) @ w_mod[l] + b_mod[l]
        mod_c = jax.nn.silu(c_ctx) @ w_mod[l] + b_mod[l]
        sh1, sc1, g1, sh2, sc2, g2 = jnp.split(mod_x[:, None, :], 6, axis=-1)
        csh1, csc1, cg1, csh2, csc2, cg2 = jnp.split(mod_c, 6, axis=-1)

        hx = rmsnorm(x, norm1_g[l]) * (1.0 + sc1) + sh1
        hc = rmsnorm(ctx, norm1_g[l]) * (1.0 + csc1) + csh1
        yx, yc = token_mixer(hx, hc, need_ctx, w_in[l], w_out[l],
                             lru_conv_w[l], lru_conv_b[l], lru_wr[l], lru_br[l], lru_wi[l], lru_bi[l], lru_lambda[l],
                             mla_q_norm_g[l], mla_wqb[l], mla_kv_norm_g[l], mla_wkvb[l],
                             hy_conv_w[l], hy_conv_b[l], hy_f_w1[l], hy_f_b1[l], hy_f_w2[l], hy_f_b2[l], hy_f_w3[l],
                             hy_f_freq[l], hy_decay[l], hy_skip[l])
        x = x + g1 * yx
        hx = rmsnorm(x, norm2_g[l]) * (1.0 + sc2) + sh2
        x = x + g2 * peer(hx, peer_wq[l], peer_keys[l], peer_u[l], peer_v[l])
        if need_ctx:
            ctx = ctx + cg1 * yc
            hc = rmsnorm(ctx, norm2_g[l]) * (1.0 + csc2) + csh2
            ctx = ctx + cg2 * peer(hc, peer_wq[l], peer_keys[l], peer_u[l], peer_v[l])
    return final_norm(x, final_g)
```

```python
import functools
import math

import jax
import jax.numpy as jnp
import numpy as np
from jax import lax
from jax.experimental import pallas as pl
from jax.experimental.pallas import tpu as pltpu

D_MODEL = 1024
BATCH = 2
SEQ = 8192
DEPTH = 2

GRID_W = 64
CTX_LEN = 256
EPS = 1e-6

D_LRU = 384
LRU_BLOCKS = 6
LRU_BLOCK_DIM = D_LRU // LRU_BLOCKS
LRU_CONV = 4
LRU_C = 8.0

MLA_HEADS = 6
MLA_NOPE = 64
MLA_ROPE = 32
MLA_V = 64
MLA_Q_RANK = 384
MLA_KV_RANK = 256
MLA_SCALE = (MLA_NOPE + MLA_ROPE) ** -0.5
ROPE_PAIRS = MLA_ROPE // 4
ROPE_THETA = 10000.0
ATTN_BLOCK = 128

D_HY = 256
HY_CONV = 3
HY_ORDER = 2
HY_BANDS = 8
HY_EMB = 1 + 2 * HY_BANDS
HY_FFN = 64

D_MIX = D_LRU + MLA_HEADS * MLA_V + D_HY
D_IN = 2 * D_LRU + MLA_Q_RANK + MLA_KV_RANK + MLA_ROPE + 3 * D_HY
IN_SPLITS = [D_LRU, 2 * D_LRU, 2 * D_LRU + MLA_Q_RANK,
             2 * D_LRU + MLA_Q_RANK + MLA_KV_RANK,
             2 * D_LRU + MLA_Q_RANK + MLA_KV_RANK + MLA_ROPE]

PEER_HEADS = 8
PEER_NKEYS = 128
PEER_EXPERTS = PEER_NKEYS * PEER_NKEYS
PEER_DQ = 256
PEER_TOPK = 16
PEER_BLOCK = 128

F32 = jnp.float32


def rmsnorm(x, g):
    xf = x.astype(F32)
    y = xf * lax.rsqrt(jnp.mean(xf * xf, axis=-1, keepdims=True) + EPS)
    return (y * g.astype(F32)).astype(x.dtype)


def depthwise_conv(x, w, b, pad_l, pad_r):
    y = lax.conv_general_dilated(
        x, w[:, None, :].astype(x.dtype), window_strides=(1,),
        padding=[(pad_l, pad_r)], dimension_numbers=('NWC', 'WIO', 'NWC'),
        feature_group_count=x.shape[-1])
    return y + b.astype(x.dtype)


def linear_scan(a, b, h0):
    b = b.at[:, 0].add(a[:, 0] * h0)

    def combine(e1, e2):
        a1, b1 = e1
        a2, b2 = e2
        return a1 * a2, a2 * b1 + b2

    _, h = lax.associative_scan(combine, (a, b), axis=1)
    return h


def rglru_coeffs(x, wr, br, wi, bi, lam):
    xb = x.reshape(x.shape[:-1] + (LRU_BLOCKS, LRU_BLOCK_DIM))
    r = jax.nn.sigmoid(jnp.einsum('blnd,nde->blne', xb, wr.astype(F32)).reshape(x.shape) + br.astype(F32))
    i = jax.nn.sigmoid(jnp.einsum('blnd,nde->blne', xb, wi.astype(F32)).reshape(x.shape) + bi.astype(F32))
    log_a = -LRU_C * r * jax.nn.softplus(-lam.astype(F32))
    a = jnp.exp(log_a)
    b = jnp.sqrt(-jnp.expm1(2.0 * log_a)) * (i * x)
    return a, b


def rglru_bidir(xx, xc, wr, br, wi, bi, lam):
    xx32, xc32 = xx.astype(F32), xc.astype(F32)
    h0 = jnp.zeros((xx.shape[0], D_LRU), F32)
    ys_x, ys_c = [], []
    for d in range(2):
        flip = (lambda t: t[:, ::-1]) if d == 1 else (lambda t: t)
        ac, bc = rglru_coeffs(flip(xc32), wr[d], br[d], wi[d], bi[d], lam[d])
        ax, bx = rglru_coeffs(flip(xx32), wr[d], br[d], wi[d], bi[d], lam[d])
        hc = linear_scan(ac, bc, h0)
        hx = linear_scan(ax, bx, hc[:, -1])
        ys_c.append(flip(hc))
        ys_x.append(flip(hx))
    return (ys_x[0] + ys_x[1]).astype(xx.dtype), (ys_c[0] + ys_c[1]).astype(xc.dtype)


def rope_axis(x, ang):
    n = x.shape[-1] // 2
    x1, x2 = x[..., :n], x[..., n:]
    cos, sin = jnp.cos(ang).astype(x.dtype), jnp.sin(ang).astype(x.dtype)
    return jnp.concatenate([x1 * cos - x2 * sin, x2 * cos + x1 * sin], axis=-1)


def rope_2d(x, ang_row, ang_col):
    h = MLA_ROPE // 2
    return jnp.concatenate([rope_axis(x[..., :h], ang_row), rope_axis(x[..., h:], ang_col)], axis=-1)


def mla_project(cq, ckv, kr, q_g, wqb, kv_g, wkvb, angles):
    B, L = cq.shape[:2]
    q = (rmsnorm(cq, q_g) @ wqb).reshape(B, L, MLA_HEADS, MLA_NOPE + MLA_ROPE)
    kv = (rmsnorm(ckv, kv_g) @ wkvb).reshape(B, L, MLA_HEADS, MLA_NOPE + MLA_V)
    q_nope, q_rope = q[..., :MLA_NOPE], q[..., MLA_NOPE:]
    k_nope, v = kv[..., :MLA_NOPE], kv[..., MLA_NOPE:]
    if angles is not None:
        ang_r, ang_c = angles
        q_rope = rope_2d(q_rope, ang_r[:, None, :], ang_c[:, None, :])
        kr = rope_2d(kr, ang_r, ang_c)
    k_rope = jnp.broadcast_to(kr[:, :, None, :], (B, L, MLA_HEADS, MLA_ROPE))
    q = jnp.concatenate([q_nope, q_rope], axis=-1)
    k = jnp.concatenate([k_nope, k_rope], axis=-1)
    return q, k, v


def attend(q, k, v):
    s = jnp.einsum('bqhd,bkhd->bhqk', q, k, preferred_element_type=F32) * MLA_SCALE
    p = jax.nn.softmax(s, axis=-1).astype(v.dtype)
    return jnp.einsum('bhqk,bkhd->bqhd', p, v)


def hyena_filter_spectra(L, w1, b1, w2, b2, w3, freq, decay):
    t = jnp.arange(L, dtype=F32) / L
    bands = jnp.linspace(1e-4, HY_BANDS - 1, HY_BANDS, dtype=F32)
    wpos = (2.0 * math.pi) * t[:, None] * bands[None, :]
    z = jnp.concatenate([t[:, None], jnp.cos(wpos), jnp.sin(wpos)], axis=-1)
    fr = freq.astype(F32)
    h = jnp.sin(fr * (z @ w1.astype(F32) + b1.astype(F32)))
    h = jnp.sin(fr * (h @ w2.astype(F32) + b2.astype(F32)))
    k = (h @ w3.astype(F32)).reshape(L, 2 * HY_ORDER, D_HY) * jnp.exp(-t[:, None, None] * decay.astype(F32))
    k = k * lax.rsqrt(jnp.sum(k * k, axis=0, keepdims=True) + EPS)
    k = k.reshape(L, HY_ORDER, 2, D_HY)
    kf, kb = k[:, :, 0], k[:, :, 1]
    k_circ = jnp.concatenate([kf, jnp.zeros((1, HY_ORDER, D_HY), F32), kb[:0:-1]], axis=0)
    return jnp.fft.rfft(k_circ, axis=0)


def fft_conv(z, kspec, skip):
    L = z.shape[1]
    z32 = z.astype(F32)
    zf = jnp.fft.rfft(z32, n=2 * L, axis=1)
    y = jnp.fft.irfft(zf * kspec[None], n=2 * L, axis=1)[:, :L]
    return (y + z32 * skip.astype(F32)).astype(z.dtype)


def hyena(u, conv_w, conv_b, w1, b1, w2, b2, w3, freq, decay, skip):
    L = u.shape[1]
    u = depthwise_conv(u, conv_w, conv_b, 1, 1)
    v, x1, x2 = jnp.split(u, 3, axis=-1)
    kspec = hyena_filter_spectra(L, w1, b1, w2, b2, w3, freq, decay)
    y = x1 * fft_conv(v, kspec[:, 0], skip[0])
    y = x2 * fft_conv(y, kspec[:, 1], skip[1])
    return y


BF16 = jnp.bfloat16
LANES = 128
SUBLANES = 8
VMEM_LIMIT = 48 * 1024 * 1024
SEQ_ALL = CTX_LEN + SEQ
LRU_TT = CTX_LEN
LRU_NT = SEQ_ALL // LRU_TT


def _gelu_tanh(x):
    return 0.5 * x * (1.0 + jnp.tanh(0.7978845608028654 * (x + 0.044715 * (x * x * x))))


def _lru_coeff_kernel(u_ref, up_ref, un_ref, cw_ref, cb_ref, wg_ref, bg_ref, c_ref,
                      af_ref, bf_ref, ab_ref, bb_ref):
    i = pl.program_id(1)
    x = u_ref[0]
    tt = x.shape[0]
    use_prev = i >= 2
    use_next = jnp.logical_and(i >= 1, i < LRU_NT - 1)
    p2 = jnp.where(use_prev, up_ref[0][SUBLANES - 2:SUBLANES], 0.0)
    n1 = jnp.where(use_next, un_ref[0][0:1], 0.0)
    ext = jnp.concatenate([p2, x, n1], axis=0)
    xc = cb_ref[...] + sum(cw_ref[k:k + 1, :] * ext[k:k + tt] for k in range(LRU_CONV))
    g = jnp.dot(xc.astype(wg_ref.dtype), wg_ref[...], preferred_element_type=F32) + bg_ref[...]
    for d, (a_ref, b_ref) in enumerate(((af_ref, bf_ref), (ab_ref, bb_ref))):
        r = jax.nn.sigmoid(g[:, (2 * d) * D_LRU:(2 * d + 1) * D_LRU])
        gi = jax.nn.sigmoid(g[:, (2 * d + 1) * D_LRU:(2 * d + 2) * D_LRU])
        log_a = c_ref[d:d + 1, :] * r
        a_ref[0] = jnp.exp(log_a)
        b_ref[0] = jnp.sqrt(1.0 - jnp.exp(2.0 * log_a)) * (gi * xc)


def _lru_scan_rows(a_ref, b_ref, h_scr, emit, reverse):
    tt = a_ref.shape[1]
    nb = a_ref.shape[0]

    def body(s, hs):
        t = (tt - 1 - s) if reverse else s
        out = []
        for b in range(nb):
            h = a_ref[b, pl.ds(t, 1), :] * hs[b] + b_ref[b, pl.ds(t, 1), :]
            emit(b, t, h)
            out.append(h)
        return tuple(out)

    hs = lax.fori_loop(0, tt, body, tuple(h_scr[b:b + 1, :] for b in range(nb)), unroll=8)
    for b in range(nb):
        h_scr[b:b + 1, :] = hs[b]


def _lru_fwd_kernel(a_ref, b_ref, hf_ref, h_scr):
    @pl.when(pl.program_id(0) == 0)
    def _():
        h_scr[...] = jnp.zeros_like(h_scr)

    def emit(b, t, h):
        hf_ref[b, pl.ds(t, 1), :] = h

    _lru_scan_rows(a_ref, b_ref, h_scr, emit, reverse=False)


def _lru_bwd_kernel(a_ref, b_ref, hf_ref, gate_ref, y_ref, h_scr):
    @pl.when(pl.program_id(0) == 0)
    def _():
        h_scr[...] = jnp.zeros_like(h_scr)

    def emit(b, t, h):
        y_ref[b, pl.ds(t, 1), :] = h

    _lru_scan_rows(a_ref, b_ref, h_scr, emit, reverse=True)
    y_ref[...] = (y_ref[...] + hf_ref[...]) * _gelu_tanh(gate_ref[...])


def rglru_mixer(u_lru, u_gate, conv_w, conv_b, wr, br, wi, bi, lam, cdt=BF16, interpret=False):
    B = u_lru.shape[0]
    tt, nt = LRU_TT, LRU_NT
    hb = tt // SUBLANES

    def blockdiag(w):
        eye = jnp.eye(LRU_BLOCKS, dtype=w.dtype)
        return jnp.einsum('nde,nm->ndme', w, eye).reshape(D_LRU, D_LRU)

    wg = jnp.concatenate([blockdiag(wr[0]), blockdiag(wi[0]), blockdiag(wr[1]), blockdiag(wi[1])], axis=1).astype(cdt)
    bg = jnp.concatenate([br[0], bi[0], br[1], bi[1]]).reshape(1, 4 * D_LRU).astype(F32)
    cdec = -LRU_C * jax.nn.softplus(-lam.astype(F32))

    tile = pl.BlockSpec((1, tt, D_LRU), lambda b, i: (b, i, 0))
    full = lambda shape: pl.BlockSpec(shape, lambda b, i: (0,) * len(shape))
    coeff_shape = jax.ShapeDtypeStruct((B, SEQ_ALL, D_LRU), F32)
    a_f, b_f, a_b, b_b = pl.pallas_call(
        _lru_coeff_kernel,
        grid=(B, nt),
        in_specs=[tile,
                  pl.BlockSpec((1, SUBLANES, D_LRU), lambda b, i: (b, jnp.maximum(i * hb - 1, 0), 0)),
                  pl.BlockSpec((1, SUBLANES, D_LRU), lambda b, i: (b, jnp.minimum((i + 1) * hb, nt * hb - 1), 0)),
                  full((LRU_CONV, D_LRU)), full((1, D_LRU)), full((D_LRU, 4 * D_LRU)), full((1, 4 * D_LRU)),
                  full((2, D_LRU))],
        out_specs=[tile] * 4,
        out_shape=[coeff_shape] * 4,
        compiler_params=pltpu.CompilerParams(dimension_semantics=("arbitrary", "arbitrary")),
        name="lru_coeff", interpret=interpret,
    )(u_lru, u_lru, u_lru, conv_w.astype(F32), conv_b.reshape(1, D_LRU).astype(F32), wg, bg, cdec)

    seq = pl.BlockSpec((B, tt, D_LRU), lambda s: (0, s, 0))
    h_f = pl.pallas_call(
        _lru_fwd_kernel,
        grid=(nt,),
        in_specs=[seq, seq],
        out_specs=seq,
        out_shape=coeff_shape,
        scratch_shapes=[pltpu.VMEM((B, D_LRU), F32)],
        compiler_params=pltpu.CompilerParams(dimension_semantics=("arbitrary",)),
        name="lru_fwd", interpret=interpret,
    )(a_f, b_f)

    rseq = pl.BlockSpec((B, tt, D_LRU), lambda s: (0, jnp.where(s == 0, 0, nt - s), 0))
    return pl.pallas_call(
        _lru_bwd_kernel,
        grid=(nt,),
        in_specs=[rseq, rseq, rseq, rseq],
        out_specs=rseq,
        out_shape=coeff_shape,
        scratch_shapes=[pltpu.VMEM((B, D_LRU), F32)],
        compiler_params=pltpu.CompilerParams(dimension_semantics=("arbitrary",)),
        name="lru_bwd", interpret=interpret,
    )(a_b, b_b, h_f, u_gate)


ATT_TQ = 256


ATT_HG = 2


def _flash_kernel(q_ref, k_ref, v_ref, o_ref, *, tk):
    hg, tq = q_ref.shape[1], q_ref.shape[2]
    n_chunks = k_ref.shape[2] // tk
    qs = [q_ref[0, h] for h in range(hg)]

    def body(c, carry):
        off = pl.multiple_of(c * tk, LANES)
        out = []
        for h in range(hg):
            m, l, acc = carry[h]
            s = lax.dot_general(qs[h], k_ref[0, h, pl.ds(off, tk), :], (((1,), (1,)), ((), ())),
                                preferred_element_type=F32)
            m_new = jnp.maximum(m, jnp.max(s, axis=1, keepdims=True))
            alpha = jnp.exp(m - m_new)
            p = jnp.exp(s - m_new)
            l = alpha * l + jnp.sum(p, axis=1, keepdims=True)
            pv = jnp.dot(p.astype(v_ref.dtype), v_ref[0, h, pl.ds(off, tk), :], preferred_element_type=F32)
            out.append((m_new, l, alpha * acc + pv))
        return tuple(out)

    init = tuple((jnp.full((tq, 1), NEG_BIG, F32), jnp.zeros((tq, 1), F32), jnp.zeros((tq, v_ref.shape[3]), F32))
                 for _ in range(hg))
    res = lax.fori_loop(0, n_chunks, body, init, unroll=True)
    o_ref[0] = jnp.concatenate([acc / l for _, l, acc in res], axis=1)


def flash_attention(q, k, v, q_row0, n_q, n_k, tk, interpret=False):
    B, H, R, dk = q.shape
    dv = v.shape[3]
    tq = ATT_TQ
    assert n_q % tq == 0 and q_row0 % tq == 0 and n_k % tk == 0 and tk % LANES == 0 and R % n_k == 0
    q0 = q_row0 // tq
    return pl.pallas_call(
        functools.partial(_flash_kernel, tk=tk),
        grid=(B, H // ATT_HG, n_q // tq),
        in_specs=[pl.BlockSpec((1, ATT_HG, tq, dk), lambda b, g, i: (b, g, i + q0, 0)),
                  pl.BlockSpec((1, ATT_HG, n_k, dk), lambda b, g, i: (b, g, 0, 0)),
                  pl.BlockSpec((1, ATT_HG, n_k, dv), lambda b, g, i: (b, g, 0, 0))],
        out_specs=pl.BlockSpec((1, tq, ATT_HG * dv), lambda b, g, i: (b, i, g)),
        out_shape=jax.ShapeDtypeStruct((B, n_q, H * dv), F32),
        compiler_params=pltpu.CompilerParams(
            dimension_semantics=("arbitrary", "arbitrary", "arbitrary"), vmem_limit_bytes=VMEM_LIMIT),
        name="flash_attention", interpret=interpret,
    )(q, k, v)


HY_CB = 32
HY_FR = 128


def _dft_tables(n1, n_in, n_out, cdt):
    n = n1 * LANES
    k1 = np.arange(n1)[:, None]
    f1 = np.exp(-2j * np.pi * k1 * np.arange(n_in)[None, :] / n1)
    tw = np.exp(-2j * np.pi * k1 * np.arange(LANES)[None, :] / n)
    f2 = np.exp(-2j * np.pi * np.outer(np.arange(LANES), np.arange(LANES)) / LANES)
    g1 = np.conj(np.exp(-2j * np.pi * k1 * np.arange(n_out)[None, :] / n1)).T / n
    blk = lambda m: np.block([[m.real, -m.imag], [m.imag, m.real]])
    return dict(
        ma=jnp.asarray(blk(f1), cdt),
        ma_re=jnp.asarray(np.concatenate([f1.real, f1.imag], axis=0), cdt),
        tr=jnp.asarray(tw.real, F32), ti=jnp.asarray(tw.imag, F32),
        f2=jnp.asarray(np.concatenate([f2.real, f2.imag], axis=1), cdt),
        f2c=jnp.asarray(np.concatenate([f2.real, -f2.imag], axis=1), cdt),
        mi=jnp.asarray(blk(g1), cdt),
    )


def _cmul_right(re, im, f_ref):
    n = re.shape[0]
    p = jnp.dot(jnp.concatenate([re, im], axis=0).astype(f_ref.dtype), f_ref[...], preferred_element_type=F32)
    return p[:n, :LANES] - p[n:, LANES:], p[:n, LANES:] + p[n:, :LANES]


def _fft_filter_kernel(k_ref, ma_ref, tr_ref, ti_ref, f2_ref, xr_ref, xi_ref):
    n1 = tr_ref.shape[0]

    def body(c, carry):
        o = jnp.dot(ma_ref[...], k_ref[c].astype(ma_ref.dtype), preferred_element_type=F32)
        ar, ai = o[:n1], o[n1:]
        xr, xi = _cmul_right(ar * tr_ref[...] - ai * ti_ref[...], ar * ti_ref[...] + ai * tr_ref[...], f2_ref)
        xr_ref[c] = xr
        xi_ref[c] = xi
        return carry

    lax.fori_loop(0, k_ref.shape[0], body, 0)


def _fft_conv_kernel(z_ref, gate_ref, kfr_ref, kfi_ref, kbr_ref, kbi_ref, skip_ref,
                     ma_ref, tr_ref, ti_ref, f2_ref, f2c_ref, mi_ref, o_ref):
    n1 = tr_ref.shape[0]
    n_out = o_ref.shape[2]

    def body(c, carry):
        z0, z1 = z_ref[0, c], z_ref[1, c]
        o = jnp.dot(ma_ref[...], jnp.concatenate([z0, z1], axis=0).astype(ma_ref.dtype), preferred_element_type=F32)
        ar, ai = o[:n1], o[n1:]
        tr, ti = tr_ref[...], ti_ref[...]
        xr, xi = _cmul_right(ar * tr - ai * ti, ar * ti + ai * tr, f2_ref)
        hr = kfr_ref[c] + kbr_ref[c]
        hi = kfi_ref[c] - kbi_ref[c]
        br, bi = _cmul_right(xr * hr - xi * hi, xr * hi + xi * hr, f2c_ref)
        b2 = jnp.concatenate([br * tr + bi * ti, bi * tr - br * ti], axis=0).astype(mi_ref.dtype)
        y = jnp.dot(mi_ref[...], b2, preferred_element_type=F32)
        sk = skip_ref[c]
        o_ref[0, c] = gate_ref[0, c] * (y[:n_out] + sk * z0[:n_out])
        o_ref[1, c] = gate_ref[1, c] * (y[n_out:] + sk * z1[:n_out])
        return carry

    lax.fori_loop(0, z_ref.shape[1], body, 0)


def fft_filter_spectra(k_rows, n1, cdt=BF16, interpret=False):
    R, n_in, _ = k_rows.shape
    t = _dft_tables(n1, n_in, n_in, cdt)
    full = lambda a: pl.BlockSpec(a.shape, lambda i: (0,) * a.ndim)
    spec = jax.ShapeDtypeStruct((R, n1, LANES), F32)
    consts = (t["ma_re"], t["tr"], t["ti"], t["f2"])
    return pl.pallas_call(
        _fft_filter_kernel,
        grid=(R // HY_CB,),
        in_specs=[pl.BlockSpec((HY_CB, n_in, LANES), lambda i: (i, 0, 0))] + [full(a) for a in consts],
        out_specs=[pl.BlockSpec((HY_CB, n1, LANES), lambda i: (i, 0, 0))] * 2,
        out_shape=[spec, spec],
        compiler_params=pltpu.CompilerParams(dimension_semantics=("arbitrary",)),
        name="fft_filter", interpret=interpret,
    )(k_rows, *consts)


def fft_long_conv(z, gate, kf, kb, skip, n1, cdt=BF16, interpret=False):
    B, C, n_in, _ = z.shape
    assert B == 2 and C % HY_CB == 0
    t = _dft_tables(n1, n_in, n_in, cdt)
    full = lambda a: pl.BlockSpec(a.shape, lambda i: (0,) * a.ndim)
    sig = pl.BlockSpec((B, HY_CB, n_in, LANES), lambda i: (0, i, 0, 0))
    spc = pl.BlockSpec((HY_CB, n1, LANES), lambda i: (i, 0, 0))
    consts = (t["ma"], t["tr"], t["ti"], t["f2"], t["f2c"], t["mi"])
    skip_b = jnp.broadcast_to(skip.astype(F32)[:, None, None], (C, 1, LANES))
    return pl.pallas_call(
        _fft_conv_kernel,
        grid=(C // HY_CB,),
        in_specs=[sig, sig, spc, spc, spc, spc, pl.BlockSpec((HY_CB, 1, LANES), lambda i: (i, 0, 0))]
                 + [full(a) for a in consts],
        out_specs=sig,
        out_shape=jax.ShapeDtypeStruct(z.shape, F32),
        compiler_params=pltpu.CompilerParams(dimension_semantics=("arbitrary",)),
        name="fft_conv", interpret=interpret,
    )(z, gate, kf[0], kf[1], kb[0], kb[1], skip_b, *consts)


def _hy_filter_kernel(z_ref, w1_ref, b1_ref, w2_ref, b2_ref, fr_ref, w3_ref, dec_ref, k_ref, h_scr):
    hp = lax.Precision.HIGHEST

    @pl.when(pl.program_id(0) == 0)
    def _():
        h = jnp.sin(fr_ref[...] * (jnp.dot(w1_ref[...], z_ref[...], precision=hp, preferred_element_type=F32) + b1_ref[...]))
        h_scr[...] = jnp.sin(fr_ref[...] * (jnp.dot(w2_ref[...], h, precision=hp, preferred_element_type=F32) + b2_ref[...]))

    t = z_ref[0:1, :]
    k = jnp.dot(w3_ref[...], h_scr[...], precision=hp, preferred_element_type=F32) * jnp.exp(-t * dec_ref[...])
    k_ref[...] = k * lax.rsqrt(jnp.sum(k * k, axis=1, keepdims=True) + EPS)


def hyena_filters(L, w1, b1, w2, b2, w3, freq, decay, interpret=False):
    t = jnp.arange(L, dtype=F32) / L
    bands = jnp.linspace(1e-4, HY_BANDS - 1, HY_BANDS, dtype=F32)
    wpos = (2.0 * math.pi) * t[None, :] * bands[:, None]
    emb = 24
    z = jnp.concatenate([t[None, :], jnp.cos(wpos), jnp.sin(wpos), jnp.zeros((emb - HY_EMB, L), F32)], axis=0)
    w1p = jnp.concatenate([w1.T, jnp.zeros((HY_FFN, emb - HY_EMB), F32)], axis=1)
    R = 2 * HY_ORDER * D_HY
    col = lambda v: v.reshape(-1, 1).astype(F32)
    full = lambda shape: pl.BlockSpec(shape, lambda i: (0,) * len(shape))
    return pl.pallas_call(
        _hy_filter_kernel,
        grid=(R // HY_FR,),
        in_specs=[full((emb, L)), full((HY_FFN, emb)), full((HY_FFN, 1)), full((HY_FFN, HY_FFN)), full((HY_FFN, 1)),
                  full((HY_FFN, 1)), pl.BlockSpec((HY_FR, HY_FFN), lambda i: (i, 0)),
                  pl.BlockSpec((HY_FR, 1), lambda i: (i, 0))],
        out_specs=pl.BlockSpec((HY_FR, L), lambda i: (i, 0)),
        out_shape=jax.ShapeDtypeStruct((R, L), F32),
        scratch_shapes=[pltpu.VMEM((HY_FFN, L), F32)],
        compiler_params=pltpu.CompilerParams(dimension_semantics=("arbitrary",)),
        name="hy_filter", interpret=interpret,
    )(z, w1p, col(b1), w2.T.astype(F32), col(b2), col(freq), w3.T.astype(F32), col(decay))


def _hy_shortconv_kernel(u_ref, up_ref, un_ref, w_ref, b_ref, o_ref):
    i = pl.program_id(1)
    n = pl.num_programs(1)
    x = u_ref[0]
    tt = x.shape[0]
    p1 = jnp.where(i > 0, up_ref[0][SUBLANES - 1:SUBLANES], 0.0)
    n1 = jnp.where(i < n - 1, un_ref[0][0:1], 0.0)
    ext = jnp.concatenate([p1, x, n1], axis=0)
    o_ref[0] = b_ref[...] + sum(w_ref[k:k + 1, :] * ext[k:k + tt] for k in range(HY_CONV))


def hyena_shortconv(u, w, b, interpret=False):
    B, L, C = u.shape
    tt = min(256, L)
    hb = tt // SUBLANES
    nt = L // tt
    tile = pl.BlockSpec((1, tt, C), lambda bb, i: (bb, i, 0))
    return pl.pallas_call(
        _hy_shortconv_kernel,
        grid=(B, nt),
        in_specs=[tile,
                  pl.BlockSpec((1, SUBLANES, C), lambda bb, i: (bb, jnp.maximum(i * hb - 1, 0), 0)),
                  pl.BlockSpec((1, SUBLANES, C), lambda bb, i: (bb, jnp.minimum((i + 1) * hb, nt * hb - 1), 0)),
                  pl.BlockSpec((HY_CONV, C), lambda bb, i: (0, 0)), pl.BlockSpec((1, C), lambda bb, i: (0, 0))],
        out_specs=tile,
        out_shape=jax.ShapeDtypeStruct(u.shape, F32),
        compiler_params=pltpu.CompilerParams(dimension_semantics=("arbitrary", "arbitrary")),
        name="hy_shortconv", interpret=interpret,
    )(u, u, u, w.astype(F32), b.reshape(1, C).astype(F32))


def hyena_pallas(u, conv_w, conv_b, w1, b1, w2, b2, w3, freq, decay, skip, cdt=BF16, interpret=False):
    B, L, _ = u.shape
    rows = L // LANES
    n1 = max(2 * rows, SUBLANES)
    n_in = n1 // 2 if n1 == 2 * rows else n1
    uc = hyena_shortconv(u, conv_w, conv_b, interpret=interpret)
    ucm = uc.transpose(0, 2, 1).reshape(B, 3 * D_HY, rows, LANES)
    if n_in != rows:
        ucm = jnp.pad(ucm, ((0, 0), (0, 0), (0, n_in - rows), (0, 0)))
    v, x1, x2 = ucm[:, :D_HY], ucm[:, D_HY:2 * D_HY], ucm[:, 2 * D_HY:]

    k = hyena_filters(L, w1, b1, w2, b2, w3, freq, decay, interpret=interpret)
    k = k.reshape(HY_ORDER, 2, D_HY, L)
    k = k.at[:, 1, :, 0].set(0.0)
    k = k.reshape(2 * HY_ORDER * D_HY, rows, LANES)
    if n_in != rows:
        k = jnp.pad(k, ((0, 0), (0, n_in - rows), (0, 0)))
    sr, si = fft_filter_spectra(k, n1, cdt=cdt, interpret=interpret)
    sr = sr.reshape(HY_ORDER, 2, D_HY, n1, LANES)
    si = si.reshape(HY_ORDER, 2, D_HY, n1, LANES)

    y = v
    for o, gate in enumerate((x1, x2)):
        y = fft_long_conv(y, gate, (sr[o, 0], si[o, 0]), (sr[o, 1], si[o, 1]), skip[o], n1, cdt=cdt, interpret=interpret)
    return y[:, :, :rows].reshape(B, D_HY, L).transpose(0, 2, 1)


ROW_T = CTX_LEN
ROW_NT = SEQ_ALL // ROW_T
MOD_ROWS = SUBLANES
MLA_DK = LANES
ROPE_SWAP = np.arange(MLA_ROPE).reshape(2, 2, MLA_ROPE // 4)[:, ::-1].reshape(-1)


def _rms(x):
    return x * lax.rsqrt(jnp.mean(x * x, axis=-1, keepdims=True) + EPS)


def _mod_kernel(c_ref, w_ref, b_ref, o_ref):
    c = c_ref[...]
    s = (c * jax.nn.sigmoid(c)).astype(BF16)
    o_ref[...] = jnp.dot(s, w_ref[...].astype(BF16), preferred_element_type=F32) + b_ref[...]


def adaln_modulation(c, c_ctx, w_mod, b_mod, interpret=False):
    B, D = c.shape
    cs = jnp.concatenate([c, c_ctx[None], jnp.zeros((MOD_ROWS - B - 1, D), F32)], axis=0)
    out = pl.pallas_call(
        _mod_kernel,
        grid=(6,),
        in_specs=[pl.BlockSpec((MOD_ROWS, D), lambda j: (0, 0)),
                  pl.BlockSpec((D, D), lambda j: (0, j)),
                  pl.BlockSpec((1, D), lambda j: (0, j))],
        out_specs=pl.BlockSpec((MOD_ROWS, D), lambda j: (0, j)),
        out_shape=jax.ShapeDtypeStruct((MOD_ROWS, 6 * D), F32),
        compiler_params=pltpu.CompilerParams(dimension_semantics=("arbitrary",)),
        name="adaln_mod", interpret=interpret,
    )(cs, w_mod, b_mod.reshape(1, 6 * D))
    mod = out.reshape(MOD_ROWS, 6, D)
    return jnp.stack([jnp.broadcast_to(mod[B][None], (B, 6, D)), mod[:B]], axis=1)


IN_COLS = (D_LRU, D_LRU, MLA_Q_RANK, MLA_KV_RANK, 3 * D_HY, 2 * MLA_ROPE)


def _in_proj_kernel(s_ref, m_ref, g_ref, w_ref, *out_refs):
    m = m_ref[0, 0]
    h = _rms(s_ref[0]) * g_ref[...] * (1.0 + m[1:2]) + m[0:1]
    u = jnp.dot(h.astype(w_ref.dtype), w_ref[...], preferred_element_type=F32)
    off = 0
    for ref, n in zip(out_refs, IN_COLS):
        ref[0] = u[:, off:off + n]
        off += n


def in_projection(s, mod, norm_g, w_in, cdt=BF16, interpret=False):
    B, R, D = s.shape
    o = (0,) + tuple(IN_SPLITS) + (D_IN,)
    kr = w_in[:, o[4]:o[5]]
    w = jnp.concatenate([w_in[:, :o[4]], w_in[:, o[5]:], kr, kr[:, ROPE_SWAP]], axis=1).astype(cdt)
    return pl.pallas_call(
        _in_proj_kernel,
        grid=(B, R // ROW_T),
        in_specs=[pl.BlockSpec((1, ROW_T, D), lambda b, i: (b, i, 0)),
                  pl.BlockSpec((1, 1, 6, D), lambda b, i: (b, jnp.minimum(i, 1), 0, 0)),
                  pl.BlockSpec((1, D), lambda b, i: (0, 0)),
                  pl.BlockSpec(w.shape, lambda b, i: (0, 0))],
        out_specs=[pl.BlockSpec((1, ROW_T, n), lambda b, i: (b, i, 0)) for n in IN_COLS],
        out_shape=[jax.ShapeDtypeStruct((B, R, n), F32) for n in IN_COLS],
        compiler_params=pltpu.CompilerParams(
            dimension_semantics=("arbitrary", "arbitrary"), vmem_limit_bytes=VMEM_LIMIT),
        name="in_proj", interpret=interpret,
    )(s, mod, norm_g.reshape(1, D), w)


def _mla_proj_kernel(cq_ref, ckv_ref, kr_ref, rc_ref, rs_ref, qg_ref, wq_ref, wqs_ref, kg_ref, wkv_ref,
                     q_ref, k_ref, v_ref):
    cdt = wq_ref.dtype
    cqn = (_rms(cq_ref[0]) * qg_ref[...]).astype(cdt)
    qa = jnp.dot(cqn, wq_ref[...], preferred_element_type=F32)
    qs = jnp.dot(cqn, wqs_ref[...], preferred_element_type=F32)
    ckvn = (_rms(ckv_ref[0]) * kg_ref[...]).astype(cdt)
    kv = jnp.dot(ckvn, wkv_ref[...], preferred_element_type=F32)
    rc, rs = rc_ref[...], rs_ref[...]
    kr = kr_ref[0]
    k_rope = (kr[:, :MLA_ROPE] * rc[:, MLA_NOPE:MLA_NOPE + MLA_ROPE]
              + kr[:, MLA_ROPE:] * rs[:, MLA_NOPE:MLA_NOPE + MLA_ROPE])
    pad = jnp.zeros((kr.shape[0], MLA_DK - MLA_NOPE - MLA_ROPE), F32)
    for h in range(MLA_HEADS):
        sl = slice(h * MLA_DK, (h + 1) * MLA_DK)
        q_ref[0, h] = ((qa[:, sl] * rc + qs[:, sl] * rs) * MLA_SCALE).astype(q_ref.dtype)
        kvh = kv[:, h * (MLA_NOPE + MLA_V):(h + 1) * (MLA_NOPE + MLA_V)]
        k_ref[0, h] = jnp.concatenate([kvh[:, :MLA_NOPE], k_rope, pad], axis=1).astype(k_ref.dtype)
        v_ref[0, h] = kvh[:, MLA_NOPE:].astype(v_ref.dtype)


def _rope_tables():
    grid_rows = SEQ // GRID_W
    row = jnp.repeat(jnp.arange(grid_rows, dtype=F32), GRID_W)
    col = jnp.tile(jnp.arange(GRID_W, dtype=F32), grid_rows)
    inv_freq = ROPE_THETA ** (-jnp.arange(ROPE_PAIRS, dtype=F32) / ROPE_PAIRS)
    ang_r, ang_c = row[:, None] * inv_freq, col[:, None] * inv_freq
    cos = jnp.concatenate([jnp.cos(ang_r), jnp.cos(ang_r), jnp.cos(ang_c), jnp.cos(ang_c)], axis=1)
    sin = jnp.concatenate([-jnp.sin(ang_r), jnp.sin(ang_r), -jnp.sin(ang_c), jnp.sin(ang_c)], axis=1)
    npad = MLA_DK - MLA_NOPE - MLA_ROPE
    cos = jnp.concatenate([jnp.ones((SEQ, MLA_NOPE), F32), cos, jnp.zeros((SEQ, npad), F32)], axis=1)
    sin = jnp.concatenate([jnp.zeros((SEQ, MLA_NOPE), F32), sin, jnp.zeros((SEQ, npad), F32)], axis=1)
    ctx_cos = jnp.concatenate([jnp.ones((CTX_LEN, MLA_NOPE + MLA_ROPE), F32), jnp.zeros((CTX_LEN, npad), F32)], axis=1)
    return (jnp.concatenate([ctx_cos, cos], axis=0),
            jnp.concatenate([jnp.zeros((CTX_LEN, MLA_DK), F32), sin], axis=0))


def mla_projection(cq, ckv, kr2, q_g, wqb, kv_g, wkvb, cdt=BF16, interpret=False):
    B, R, _ = cq.shape
    dq = MLA_NOPE + MLA_ROPE
    wq3 = wqb.reshape(MLA_Q_RANK, MLA_HEADS, dq)
    zq = jnp.zeros((MLA_Q_RANK, MLA_HEADS, MLA_DK - dq), wqb.dtype)
    wq = jnp.concatenate([wq3, zq], axis=2).reshape(MLA_Q_RANK, MLA_HEADS * MLA_DK).astype(cdt)
    wqs = jnp.concatenate([jnp.zeros((MLA_Q_RANK, MLA_HEADS, MLA_NOPE), wqb.dtype),
                           wq3[:, :, MLA_NOPE:][:, :, ROPE_SWAP], zq], axis=2)
    wqs = wqs.reshape(MLA_Q_RANK, MLA_HEADS * MLA_DK).astype(cdt)
    rc, rs = _rope_tables()
    row = lambda n: pl.BlockSpec((1, ROW_T, n), lambda b, i: (b, i, 0))
    tab = pl.BlockSpec((ROW_T, MLA_DK), lambda b, i: (i, 0))
    full = lambda shape: pl.BlockSpec(shape, lambda b, i: (0,) * len(shape))
    head = lambda n: pl.BlockSpec((1, MLA_HEADS, ROW_T, n), lambda b, i: (b, 0, i, 0))
    return pl.pallas_call(
        _mla_proj_kernel,
        grid=(B, R // ROW_T),
        in_specs=[row(MLA_Q_RANK), row(MLA_KV_RANK), row(2 * MLA_ROPE), tab, tab,
                  full((1, MLA_Q_RANK)), full(wq.shape), full(wqs.shape),
                  full((1, MLA_KV_RANK)), full(wkvb.shape)],
        out_specs=[head(MLA_DK), head(MLA_DK), head(MLA_V)],
        out_shape=[jax.ShapeDtypeStruct((B, MLA_HEADS, R, MLA_DK), cdt),
                   jax.ShapeDtypeStruct((B, MLA_HEADS, R, MLA_DK), cdt),
                   jax.ShapeDtypeStruct((B, MLA_HEADS, R, MLA_V), cdt)],
        compiler_params=pltpu.CompilerParams(dimension_semantics=("arbitrary", "arbitrary")),
        name="mla_proj", interpret=interpret,
    )(cq, ckv, kr2, rc, rs, q_g.reshape(1, -1), wq, wqs, kv_g.reshape(1, -1), wkvb.astype(cdt))


def _out_proj_kernel(s_ref, m_ref, g_ref, ya_ref, ybx_ref, ybc_ref, yc_ref, w_ref, snew_ref, hT_ref):
    cdt = w_ref.dtype
    yb = jnp.where(pl.program_id(1) == 0, ybc_ref[0], ybx_ref[0])
    mix = (jnp.dot(ya_ref[0].astype(cdt), w_ref[0:D_LRU, :], preferred_element_type=F32)
           + jnp.dot(yb.astype(cdt), w_ref[D_LRU:D_LRU + MLA_HEADS * MLA_V, :], preferred_element_type=F32)
           + jnp.dot(yc_ref[0].astype(cdt), w_ref[D_LRU + MLA_HEADS * MLA_V:, :], preferred_element_type=F32))
    m = m_ref[0, 0]
    s_new = s_ref[0] + m[2:3] * mix
    snew_ref[0] = s_new
    h2 = _rms(s_new) * g_ref[...] * (1.0 + m[4:5]) + m[3:4]
    hT_ref[...] = h2.T.astype(hT_ref.dtype)


def out_projection(s, mod, norm2_g, ya, yb_x, yb_c, yc, w_out, cdt=BF16, interpret=False):
    B, R, D = s.shape
    nt = R // ROW_T
    row = lambda n: pl.BlockSpec((1, ROW_T, n), lambda b, i: (b, i, 0))
    return pl.pallas_call(
        _out_proj_kernel,
        grid=(B, nt),
        in_specs=[row(D),
                  pl.BlockSpec((1, 1, 6, D), lambda b, i: (b, jnp.minimum(i, 1), 0, 0)),
                  pl.BlockSpec((1, D), lambda b, i: (0, 0)),
                  row(D_LRU),
                  pl.BlockSpec((1, ROW_T, MLA_HEADS * MLA_V), lambda b, i: (b, jnp.maximum(i - 1, 0), 0)),
                  pl.BlockSpec((1, ROW_T, MLA_HEADS * MLA_V), lambda b, i: (b, 0, 0)),
                  row(D_HY),
                  pl.BlockSpec((D_MIX, D), lambda b, i: (0, 0))],
        out_specs=[row(D), pl.BlockSpec((D, ROW_T), lambda b, i: (0, b * nt + i))],
        out_shape=[jax.ShapeDtypeStruct((B, R, D), F32), jax.ShapeDtypeStruct((D, B * R), cdt)],
        compiler_params=pltpu.CompilerParams(
            dimension_semantics=("arbitrary", "arbitrary"), vmem_limit_bytes=VMEM_LIMIT),
        name="out_proj", interpret=interpret,
    )(s, mod, norm2_g.reshape(1, D), ya, yb_x, yb_c, yc, w_out.astype(cdt))


PEER_TM = 512
PEER_TE = 512
NEG_BIG = -3.0e38


def _topk_desc(s, k):
    rank = jnp.full(s.shape, float(PEER_NKEYS), F32)
    vals = []
    for r in range(k):
        m = jnp.max(s, axis=0, keepdims=True)
        hit = s == m
        rank = jnp.where(hit, float(r), rank)
        s = jnp.where(hit, NEG_BIG, s)
        vals.append(m)
    return jnp.concatenate(vals, axis=0), rank


def _route_chunk(s1, s2):
    K = PEER_TOPK
    v1, rank1 = _topk_desc(s1, K)
    v2, rank2 = _topk_desc(s2, K)
    cands = [v1[i:i + 1] + v2 for i in range(K)]
    cmax = v1[0:1] + v2[0:1]
    c = jnp.concatenate(cands, axis=0)
    z = jnp.zeros_like(cmax)
    m = cmax
    for r in range(K):
        m = jnp.max(c, axis=0, keepdims=True)
        z = z + jnp.exp(m - cmax)
        if r + 1 < K:
            c = jnp.where(c == m, NEG_BIG, c)
    thr = m
    n = jnp.zeros_like(s1)
    for i in range(K):
        n_i = jnp.sum(jnp.where(cands[i] >= thr, 1.0, 0.0), axis=0, keepdims=True)
        n = jnp.where(rank1 == float(i), n_i, n)
    w = jnp.exp(s1 - v1[0:1]) / z
    e2 = jnp.exp(s2 - v2[0:1])
    return rank2, e2, n, w


def _peer_route_kernel(hT_ref, wqT_ref, keys_ref, rank2_ref, e2_ref, n_ref, w_ref, q_scr, s1_scr, s2_scr):
    h = pl.program_id(1)

    @pl.when(h == 0)
    def _():
        q_scr[...] = jnp.dot(wqT_ref[...], hT_ref[...], preferred_element_type=F32)

    cdt = keys_ref.dtype
    half = PEER_DQ // 2
    row = pl.multiple_of(h * PEER_DQ, PEER_DQ)
    q1 = q_scr[pl.ds(row, half), :].astype(cdt)
    q2 = q_scr[pl.ds(row + half, half), :].astype(cdt)
    s1_scr[...] = jnp.dot(keys_ref[0], q1, preferred_element_type=F32)
    s2_scr[...] = jnp.dot(keys_ref[1], q2, preferred_element_type=F32)

    def chunk(c, carry):
        sl = pl.ds(pl.multiple_of(c * LANES, LANES), LANES)
        rank2, e2, n, w = _route_chunk(s1_scr[:, sl], s2_scr[:, sl])
        rank2_ref[0, :, sl] = rank2
        e2_ref[0, :, sl] = e2
        n_ref[0, :, sl] = n
        w_ref[0, :, sl] = w
        return carry

    lax.fori_loop(0, s1_scr.shape[1] // LANES, chunk, 0)


def _peer_dense_kernel(hT_ref, rank2_ref, e2_ref, n_ref, w_ref, u_ref, vT_ref, s_ref, g_ref, o_ref, acc_ref, act_ref):
    j = pl.program_id(1)

    @pl.when(j == 0)
    def _():
        acc_ref[...] = jnp.zeros_like(acc_ref)

    te = u_ref.shape[0]
    groups = te // PEER_NKEYS
    pre = jnp.dot(u_ref[...], hT_ref[...], preferred_element_type=F32)
    for al in range(groups):
        a = j * groups + al
        g = None
        for h in range(PEER_HEADS):
            n_row = n_ref[h, pl.ds(a, 1), :]
            w_row = w_ref[h, pl.ds(a, 1), :]
            t = jnp.where(rank2_ref[h] < n_row, e2_ref[h] * w_row, 0.0)
            g = t if g is None else g + t
        blk = slice(al * PEER_NKEYS, (al + 1) * PEER_NKEYS)
        act_ref[blk, :] = (g * _gelu_tanh(pre[blk, :])).astype(act_ref.dtype)
    acc_ref[...] += jnp.dot(vT_ref[...], act_ref[...], preferred_element_type=F32)

    @pl.when(j == pl.num_programs(1) - 1)
    def _():
        y = acc_ref[...].T
        for r in range(g_ref.shape[0]):
            rows = slice(r * ROW_T, (r + 1) * ROW_T)
            o_ref[rows, :] = s_ref[rows, :] + g_ref[r] * y[rows, :]


def peer_tokens(hT, s, gate, wq, keys, u_tab, v_tab, interpret=False):
    D, T = hT.shape
    cdt = hT.dtype
    tm, te = PEER_TM, PEER_TE
    assert T % tm == 0 and PEER_EXPERTS % te == 0 and tm % ROW_T == 0
    wqT = wq.T.astype(cdt)
    route_shape = jax.ShapeDtypeStruct((PEER_HEADS, PEER_NKEYS, T), F32)
    route_spec = pl.BlockSpec((1, PEER_NKEYS, tm), lambda i, hh: (hh, 0, i))
    rank2, e2, n, w = pl.pallas_call(
        _peer_route_kernel,
        grid=(T // tm, PEER_HEADS),
        in_specs=[pl.BlockSpec((D, tm), lambda i, hh: (0, i)),
                  pl.BlockSpec((PEER_HEADS * PEER_DQ, D), lambda i, hh: (0, 0)),
                  pl.BlockSpec((2, PEER_NKEYS, PEER_DQ // 2), lambda i, hh: (0, 0, 0))],
        out_specs=[route_spec] * 4,
        out_shape=[route_shape] * 4,
        scratch_shapes=[pltpu.VMEM((PEER_HEADS * PEER_DQ, tm), F32),
                        pltpu.VMEM((PEER_NKEYS, tm), F32),
                        pltpu.VMEM((PEER_NKEYS, tm), F32)],
        compiler_params=pltpu.CompilerParams(
            dimension_semantics=("arbitrary", "arbitrary"), vmem_limit_bytes=VMEM_LIMIT),
        name="peer_route", interpret=interpret,
    )(hT, wqT, keys.astype(cdt))

    head_spec = pl.BlockSpec((PEER_HEADS, PEER_NKEYS, tm), lambda i, j: (0, 0, i))
    return pl.pallas_call(
        _peer_dense_kernel,
        grid=(T // tm, PEER_EXPERTS // te),
        in_specs=[pl.BlockSpec((D, tm), lambda i, j: (0, i)),
                  head_spec, head_spec, head_spec, head_spec,
                  pl.BlockSpec((te, D), lambda i, j: (j, 0)),
                  pl.BlockSpec((D, te), lambda i, j: (0, j)),
                  pl.BlockSpec((tm, D), lambda i, j: (i, 0)),
                  pl.BlockSpec((tm // ROW_T, 1, D), lambda i, j: (i, 0, 0))],
        out_specs=pl.BlockSpec((tm, D), lambda i, j: (i, 0)),
        out_shape=jax.ShapeDtypeStruct((T, D), F32),
        scratch_shapes=[pltpu.VMEM((D, tm), F32), pltpu.VMEM((te, tm), cdt)],
        compiler_params=pltpu.CompilerParams(
            dimension_semantics=("arbitrary", "arbitrary"), vmem_limit_bytes=VMEM_LIMIT),
        name="peer_dense", interpret=interpret,
    )(hT, rank2, e2, n, w, u_tab.astype(cdt), v_tab.T.astype(cdt), s, gate)


def _final_norm_kernel(x_ref, g_ref, o_ref):
    o_ref[0] = _rms(x_ref[0]) * g_ref[...]


def final_norm(s, g):
    B, R, D = s.shape
    return pl.pallas_call(
        _final_norm_kernel,
        grid=(B, SEQ // ROW_T),
        in_specs=[pl.BlockSpec((1, ROW_T, D), lambda b, i: (b, i + CTX_LEN // ROW_T, 0)),
                  pl.BlockSpec((1, D), lambda b, i: (0, 0))],
        out_specs=pl.BlockSpec((1, ROW_T, D), lambda b, i: (b, i, 0)),
        out_shape=jax.ShapeDtypeStruct((B, SEQ, D), F32),
        compiler_params=pltpu.CompilerParams(dimension_semantics=("arbitrary", "arbitrary")),
        name="final_norm",
    )(s, g.reshape(1, D))


def kernel(x, c, ctx, c_ctx, w_mod, b_mod, norm1_g, norm2_g, w_in, w_out,
           lru_conv_w, lru_conv_b, lru_wr, lru_br, lru_wi, lru_bi, lru_lambda,
           mla_q_norm_g, mla_wqb, mla_kv_norm_g, mla_wkvb,
           hy_conv_w, hy_conv_b, hy_f_w1, hy_f_b1, hy_f_w2, hy_f_b2, hy_f_w3,
           hy_f_freq, hy_decay, hy_skip,
           peer_wq, peer_keys, peer_u, peer_v, final_g):
    B = x.shape[0]
    s = jnp.concatenate([ctx, x], axis=1)
    for l in range(DEPTH):
        need_ctx = l < DEPTH - 1
        mod = adaln_modulation(c, c_ctx, w_mod[l], b_mod[l])
        u_lru, u_gate, cq, ckv, u_hy, kr2 = in_projection(s, mod, norm1_g[l], w_in[l])

        ya = rglru_mixer(u_lru, u_gate, lru_conv_w[l], lru_conv_b[l], lru_wr[l], lru_br[l], lru_wi[l], lru_bi[l],
                         lru_lambda[l])

        q, k, v = mla_projection(cq, ckv, kr2, mla_q_norm_g[l], mla_wqb[l], mla_kv_norm_g[l], mla_wkvb[l])
        yb_x = flash_attention(q, k, v, q_row0=CTX_LEN, n_q=SEQ, n_k=SEQ_ALL, tk=SEQ_ALL // 6)
        if need_ctx:
            yb_c = flash_attention(q, k, v, q_row0=0, n_q=CTX_LEN, n_k=CTX_LEN, tk=CTX_LEN)
        else:
            yb_c = jnp.zeros((B, CTX_LEN, MLA_HEADS * MLA_V), F32)

        hy_args = (hy_conv_w[l], hy_conv_b[l], hy_f_w1[l], hy_f_b1[l], hy_f_w2[l], hy_f_b2[l], hy_f_w3[l],
                   hy_f_freq[l], hy_decay[l], hy_skip[l])
        yc_x = hyena_pallas(u_hy[:, CTX_LEN:], *hy_args)
        yc_c = hyena_pallas(u_hy[:, :CTX_LEN], *hy_args) if need_ctx else jnp.zeros((B, CTX_LEN, D_HY), F32)
        yc = jnp.concatenate([yc_c, yc_x], axis=1)

        s, hT = out_projection(s, mod, norm2_g[l], ya, yb_x, yb_c, yc, w_out[l])
        gate2 = jnp.repeat(mod[:, :, 5], jnp.array([1, ROW_NT - 1]), axis=1, total_repeat_length=ROW_NT)
        s = peer_tokens(hT, s.reshape(B * SEQ_ALL, D_MODEL), gate2.reshape(B * ROW_NT, 1, D_MODEL),
                        peer_wq[l], peer_keys[l], peer_u[l], peer_v[l]).reshape(B, SEQ_ALL, D_MODEL)
    return final_norm(s, final_g)
```

```python
import functools
import math

import jax
import jax.numpy as jnp
import numpy as np
from jax import lax
from jax.experimental import pallas as pl
from jax.experimental.pallas import tpu as pltpu

D_MODEL = 1024
BATCH = 2
SEQ = 8192
DEPTH = 2

GRID_W = 64
CTX_LEN = 256
EPS = 1e-6

D_LRU = 384
LRU_BLOCKS = 6
LRU_BLOCK_DIM = D_LRU // LRU_BLOCKS
LRU_CONV = 4
LRU_C = 8.0

MLA_HEADS = 6
MLA_NOPE = 64
MLA_ROPE = 32
MLA_V = 64
MLA_Q_RANK = 384
MLA_KV_RANK = 256
MLA_SCALE = (MLA_NOPE + MLA_ROPE) ** -0.5
ROPE_PAIRS = MLA_ROPE // 4
ROPE_THETA = 10000.0
ATTN_BLOCK = 128

D_HY = 256
HY_CONV = 3
HY_ORDER = 2
HY_BANDS = 8
HY_EMB = 1 + 2 * HY_BANDS
HY_FFN = 64

D_MIX = D_LRU + MLA_HEADS * MLA_V + D_HY
D_IN = 2 * D_LRU + MLA_Q_RANK + MLA_KV_RANK + MLA_ROPE + 3 * D_HY
IN_SPLITS = [D_LRU, 2 * D_LRU, 2 * D_LRU + MLA_Q_RANK,
             2 * D_LRU + MLA_Q_RANK + MLA_KV_RANK,
             2 * D_LRU + MLA_Q_RANK + MLA_KV_RANK + MLA_ROPE]

PEER_HEADS = 8
PEER_NKEYS = 128
PEER_EXPERTS = PEER_NKEYS * PEER_NKEYS
PEER_DQ = 256
PEER_TOPK = 16
PEER_BLOCK = 128

F32 = jnp.float32


def rmsnorm(x, g):
    xf = x.astype(F32)
    y = xf * lax.rsqrt(jnp.mean(xf * xf, axis=-1, keepdims=True) + EPS)
    return (y * g.astype(F32)).astype(x.dtype)


def depthwise_conv(x, w, b, pad_l, pad_r):
    y = lax.conv_general_dilated(
        x, w[:, None, :].astype(x.dtype), window_strides=(1,),
        padding=[(pad_l, pad_r)], dimension_numbers=('NWC', 'WIO', 'NWC'),
        feature_group_count=x.shape[-1])
    return y + b.astype(x.dtype)


def linear_scan(a, b, h0):
    b = b.at[:, 0].add(a[:, 0] * h0)

    def combine(e1, e2):
        a1, b1 = e1
        a2, b2 = e2
        return a1 * a2, a2 * b1 + b2

    _, h = lax.associative_scan(combine, (a, b), axis=1)
    return h


def rglru_coeffs(x, wr, br, wi, bi, lam):
    xb = x.reshape(x.shape[:-1] + (LRU_BLOCKS, LRU_BLOCK_DIM))
    r = jax.nn.sigmoid(jnp.einsum('blnd,nde->blne', xb, wr.astype(F32)).reshape(x.shape) + br.astype(F32))
    i = jax.nn.sigmoid(jnp.einsum('blnd,nde->blne', xb, wi.astype(F32)).reshape(x.shape) + bi.astype(F32))
    log_a = -LRU_C * r * jax.nn.softplus(-lam.astype(F32))
    a = jnp.exp(log_a)
    b = jnp.sqrt(-jnp.expm1(2.0 * log_a)) * (i * x)
    return a, b


def rglru_bidir(xx, xc, wr, br, wi, bi, lam):
    xx32, xc32 = xx.astype(F32), xc.astype(F32)
    h0 = jnp.zeros((xx.shape[0], D_LRU), F32)
    ys_x, ys_c = [], []
    for d in range(2):
        flip = (lambda t: t[:, ::-1]) if d == 1 else (lambda t: t)
        ac, bc = rglru_coeffs(flip(xc32), wr[d], br[d], wi[d], bi[d], lam[d])
        ax, bx = rglru_coeffs(flip(xx32), wr[d], br[d], wi[d], bi[d], lam[d])
        hc = linear_scan(ac, bc, h0)
        hx = linear_scan(ax, bx, hc[:, -1])
        ys_c.append(flip(hc))
        ys_x.append(flip(hx))
    return (ys_x[0] + ys_x[1]).astype(xx.dtype), (ys_c[0] + ys_c[1]).astype(xc.dtype)


def rope_axis(x, ang):
    n = x.shape[-1] // 2
    x1, x2 = x[..., :n], x[..., n:]
    cos, sin = jnp.cos(ang).astype(x.dtype), jnp.sin(ang).astype(x.dtype)
    return jnp.concatenate([x1 * cos - x2 * sin, x2 * cos + x1 * sin], axis=-1)


def rope_2d(x, ang_row, ang_col):
    h = MLA_ROPE // 2
    return jnp.concatenate([rope_axis(x[..., :h], ang_row), rope_axis(x[..., h:], ang_col)], axis=-1)


def mla_project(cq, ckv, kr, q_g, wqb, kv_g, wkvb, angles):
    B, L = cq.shape[:2]
    q = (rmsnorm(cq, q_g) @ wqb).reshape(B, L, MLA_HEADS, MLA_NOPE + MLA_ROPE)
    kv = (rmsnorm(ckv, kv_g) @ wkvb).reshape(B, L, MLA_HEADS, MLA_NOPE + MLA_V)
    q_nope, q_rope = q[..., :MLA_NOPE], q[..., MLA_NOPE:]
    k_nope, v = kv[..., :MLA_NOPE], kv[..., MLA_NOPE:]
    if angles is not None:
        ang_r, ang_c = angles
        q_rope = rope_2d(q_rope, ang_r[:, None, :], ang_c[:, None, :])
        kr = rope_2d(kr, ang_r, ang_c)
    k_rope = jnp.broadcast_to(kr[:, :, None, :], (B, L, MLA_HEADS, MLA_ROPE))
    q = jnp.concatenate([q_nope, q_rope], axis=-1)
    k = jnp.concatenate([k_nope, k_rope], axis=-1)
    return q, k, v


def attend(q, k, v):
    s = jnp.einsum('bqhd,bkhd->bhqk', q, k, preferred_element_type=F32) * MLA_SCALE
    p = jax.nn.softmax(s, axis=-1).astype(v.dtype)
    return jnp.einsum('bhqk,bkhd->bqhd', p, v)


def hyena_filter_spectra(L, w1, b1, w2, b2, w3, freq, decay):
    t = jnp.arange(L, dtype=F32) / L
    bands = jnp.linspace(1e-4, HY_BANDS - 1, HY_BANDS, dtype=F32)
    wpos = (2.0 * math.pi) * t[:, None] * bands[None, :]
    z = jnp.concatenate([t[:, None], jnp.cos(wpos), jnp.sin(wpos)], axis=-1)
    fr = freq.astype(F32)
    h = jnp.sin(fr * (z @ w1.astype(F32) + b1.astype(F32)))
    h = jnp.sin(fr * (h @ w2.astype(F32) + b2.astype(F32)))
    k = (h @ w3.astype(F32)).reshape(L, 2 * HY_ORDER, D_HY) * jnp.exp(-t[:, None, None] * decay.astype(F32))
    k = k * lax.rsqrt(jnp.sum(k * k, axis=0, keepdims=True) + EPS)
    k = k.reshape(L, HY_ORDER, 2, D_HY)
    kf, kb = k[:, :, 0], k[:, :, 1]
    k_circ = jnp.concatenate([kf, jnp.zeros((1, HY_ORDER, D_HY), F32), kb[:0:-1]], axis=0)
    return jnp.fft.rfft(k_circ, axis=0)


def fft_conv(z, kspec, skip):
    L = z.shape[1]
    z32 = z.astype(F32)
    zf = jnp.fft.rfft(z32, n=2 * L, axis=1)
    y = jnp.fft.irfft(zf * kspec[None], n=2 * L, axis=1)[:, :L]
    return (y + z32 * skip.astype(F32)).astype(z.dtype)


def hyena(u, conv_w, conv_b, w1, b1, w2, b2, w3, freq, decay, skip):
    L = u.shape[1]
    u = depthwise_conv(u, conv_w, conv_b, 1, 1)
    v, x1, x2 = jnp.split(u, 3, axis=-1)
    kspec = hyena_filter_spectra(L, w1, b1, w2, b2, w3, freq, decay)
    y = x1 * fft_conv(v, kspec[:, 0], skip[0])
    y = x2 * fft_conv(y, kspec[:, 1], skip[1])
    return y


BF16 = jnp.bfloat16
LANES = 128
SUBLANES = 8
VMEM_LIMIT = 48 * 1024 * 1024
SEQ_ALL = CTX_LEN + SEQ
LRU_TT = CTX_LEN
LRU_NT = SEQ_ALL // LRU_TT


def _gelu_tanh(x):
    return 0.5 * x * (1.0 + jnp.tanh(0.7978845608028654 * (x + 0.044715 * (x * x * x))))


def _lru_coeff_kernel(u_ref, up_ref, un_ref, cw_ref, cb_ref, wg_ref, bg_ref, c_ref,
                      af_ref, bf_ref, ab_ref, bb_ref):
    i = pl.program_id(1)
    x = u_ref[0]
    tt = x.shape[0]
    use_prev = i >= 2
    use_next = jnp.logical_and(i >= 1, i < LRU_NT - 1)
    p2 = jnp.where(use_prev, up_ref[0][SUBLANES - 2:SUBLANES], 0.0)
    n1 = jnp.where(use_next, un_ref[0][0:1], 0.0)
    ext = jnp.concatenate([p2, x, n1], axis=0)
    xc = cb_ref[...] + sum(cw_ref[k:k + 1, :] * ext[k:k + tt] for k in range(LRU_CONV))
    g = jnp.dot(xc.astype(wg_ref.dtype), wg_ref[...], preferred_element_type=F32) + bg_ref[...]
    for d, (a_ref, b_ref) in enumerate(((af_ref, bf_ref), (ab_ref, bb_ref))):
        r = jax.nn.sigmoid(g[:, (2 * d) * D_LRU:(2 * d + 1) * D_LRU])
        gi = jax.nn.sigmoid(g[:, (2 * d + 1) * D_LRU:(2 * d + 2) * D_LRU])
        log_a = c_ref[d:d + 1, :] * r
        a_ref[0] = jnp.exp(log_a)
        b_ref[0] = jnp.sqrt(1.0 - jnp.exp(2.0 * log_a)) * (gi * xc)


def _lru_scan_rows(a_ref, b_ref, h_scr, emit, reverse):
    tt = a_ref.shape[1]
    nb = a_ref.shape[0]

    def body(s, hs):
        t = (tt - 1 - s) if reverse else s
        out = []
        for b in range(nb):
            h = a_ref[b, pl.ds(t, 1), :] * hs[b] + b_ref[b, pl.ds(t, 1), :]
            emit(b, t, h)
            out.append(h)
        return tuple(out)

    hs = lax.fori_loop(0, tt, body, tuple(h_scr[b:b + 1, :] for b in range(nb)), unroll=8)
    for b in range(nb):
        h_scr[b:b + 1, :] = hs[b]


def _lru_fwd_kernel(a_ref, b_ref, hf_ref, h_scr):
    @pl.when(pl.program_id(0) == 0)
    def _():
        h_scr[...] = jnp.zeros_like(h_scr)

    def emit(b, t, h):
        hf_ref[b, pl.ds(t, 1), :] = h

    _lru_scan_rows(a_ref, b_ref, h_scr, emit, reverse=False)


def _lru_bwd_kernel(a_ref, b_ref, hf_ref, gate_ref, y_ref, h_scr):
    @pl.when(pl.program_id(0) == 0)
    def _():
        h_scr[...] = jnp.zeros_like(h_scr)

    def emit(b, t, h):
        y_ref[b, pl.ds(t, 1), :] = h

    _lru_scan_rows(a_ref, b_ref, h_scr, emit, reverse=True)
    y_ref[...] = (y_ref[...] + hf_ref[...]) * _gelu_tanh(gate_ref[...])


def rglru_mixer(u_lru, u_gate, conv_w, conv_b, wr, br, wi, bi, lam, cdt=BF16, interpret=False):
    B = u_lru.shape[0]
    tt, nt = LRU_TT, LRU_NT
    hb = tt // SUBLANES

    def blockdiag(w):
        eye = jnp.eye(LRU_BLOCKS, dtype=w.dtype)
        return jnp.einsum('nde,nm->ndme', w, eye).reshape(D_LRU, D_LRU)

    wg = jnp.concatenate([blockdiag(wr[0]), blockdiag(wi[0]), blockdiag(wr[1]), blockdiag(wi[1])], axis=1).astype(cdt)
    bg = jnp.concatenate([br[0], bi[0], br[1], bi[1]]).reshape(1, 4 * D_LRU).astype(F32)
    cdec = -LRU_C * jax.nn.softplus(-lam.astype(F32))

    tile = pl.BlockSpec((1, tt, D_LRU), lambda b, i: (b, i, 0))
    full = lambda shape: pl.BlockSpec(shape, lambda b, i: (0,) * len(shape))
    coeff_shape = jax.ShapeDtypeStruct((B, SEQ_ALL, D_LRU), F32)
    a_f, b_f, a_b, b_b = pl.pallas_call(
        _lru_coeff_kernel,
        grid=(B, nt),
        in_specs=[tile,
                  pl.BlockSpec((1, SUBLANES, D_LRU), lambda b, i: (b, jnp.maximum(i * hb - 1, 0), 0)),
                  pl.BlockSpec((1, SUBLANES, D_LRU), lambda b, i: (b, jnp.minimum((i + 1) * hb, nt * hb - 1), 0)),
                  full((LRU_CONV, D_LRU)), full((1, D_LRU)), full((D_LRU, 4 * D_LRU)), full((1, 4 * D_LRU)),
                  full((2, D_LRU))],
        out_specs=[tile] * 4,
        out_shape=[coeff_shape] * 4,
        compiler_params=pltpu.CompilerParams(dimension_semantics=("arbitrary", "arbitrary")),
        name="lru_coeff", interpret=interpret,
    )(u_lru, u_lru, u_lru, conv_w.astype(F32), conv_b.reshape(1, D_LRU).astype(F32), wg, bg, cdec)

    seq = pl.BlockSpec((B, tt, D_LRU), lambda s: (0, s, 0))
    h_f = pl.pallas_call(
        _lru_fwd_kernel,
        grid=(nt,),
        in_specs=[seq, seq],
        out_specs=seq,
        out_shape=coeff_shape,
        scratch_shapes=[pltpu.VMEM((B, D_LRU), F32)],
        compiler_params=pltpu.CompilerParams(dimension_semantics=("arbitrary",)),
        name="lru_fwd", interpret=interpret,
    )(a_f, b_f)

    rseq = pl.BlockSpec((B, tt, D_LRU), lambda s: (0, jnp.where(s == 0, 0, nt - s), 0))
    return pl.pallas_call(
        _lru_bwd_kernel,
        grid=(nt,),
        in_specs=[rseq, rseq, rseq, rseq],
        out_specs=rseq,
        out_shape=coeff_shape,
        scratch_shapes=[pltpu.VMEM((B, D_LRU), F32)],
        compiler_params=pltpu.CompilerParams(dimension_semantics=("arbitrary",)),
        name="lru_bwd", interpret=interpret,
    )(a_b, b_b, h_f, u_gate)


ATT_TQ = 256


ATT_HG = 2


def _flash_kernel(q_ref, k_ref, v_ref, o_ref, *, tk):
    hg, tq = q_ref.shape[1], q_ref.shape[2]
    n_chunks = k_ref.shape[2] // tk
    qs = [q_ref[0, h] for h in range(hg)]

    def body(c, carry):
        off = pl.multiple_of(c * tk, LANES)
        out = []
        for h in range(hg):
            m, l, acc = carry[h]
            s = lax.dot_general(qs[h], k_ref[0, h, pl.ds(off, tk), :], (((1,), (1,)), ((), ())),
                                preferred_element_type=F32)
            m_new = jnp.maximum(m, jnp.max(s, axis=1, keepdims=True))
            alpha = jnp.exp(m - m_new)
            p = jnp.exp(s - m_new)
            l = alpha * l + jnp.sum(p, axis=1, keepdims=True)
            pv = jnp.dot(p.astype(v_ref.dtype), v_ref[0, h, pl.ds(off, tk), :], preferred_element_type=F32)
            out.append((m_new, l, alpha * acc + pv))
        return tuple(out)

    init = tuple((jnp.full((tq, 1), NEG_BIG, F32), jnp.zeros((tq, 1), F32), jnp.zeros((tq, v_ref.shape[3]), F32))
                 for _ in range(hg))
    res = lax.fori_loop(0, n_chunks, body, init, unroll=True)
    o_ref[0] = jnp.concatenate([acc / l for _, l, acc in res], axis=1)


def flash_attention(q, k, v, q_row0, n_q, n_k, tk, interpret=False):
    B, H, R, dk = q.shape
    dv = v.shape[3]
    tq = ATT_TQ
    assert n_q % tq == 0 and q_row0 % tq == 0 and n_k % tk == 0 and tk % LANES == 0 and R % n_k == 0
    q0 = q_row0 // tq
    return pl.pallas_call(
        functools.partial(_flash_kernel, tk=tk),
        grid=(B, H // ATT_HG, n_q // tq),
        in_specs=[pl.BlockSpec((1, ATT_HG, tq, dk), lambda b, g, i: (b, g, i + q0, 0)),
                  pl.BlockSpec((1, ATT_HG, n_k, dk), lambda b, g, i: (b, g, 0, 0)),
                  pl.BlockSpec((1, ATT_HG, n_k, dv), lambda b, g, i: (b, g, 0, 0))],
        out_specs=pl.BlockSpec((1, tq, ATT_HG * dv), lambda b, g, i: (b, i, g)),
        out_shape=jax.ShapeDtypeStruct((B, n_q, H * dv), F32),
        compiler_params=pltpu.CompilerParams(
            dimension_semantics=("arbitrary", "arbitrary", "arbitrary"), vmem_limit_bytes=VMEM_LIMIT),
        name="flash_attention", interpret=interpret,
    )(q, k, v)


HY_CB = 32
HY_FR = 128


def _dft_tables(n1, n_in, n_out, cdt):
    n = n1 * LANES
    k1 = np.arange(n1)[:, None]
    f1 = np.exp(-2j * np.pi * k1 * np.arange(n_in)[None, :] / n1)
    tw = np.exp(-2j * np.pi * k1 * np.arange(LANES)[None, :] / n)
    f2 = np.exp(-2j * np.pi * np.outer(np.arange(LANES), np.arange(LANES)) / LANES)
    g1 = np.conj(np.exp(-2j * np.pi * k1 * np.arange(n_out)[None, :] / n1)).T / n
    blk = lambda m: np.block([[m.real, -m.imag], [m.imag, m.real]])
    return dict(
        ma=jnp.asarray(blk(f1), cdt),
        ma_re=jnp.asarray(np.concatenate([f1.real, f1.imag], axis=0), cdt),
        tr=jnp.asarray(tw.real, F32), ti=jnp.asarray(tw.imag, F32),
        f2=jnp.asarray(np.concatenate([f2.real, f2.imag], axis=1), cdt),
        f2c=jnp.asarray(np.concatenate([f2.real, -f2.imag], axis=1), cdt),
        mi=jnp.asarray(blk(g1), cdt),
    )


def _cmul_right(re, im, f_ref):
    n = re.shape[0]
    p = jnp.dot(jnp.concatenate([re, im], axis=0).astype(f_ref.dtype), f_ref[...], preferred_element_type=F32)
    return p[:n, :LANES] - p[n:, LANES:], p[:n, LANES:] + p[n:, :LANES]


def _fft_filter_kernel(k_ref, ma_ref, tr_ref, ti_ref, f2_ref, xr_ref, xi_ref):
    n1 = tr_ref.shape[0]

    def body(c, carry):
        o = jnp.dot(ma_ref[...], k_ref[c].astype(ma_ref.dtype), preferred_element_type=F32)
        ar, ai = o[:n1], o[n1:]
        xr, xi = _cmul_right(ar * tr_ref[...] - ai * ti_ref[...], ar * ti_ref[...] + ai * tr_ref[...], f2_ref)
        xr_ref[c] = xr
        xi_ref[c] = xi
        return carry

    lax.fori_loop(0, k_ref.shape[0], body, 0)


def _fft_conv_kernel(z_ref, gate_ref, kfr_ref, kfi_ref, kbr_ref, kbi_ref, skip_ref,
                     ma_ref, tr_ref, ti_ref, f2_ref, f2c_ref, mi_ref, o_ref):
    n1 = tr_ref.shape[0]
    n_out = o_ref.shape[2]

    def body(c, carry):
        z0, z1 = z_ref[0, c], z_ref[1, c]
        o = jnp.dot(ma_ref[...], jnp.concatenate([z0, z1], axis=0).astype(ma_ref.dtype), preferred_element_type=F32)
        ar, ai = o[:n1], o[n1:]
        tr, ti = tr_ref[...], ti_ref[...]
        xr, xi = _cmul_right(ar * tr - ai * ti, ar * ti + ai * tr, f2_ref)
        hr = kfr_ref[c] + kbr_ref[c]
        hi = kfi_ref[c] - kbi_ref[c]
        br, bi = _cmul_right(xr * hr - xi * hi, xr * hi + xi * hr, f2c_ref)
        b2 = jnp.concatenate([br * tr + bi * ti, bi * tr - br * ti], axis=0).astype(mi_ref.dtype)
        y = jnp.dot(mi_ref[...], b2, preferred_element_type=F32)
        sk = skip_ref[c]
        o_ref[0, c] = gate_ref[0, c] * (y[:n_out] + sk * z0[:n_out])
        o_ref[1, c] = gate_ref[1, c] * (y[n_out:] + sk * z1[:n_out])
        return carry

    lax.fori_loop(0, z_ref.shape[1], body, 0)


def fft_filter_spectra(k_rows, n1, cdt=BF16, interpret=False):
    R, n_in, _ = k_rows.shape
    t = _dft_tables(n1, n_in, n_in, cdt)
    full = lambda a: pl.BlockSpec(a.shape, lambda i: (0,) * a.ndim)
    spec = jax.ShapeDtypeStruct((R, n1, LANES), F32)
    consts = (t["ma_re"], t["tr"], t["ti"], t["f2"])
    return pl.pallas_call(
        _fft_filter_kernel,
        grid=(R // HY_CB,),
        in_specs=[pl.BlockSpec((HY_CB, n_in, LANES), lambda i: (i, 0, 0))] + [full(a) for a in consts],
        out_specs=[pl.BlockSpec((HY_CB, n1, LANES), lambda i: (i, 0, 0))] * 2,
        out_shape=[spec, spec],
        compiler_params=pltpu.CompilerParams(dimension_semantics=("arbitrary",)),
        name="fft_filter", interpret=interpret,
    )(k_rows, *consts)


def fft_long_conv(z, gate, kf, kb, skip, n1, cdt=BF16, interpret=False):
    B, C, n_in, _ = z.shape
    assert B == 2 and C % HY_CB == 0
    t = _dft_tables(n1, n_in, n_in, cdt)
    full = lambda a: pl.BlockSpec(a.shape, lambda i: (0,) * a.ndim)
    sig = pl.BlockSpec((B, HY_CB, n_in, LANES), lambda i: (0, i, 0, 0))
    spc = pl.BlockSpec((HY_CB, n1, LANES), lambda i: (i, 0, 0))
    consts = (t["ma"], t["tr"], t["ti"], t["f2"], t["f2c"], t["mi"])
    skip_b = jnp.broadcast_to(skip.astype(F32)[:, None, None], (C, 1, LANES))
    return pl.pallas_call(
        _fft_conv_kernel,
        grid=(C // HY_CB,),
        in_specs=[sig, sig, spc, spc, spc, spc, pl.BlockSpec((HY_CB, 1, LANES), lambda i: (i, 0, 0))]
                 + [full(a) for a in consts],
        out_specs=sig,
        out_shape=jax.ShapeDtypeStruct(z.shape, F32),
        compiler_params=pltpu.CompilerParams(dimension_semantics=("arbitrary",)),
        name="fft_conv", interpret=interpret,
    )(z, gate, kf[0], kf[1], kb[0], kb[1], skip_b, *consts)


def _hy_filter_kernel(z_ref, w1_ref, b1_ref, w2_ref, b2_ref, fr_ref, w3_ref, dec_ref, k_ref, h_scr):
    hp = lax.Precision.HIGHEST

    @pl.when(pl.program_id(0) == 0)
    def _():
        h = jnp.sin(fr_ref[...] * (jnp.dot(w1_ref[...], z_ref[...], precision=hp, preferred_element_type=F32) + b1_ref[...]))
        h_scr[...] = jnp.sin(fr_ref[...] * (jnp.dot(w2_ref[...], h, precision=hp, preferred_element_type=F32) + b2_ref[...]))

    t = z_ref[0:1, :]
    k = jnp.dot(w3_ref[...], h_scr[...], precision=hp, preferred_element_type=F32) * jnp.exp(-t * dec_ref[...])
    k_ref[...] = k * lax.rsqrt(jnp.sum(k * k, axis=1, keepdims=True) + EPS)


def hyena_filters(L, w1, b1, w2, b2, w3, freq, decay, interpret=False):
    t = jnp.arange(L, dtype=F32) / L
    bands = jnp.linspace(1e-4, HY_BANDS - 1, HY_BANDS, dtype=F32)
    wpos = (2.0 * math.pi) * t[None, :] * bands[:, None]
    emb = 24
    z = jnp.concatenate([t[None, :], jnp.cos(wpos), jnp.sin(wpos), jnp.zeros((emb - HY_EMB, L), F32)], axis=0)
    w1p = jnp.concatenate([w1.T, jnp.zeros((HY_FFN, emb - HY_EMB), F32)], axis=1)
    R = 2 * HY_ORDER * D_HY
    col = lambda v: v.reshape(-1, 1).astype(F32)
    full = lambda shape: pl.BlockSpec(shape, lambda i: (0,) * len(shape))
    return pl.pallas_call(
        _hy_filter_kernel,
        grid=(R // HY_FR,),
        in_specs=[full((emb, L)), full((HY_FFN, emb)), full((HY_FFN, 1)), full((HY_FFN, HY_FFN)), full((HY_FFN, 1)),
                  full((HY_FFN, 1)), pl.BlockSpec((HY_FR, HY_FFN), lambda i: (i, 0)),
                  pl.BlockSpec((HY_FR, 1), lambda i: (i, 0))],
        out_specs=pl.BlockSpec((HY_FR, L), lambda i: (i, 0)),
        out_shape=jax.ShapeDtypeStruct((R, L), F32),
        scratch_shapes=[pltpu.VMEM((HY_FFN, L), F32)],
        compiler_params=pltpu.CompilerParams(dimension_semantics=("arbitrary",)),
        name="hy_filter", interpret=interpret,
    )(z, w1p, col(b1), w2.T.astype(F32), col(b2), col(freq), w3.T.astype(F32), col(decay))


def _hy_shortconv_kernel(u_ref, up_ref, un_ref, w_ref, b_ref, o_ref):
    i = pl.program_id(1)
    n = pl.num_programs(1)
    x = u_ref[0]
    tt = x.shape[0]
    p1 = jnp.where(i > 0, up_ref[0][SUBLANES - 1:SUBLANES], 0.0)
    n1 = jnp.where(i < n - 1, un_ref[0][0:1], 0.0)
    ext = jnp.concatenate([p1, x, n1], axis=0)
    o_ref[0] = b_ref[...] + sum(w_ref[k:k + 1, :] * ext[k:k + tt] for k in range(HY_CONV))


def hyena_shortconv(u, w, b, interpret=False):
    B, L, C = u.shape
    tt = min(256, L)
    hb = tt // SUBLANES
    nt = L // tt
    tile = pl.BlockSpec((1, tt, C), lambda bb, i: (bb, i, 0))
    return pl.pallas_call(
        _hy_shortconv_kernel,
        grid=(B, nt),
        in_specs=[tile,
                  pl.BlockSpec((1, SUBLANES, C), lambda bb, i: (bb, jnp.maximum(i * hb - 1, 0), 0)),
                  pl.BlockSpec((1, SUBLANES, C), lambda bb, i: (bb, jnp.minimum((i + 1) * hb, nt * hb - 1), 0)),
                  pl.BlockSpec((HY_CONV, C), lambda bb, i: (0, 0)), pl.BlockSpec((1, C), lambda bb, i: (0, 0))],
        out_specs=tile,
        out_shape=jax.ShapeDtypeStruct(u.shape, F32),
        compiler_params=pltpu.CompilerParams(dimension_semantics=("arbitrary", "arbitrary")),
        name="hy_shortconv", interpret=interpret,
    )(u, u, u, w.astype(F32), b.reshape(1, C).astype(F32))


def hyena_pallas(u, conv_w, conv_b, w1, b1, w2, b2, w3, freq, decay, skip, cdt=BF16, interpret=False):
    B, L, _ = u.shape
    rows = L // LANES
    n1 = max(2 * rows, SUBLANES)
    n_in = n1 // 2 if n1 == 2 * rows else n1
    uc = hyena_shortconv(u, conv_w, conv_b, interpret=interpret)
    ucm = uc.transpose(0, 2, 1).reshape(B, 3 * D_HY, rows, LANES)
    if n_in != rows:
        ucm = jnp.pad(ucm, ((0, 0), (0, 0), (0, n_in - rows), (0, 0)))
    v, x1, x2 = ucm[:, :D_HY], ucm[:, D_HY:2 * D_HY], ucm[:, 2 * D_HY:]

    k = hyena_filters(L, w1, b1, w2, b2, w3, freq, decay, interpret=interpret)
    k = k.reshape(HY_ORDER, 2, D_HY, L)
    k = k.at[:, 1, :, 0].set(0.0)
    k = k.reshape(2 * HY_ORDER * D_HY, rows, LANES)
    if n_in != rows:
        k = jnp.pad(k, ((0, 0), (0, n_in - rows), (0, 0)))
    sr, si = fft_filter_spectra(k, n1, cdt=cdt, interpret=interpret)
    sr = sr.reshape(HY_ORDER, 2, D_HY, n1, LANES)
    si = si.reshape(HY_ORDER, 2, D_HY, n1, LANES)

    y = v
    for o, gate in enumerate((x1, x2)):
        y = fft_long_conv(y, gate, (sr[o, 0], si[o, 0]), (sr[o, 1], si[o, 1]), skip[o], n1, cdt=cdt, interpret=interpret)
    return y[:, :, :rows].reshape(B, D_HY, L).transpose(0, 2, 1)


ROW_T = CTX_LEN
ROW_NT = SEQ_ALL // ROW_T
MOD_ROWS = SUBLANES
MLA_DK = LANES
ROPE_SWAP = np.arange(MLA_ROPE).reshape(2, 2, MLA_ROPE // 4)[:, ::-1].reshape(-1)


def _rms(x):
    return x * lax.rsqrt(jnp.mean(x * x, axis=-1, keepdims=True) + EPS)


def _mod_kernel(c_ref, w_ref, b_ref, o_ref):
    c = c_ref[...]
    s = (c * jax.nn.sigmoid(c)).astype(BF16)
    o_ref[...] = jnp.dot(s, w_ref[...].astype(BF16), preferred_element_type=F32) + b_ref[...]


def adaln_modulation(c, c_ctx, w_mod, b_mod, interpret=False):
    B, D = c.shape
    cs = jnp.concatenate([c, c_ctx[None], jnp.zeros((MOD_ROWS - B - 1, D), F32)], axis=0)
    out = pl.pallas_call(
        _mod_kernel,
        grid=(6,),
        in_specs=[pl.BlockSpec((MOD_ROWS, D), lambda j: (0, 0)),
                  pl.BlockSpec((D, D), lambda j: (0, j)),
                  pl.BlockSpec((1, D), lambda j: (0, j))],
        out_specs=pl.BlockSpec((MOD_ROWS, D), lambda j: (0, j)),
        out_shape=jax.ShapeDtypeStruct((MOD_ROWS, 6 * D), F32),
        compiler_params=pltpu.CompilerParams(dimension_semantics=("arbitrary",)),
        name="adaln_mod", interpret=interpret,
    )(cs, w_mod, b_mod.reshape(1, 6 * D))
    mod = out.reshape(MOD_ROWS, 6, D)
    return jnp.stack([jnp.broadcast_to(mod[B][None], (B, 6, D)), mod[:B]], axis=1)


IN_COLS = (D_LRU, D_LRU, MLA_Q_RANK, MLA_KV_RANK, 3 * D_HY, 2 * MLA_ROPE)


def _in_proj_kernel(s_ref, m_ref, g_ref, w_ref, *out_refs):
    m = m_ref[0, 0]
    h = _rms(s_ref[0]) * g_ref[...] * (1.0 + m[1:2]) + m[0:1]
    u = jnp.dot(h.astype(w_ref.dtype), w_ref[...], preferred_element_type=F32)
    off = 0
    for ref, n in zip(out_refs, IN_COLS):
        ref[0] = u[:, off:off + n]
        off += n


def in_projection(s, mod, norm_g, w_in, cdt=BF16, interpret=False):
    B, R, D = s.shape
    o = (0,) + tuple(IN_SPLITS) + (D_IN,)
    kr = w_in[:, o[4]:o[5]]
    w = jnp.concatenate([w_in[:, :o[4]], w_in[:, o[5]:], kr, kr[:, ROPE_SWAP]], axis=1).astype(cdt)
    return pl.pallas_call(
        _in_proj_kernel,
        grid=(B, R // ROW_T),
        in_specs=[pl.BlockSpec((1, ROW_T, D), lambda b, i: (b, i, 0)),
                  pl.BlockSpec((1, 1, 6, D), lambda b, i: (b, jnp.minimum(i, 1), 0, 0)),
                  pl.BlockSpec((1, D), lambda b, i: (0, 0)),
                  pl.BlockSpec(w.shape, lambda b, i: (0, 0))],
        out_specs=[pl.BlockSpec((1, ROW_T, n), lambda b, i: (b, i, 0)) for n in IN_COLS],
        out_shape=[jax.ShapeDtypeStruct((B, R, n), F32) for n in IN_COLS],
        compiler_params=pltpu.CompilerParams(
            dimension_semantics=("arbitrary", "arbitrary"), vmem_limit_bytes=VMEM_LIMIT),
        name="in_proj", interpret=interpret,
    )(s, mod, norm_g.reshape(1, D), w)


def _mla_proj_kernel(cq_ref, ckv_ref, kr_ref, rc_ref, rs_ref, qg_ref, wq_ref, wqs_ref, kg_ref, wkv_ref,
                     q_ref, k_ref, v_ref):
    cdt = wq_ref.dtype
    cqn = (_rms(cq_ref[0]) * qg_ref[...]).astype(cdt)
    qa = jnp.dot(cqn, wq_ref[...], preferred_element_type=F32)
    qs = jnp.dot(cqn, wqs_ref[...], preferred_element_type=F32)
    ckvn = (_rms(ckv_ref[0]) * kg_ref[...]).astype(cdt)
    kv = jnp.dot(ckvn, wkv_ref[...], preferred_element_type=F32)
    rc, rs = rc_ref[...], rs_ref[...]
    kr = kr_ref[0]
    k_rope = (kr[:, :MLA_ROPE] * rc[:, MLA_NOPE:MLA_NOPE + MLA_ROPE]
              + kr[:, MLA_ROPE:] * rs[:, MLA_NOPE:MLA_NOPE + MLA_ROPE])
    pad = jnp.zeros((kr.shape[0], MLA_DK - MLA_NOPE - MLA_ROPE), F32)
    for h in range(MLA_HEADS):
        sl = slice(h * MLA_DK, (h + 1) * MLA_DK)
        q_ref[0, h] = ((qa[:, sl] * rc + qs[:, sl] * rs) * MLA_SCALE).astype(q_ref.dtype)
        kvh = kv[:, h * (MLA_NOPE + MLA_V):(h + 1) * (MLA_NOPE + MLA_V)]
        k_ref[0, h] = jnp.concatenate([kvh[:, :MLA_NOPE], k_rope, pad], axis=1).astype(k_ref.dtype)
        v_ref[0, h] = kvh[:, MLA_NOPE:].astype(v_ref.dtype)


def _rope_tables():
    grid_rows = SEQ // GRID_W
    row = jnp.repeat(jnp.arange(grid_rows, dtype=F32), GRID_W)
    col = jnp.tile(jnp.arange(GRID_W, dtype=F32), grid_rows)
    inv_freq = ROPE_THETA ** (-jnp.arange(ROPE_PAIRS, dtype=F32) / ROPE_PAIRS)
    ang_r, ang_c = row[:, None] * inv_freq, col[:, None] * inv_freq
    cos = jnp.concatenate([jnp.cos(ang_r), jnp.cos(ang_r), jnp.cos(ang_c), jnp.cos(ang_c)], axis=1)
    sin = jnp.concatenate([-jnp.sin(ang_r), jnp.sin(ang_r), -jnp.sin(ang_c), jnp.sin(ang_c)], axis=1)
    npad = MLA_DK - MLA_NOPE - MLA_ROPE
    cos = jnp.concatenate([jnp.ones((SEQ, MLA_NOPE), F32), cos, jnp.zeros((SEQ, npad), F32)], axis=1)
    sin = jnp.concatenate([jnp.zeros((SEQ, MLA_NOPE), F32), sin, jnp.zeros((SEQ, npad), F32)], axis=1)
    ctx_cos = jnp.concatenate([jnp.ones((CTX_LEN, MLA_NOPE + MLA_ROPE), F32), jnp.zeros((CTX_LEN, npad), F32)], axis=1)
    return (jnp.concatenate([ctx_cos, cos], axis=0),
            jnp.concatenate([jnp.zeros((CTX_LEN, MLA_DK), F32), sin], axis=0))


def mla_projection(cq, ckv, kr2, q_g, wqb, kv_g, wkvb, cdt=BF16, interpret=False):
    B, R, _ = cq.shape
    dq = MLA_NOPE + MLA_ROPE
    wq3 = wqb.reshape(MLA_Q_RANK, MLA_HEADS, dq)
    zq = jnp.zeros((MLA_Q_RANK, MLA_HEADS, MLA_DK - dq), wqb.dtype)
    wq = jnp.concatenate([wq3, zq], axis=2).reshape(MLA_Q_RANK, MLA_HEADS * MLA_DK).astype(cdt)
    wqs = jnp.concatenate([jnp.zeros((MLA_Q_RANK, MLA_HEADS, MLA_NOPE), wqb.dtype),
                           wq3[:, :, MLA_NOPE:][:, :, ROPE_SWAP], zq], axis=2)
    wqs = wqs.reshape(MLA_Q_RANK, MLA_HEADS * MLA_DK).astype(cdt)
    rc, rs = _rope_tables()
    row = lambda n: pl.BlockSpec((1, ROW_T, n), lambda b, i: (b, i, 0))
    tab = pl.BlockSpec((ROW_T, MLA_DK), lambda b, i: (i, 0))
    full = lambda shape: pl.BlockSpec(shape, lambda b, i: (0,) * len(shape))
    head = lambda n: pl.BlockSpec((1, MLA_HEADS, ROW_T, n), lambda b, i: (b, 0, i, 0))
    return pl.pallas_call(
        _mla_proj_kernel,
        grid=(B, R // ROW_T),
        in_specs=[row(MLA_Q_RANK), row(MLA_KV_RANK), row(2 * MLA_ROPE), tab, tab,
                  full((1, MLA_Q_RANK)), full(wq.shape), full(wqs.shape),
                  full((1, MLA_KV_RANK)), full(wkvb.shape)],
        out_specs=[head(MLA_DK), head(MLA_DK), head(MLA_V)],
        out_shape=[jax.ShapeDtypeStruct((B, MLA_HEADS, R, MLA_DK), cdt),
                   jax.ShapeDtypeStruct((B, MLA_HEADS, R, MLA_DK), cdt),
                   jax.ShapeDtypeStruct((B, MLA_HEADS, R, MLA_V), cdt)],
        compiler_params=pltpu.CompilerParams(dimension_semantics=("arbitrary", "arbitrary")),
        name="mla_proj", interpret=interpret,
    )(cq, ckv, kr2, rc, rs, q_g.reshape(1, -1), wq, wqs, kv_g.reshape(1, -1), wkvb.astype(cdt))


def _out_proj_kernel(s_ref, m_ref, g_ref, ya_ref, ybx_ref, ybc_ref, yc_ref, w_ref, snew_ref, hT_ref):
    cdt = w_ref.dtype
    yb = jnp.where(pl.program_id(1) == 0, ybc_ref[0], ybx_ref[0])
    mix = (jnp.dot(ya_ref[0].astype(cdt), w_ref[0:D_LRU, :], preferred_element_type=F32)
           + jnp.dot(yb.astype(cdt), w_ref[D_LRU:D_LRU + MLA_HEADS * MLA_V, :], preferred_element_type=F32)
           + jnp.dot(yc_ref[0].astype(cdt), w_ref[D_LRU + MLA_HEADS * MLA_V:, :], preferred_element_type=F32))
    m = m_ref[0, 0]
    s_new = s_ref[0] + m[2:3] * mix
    snew_ref[0] = s_new
    h2 = _rms(s_new) * g_ref[...] * (1.0 + m[4:5]) + m[3:4]
    hT_ref[...] = h2.T.astype(hT_ref.dtype)


def out_projection(s, mod, norm2_g, ya, yb_x, yb_c, yc, w_out, cdt=BF16, interpret=False):
    B, R, D = s.shape
    nt = R // ROW_T
    row = lambda n: pl.BlockSpec((1, ROW_T, n), lambda b, i: (b, i, 0))
    return pl.pallas_call(
        _out_proj_kernel,
        grid=(B, nt),
        in_specs=[row(D),
                  pl.BlockSpec((1, 1, 6, D), lambda b, i: (b, jnp.minimum(i, 1), 0, 0)),
                  pl.BlockSpec((1, D), lambda b, i: (0, 0)),
                  row(D_LRU),
                  pl.BlockSpec((1, ROW_T, MLA_HEADS * MLA_V), lambda b, i: (b, jnp.maximum(i - 1, 0), 0)),
                  pl.BlockSpec((1, ROW_T, MLA_HEADS * MLA_V), lambda b, i: (b, 0, 0)),
                  row(D_HY),
                  pl.BlockSpec((D_MIX, D), lambda b, i: (0, 0))],
        out_specs=[row(D), pl.BlockSpec((D, ROW_T), lambda b, i: (0, b * nt + i))],
        out_shape=[jax.ShapeDtypeStruct((B, R, D), F32), jax.ShapeDtypeStruct((D, B * R), cdt)],
        compiler_params=pltpu.CompilerParams(
            dimension_semantics=("arbitrary", "arbitrary"), vmem_limit_bytes=VMEM_LIMIT),
        name="out_proj", interpret=interpret,
    )(s, mod, norm2_g.reshape(1, D), ya, yb_x, yb_c, yc, w_out.astype(cdt))


PEER_TM = 512
PEER_TE = 512
NEG_BIG = -3.0e38
PACKED_ROWS = 2 * SUBLANES


def _topk_desc(s, k):
    rank = jnp.full(s.shape, float(PEER_NKEYS), F32)
    vals = []
    for r in range(k):
        m = jnp.max(s, axis=0, keepdims=True)
        hit = s == m
        rank = jnp.where(hit, float(r), rank)
        s = jnp.where(hit, NEG_BIG, s)
        vals.append(m)
    return jnp.concatenate(vals, axis=0), rank


def _route_chunk(s1, s2):
    K = PEER_TOPK
    v1, rank1 = _topk_desc(s1, K)
    v2, rank2 = _topk_desc(s2, K)
    half = K // 2
    cands = ([v1[0:1] + v2] + [v1[i:i + 1] + v2[:half] for i in range(1, half)]
             + [v1[i:i + 1] + v2[0:1] for i in range(half, K)])
    cmax = v1[0:1] + v2[0:1]
    c = jnp.concatenate(cands[:half] + [v1[half:] + v2[0:1]], axis=0)
    z = jnp.zeros_like(cmax)
    m = cmax
    for r in range(K):
        m = jnp.max(c, axis=0, keepdims=True)
        z = z + jnp.exp(m - cmax)
        if r + 1 < K:
            c = jnp.where(c == m, NEG_BIG, c)
    thr = m
    n = jnp.zeros_like(s1)
    for i in range(K):
        n_i = jnp.sum(jnp.where(cands[i] >= thr, 1.0, 0.0), axis=0, keepdims=True)
        n = jnp.where(rank1 == float(i), n_i, n)
    w = jnp.exp(s1 - v1[0:1]) / z
    e2 = jnp.exp(s2 - v2[0:1])
    return rank2, e2, n, w


def _peer_route_kernel(hT_ref, wqT_ref, keys_ref, rank2_ref, e2_ref, n_ref, w_ref, q_scr, s1_scr, s2_scr):
    h = pl.program_id(1)

    @pl.when(h == 0)
    def _():
        q_scr[...] = jnp.dot(wqT_ref[...], hT_ref[...], preferred_element_type=F32)

    cdt = keys_ref.dtype
    half = PEER_DQ // 2
    row = pl.multiple_of(h * PEER_DQ, PEER_DQ)
    q1 = q_scr[pl.ds(row, half), :].astype(cdt)
    q2 = q_scr[pl.ds(row + half, half), :].astype(cdt)
    s1_scr[...] = jnp.dot(keys_ref[0], q1, preferred_element_type=F32)
    s2_scr[...] = jnp.dot(keys_ref[1], q2, preferred_element_type=F32)

    def chunk(c, carry):
        sl = pl.ds(pl.multiple_of(c * LANES, LANES), LANES)
        rank2, e2, n, w = _route_chunk(s1_scr[:, sl], s2_scr[:, sl])
        rank2_ref[0, :, sl] = rank2.astype(rank2_ref.dtype)
        e2_ref[0, :, sl] = e2.astype(e2_ref.dtype)
        n_ref[0, :, sl] = n
        w_ref[0, :, sl] = w
        return carry

    lax.fori_loop(0, s1_scr.shape[1] // LANES, chunk, 0)


def _peer_dense_kernel(hT_ref, rank2_ref, e2_ref, n_ref, w_ref, u_ref, vT_ref, s_ref, g_ref, o_ref, acc_ref, pre_ref):
    j = pl.program_id(1)
    n_chunks = pl.num_programs(1) - 1

    @pl.when(j == 0)
    def _():
        acc_ref[...] = jnp.zeros_like(acc_ref)
        pre_ref[1] = jnp.zeros(pre_ref.shape[1:], pre_ref.dtype)

    te, tm = u_ref.shape[0], hT_ref.shape[1]
    groups = te // PEER_NKEYS
    gdt = rank2_ref.dtype
    mxu_k = 2 * PEER_NKEYS

    def step(rd, wr):
        pre_ref[wr] = jnp.dot(u_ref[...], hT_ref[...], preferred_element_type=F32)
        out = None
        for k0 in range(0, te, mxu_k):
            acts = []
            for al in range(k0 // PEER_NKEYS, (k0 + mxu_k) // PEER_NKEYS):
                a = jnp.maximum(j - 1, 0) * groups + al
                g = None
                for h in range(PEER_HEADS):
                    n16 = jnp.broadcast_to(n_ref[h, pl.ds(a, 1), :], (PACKED_ROWS, tm)).astype(gdt)
                    w16 = jnp.broadcast_to(w_ref[h, pl.ds(a, 1), :], (PACKED_ROWS, tm)).astype(gdt)
                    nb = jnp.concatenate([n16] * (PEER_NKEYS // PACKED_ROWS), axis=0)
                    wb = jnp.concatenate([w16] * (PEER_NKEYS // PACKED_ROWS), axis=0)
                    t = jnp.where(rank2_ref[h] < nb, e2_ref[h] * wb, jnp.zeros_like(wb))
                    g = t if g is None else g + t
                blk = slice(al * PEER_NKEYS, (al + 1) * PEER_NKEYS)
                acts.append((g * _gelu_tanh(pre_ref[rd, blk, :]).astype(gdt)).astype(vT_ref.dtype))
            part = jnp.dot(vT_ref[:, k0:k0 + mxu_k], jnp.concatenate(acts, axis=0), preferred_element_type=F32)
            out = part if out is None else out + part
        acc_ref[...] += out

    for parity in range(2):
        pl.when(j % 2 == parity)(functools.partial(step, 1 - parity, parity))

    @pl.when(j == n_chunks)
    def _():
        y = acc_ref[...].T
        for r in range(g_ref.shape[0]):
            rows = slice(r * ROW_T, (r + 1) * ROW_T)
            o_ref[rows, :] = s_ref[rows, :] + g_ref[r] * y[rows, :]


def peer_tokens(hT, s, gate, wq, keys, u_tab, v_tab, interpret=False):
    D, T = hT.shape
    cdt = hT.dtype
    tm, te = PEER_TM, PEER_TE
    assert T % tm == 0 and PEER_EXPERTS % te == 0 and tm % ROW_T == 0
    wqT = wq.T.astype(cdt)
    route_shape = jax.ShapeDtypeStruct((PEER_HEADS, PEER_NKEYS, T), F32)
    packed_shape = jax.ShapeDtypeStruct((PEER_HEADS, PEER_NKEYS, T), cdt)
    route_spec = pl.BlockSpec((1, PEER_NKEYS, tm), lambda i, hh: (hh, 0, i))
    rank2, e2, n, w = pl.pallas_call(
        _peer_route_kernel,
        grid=(T // tm, PEER_HEADS),
        in_specs=[pl.BlockSpec((D, tm), lambda i, hh: (0, i)),
                  pl.BlockSpec((PEER_HEADS * PEER_DQ, D), lambda i, hh: (0, 0)),
                  pl.BlockSpec((2, PEER_NKEYS, PEER_DQ // 2), lambda i, hh: (0, 0, 0))],
        out_specs=[route_spec] * 4,
        out_shape=[packed_shape, packed_shape, route_shape, route_shape],
        scratch_shapes=[pltpu.VMEM((PEER_HEADS * PEER_DQ, tm), F32),
                        pltpu.VMEM((PEER_NKEYS, tm), F32),
                        pltpu.VMEM((PEER_NKEYS, tm), F32)],
        compiler_params=pltpu.CompilerParams(
            dimension_semantics=("arbitrary", "arbitrary"), vmem_limit_bytes=VMEM_LIMIT),
        name="peer_route", interpret=interpret,
    )(hT, wqT, keys.astype(cdt))

    head_spec = pl.BlockSpec((PEER_HEADS, PEER_NKEYS, tm), lambda i, j: (0, 0, i))
    nj = PEER_EXPERTS // te
    return pl.pallas_call(
        _peer_dense_kernel,
        grid=(T // tm, nj + 1),
        in_specs=[pl.BlockSpec((D, tm), lambda i, j: (0, i)),
                  head_spec, head_spec, head_spec, head_spec,
                  pl.BlockSpec((te, D), lambda i, j: (jnp.minimum(j, nj - 1), 0)),
                  pl.BlockSpec((D, te), lambda i, j: (0, jnp.maximum(j - 1, 0))),
                  pl.BlockSpec((tm, D), lambda i, j: (i, 0)),
                  pl.BlockSpec((tm // ROW_T, 1, D), lambda i, j: (i, 0, 0))],
        out_specs=pl.BlockSpec((tm, D), lambda i, j: (i, 0)),
        out_shape=jax.ShapeDtypeStruct((T, D), F32),
        scratch_shapes=[pltpu.VMEM((D, tm), F32), pltpu.VMEM((2, te, tm), F32)],
        compiler_params=pltpu.CompilerParams(
            dimension_semantics=("arbitrary", "arbitrary"), vmem_limit_bytes=VMEM_LIMIT),
        name="peer_dense", interpret=interpret,
    )(hT, rank2, e2, n, w, u_tab.astype(cdt), v_tab.T.astype(cdt), s, gate)


def _final_norm_kernel(x_ref, g_ref, o_ref):
    o_ref[0] = _rms(x_ref[0]) * g_ref[...]


def final_norm(s, g):
    B, R, D = s.shape
    return pl.pallas_call(
        _final_norm_kernel,
        grid=(B, SEQ // ROW_T),
        in_specs=[pl.BlockSpec((1, ROW_T, D), lambda b, i: (b, i + CTX_LEN // ROW_T, 0)),
                  pl.BlockSpec((1, D), lambda b, i: (0, 0))],
        out_specs=pl.BlockSpec((1, ROW_T, D), lambda b, i: (b, i, 0)),
        out_shape=jax.ShapeDtypeStruct((B, SEQ, D), F32),
        compiler_params=pltpu.CompilerParams(dimension_semantics=("arbitrary", "arbitrary")),
        name="final_norm",
    )(s, g.reshape(1, D))


def kernel(x, c, ctx, c_ctx, w_mod, b_mod, norm1_g, norm2_g, w_in, w_out,
           lru_conv_w, lru_conv_b, lru_wr, lru_br, lru_wi, lru_bi, lru_lambda,
           mla_q_norm_g, mla_wqb, mla_kv_norm_g, mla_wkvb,
           hy_conv_w, hy_conv_b, hy_f_w1, hy_f_b1, hy_f_w2, hy_f_b2, hy_f_w3,
           hy_f_freq, hy_decay, hy_skip,
           peer_wq, peer_keys, peer_u, peer_v, final_g):
    B = x.shape[0]
    s = jnp.concatenate([ctx, x], axis=1)
    for l in range(DEPTH):
        need_ctx = l < DEPTH - 1
        mod = adaln_modulation(c, c_ctx, w_mod[l], b_mod[l])
        u_lru, u_gate, cq, ckv, u_hy, kr2 = in_projection(s, mod, norm1_g[l], w_in[l])

        ya = rglru_mixer(u_lru, u_gate, lru_conv_w[l], lru_conv_b[l], lru_wr[l], lru_br[l], lru_wi[l], lru_bi[l],
                         lru_lambda[l])

        q, k, v = mla_projection(cq, ckv, kr2, mla_q_norm_g[l], mla_wqb[l], mla_kv_norm_g[l], mla_wkvb[l])
        yb_x = flash_attention(q, k, v, q_row0=CTX_LEN, n_q=SEQ, n_k=SEQ_ALL, tk=SEQ_ALL // 6)
        if need_ctx:
            yb_c = flash_attention(q, k, v, q_row0=0, n_q=CTX_LEN, n_k=CTX_LEN, tk=CTX_LEN)
        else:
            yb_c = jnp.zeros((B, CTX_LEN, MLA_HEADS * MLA_V), F32)

        hy_args = (hy_conv_w[l], hy_conv_b[l], hy_f_w1[l], hy_f_b1[l], hy_f_w2[l], hy_f_b2[l], hy_f_w3[l],
                   hy_f_freq[l], hy_decay[l], hy_skip[l])
        yc_x = hyena_pallas(u_hy[:, CTX_LEN:], *hy_args)
        yc_c = hyena_pallas(u_hy[:, :CTX_LEN], *hy_args) if need_ctx else jnp.zeros((B, CTX_LEN, D_HY), F32)
        yc = jnp.concatenate([yc_c, yc_x], axis=1)

        s, hT = out_projection(s, mod, norm2_g[l], ya, yb_x, yb_c, yc, w_out[l])
        gate2 = jnp.repeat(mod[:, :, 5], jnp.array([1, ROW_NT - 1]), axis=1, total_repeat_length=ROW_NT)
        s = peer_tokens(hT, s.reshape(B * SEQ_ALL, D_MODEL), gate2.reshape(B * ROW_NT, 1, D_MODEL),
                        peer_wq[l], peer_keys[l], peer_u[l], peer_v[l]).reshape(B, SEQ_ALL, D_MODEL)
    return final_norm(s, final_g)
```

```python
import functools
import math

import jax
import jax.numpy as jnp
import numpy as np
from jax import lax
from jax.experimental import pallas as pl
from jax.experimental.pallas import tpu as pltpu

D_MODEL = 1024
BATCH = 2
SEQ = 8192
DEPTH = 2

GRID_W = 64
CTX_LEN = 256
EPS = 1e-6

D_LRU = 384
LRU_BLOCKS = 6
LRU_BLOCK_DIM = D_LRU // LRU_BLOCKS
LRU_CONV = 4
LRU_C = 8.0

MLA_HEADS = 6
MLA_NOPE = 64
MLA_ROPE = 32
MLA_V = 64
MLA_Q_RANK = 384
MLA_KV_RANK = 256
MLA_SCALE = (MLA_NOPE + MLA_ROPE) ** -0.5
ROPE_PAIRS = MLA_ROPE // 4
ROPE_THETA = 10000.0
ATTN_BLOCK = 128

D_HY = 256
HY_CONV = 3
HY_ORDER = 2
HY_BANDS = 8
HY_EMB = 1 + 2 * HY_BANDS
HY_FFN = 64

D_MIX = D_LRU + MLA_HEADS * MLA_V + D_HY
D_IN = 2 * D_LRU + MLA_Q_RANK + MLA_KV_RANK + MLA_ROPE + 3 * D_HY
IN_SPLITS = [D_LRU, 2 * D_LRU, 2 * D_LRU + MLA_Q_RANK,
             2 * D_LRU + MLA_Q_RANK + MLA_KV_RANK,
             2 * D_LRU + MLA_Q_RANK + MLA_KV_RANK + MLA_ROPE]

PEER_HEADS = 8
PEER_NKEYS = 128
PEER_EXPERTS = PEER_NKEYS * PEER_NKEYS
PEER_DQ = 256
PEER_TOPK = 16
PEER_BLOCK = 128

F32 = jnp.float32


def rmsnorm(x, g):
    xf = x.astype(F32)
    y = xf * lax.rsqrt(jnp.mean(xf * xf, axis=-1, keepdims=True) + EPS)
    return (y * g.astype(F32)).astype(x.dtype)


def depthwise_conv(x, w, b, pad_l, pad_r):
    y = lax.conv_general_dilated(
        x, w[:, None, :].astype(x.dtype), window_strides=(1,),
        padding=[(pad_l, pad_r)], dimension_numbers=('NWC', 'WIO', 'NWC'),
        feature_group_count=x.shape[-1])
    return y + b.astype(x.dtype)


def linear_scan(a, b, h0):
    b = b.at[:, 0].add(a[:, 0] * h0)

    def combine(e1, e2):
        a1, b1 = e1
        a2, b2 = e2
        return a1 * a2, a2 * b1 + b2

    _, h = lax.associative_scan(combine, (a, b), axis=1)
    return h


def rglru_coeffs(x, wr, br, wi, bi, lam):
    xb = x.reshape(x.shape[:-1] + (LRU_BLOCKS, LRU_BLOCK_DIM))
    r = jax.nn.sigmoid(jnp.einsum('blnd,nde->blne', xb, wr.astype(F32)).reshape(x.shape) + br.astype(F32))
    i = jax.nn.sigmoid(jnp.einsum('blnd,nde->blne', xb, wi.astype(F32)).reshape(x.shape) + bi.astype(F32))
    log_a = -LRU_C * r * jax.nn.softplus(-lam.astype(F32))
    a = jnp.exp(log_a)
    b = jnp.sqrt(-jnp.expm1(2.0 * log_a)) * (i * x)
    return a, b


def rglru_bidir(xx, xc, wr, br, wi, bi, lam):
    xx32, xc32 = xx.astype(F32), xc.astype(F32)
    h0 = jnp.zeros((xx.shape[0], D_LRU), F32)
    ys_x, ys_c = [], []
    for d in range(2):
        flip = (lambda t: t[:, ::-1]) if d == 1 else (lambda t: t)
        ac, bc = rglru_coeffs(flip(xc32), wr[d], br[d], wi[d], bi[d], lam[d])
        ax, bx = rglru_coeffs(flip(xx32), wr[d], br[d], wi[d], bi[d], lam[d])
        hc = linear_scan(ac, bc, h0)
        hx = linear_scan(ax, bx, hc[:, -1])
        ys_c.append(flip(hc))
        ys_x.append(flip(hx))
    return (ys_x[0] + ys_x[1]).astype(xx.dtype), (ys_c[0] + ys_c[1]).astype(xc.dtype)


def rope_axis(x, ang):
    n = x.shape[-1] // 2
    x1, x2 = x[..., :n], x[..., n:]
    cos, sin = jnp.cos(ang).astype(x.dtype), jnp.sin(ang).astype(x.dtype)
    return jnp.concatenate([x1 * cos - x2 * sin, x2 * cos + x1 * sin], axis=-1)


def rope_2d(x, ang_row, ang_col):
    h = MLA_ROPE // 2
    return jnp.concatenate([rope_axis(x[..., :h], ang_row), rope_axis(x[..., h:], ang_col)], axis=-1)


def mla_project(cq, ckv, kr, q_g, wqb, kv_g, wkvb, angles):
    B, L = cq.shape[:2]
    q = (rmsnorm(cq, q_g) @ wqb).reshape(B, L, MLA_HEADS, MLA_NOPE + MLA_ROPE)
    kv = (rmsnorm(ckv, kv_g) @ wkvb).reshape(B, L, MLA_HEADS, MLA_NOPE + MLA_V)
    q_nope, q_rope = q[..., :MLA_NOPE], q[..., MLA_NOPE:]
    k_nope, v = kv[..., :MLA_NOPE], kv[..., MLA_NOPE:]
    if angles is not None:
        ang_r, ang_c = angles
        q_rope = rope_2d(q_rope, ang_r[:, None, :], ang_c[:, None, :])
        kr = rope_2d(kr, ang_r, ang_c)
    k_rope = jnp.broadcast_to(kr[:, :, None, :], (B, L, MLA_HEADS, MLA_ROPE))
    q = jnp.concatenate([q_nope, q_rope], axis=-1)
    k = jnp.concatenate([k_nope, k_rope], axis=-1)
    return q, k, v


def attend(q, k, v):
    s = jnp.einsum('bqhd,bkhd->bhqk', q, k, preferred_element_type=F32) * MLA_SCALE
    p = jax.nn.softmax(s, axis=-1).astype(v.dtype)
    return jnp.einsum('bhqk,bkhd->bqhd', p, v)


def hyena_filter_spectra(L, w1, b1, w2, b2, w3, freq, decay):
    t = jnp.arange(L, dtype=F32) / L
    bands = jnp.linspace(1e-4, HY_BANDS - 1, HY_BANDS, dtype=F32)
    wpos = (2.0 * math.pi) * t[:, None] * bands[None, :]
    z = jnp.concatenate([t[:, None], jnp.cos(wpos), jnp.sin(wpos)], axis=-1)
    fr = freq.astype(F32)
    h = jnp.sin(fr * (z @ w1.astype(F32) + b1.astype(F32)))
    h = jnp.sin(fr * (h @ w2.astype(F32) + b2.astype(F32)))
    k = (h @ w3.astype(F32)).reshape(L, 2 * HY_ORDER, D_HY) * jnp.exp(-t[:, None, None] * decay.astype(F32))
    k = k * lax.rsqrt(jnp.sum(k * k, axis=0, keepdims=True) + EPS)
    k = k.reshape(L, HY_ORDER, 2, D_HY)
    kf, kb = k[:, :, 0], k[:, :, 1]
    k_circ = jnp.concatenate([kf, jnp.zeros((1, HY_ORDER, D_HY), F32), kb[:0:-1]], axis=0)
    return jnp.fft.rfft(k_circ, axis=0)


def fft_conv(z, kspec, skip):
    L = z.shape[1]
    z32 = z.astype(F32)
    zf = jnp.fft.rfft(z32, n=2 * L, axis=1)
    y = jnp.fft.irfft(zf * kspec[None], n=2 * L, axis=1)[:, :L]
    return (y + z32 * skip.astype(F32)).astype(z.dtype)


def hyena(u, conv_w, conv_b, w1, b1, w2, b2, w3, freq, decay, skip):
    L = u.shape[1]
    u = depthwise_conv(u, conv_w, conv_b, 1, 1)
    v, x1, x2 = jnp.split(u, 3, axis=-1)
    kspec = hyena_filter_spectra(L, w1, b1, w2, b2, w3, freq, decay)
    y = x1 * fft_conv(v, kspec[:, 0], skip[0])
    y = x2 * fft_conv(y, kspec[:, 1], skip[1])
    return y


BF16 = jnp.bfloat16
LANES = 128
SUBLANES = 8
VMEM_LIMIT = 56 * 1024 * 1024
SEQ_ALL = CTX_LEN + SEQ
LRU_TT = CTX_LEN
LRU_NT = SEQ_ALL // LRU_TT


def _gelu_tanh(x):
    return 0.5 * x * (1.0 + jnp.tanh(0.7978845608028654 * (x + 0.044715 * (x * x * x))))


def _lru_coeff_kernel(u_ref, up_ref, un_ref, cw_ref, cb_ref, wg_ref, bg_ref, c_ref,
                      af_ref, bf_ref, ab_ref, bb_ref):
    i = pl.program_id(1)
    x = u_ref[0]
    tt = x.shape[0]
    use_prev = i >= 2
    use_next = jnp.logical_and(i >= 1, i < LRU_NT - 1)
    p2 = jnp.where(use_prev, up_ref[0][SUBLANES - 2:SUBLANES], 0.0)
    n1 = jnp.where(use_next, un_ref[0][0:1], 0.0)
    ext = jnp.concatenate([p2, x, n1], axis=0)
    xc = cb_ref[...] + sum(cw_ref[k:k + 1, :] * ext[k:k + tt] for k in range(LRU_CONV))
    g = jnp.dot(xc.astype(wg_ref.dtype), wg_ref[...], preferred_element_type=F32) + bg_ref[...]
    for d, (a_ref, b_ref) in enumerate(((af_ref, bf_ref), (ab_ref, bb_ref))):
        r = jax.nn.sigmoid(g[:, (2 * d) * D_LRU:(2 * d + 1) * D_LRU])
        gi = jax.nn.sigmoid(g[:, (2 * d + 1) * D_LRU:(2 * d + 2) * D_LRU])
        log_a = c_ref[d:d + 1, :] * r
        a_ref[0] = jnp.exp(log_a)
        b_ref[0] = jnp.sqrt(1.0 - jnp.exp(2.0 * log_a)) * (gi * xc)


def _lru_scan_rows(a_ref, b_ref, h_scr, emit, reverse):
    tt = a_ref.shape[1]
    nb = a_ref.shape[0]

    def body(s, hs):
        t = (tt - 1 - s) if reverse else s
        out = []
        for b in range(nb):
            h = a_ref[b, pl.ds(t, 1), :] * hs[b] + b_ref[b, pl.ds(t, 1), :]
            emit(b, t, h)
            out.append(h)
        return tuple(out)

    hs = lax.fori_loop(0, tt, body, tuple(h_scr[b:b + 1, :] for b in range(nb)), unroll=8)
    for b in range(nb):
        h_scr[b:b + 1, :] = hs[b]


def _lru_fwd_kernel(a_ref, b_ref, hf_ref, h_scr):
    @pl.when(pl.program_id(0) == 0)
    def _():
        h_scr[...] = jnp.zeros_like(h_scr)

    def emit(b, t, h):
        hf_ref[b, pl.ds(t, 1), :] = h

    _lru_scan_rows(a_ref, b_ref, h_scr, emit, reverse=False)


def _lru_bwd_kernel(a_ref, b_ref, hf_ref, gate_ref, y_ref, h_scr):
    @pl.when(pl.program_id(0) == 0)
    def _():
        h_scr[...] = jnp.zeros_like(h_scr)

    def emit(b, t, h):
        y_ref[b, pl.ds(t, 1), :] = h

    _lru_scan_rows(a_ref, b_ref, h_scr, emit, reverse=True)
    y_ref[...] = (y_ref[...] + hf_ref[...]) * _gelu_tanh(gate_ref[...])


def rglru_mixer(u_lru, u_gate, conv_w, conv_b, wr, br, wi, bi, lam, cdt=BF16, interpret=False):
    B = u_lru.shape[0]
    tt, nt = LRU_TT, LRU_NT
    hb = tt // SUBLANES

    def blockdiag(w):
        eye = jnp.eye(LRU_BLOCKS, dtype=w.dtype)
        return jnp.einsum('nde,nm->ndme', w, eye).reshape(D_LRU, D_LRU)

    wg = jnp.concatenate([blockdiag(wr[0]), blockdiag(wi[0]), blockdiag(wr[1]), blockdiag(wi[1])], axis=1).astype(cdt)
    bg = jnp.concatenate([br[0], bi[0], br[1], bi[1]]).reshape(1, 4 * D_LRU).astype(F32)
    cdec = -LRU_C * jax.nn.softplus(-lam.astype(F32))

    tile = pl.BlockSpec((1, tt, D_LRU), lambda b, i: (b, i, 0))
    full = lambda shape: pl.BlockSpec(shape, lambda b, i: (0,) * len(shape))
    coeff_shape = jax.ShapeDtypeStruct((B, SEQ_ALL, D_LRU), F32)
    a_f, b_f, a_b, b_b = pl.pallas_call(
        _lru_coeff_kernel,
        grid=(B, nt),
        in_specs=[tile,
                  pl.BlockSpec((1, SUBLANES, D_LRU), lambda b, i: (b, jnp.maximum(i * hb - 1, 0), 0)),
                  pl.BlockSpec((1, SUBLANES, D_LRU), lambda b, i: (b, jnp.minimum((i + 1) * hb, nt * hb - 1), 0)),
                  full((LRU_CONV, D_LRU)), full((1, D_LRU)), full((D_LRU, 4 * D_LRU)), full((1, 4 * D_LRU)),
                  full((2, D_LRU))],
        out_specs=[tile] * 4,
        out_shape=[coeff_shape] * 4,
        compiler_params=pltpu.CompilerParams(dimension_semantics=("arbitrary", "arbitrary")),
        name="lru_coeff", interpret=interpret,
    )(u_lru, u_lru, u_lru, conv_w.astype(F32), conv_b.reshape(1, D_LRU).astype(F32), wg, bg, cdec)

    seq = pl.BlockSpec((B, tt, D_LRU), lambda s: (0, s, 0))
    h_f = pl.pallas_call(
        _lru_fwd_kernel,
        grid=(nt,),
        in_specs=[seq, seq],
        out_specs=seq,
        out_shape=coeff_shape,
        scratch_shapes=[pltpu.VMEM((B, D_LRU), F32)],
        compiler_params=pltpu.CompilerParams(dimension_semantics=("arbitrary",)),
        name="lru_fwd", interpret=interpret,
    )(a_f, b_f)

    rseq = pl.BlockSpec((B, tt, D_LRU), lambda s: (0, jnp.where(s == 0, 0, nt - s), 0))
    return pl.pallas_call(
        _lru_bwd_kernel,
        grid=(nt,),
        in_specs=[rseq, rseq, rseq, rseq],
        out_specs=rseq,
        out_shape=coeff_shape,
        scratch_shapes=[pltpu.VMEM((B, D_LRU), F32)],
        compiler_params=pltpu.CompilerParams(dimension_semantics=("arbitrary",)),
        name="lru_bwd", interpret=interpret,
    )(a_b, b_b, h_f, u_gate)


ATT_TQ = 256


ATT_HG = 2


def _flash_kernel(q_ref, k_ref, v_ref, o_ref, *, tk):
    hg, tq = q_ref.shape[1], q_ref.shape[2]
    n_chunks = k_ref.shape[2] // tk
    qs = [q_ref[0, h] for h in range(hg)]

    def body(c, carry):
        off = pl.multiple_of(c * tk, LANES)
        out = []
        for h in range(hg):
            m, l, acc = carry[h]
            s = lax.dot_general(qs[h], k_ref[0, h, pl.ds(off, tk), :], (((1,), (1,)), ((), ())),
                                preferred_element_type=F32)
            m_new = jnp.maximum(m, jnp.max(s, axis=1, keepdims=True))
            alpha = jnp.exp(m - m_new)
            p = jnp.exp(s - m_new)
            l = alpha * l + jnp.sum(p, axis=1, keepdims=True)
            pv = jnp.dot(p.astype(v_ref.dtype), v_ref[0, h, pl.ds(off, tk), :], preferred_element_type=F32)
            out.append((m_new, l, alpha * acc + pv))
        return tuple(out)

    init = tuple((jnp.full((tq, 1), NEG_BIG, F32), jnp.zeros((tq, 1), F32), jnp.zeros((tq, v_ref.shape[3]), F32))
                 for _ in range(hg))
    res = lax.fori_loop(0, n_chunks, body, init, unroll=True)
    o_ref[0] = jnp.concatenate([acc / l for _, l, acc in res], axis=1)


def flash_attention(q, k, v, q_row0, n_q, n_k, tk, interpret=False):
    B, H, R, dk = q.shape
    dv = v.shape[3]
    tq = ATT_TQ
    assert n_q % tq == 0 and q_row0 % tq == 0 and n_k % tk == 0 and tk % LANES == 0 and R % n_k == 0
    q0 = q_row0 // tq
    return pl.pallas_call(
        functools.partial(_flash_kernel, tk=tk),
        grid=(B, H // ATT_HG, n_q // tq),
        in_specs=[pl.BlockSpec((1, ATT_HG, tq, dk), lambda b, g, i: (b, g, i + q0, 0)),
                  pl.BlockSpec((1, ATT_HG, n_k, dk), lambda b, g, i: (b, g, 0, 0)),
                  pl.BlockSpec((1, ATT_HG, n_k, dv), lambda b, g, i: (b, g, 0, 0))],
        out_specs=pl.BlockSpec((1, tq, ATT_HG * dv), lambda b, g, i: (b, i, g)),
        out_shape=jax.ShapeDtypeStruct((B, n_q, H * dv), F32),
        compiler_params=pltpu.CompilerParams(
            dimension_semantics=("arbitrary", "arbitrary", "arbitrary"), vmem_limit_bytes=VMEM_LIMIT),
        name="flash_attention", interpret=interpret,
    )(q, k, v)


HY_CB = 32
HY_FR = 128
HY_UNROLL = 8


def _dft_tables(n1, n_in, n_out, cdt):
    n = n1 * LANES
    k1 = np.arange(n1)[:, None]
    f1 = np.exp(-2j * np.pi * k1 * np.arange(n_in)[None, :] / n1)
    tw = np.exp(-2j * np.pi * k1 * np.arange(LANES)[None, :] / n)
    f2 = np.exp(-2j * np.pi * np.outer(np.arange(LANES), np.arange(LANES)) / LANES)
    g1 = np.conj(np.exp(-2j * np.pi * k1 * np.arange(n_out)[None, :] / n1)).T / n
    blk = lambda m: np.block([[m.real, -m.imag], [m.imag, m.real]])
    return dict(
        ma=jnp.asarray(blk(f1), cdt),
        ma_re=jnp.asarray(np.concatenate([f1.real, f1.imag], axis=0), cdt),
        tr=jnp.asarray(tw.real, F32), ti=jnp.asarray(tw.imag, F32),
        f2=jnp.asarray(np.concatenate([f2.real, f2.imag], axis=1), cdt),
        f2c=jnp.asarray(np.concatenate([f2.real, -f2.imag], axis=1), cdt),
        mi=jnp.asarray(blk(g1), cdt),
    )


def _cmul_right(re, im, f_ref):
    n = re.shape[0]
    p = jnp.dot(jnp.concatenate([re, im], axis=0).astype(f_ref.dtype), f_ref[...], preferred_element_type=F32)
    return p[:n, :LANES] - p[n:, LANES:], p[:n, LANES:] + p[n:, :LANES]


def _fft_filter_kernel(k_ref, ma_ref, tr_ref, ti_ref, f2_ref, xr_ref, xi_ref):
    n1 = tr_ref.shape[0]

    def body(c, carry):
        o = jnp.dot(ma_ref[...], k_ref[c].astype(ma_ref.dtype), preferred_element_type=F32)
        ar, ai = o[:n1], o[n1:]
        xr, xi = _cmul_right(ar * tr_ref[...] - ai * ti_ref[...], ar * ti_ref[...] + ai * tr_ref[...], f2_ref)
        xr_ref[c] = xr
        xi_ref[c] = xi
        return carry

    lax.fori_loop(0, k_ref.shape[0], body, 0, unroll=HY_UNROLL)


def _fft_conv_kernel(z_ref, gate_ref, kfr_ref, kfi_ref, kbr_ref, kbi_ref, skip_ref,
                     ma_ref, tr_ref, ti_ref, f2_ref, f2c_ref, mi_ref, o_ref):
    n1 = tr_ref.shape[0]
    n_out = o_ref.shape[2]

    def body(c, carry):
        z0, z1 = z_ref[0, c], z_ref[1, c]
        o = jnp.dot(ma_ref[...], jnp.concatenate([z0, z1], axis=0).astype(ma_ref.dtype), preferred_element_type=F32)
        ar, ai = o[:n1], o[n1:]
        tr, ti = tr_ref[...], ti_ref[...]
        xr, xi = _cmul_right(ar * tr - ai * ti, ar * ti + ai * tr, f2_ref)
        hr = kfr_ref[c] + kbr_ref[c]
        hi = kfi_ref[c] - kbi_ref[c]
        br, bi = _cmul_right(xr * hr - xi * hi, xr * hi + xi * hr, f2c_ref)
        b2 = jnp.concatenate([br * tr + bi * ti, bi * tr - br * ti], axis=0).astype(mi_ref.dtype)
        y = jnp.dot(mi_ref[...], b2, preferred_element_type=F32)
        sk = skip_ref[c]
        o_ref[0, c] = gate_ref[0, c] * (y[:n_out] + sk * z0[:n_out])
        o_ref[1, c] = gate_ref[1, c] * (y[n_out:] + sk * z1[:n_out])
        return carry

    lax.fori_loop(0, z_ref.shape[1], body, 0, unroll=HY_UNROLL)


def fft_filter_spectra(k_rows, n1, cdt=BF16, interpret=False):
    R, n_in, _ = k_rows.shape
    t = _dft_tables(n1, n_in, n_in, cdt)
    full = lambda a: pl.BlockSpec(a.shape, lambda i: (0,) * a.ndim)
    spec = jax.ShapeDtypeStruct((R, n1, LANES), F32)
    consts = (t["ma_re"], t["tr"], t["ti"], t["f2"])
    return pl.pallas_call(
        _fft_filter_kernel,
        grid=(R // HY_CB,),
        in_specs=[pl.BlockSpec((HY_CB, n_in, LANES), lambda i: (i, 0, 0))] + [full(a) for a in consts],
        out_specs=[pl.BlockSpec((HY_CB, n1, LANES), lambda i: (i, 0, 0))] * 2,
        out_shape=[spec, spec],
        compiler_params=pltpu.CompilerParams(dimension_semantics=("arbitrary",)),
        name="fft_filter", interpret=interpret,
    )(k_rows, *consts)


def fft_long_conv(z, gate, kf, kb, skip, n1, cdt=BF16, interpret=False):
    B, C, n_in, _ = z.shape
    assert B == 2 and C % HY_CB == 0
    t = _dft_tables(n1, n_in, n_in, cdt)
    full = lambda a: pl.BlockSpec(a.shape, lambda i: (0,) * a.ndim)
    sig = pl.BlockSpec((B, HY_CB, n_in, LANES), lambda i: (0, i, 0, 0))
    spc = pl.BlockSpec((HY_CB, n1, LANES), lambda i: (i, 0, 0))
    consts = (t["ma"], t["tr"], t["ti"], t["f2"], t["f2c"], t["mi"])
    skip_b = jnp.broadcast_to(skip.astype(F32)[:, None, None], (C, 1, LANES))
    return pl.pallas_call(
        _fft_conv_kernel,
        grid=(C // HY_CB,),
        in_specs=[sig, sig, spc, spc, spc, spc, pl.BlockSpec((HY_CB, 1, LANES), lambda i: (i, 0, 0))]
                 + [full(a) for a in consts],
        out_specs=sig,
        out_shape=jax.ShapeDtypeStruct(z.shape, F32),
        compiler_params=pltpu.CompilerParams(dimension_semantics=("arbitrary",)),
        name="fft_conv", interpret=interpret,
    )(z, gate, kf[0], kf[1], kb[0], kb[1], skip_b, *consts)


def _hy_filter_kernel(z_ref, w1_ref, b1_ref, w2_ref, b2_ref, fr_ref, w3_ref, dec_ref, k_ref, h_scr):
    hp = lax.Precision.HIGHEST

    @pl.when(pl.program_id(0) == 0)
    def _():
        h = jnp.sin(fr_ref[...] * (jnp.dot(w1_ref[...], z_ref[...], precision=hp, preferred_element_type=F32) + b1_ref[...]))
        h_scr[...] = jnp.sin(fr_ref[...] * (jnp.dot(w2_ref[...], h, precision=hp, preferred_element_type=F32) + b2_ref[...]))

    t = z_ref[0:1, :]
    k = jnp.dot(w3_ref[...], h_scr[...], precision=hp, preferred_element_type=F32) * jnp.exp(-t * dec_ref[...])
    k_ref[...] = k * lax.rsqrt(jnp.sum(k * k, axis=1, keepdims=True) + EPS)


def hyena_filters(L, w1, b1, w2, b2, w3, freq, decay, interpret=False):
    t = jnp.arange(L, dtype=F32) / L
    bands = jnp.linspace(1e-4, HY_BANDS - 1, HY_BANDS, dtype=F32)
    wpos = (2.0 * math.pi) * t[None, :] * bands[:, None]
    emb = 24
    z = jnp.concatenate([t[None, :], jnp.cos(wpos), jnp.sin(wpos), jnp.zeros((emb - HY_EMB, L), F32)], axis=0)
    w1p = jnp.concatenate([w1.T, jnp.zeros((HY_FFN, emb - HY_EMB), F32)], axis=1)
    R = 2 * HY_ORDER * D_HY
    col = lambda v: v.reshape(-1, 1).astype(F32)
    full = lambda shape: pl.BlockSpec(shape, lambda i: (0,) * len(shape))
    return pl.pallas_call(
        _hy_filter_kernel,
        grid=(R // HY_FR,),
        in_specs=[full((emb, L)), full((HY_FFN, emb)), full((HY_FFN, 1)), full((HY_FFN, HY_FFN)), full((HY_FFN, 1)),
                  full((HY_FFN, 1)), pl.BlockSpec((HY_FR, HY_FFN), lambda i: (i, 0)),
                  pl.BlockSpec((HY_FR, 1), lambda i: (i, 0))],
        out_specs=pl.BlockSpec((HY_FR, L), lambda i: (i, 0)),
        out_shape=jax.ShapeDtypeStruct((R, L), F32),
        scratch_shapes=[pltpu.VMEM((HY_FFN, L), F32)],
        compiler_params=pltpu.CompilerParams(dimension_semantics=("arbitrary",)),
        name="hy_filter", interpret=interpret,
    )(z, w1p, col(b1), w2.T.astype(F32), col(b2), col(freq), w3.T.astype(F32), col(decay))


def _hy_shortconv_kernel(u_ref, up_ref, un_ref, w_ref, b_ref, o_ref):
    i = pl.program_id(1)
    n = pl.num_programs(1)
    x = u_ref[0]
    tt = x.shape[0]
    p1 = jnp.where(i > 0, up_ref[0][SUBLANES - 1:SUBLANES], 0.0)
    n1 = jnp.where(i < n - 1, un_ref[0][0:1], 0.0)
    ext = jnp.concatenate([p1, x, n1], axis=0)
    o_ref[0] = b_ref[...] + sum(w_ref[k:k + 1, :] * ext[k:k + tt] for k in range(HY_CONV))


def hyena_shortconv(u, w, b, interpret=False):
    B, L, C = u.shape
    tt = min(256, L)
    hb = tt // SUBLANES
    nt = L // tt
    tile = pl.BlockSpec((1, tt, C), lambda bb, i: (bb, i, 0))
    return pl.pallas_call(
        _hy_shortconv_kernel,
        grid=(B, nt),
        in_specs=[tile,
                  pl.BlockSpec((1, SUBLANES, C), lambda bb, i: (bb, jnp.maximum(i * hb - 1, 0), 0)),
                  pl.BlockSpec((1, SUBLANES, C), lambda bb, i: (bb, jnp.minimum((i + 1) * hb, nt * hb - 1), 0)),
                  pl.BlockSpec((HY_CONV, C), lambda bb, i: (0, 0)), pl.BlockSpec((1, C), lambda bb, i: (0, 0))],
        out_specs=tile,
        out_shape=jax.ShapeDtypeStruct(u.shape, F32),
        compiler_params=pltpu.CompilerParams(dimension_semantics=("arbitrary", "arbitrary")),
        name="hy_shortconv", interpret=interpret,
    )(u, u, u, w.astype(F32), b.reshape(1, C).astype(F32))


def hyena_pallas(u, conv_w, conv_b, w1, b1, w2, b2, w3, freq, decay, skip, cdt=BF16, interpret=False):
    B, L, _ = u.shape
    rows = L // LANES
    n1 = max(2 * rows, SUBLANES)
    n_in = n1 // 2 if n1 == 2 * rows else n1
    uc = hyena_shortconv(u, conv_w, conv_b, interpret=interpret)
    ucm = uc.transpose(0, 2, 1).reshape(B, 3 * D_HY, rows, LANES)
    if n_in != rows:
        ucm = jnp.pad(ucm, ((0, 0), (0, 0), (0, n_in - rows), (0, 0)))
    v, x1, x2 = ucm[:, :D_HY], ucm[:, D_HY:2 * D_HY], ucm[:, 2 * D_HY:]

    k = hyena_filters(L, w1, b1, w2, b2, w3, freq, decay, interpret=interpret)
    k = k.reshape(HY_ORDER, 2, D_HY, L)
    k = k.at[:, 1, :, 0].set(0.0)
    k = k.reshape(2 * HY_ORDER * D_HY, rows, LANES)
    if n_in != rows:
        k = jnp.pad(k, ((0, 0), (0, n_in - rows), (0, 0)))
    sr, si = fft_filter_spectra(k, n1, cdt=cdt, interpret=interpret)
    sr = sr.reshape(HY_ORDER, 2, D_HY, n1, LANES)
    si = si.reshape(HY_ORDER, 2, D_HY, n1, LANES)

    y = v
    for o, gate in enumerate((x1, x2)):
        y = fft_long_conv(y, gate, (sr[o, 0], si[o, 0]), (sr[o, 1], si[o, 1]), skip[o], n1, cdt=cdt, interpret=interpret)
    return y[:, :, :rows].reshape(B, D_HY, L).transpose(0, 2, 1)


ROW_T = CTX_LEN
ROW_NT = SEQ_ALL // ROW_T
MOD_ROWS = SUBLANES
MLA_DK = LANES
ROPE_SWAP = np.arange(MLA_ROPE).reshape(2, 2, MLA_ROPE // 4)[:, ::-1].reshape(-1)


def _rms(x):
    return x * lax.rsqrt(jnp.mean(x * x, axis=-1, keepdims=True) + EPS)


def _mod_kernel(c_ref, w_ref, b_ref, o_ref):
    c = c_ref[...]
    s = (c * jax.nn.sigmoid(c)).astype(BF16)
    o_ref[...] = jnp.dot(s, w_ref[...].astype(BF16), preferred_element_type=F32) + b_ref[...]


def adaln_modulation(c, c_ctx, w_mod, b_mod, interpret=False):
    B, D = c.shape
    cs = jnp.concatenate([c, c_ctx[None], jnp.zeros((MOD_ROWS - B - 1, D), F32)], axis=0)
    out = pl.pallas_call(
        _mod_kernel,
        grid=(6,),
        in_specs=[pl.BlockSpec((MOD_ROWS, D), lambda j: (0, 0)),
                  pl.BlockSpec((D, D), lambda j: (0, j)),
                  pl.BlockSpec((1, D), lambda j: (0, j))],
        out_specs=pl.BlockSpec((MOD_ROWS, D), lambda j: (0, j)),
        out_shape=jax.ShapeDtypeStruct((MOD_ROWS, 6 * D), F32),
        compiler_params=pltpu.CompilerParams(dimension_semantics=("arbitrary",)),
        name="adaln_mod", interpret=interpret,
    )(cs, w_mod, b_mod.reshape(1, 6 * D))
    mod = out.reshape(MOD_ROWS, 6, D)
    return jnp.stack([jnp.broadcast_to(mod[B][None], (B, 6, D)), mod[:B]], axis=1)


IN_COLS = (D_LRU, D_LRU, MLA_Q_RANK, MLA_KV_RANK, 3 * D_HY, 2 * MLA_ROPE)


def _in_proj_kernel(s_ref, m_ref, g_ref, w_ref, *out_refs):
    m = m_ref[0, 0]
    h = _rms(s_ref[0]) * g_ref[...] * (1.0 + m[1:2]) + m[0:1]
    u = jnp.dot(h.astype(w_ref.dtype), w_ref[...], preferred_element_type=F32)
    off = 0
    for ref, n in zip(out_refs, IN_COLS):
        ref[0] = u[:, off:off + n]
        off += n


def in_projection(s, mod, norm_g, w_in, cdt=BF16, interpret=False):
    B, R, D = s.shape
    o = (0,) + tuple(IN_SPLITS) + (D_IN,)
    kr = w_in[:, o[4]:o[5]]
    w = jnp.concatenate([w_in[:, :o[4]], w_in[:, o[5]:], kr, kr[:, ROPE_SWAP]], axis=1).astype(cdt)
    return pl.pallas_call(
        _in_proj_kernel,
        grid=(B, R // ROW_T),
        in_specs=[pl.BlockSpec((1, ROW_T, D), lambda b, i: (b, i, 0)),
                  pl.BlockSpec((1, 1, 6, D), lambda b, i: (b, jnp.minimum(i, 1), 0, 0)),
                  pl.BlockSpec((1, D), lambda b, i: (0, 0)),
                  pl.BlockSpec(w.shape, lambda b, i: (0, 0))],
        out_specs=[pl.BlockSpec((1, ROW_T, n), lambda b, i: (b, i, 0)) for n in IN_COLS],
        out_shape=[jax.ShapeDtypeStruct((B, R, n), F32) for n in IN_COLS],
        compiler_params=pltpu.CompilerParams(
            dimension_semantics=("arbitrary", "arbitrary"), vmem_limit_bytes=VMEM_LIMIT),
        name="in_proj", interpret=interpret,
    )(s, mod, norm_g.reshape(1, D), w)


def _mla_proj_kernel(cq_ref, ckv_ref, kr_ref, rc_ref, rs_ref, qg_ref, wq_ref, wqs_ref, kg_ref, wkv_ref,
                     q_ref, k_ref, v_ref):
    cdt = wq_ref.dtype
    cqn = (_rms(cq_ref[0]) * qg_ref[...]).astype(cdt)
    qa = jnp.dot(cqn, wq_ref[...], preferred_element_type=F32)
    qs = jnp.dot(cqn, wqs_ref[...], preferred_element_type=F32)
    ckvn = (_rms(ckv_ref[0]) * kg_ref[...]).astype(cdt)
    kv = jnp.dot(ckvn, wkv_ref[...], preferred_element_type=F32)
    rc, rs = rc_ref[...], rs_ref[...]
    kr = kr_ref[0]
    k_rope = (kr[:, :MLA_ROPE] * rc[:, MLA_NOPE:MLA_NOPE + MLA_ROPE]
              + kr[:, MLA_ROPE:] * rs[:, MLA_NOPE:MLA_NOPE + MLA_ROPE])
    pad = jnp.zeros((kr.shape[0], MLA_DK - MLA_NOPE - MLA_ROPE), F32)
    for h in range(MLA_HEADS):
        sl = slice(h * MLA_DK, (h + 1) * MLA_DK)
        q_ref[0, h] = ((qa[:, sl] * rc + qs[:, sl] * rs) * MLA_SCALE).astype(q_ref.dtype)
        kvh = kv[:, h * (MLA_NOPE + MLA_V):(h + 1) * (MLA_NOPE + MLA_V)]
        k_ref[0, h] = jnp.concatenate([kvh[:, :MLA_NOPE], k_rope, pad], axis=1).astype(k_ref.dtype)
        v_ref[0, h] = kvh[:, MLA_NOPE:].astype(v_ref.dtype)


def _rope_tables():
    grid_rows = SEQ // GRID_W
    row = jnp.repeat(jnp.arange(grid_rows, dtype=F32), GRID_W)
    col = jnp.tile(jnp.arange(GRID_W, dtype=F32), grid_rows)
    inv_freq = ROPE_THETA ** (-jnp.arange(ROPE_PAIRS, dtype=F32) / ROPE_PAIRS)
    ang_r, ang_c = row[:, None] * inv_freq, col[:, None] * inv_freq
    cos = jnp.concatenate([jnp.cos(ang_r), jnp.cos(ang_r), jnp.cos(ang_c), jnp.cos(ang_c)], axis=1)
    sin = jnp.concatenate([-jnp.sin(ang_r), jnp.sin(ang_r), -jnp.sin(ang_c), jnp.sin(ang_c)], axis=1)
    npad = MLA_DK - MLA_NOPE - MLA_ROPE
    cos = jnp.concatenate([jnp.ones((SEQ, MLA_NOPE), F32), cos, jnp.zeros((SEQ, npad), F32)], axis=1)
    sin = jnp.concatenate([jnp.zeros((SEQ, MLA_NOPE), F32), sin, jnp.zeros((SEQ, npad), F32)], axis=1)
    ctx_cos = jnp.concatenate([jnp.ones((CTX_LEN, MLA_NOPE + MLA_ROPE), F32), jnp.zeros((CTX_LEN, npad), F32)], axis=1)
    return (jnp.concatenate([ctx_cos, cos], axis=0),
            jnp.concatenate([jnp.zeros((CTX_LEN, MLA_DK), F32), sin], axis=0))


def mla_projection(cq, ckv, kr2, q_g, wqb, kv_g, wkvb, cdt=BF16, interpret=False):
    B, R, _ = cq.shape
    dq = MLA_NOPE + MLA_ROPE
    wq3 = wqb.reshape(MLA_Q_RANK, MLA_HEADS, dq)
    zq = jnp.zeros((MLA_Q_RANK, MLA_HEADS, MLA_DK - dq), wqb.dtype)
    wq = jnp.concatenate([wq3, zq], axis=2).reshape(MLA_Q_RANK, MLA_HEADS * MLA_DK).astype(cdt)
    wqs = jnp.concatenate([jnp.zeros((MLA_Q_RANK, MLA_HEADS, MLA_NOPE), wqb.dtype),
                           wq3[:, :, MLA_NOPE:][:, :, ROPE_SWAP], zq], axis=2)
    wqs = wqs.reshape(MLA_Q_RANK, MLA_HEADS * MLA_DK).astype(cdt)
    rc, rs = _rope_tables()
    row = lambda n: pl.BlockSpec((1, ROW_T, n), lambda b, i: (b, i, 0))
    tab = pl.BlockSpec((ROW_T, MLA_DK), lambda b, i: (i, 0))
    full = lambda shape: pl.BlockSpec(shape, lambda b, i: (0,) * len(shape))
    head = lambda n: pl.BlockSpec((1, MLA_HEADS, ROW_T, n), lambda b, i: (b, 0, i, 0))
    return pl.pallas_call(
        _mla_proj_kernel,
        grid=(B, R // ROW_T),
        in_specs=[row(MLA_Q_RANK), row(MLA_KV_RANK), row(2 * MLA_ROPE), tab, tab,
                  full((1, MLA_Q_RANK)), full(wq.shape), full(wqs.shape),
                  full((1, MLA_KV_RANK)), full(wkvb.shape)],
        out_specs=[head(MLA_DK), head(MLA_DK), head(MLA_V)],
        out_shape=[jax.ShapeDtypeStruct((B, MLA_HEADS, R, MLA_DK), cdt),
                   jax.ShapeDtypeStruct((B, MLA_HEADS, R, MLA_DK), cdt),
                   jax.ShapeDtypeStruct((B, MLA_HEADS, R, MLA_V), cdt)],
        compiler_params=pltpu.CompilerParams(dimension_semantics=("arbitrary", "arbitrary")),
        name="mla_proj", interpret=interpret,
    )(cq, ckv, kr2, rc, rs, q_g.reshape(1, -1), wq, wqs, kv_g.reshape(1, -1), wkvb.astype(cdt))


def _out_proj_kernel(s_ref, m_ref, g_ref, ya_ref, ybx_ref, ybc_ref, yc_ref, w_ref, snew_ref, hT_ref):
    cdt = w_ref.dtype
    yb = jnp.where(pl.program_id(1) == 0, ybc_ref[0], ybx_ref[0])
    mix = (jnp.dot(ya_ref[0].astype(cdt), w_ref[0:D_LRU, :], preferred_element_type=F32)
           + jnp.dot(yb.astype(cdt), w_ref[D_LRU:D_LRU + MLA_HEADS * MLA_V, :], preferred_element_type=F32)
           + jnp.dot(yc_ref[0].astype(cdt), w_ref[D_LRU + MLA_HEADS * MLA_V:, :], preferred_element_type=F32))
    m = m_ref[0, 0]
    s_new = s_ref[0] + m[2:3] * mix
    snew_ref[0] = s_new
    h2 = _rms(s_new) * g_ref[...] * (1.0 + m[4:5]) + m[3:4]
    hT_ref[...] = h2.T.astype(hT_ref.dtype)


def out_projection(s, mod, norm2_g, ya, yb_x, yb_c, yc, w_out, cdt=BF16, interpret=False):
    B, R, D = s.shape
    nt = R // ROW_T
    row = lambda n: pl.BlockSpec((1, ROW_T, n), lambda b, i: (b, i, 0))
    return pl.pallas_call(
        _out_proj_kernel,
        grid=(B, nt),
        in_specs=[row(D),
                  pl.BlockSpec((1, 1, 6, D), lambda b, i: (b, jnp.minimum(i, 1), 0, 0)),
                  pl.BlockSpec((1, D), lambda b, i: (0, 0)),
                  row(D_LRU),
                  pl.BlockSpec((1, ROW_T, MLA_HEADS * MLA_V), lambda b, i: (b, jnp.maximum(i - 1, 0), 0)),
                  pl.BlockSpec((1, ROW_T, MLA_HEADS * MLA_V), lambda b, i: (b, 0, 0)),
                  row(D_HY),
                  pl.BlockSpec((D_MIX, D), lambda b, i: (0, 0))],
        out_specs=[row(D), pl.BlockSpec((D, ROW_T), lambda b, i: (0, b * nt + i))],
        out_shape=[jax.ShapeDtypeStruct((B, R, D), F32), jax.ShapeDtypeStruct((D, B * R), cdt)],
        compiler_params=pltpu.CompilerParams(
            dimension_semantics=("arbitrary", "arbitrary"), vmem_limit_bytes=VMEM_LIMIT),
        name="out_proj", interpret=interpret,
    )(s, mod, norm2_g.reshape(1, D), ya, yb_x, yb_c, yc, w_out.astype(cdt))


PEER_TM = 768
PEER_TE = 1024
PEER_KP = 512
NEG_BIG = -3.0e38
PACKED_ROWS = 2 * SUBLANES


def _topk_desc(s, k):
    rank = jnp.full(s.shape, float(PEER_NKEYS), F32)
    vals = []
    for r in range(k):
        m = jnp.max(s, axis=0, keepdims=True)
        hit = s == m
        rank = jnp.where(hit, float(r), rank)
        s = jnp.where(hit, NEG_BIG, s)
        vals.append(m)
    return jnp.concatenate(vals, axis=0), rank


def _route_chunk(s1, s2):
    K = PEER_TOPK
    v1, rank1 = _topk_desc(s1, K)
    v2, rank2 = _topk_desc(s2, K)
    half = K // 2
    cands = ([v1[0:1] + v2] + [v1[i:i + 1] + v2[:half] for i in range(1, half)]
             + [v1[i:i + 1] + v2[0:1] for i in range(half, K)])
    cmax = v1[0:1] + v2[0:1]
    c = jnp.concatenate(cands[:half] + [v1[half:] + v2[0:1]], axis=0)
    z = jnp.zeros_like(cmax)
    m = cmax
    for r in range(K):
        m = jnp.max(c, axis=0, keepdims=True)
        z = z + jnp.exp(m - cmax)
        if r + 1 < K:
            c = jnp.where(c == m, NEG_BIG, c)
    thr = m
    n = jnp.zeros_like(s1)
    for i in range(K):
        n_i = jnp.sum(jnp.where(cands[i] >= thr, 1.0, 0.0), axis=0, keepdims=True)
        n = jnp.where(rank1 == float(i), n_i, n)
    w = jnp.exp(s1 - v1[0:1]) / z
    e2 = jnp.exp(s2 - v2[0:1])
    return rank2, e2, n, w


def _peer_route_kernel(hT_ref, wqT_ref, keys_ref, rank2_ref, e2_ref, n_ref, w_ref, q_scr, s1_scr, s2_scr):
    h = pl.program_id(1)

    @pl.when(h == 0)
    def _():
        q_scr[...] = jnp.dot(wqT_ref[...], hT_ref[...], preferred_element_type=F32)

    cdt = keys_ref.dtype
    half = PEER_DQ // 2
    row = pl.multiple_of(h * PEER_DQ, PEER_DQ)
    q1 = q_scr[pl.ds(row, half), :].astype(cdt)
    q2 = q_scr[pl.ds(row + half, half), :].astype(cdt)
    s1_scr[...] = jnp.dot(keys_ref[0], q1, preferred_element_type=F32)
    s2_scr[...] = jnp.dot(keys_ref[1], q2, preferred_element_type=F32)

    def chunk(c, carry):
        sl = pl.ds(pl.multiple_of(c * LANES, LANES), LANES)
        rank2, e2, n, w = _route_chunk(s1_scr[:, sl], s2_scr[:, sl])
        rank2_ref[0, :, sl] = rank2.astype(rank2_ref.dtype)
        e2_ref[0, :, sl] = e2.astype(e2_ref.dtype)
        n_ref[0, :, sl] = n
        w_ref[0, :, sl] = w
        return carry

    lax.fori_loop(0, s1_scr.shape[1] // LANES, chunk, 0)


def _peer_dense_kernel(hT_ref, rank2_ref, e2_ref, n_ref, w_ref, u_ref, vT_ref, s_ref, g_ref, o_ref, acc_ref, pre_ref):
    j = pl.program_id(1)
    n_chunks = pl.num_programs(1) - 1

    @pl.when(j == 0)
    def _():
        acc_ref[...] = jnp.zeros_like(acc_ref)
        pre_ref[1] = jnp.zeros(pre_ref.shape[1:], pre_ref.dtype)

    te, tm = u_ref.shape[0], hT_ref.shape[1]
    gdt = rank2_ref.dtype

    def step(rd, wr):
        pre_ref[wr] = jnp.dot(u_ref[...], hT_ref[...], preferred_element_type=F32)
        out = None
        for k0 in range(0, te, PEER_KP):
            acts = []
            for al in range(k0 // PEER_NKEYS, (k0 + PEER_KP) // PEER_NKEYS):
                g = None
                for h in range(PEER_HEADS):
                    n16 = jnp.broadcast_to(n_ref[h, 0, al:al + 1, :], (PACKED_ROWS, tm)).astype(gdt)
                    w16 = jnp.broadcast_to(w_ref[h, 0, al:al + 1, :], (PACKED_ROWS, tm)).astype(gdt)
                    nb = jnp.concatenate([n16] * (PEER_NKEYS // PACKED_ROWS), axis=0)
                    wb = jnp.concatenate([w16] * (PEER_NKEYS // PACKED_ROWS), axis=0)
                    t = jnp.where(rank2_ref[h] < nb, e2_ref[h] * wb, jnp.zeros_like(wb))
                    g = t if g is None else g + t
                blk = slice(al * PEER_NKEYS, (al + 1) * PEER_NKEYS)
                acts.append((g * _gelu_tanh(pre_ref[rd, blk, :]).astype(gdt)).astype(vT_ref.dtype))
            part = jnp.dot(vT_ref[0, :, k0:k0 + PEER_KP], jnp.concatenate(acts, axis=0), preferred_element_type=F32)
            out = part if out is None else out + part
        acc_ref[...] += out

    for parity in range(2):
        pl.when(j % 2 == parity)(functools.partial(step, 1 - parity, parity))

    @pl.when(j == n_chunks)
    def _():
        y = acc_ref[...].T
        for r in range(g_ref.shape[0]):
            rows = slice(r * ROW_T, (r + 1) * ROW_T)
            o_ref[rows, :] = s_ref[rows, :] + g_ref[r] * y[rows, :]


def peer_tokens(hT, s, gate, wq, keys, u_tab, v_tab, interpret=False):
    D, T = hT.shape
    cdt = hT.dtype
    tm, te = PEER_TM, PEER_TE
    assert T % tm == 0 and PEER_EXPERTS % te == 0 and tm % ROW_T == 0
    wqT = wq.T.astype(cdt)
    route_shape = jax.ShapeDtypeStruct((PEER_HEADS, PEER_NKEYS, T), F32)
    packed_shape = jax.ShapeDtypeStruct((PEER_HEADS, PEER_NKEYS, T), cdt)
    route_spec = pl.BlockSpec((1, PEER_NKEYS, tm), lambda i, hh: (hh, 0, i))
    rank2, e2, n, w = pl.pallas_call(
        _peer_route_kernel,
        grid=(T // tm, PEER_HEADS),
        in_specs=[pl.BlockSpec((D, tm), lambda i, hh: (0, i)),
                  pl.BlockSpec((PEER_HEADS * PEER_DQ, D), lambda i, hh: (0, 0)),
                  pl.BlockSpec((2, PEER_NKEYS, PEER_DQ // 2), lambda i, hh: (0, 0, 0))],
        out_specs=[route_spec] * 4,
        out_shape=[packed_shape, packed_shape, route_shape, route_shape],
        scratch_shapes=[pltpu.VMEM((PEER_HEADS * PEER_DQ, tm), F32),
                        pltpu.VMEM((PEER_NKEYS, tm), F32),
                        pltpu.VMEM((PEER_NKEYS, tm), F32)],
        compiler_params=pltpu.CompilerParams(
            dimension_semantics=("arbitrary", "arbitrary"), vmem_limit_bytes=VMEM_LIMIT),
        name="peer_route", interpret=interpret,
    )(hT, wqT, keys.astype(cdt))

    head_spec = pl.BlockSpec((PEER_HEADS, PEER_NKEYS, tm), lambda i, j: (0, 0, i))
    nj = PEER_EXPERTS // te
    groups = te // PEER_NKEYS
    row_spec = pl.BlockSpec((PEER_HEADS, 1, groups, tm), lambda i, j: (0, jnp.maximum(j - 1, 0), 0, i))
    n = n.reshape(PEER_HEADS, nj, groups, T)
    w = w.reshape(PEER_HEADS, nj, groups, T)
    vT = v_tab.astype(cdt).reshape(nj, te, D).transpose(0, 2, 1)
    return pl.pallas_call(
        _peer_dense_kernel,
        grid=(T // tm, nj + 1),
        in_specs=[pl.BlockSpec((D, tm), lambda i, j: (0, i)),
                  head_spec, head_spec, row_spec, row_spec,
                  pl.BlockSpec((te, D), lambda i, j: (jnp.minimum(j, nj - 1), 0)),
                  pl.BlockSpec((1, D, te), lambda i, j: (jnp.maximum(j - 1, 0), 0, 0)),
                  pl.BlockSpec((tm, D), lambda i, j: (i, 0)),
                  pl.BlockSpec((tm // ROW_T, 1, D), lambda i, j: (i, 0, 0))],
        out_specs=pl.BlockSpec((tm, D), lambda i, j: (i, 0)),
        out_shape=jax.ShapeDtypeStruct((T, D), F32),
        scratch_shapes=[pltpu.VMEM((D, tm), F32), pltpu.VMEM((2, te, tm), F32)],
        compiler_params=pltpu.CompilerParams(
            dimension_semantics=("arbitrary", "arbitrary"), vmem_limit_bytes=VMEM_LIMIT),
        name="peer_dense", interpret=interpret,
    )(hT, rank2, e2, n, w, u_tab.astype(cdt), vT, s, gate)


def _final_norm_kernel(x_ref, g_ref, o_ref):
    o_ref[0] = _rms(x_ref[0]) * g_ref[...]


def final_norm(s, g):
    B, R, D = s.shape
    return pl.pallas_call(
        _final_norm_kernel,
        grid=(B, SEQ // ROW_T),
        in_specs=[pl.BlockSpec((1, ROW_T, D), lambda b, i: (b, i + CTX_LEN // ROW_T, 0)),
                  pl.BlockSpec((1, D), lambda b, i: (0, 0))],
        out_specs=pl.BlockSpec((1, ROW_T, D), lambda b, i: (b, i, 0)),
        out_shape=jax.ShapeDtypeStruct((B, SEQ, D), F32),
        compiler_params=pltpu.CompilerParams(dimension_semantics=("arbitrary", "arbitrary")),
        name="final_norm",
    )(s, g.reshape(1, D))


def kernel(x, c, ctx, c_ctx, w_mod, b_mod, norm1_g, norm2_g, w_in, w_out,
           lru_conv_w, lru_conv_b, lru_wr, lru_br, lru_wi, lru_bi, lru_lambda,
           mla_q_norm_g, mla_wqb, mla_kv_norm_g, mla_wkvb,
           hy_conv_w, hy_conv_b, hy_f_w1, hy_f_b1, hy_f_w2, hy_f_b2, hy_f_w3,
           hy_f_freq, hy_decay, hy_skip,
           peer_wq, peer_keys, peer_u, peer_v, final_g):
    B = x.shape[0]
    s = jnp.concatenate([ctx, x], axis=1)
    for l in range(DEPTH):
        need_ctx = l < DEPTH - 1
        mod = adaln_modulation(c, c_ctx, w_mod[l], b_mod[l])
        u_lru, u_gate, cq, ckv, u_hy, kr2 = in_projection(s, mod, norm1_g[l], w_in[l])

        ya = rglru_mixer(u_lru, u_gate, lru_conv_w[l], lru_conv_b[l], lru_wr[l], lru_br[l], lru_wi[l], lru_bi[l],
                         lru_lambda[l])

        q, k, v = mla_projection(cq, ckv, kr2, mla_q_norm_g[l], mla_wqb[l], mla_kv_norm_g[l], mla_wkvb[l])
        yb_x = flash_attention(q, k, v, q_row0=CTX_LEN, n_q=SEQ, n_k=SEQ_ALL, tk=SEQ_ALL // 3)
        if need_ctx:
            yb_c = flash_attention(q, k, v, q_row0=0, n_q=CTX_LEN, n_k=CTX_LEN, tk=CTX_LEN)
        else:
            yb_c = jnp.zeros((B, CTX_LEN, MLA_HEADS * MLA_V), F32)

        hy_args = (hy_conv_w[l], hy_conv_b[l], hy_f_w1[l], hy_f_b1[l], hy_f_w2[l], hy_f_b2[l], hy_f_w3[l],
                   hy_f_freq[l], hy_decay[l], hy_skip[l])
        yc_x = hyena_pallas(u_hy[:, CTX_LEN:], *hy_args)
        yc_c = hyena_pallas(u_hy[:, :CTX_LEN], *hy_args) if need_ctx else jnp.zeros((B, CTX_LEN, D_HY), F32)
        yc = jnp.concatenate([yc_c, yc_x], axis=1)

        s, hT = out_projection(s, mod, norm2_g[l], ya, yb_x, yb_c, yc, w_out[l])
        gate2 = jnp.repeat(mod[:, :, 5], jnp.array([1, ROW_NT - 1]), axis=1, total_repeat_length=ROW_NT)
        s = peer_tokens(hT, s.reshape(B * SEQ_ALL, D_MODEL), gate2.reshape(B * ROW_NT, 1, D_MODEL),
                        peer_wq[l], peer_keys[l], peer_u[l], peer_v[l]).reshape(B, SEQ_ALL, D_MODEL)
    return final_norm(s, final_g)
```

```python
import functools
import math

import jax
import jax.numpy as jnp
import numpy as np
from jax import lax
from jax.experimental import pallas as pl
from jax.experimental.pallas import tpu as pltpu

D_MODEL = 1024
BATCH = 2
SEQ = 8192
DEPTH = 2

GRID_W = 64
CTX_LEN = 256
EPS = 1e-6

D_LRU = 384
LRU_BLOCKS = 6
LRU_BLOCK_DIM = D_LRU // LRU_BLOCKS
LRU_CONV = 4
LRU_C = 8.0

MLA_HEADS = 6
MLA_NOPE = 64
MLA_ROPE = 32
MLA_V = 64
MLA_Q_RANK = 384
MLA_KV_RANK = 256
MLA_SCALE = (MLA_NOPE + MLA_ROPE) ** -0.5
ROPE_PAIRS = MLA_ROPE // 4
ROPE_THETA = 10000.0
ATTN_BLOCK = 128

D_HY = 256
HY_CONV = 3
HY_ORDER = 2
HY_BANDS = 8
HY_EMB = 1 + 2 * HY_BANDS
HY_FFN = 64

D_MIX = D_LRU + MLA_HEADS * MLA_V + D_HY
D_IN = 2 * D_LRU + MLA_Q_RANK + MLA_KV_RANK + MLA_ROPE + 3 * D_HY
IN_SPLITS = [D_LRU, 2 * D_LRU, 2 * D_LRU + MLA_Q_RANK,
             2 * D_LRU + MLA_Q_RANK + MLA_KV_RANK,
             2 * D_LRU + MLA_Q_RANK + MLA_KV_RANK + MLA_ROPE]

PEER_HEADS = 8
PEER_NKEYS = 128
PEER_EXPERTS = PEER_NKEYS * PEER_NKEYS
PEER_DQ = 256
PEER_TOPK = 16
PEER_BLOCK = 128

F32 = jnp.float32


BF16 = jnp.bfloat16
LANES = 128
SUBLANES = 8
VMEM_LIMIT = 56 * 1024 * 1024
SEQ_ALL = CTX_LEN + SEQ
LRU_TT = CTX_LEN
LRU_NT = SEQ_ALL // LRU_TT


def _gelu_tanh(x):
    return 0.5 * x * (1.0 + jnp.tanh(0.7978845608028654 * (x + 0.044715 * (x * x * x))))


def _lru_coeff_kernel(u_ref, up_ref, un_ref, cw_ref, cb_ref, wg_ref, bg_ref, c_ref,
                      af_ref, bf_ref, ab_ref, bb_ref):
    i = pl.program_id(1)
    x = u_ref[0]
    tt = x.shape[0]
    use_prev = i >= 2
    use_next = jnp.logical_and(i >= 1, i < LRU_NT - 1)
    p2 = jnp.where(use_prev, up_ref[0][SUBLANES - 2:SUBLANES], 0.0)
    n1 = jnp.where(use_next, un_ref[0][0:1], 0.0)
    ext = jnp.concatenate([p2, x, n1], axis=0)
    xc = cb_ref[...] + sum(cw_ref[k:k + 1, :] * ext[k:k + tt] for k in range(LRU_CONV))
    g = jnp.dot(xc.astype(wg_ref.dtype), wg_ref[...], preferred_element_type=F32) + bg_ref[...]
    for d, (a_ref, b_ref) in enumerate(((af_ref, bf_ref), (ab_ref, bb_ref))):
        r = jax.nn.sigmoid(g[:, (2 * d) * D_LRU:(2 * d + 1) * D_LRU])
        gi = jax.nn.sigmoid(g[:, (2 * d + 1) * D_LRU:(2 * d + 2) * D_LRU])
        log_a = c_ref[d:d + 1, :] * r
        a_ref[0] = jnp.exp(log_a)
        b_ref[0] = jnp.sqrt(1.0 - jnp.exp(2.0 * log_a)) * (gi * xc)


def _lru_scan_rows(a_ref, b_ref, h_scr, emit, reverse):
    tt = a_ref.shape[1]
    nb = a_ref.shape[0]

    def body(s, hs):
        t = (tt - 1 - s) if reverse else s
        out = []
        for b in range(nb):
            h = a_ref[b, pl.ds(t, 1), :] * hs[b] + b_ref[b, pl.ds(t, 1), :]
            emit(b, t, h)
            out.append(h)
        return tuple(out)

    hs = lax.fori_loop(0, tt, body, tuple(h_scr[b:b + 1, :] for b in range(nb)), unroll=8)
    for b in range(nb):
        h_scr[b:b + 1, :] = hs[b]


def _lru_fwd_kernel(a_ref, b_ref, hf_ref, h_scr):
    @pl.when(pl.program_id(0) == 0)
    def _():
        h_scr[...] = jnp.zeros_like(h_scr)

    def emit(b, t, h):
        hf_ref[b, pl.ds(t, 1), :] = h

    _lru_scan_rows(a_ref, b_ref, h_scr, emit, reverse=False)


def _lru_bwd_kernel(a_ref, b_ref, hf_ref, gate_ref, y_ref, h_scr):
    @pl.when(pl.program_id(0) == 0)
    def _():
        h_scr[...] = jnp.zeros_like(h_scr)

    def emit(b, t, h):
        y_ref[b, pl.ds(t, 1), :] = h

    _lru_scan_rows(a_ref, b_ref, h_scr, emit, reverse=True)
    y_ref[...] = (y_ref[...] + hf_ref[...]) * _gelu_tanh(gate_ref[...])


def rglru_mixer(u_lru, u_gate, conv_w, conv_b, wr, br, wi, bi, lam, cdt=BF16, interpret=False):
    B = u_lru.shape[0]
    tt, nt = LRU_TT, LRU_NT
    hb = tt // SUBLANES

    def blockdiag(w):
        eye = jnp.eye(LRU_BLOCKS, dtype=w.dtype)
        return jnp.einsum('nde,nm->ndme', w, eye).reshape(D_LRU, D_LRU)

    wg = jnp.concatenate([blockdiag(wr[0]), blockdiag(wi[0]), blockdiag(wr[1]), blockdiag(wi[1])], axis=1).astype(cdt)
    bg = jnp.concatenate([br[0], bi[0], br[1], bi[1]]).reshape(1, 4 * D_LRU).astype(F32)
    cdec = -LRU_C * jax.nn.softplus(-lam.astype(F32))

    tile = pl.BlockSpec((1, tt, D_LRU), lambda b, i: (b, i, 0))
    full = lambda shape: pl.BlockSpec(shape, lambda b, i: (0,) * len(shape))
    coeff_shape = jax.ShapeDtypeStruct((B, SEQ_ALL, D_LRU), F32)
    a_f, b_f, a_b, b_b = pl.pallas_call(
        _lru_coeff_kernel,
        grid=(B, nt),
        in_specs=[tile,
                  pl.BlockSpec((1, SUBLANES, D_LRU), lambda b, i: (b, jnp.maximum(i * hb - 1, 0), 0)),
                  pl.BlockSpec((1, SUBLANES, D_LRU), lambda b, i: (b, jnp.minimum((i + 1) * hb, nt * hb - 1), 0)),
                  full((LRU_CONV, D_LRU)), full((1, D_LRU)), full((D_LRU, 4 * D_LRU)), full((1, 4 * D_LRU)),
                  full((2, D_LRU))],
        out_specs=[tile] * 4,
        out_shape=[coeff_shape] * 4,
        compiler_params=pltpu.CompilerParams(dimension_semantics=("arbitrary", "arbitrary")),
        name="lru_coeff", interpret=interpret,
    )(u_lru, u_lru, u_lru, conv_w.astype(F32), conv_b.reshape(1, D_LRU).astype(F32), wg, bg, cdec)

    seq = pl.BlockSpec((B, tt, D_LRU), lambda s: (0, s, 0))
    h_f = pl.pallas_call(
        _lru_fwd_kernel,
        grid=(nt,),
        in_specs=[seq, seq],
        out_specs=seq,
        out_shape=coeff_shape,
        scratch_shapes=[pltpu.VMEM((B, D_LRU), F32)],
        compiler_params=pltpu.CompilerParams(dimension_semantics=("arbitrary",)),
        name="lru_fwd", interpret=interpret,
    )(a_f, b_f)

    rseq = pl.BlockSpec((B, tt, D_LRU), lambda s: (0, jnp.where(s == 0, 0, nt - s), 0))
    return pl.pallas_call(
        _lru_bwd_kernel,
        grid=(nt,),
        in_specs=[rseq, rseq, rseq, rseq],
        out_specs=rseq,
        out_shape=coeff_shape,
        scratch_shapes=[pltpu.VMEM((B, D_LRU), F32)],
        compiler_params=pltpu.CompilerParams(dimension_semantics=("arbitrary",)),
        name="lru_bwd", interpret=interpret,
    )(a_b, b_b, h_f, u_gate)


ATT_TQ = 256


ATT_HG = 2


def _flash_kernel(*refs, tk):
    q_refs, (k_ref, v_ref, o_ref) = refs[:-3], refs[-3:]
    hg, tq = q_refs[0].shape[1], q_refs[0].shape[2]
    dv = v_ref.shape[3] // 2
    n_chunks = k_ref.shape[2] // tk
    chains = [(q_ref[0, h], h) for q_ref in q_refs for h in range(hg)]

    def body(c, carry):
        off = pl.multiple_of(c * tk, LANES)
        out = []
        for (q, h), (m, acc) in zip(chains, carry):
            s = lax.dot_general(q, k_ref[0, h, pl.ds(off, tk), :], (((1,), (1,)), ((), ())),
                                preferred_element_type=F32)
            m_new = jnp.maximum(m, jnp.max(s, axis=1, keepdims=True))
            p = jnp.exp(s - m_new).astype(v_ref.dtype)
            pv = jnp.dot(p, v_ref[0, h, pl.ds(off, tk), :], preferred_element_type=F32)
            out.append((m_new, jnp.exp(m - m_new) * acc + pv))
        return tuple(out)

    init = tuple((jnp.full((tq, 1), NEG_BIG, F32), jnp.zeros((tq, 2 * dv), F32)) for _ in chains)
    res = lax.fori_loop(0, n_chunks, body, init, unroll=True)
    for t in range(len(q_refs)):
        o_ref[0, t * tq:(t + 1) * tq, :] = jnp.concatenate(
            [acc[:, :dv] / acc[:, dv:] for _, acc in res[t * hg:(t + 1) * hg]], axis=1)


def flash_attention(q, k, v, q_row0, n_q, n_k, tk, n_sub, interpret=False):
    B, H, R, dk = q.shape
    dv = v.shape[3] // 2
    tq = ATT_TQ
    assert n_q % (n_sub * tq) == 0 and q_row0 % tq == 0 and n_k % tk == 0 and tk % LANES == 0 and R % n_k == 0
    q0 = q_row0 // tq
    q_spec = lambda t: pl.BlockSpec((1, ATT_HG, tq, dk), lambda b, g, i: (b, g, n_sub * i + t + q0, 0))
    return pl.pallas_call(
        functools.partial(_flash_kernel, tk=tk),
        grid=(B, H // ATT_HG, n_q // (n_sub * tq)),
        in_specs=[q_spec(t) for t in range(n_sub)]
                 + [pl.BlockSpec((1, ATT_HG, n_k, dk), lambda b, g, i: (b, g, 0, 0)),
                    pl.BlockSpec((1, ATT_HG, n_k, 2 * dv), lambda b, g, i: (b, g, 0, 0))],
        out_specs=pl.BlockSpec((1, n_sub * tq, ATT_HG * dv), lambda b, g, i: (b, i, g)),
        out_shape=jax.ShapeDtypeStruct((B, n_q, H * dv), F32),
        compiler_params=pltpu.CompilerParams(
            dimension_semantics=("arbitrary", "arbitrary", "arbitrary"), vmem_limit_bytes=VMEM_LIMIT),
        name="flash_attention", interpret=interpret,
    )(*([q] * n_sub), k, v)


HY_CB = 32
HY_FR = 128
HY_UNROLL = 8


def _dft_tables(n1, n_in, n_out, cdt):
    n = n1 * LANES
    k1 = np.arange(n1)[:, None]
    f1 = np.exp(-2j * np.pi * k1 * np.arange(n_in)[None, :] / n1)
    tw = np.exp(-2j * np.pi * k1 * np.arange(LANES)[None, :] / n)
    f2 = np.exp(-2j * np.pi * np.outer(np.arange(LANES), np.arange(LANES)) / LANES)
    g1 = np.conj(np.exp(-2j * np.pi * k1 * np.arange(n_out)[None, :] / n1)).T / n
    blk = lambda m: np.block([[m.real, -m.imag], [m.imag, m.real]])
    return dict(
        ma=jnp.asarray(blk(f1), cdt),
        ma_re=jnp.asarray(np.concatenate([f1.real, f1.imag], axis=0), cdt),
        tr=jnp.asarray(tw.real, F32), ti=jnp.asarray(tw.imag, F32),
        f2=jnp.asarray(np.concatenate([f2.real, f2.imag], axis=1), cdt),
        f2c=jnp.asarray(np.concatenate([f2.real, -f2.imag], axis=1), cdt),
        mi=jnp.asarray(blk(g1), cdt),
    )


def _cmul_right(re, im, f_ref):
    n = re.shape[0]
    p = jnp.dot(jnp.concatenate([re, im], axis=0).astype(f_ref.dtype), f_ref[...], preferred_element_type=F32)
    return p[:n, :LANES] - p[n:, LANES:], p[:n, LANES:] + p[n:, :LANES]


def _fft_filter_kernel(k_ref, ma_ref, tr_ref, ti_ref, f2_ref, xr_ref, xi_ref):
    n1 = tr_ref.shape[0]

    def body(c, carry):
        o = jnp.dot(ma_ref[...], k_ref[c].astype(ma_ref.dtype), preferred_element_type=F32)
        ar, ai = o[:n1], o[n1:]
        xr, xi = _cmul_right(ar * tr_ref[...] - ai * ti_ref[...], ar * ti_ref[...] + ai * tr_ref[...], f2_ref)
        xr_ref[c] = xr
        xi_ref[c] = xi
        return carry

    lax.fori_loop(0, k_ref.shape[0], body, 0, unroll=HY_UNROLL)


def _fft_conv_kernel(z_ref, gate_ref, kfr_ref, kfi_ref, kbr_ref, kbi_ref, skip_ref,
                     ma_ref, tr_ref, ti_ref, f2_ref, f2c_ref, mi_ref, o_ref):
    n1 = tr_ref.shape[0]
    n_out = o_ref.shape[2]

    def body(c, carry):
        z0, z1 = z_ref[0, c], z_ref[1, c]
        o = jnp.dot(ma_ref[...], jnp.concatenate([z0, z1], axis=0).astype(ma_ref.dtype), preferred_element_type=F32)
        ar, ai = o[:n1], o[n1:]
        tr, ti = tr_ref[...], ti_ref[...]
        xr, xi = _cmul_right(ar * tr - ai * ti, ar * ti + ai * tr, f2_ref)
        hr = kfr_ref[c] + kbr_ref[c]
        hi = kfi_ref[c] - kbi_ref[c]
        br, bi = _cmul_right(xr * hr - xi * hi, xr * hi + xi * hr, f2c_ref)
        b2 = jnp.concatenate([br * tr + bi * ti, bi * tr - br * ti], axis=0).astype(mi_ref.dtype)
        y = jnp.dot(mi_ref[...], b2, preferred_element_type=F32)
        sk = skip_ref[c]
        o_ref[0, c] = gate_ref[0, c] * (y[:n_out] + sk * z0[:n_out])
        o_ref[1, c] = gate_ref[1, c] * (y[n_out:] + sk * z1[:n_out])
        return carry

    lax.fori_loop(0, z_ref.shape[1], body, 0, unroll=HY_UNROLL)


def fft_filter_spectra(k_rows, n1, cdt=BF16, interpret=False):
    R, n_in, _ = k_rows.shape
    t = _dft_tables(n1, n_in, n_in, cdt)
    full = lambda a: pl.BlockSpec(a.shape, lambda i: (0,) * a.ndim)
    spec = jax.ShapeDtypeStruct((R, n1, LANES), F32)
    consts = (t["ma_re"], t["tr"], t["ti"], t["f2"])
    return pl.pallas_call(
        _fft_filter_kernel,
        grid=(R // HY_CB,),
        in_specs=[pl.BlockSpec((HY_CB, n_in, LANES), lambda i: (i, 0, 0))] + [full(a) for a in consts],
        out_specs=[pl.BlockSpec((HY_CB, n1, LANES), lambda i: (i, 0, 0))] * 2,
        out_shape=[spec, spec],
        compiler_params=pltpu.CompilerParams(dimension_semantics=("arbitrary",)),
        name="fft_filter", interpret=interpret,
    )(k_rows, *consts)


def fft_long_conv(z, gate, kf, kb, skip, n1, cdt=BF16, interpret=False):
    B, C, n_in, _ = z.shape
    assert B == 2 and C % HY_CB == 0
    t = _dft_tables(n1, n_in, n_in, cdt)
    full = lambda a: pl.BlockSpec(a.shape, lambda i: (0,) * a.ndim)
    sig = pl.BlockSpec((B, HY_CB, n_in, LANES), lambda i: (0, i, 0, 0))
    spc = pl.BlockSpec((HY_CB, n1, LANES), lambda i: (i, 0, 0))
    consts = (t["ma"], t["tr"], t["ti"], t["f2"], t["f2c"], t["mi"])
    skip_b = jnp.broadcast_to(skip.astype(F32)[:, None, None], (C, 1, LANES))
    return pl.pallas_call(
        _fft_conv_kernel,
        grid=(C // HY_CB,),
        in_specs=[sig, sig, spc, spc, spc, spc, pl.BlockSpec((HY_CB, 1, LANES), lambda i: (i, 0, 0))]
                 + [full(a) for a in consts],
        out_specs=sig,
        out_shape=jax.ShapeDtypeStruct(z.shape, F32),
        compiler_params=pltpu.CompilerParams(dimension_semantics=("arbitrary",)),
        name="fft_conv", interpret=interpret,
    )(z, gate, kf[0], kf[1], kb[0], kb[1], skip_b, *consts)


def _hy_filter_kernel(z_ref, w1_ref, b1_ref, w2_ref, b2_ref, fr_ref, w3_ref, dec_ref, k_ref, h_scr):
    hp = lax.Precision.HIGHEST

    @pl.when(pl.program_id(0) == 0)
    def _():
        h = jnp.sin(fr_ref[...] * (jnp.dot(w1_ref[...], z_ref[...], precision=hp, preferred_element_type=F32) + b1_ref[...]))
        h_scr[...] = jnp.sin(fr_ref[...] * (jnp.dot(w2_ref[...], h, precision=hp, preferred_element_type=F32) + b2_ref[...]))

    t = z_ref[0:1, :]
    k = jnp.dot(w3_ref[...], h_scr[...], precision=hp, preferred_element_type=F32) * jnp.exp(-t * dec_ref[...])
    k_ref[...] = k * lax.rsqrt(jnp.sum(k * k, axis=1, keepdims=True) + EPS)


def hyena_filters(L, w1, b1, w2, b2, w3, freq, decay, interpret=False):
    t = jnp.arange(L, dtype=F32) / L
    bands = jnp.linspace(1e-4, HY_BANDS - 1, HY_BANDS, dtype=F32)
    wpos = (2.0 * math.pi) * t[None, :] * bands[:, None]
    emb = 24
    z = jnp.concatenate([t[None, :], jnp.cos(wpos), jnp.sin(wpos), jnp.zeros((emb - HY_EMB, L), F32)], axis=0)
    w1p = jnp.concatenate([w1.T, jnp.zeros((HY_FFN, emb - HY_EMB), F32)], axis=1)
    R = 2 * HY_ORDER * D_HY
    col = lambda v: v.reshape(-1, 1).astype(F32)
    full = lambda shape: pl.BlockSpec(shape, lambda i: (0,) * len(shape))
    return pl.pallas_call(
        _hy_filter_kernel,
        grid=(R // HY_FR,),
        in_specs=[full((emb, L)), full((HY_FFN, emb)), full((HY_FFN, 1)), full((HY_FFN, HY_FFN)), full((HY_FFN, 1)),
                  full((HY_FFN, 1)), pl.BlockSpec((HY_FR, HY_FFN), lambda i: (i, 0)),
                  pl.BlockSpec((HY_FR, 1), lambda i: (i, 0))],
        out_specs=pl.BlockSpec((HY_FR, L), lambda i: (i, 0)),
        out_shape=jax.ShapeDtypeStruct((R, L), F32),
        scratch_shapes=[pltpu.VMEM((HY_FFN, L), F32)],
        compiler_params=pltpu.CompilerParams(dimension_semantics=("arbitrary",)),
        name="hy_filter", interpret=interpret,
    )(z, w1p, col(b1), w2.T.astype(F32), col(b2), col(freq), w3.T.astype(F32), col(decay))


def _hy_shortconv_kernel(u_ref, up_ref, un_ref, w_ref, b_ref, o_ref):
    i = pl.program_id(1)
    n = pl.num_programs(1)
    x = u_ref[0]
    tt = x.shape[0]
    p1 = jnp.where(i > 0, up_ref[0][SUBLANES - 1:SUBLANES], 0.0)
    n1 = jnp.where(i < n - 1, un_ref[0][0:1], 0.0)
    ext = jnp.concatenate([p1, x, n1], axis=0)
    o_ref[0] = b_ref[...] + sum(w_ref[k:k + 1, :] * ext[k:k + tt] for k in range(HY_CONV))


def hyena_shortconv(u, w, b, interpret=False):
    B, L, C = u.shape
    tt = min(256, L)
    hb = tt // SUBLANES
    nt = L // tt
    tile = pl.BlockSpec((1, tt, C), lambda bb, i: (bb, i, 0))
    return pl.pallas_call(
        _hy_shortconv_kernel,
        grid=(B, nt),
        in_specs=[tile,
                  pl.BlockSpec((1, SUBLANES, C), lambda bb, i: (bb, jnp.maximum(i * hb - 1, 0), 0)),
                  pl.BlockSpec((1, SUBLANES, C), lambda bb, i: (bb, jnp.minimum((i + 1) * hb, nt * hb - 1), 0)),
                  pl.BlockSpec((HY_CONV, C), lambda bb, i: (0, 0)), pl.BlockSpec((1, C), lambda bb, i: (0, 0))],
        out_specs=tile,
        out_shape=jax.ShapeDtypeStruct(u.shape, F32),
        compiler_params=pltpu.CompilerParams(dimension_semantics=("arbitrary", "arbitrary")),
        name="hy_shortconv", interpret=interpret,
    )(u, u, u, w.astype(F32), b.reshape(1, C).astype(F32))


def hyena_pallas(u, conv_w, conv_b, w1, b1, w2, b2, w3, freq, decay, skip, cdt=BF16, interpret=False):
    B, L, _ = u.shape
    rows = L // LANES
    n1 = max(2 * rows, SUBLANES)
    n_in = n1 // 2 if n1 == 2 * rows else n1
    uc = hyena_shortconv(u, conv_w, conv_b, interpret=interpret)
    ucm = uc.transpose(0, 2, 1).reshape(B, 3 * D_HY, rows, LANES)
    if n_in != rows:
        ucm = jnp.pad(ucm, ((0, 0), (0, 0), (0, n_in - rows), (0, 0)))
    v, x1, x2 = ucm[:, :D_HY], ucm[:, D_HY:2 * D_HY], ucm[:, 2 * D_HY:]

    k = hyena_filters(L, w1, b1, w2, b2, w3, freq, decay, interpret=interpret)
    k = k.reshape(HY_ORDER, 2, D_HY, L)
    k = k.at[:, 1, :, 0].set(0.0)
    k = k.reshape(2 * HY_ORDER * D_HY, rows, LANES)
    if n_in != rows:
        k = jnp.pad(k, ((0, 0), (0, n_in - rows), (0, 0)))
    sr, si = fft_filter_spectra(k, n1, cdt=cdt, interpret=interpret)
    sr = sr.reshape(HY_ORDER, 2, D_HY, n1, LANES)
    si = si.reshape(HY_ORDER, 2, D_HY, n1, LANES)

    y = v
    for o, gate in enumerate((x1, x2)):
        y = fft_long_conv(y, gate, (sr[o, 0], si[o, 0]), (sr[o, 1], si[o, 1]), skip[o], n1, cdt=cdt, interpret=interpret)
    return y[:, :, :rows].reshape(B, D_HY, L).transpose(0, 2, 1)


ROW_T = CTX_LEN
ROW_NT = SEQ_ALL // ROW_T
MOD_ROWS = SUBLANES
MLA_DK = LANES
ROPE_SWAP = np.arange(MLA_ROPE).reshape(2, 2, MLA_ROPE // 4)[:, ::-1].reshape(-1)


def _rms(x):
    return x * lax.rsqrt(jnp.mean(x * x, axis=-1, keepdims=True) + EPS)


def _mod_kernel(c_ref, w_ref, b_ref, o_ref):
    c = c_ref[...]
    s = (c * jax.nn.sigmoid(c)).astype(BF16)
    o_ref[...] = jnp.dot(s, w_ref[...].astype(BF16), preferred_element_type=F32) + b_ref[...]


def adaln_modulation(c, c_ctx, w_mod, b_mod, interpret=False):
    B, D = c.shape
    cs = jnp.concatenate([c, c_ctx[None], jnp.zeros((MOD_ROWS - B - 1, D), F32)], axis=0)
    out = pl.pallas_call(
        _mod_kernel,
        grid=(6,),
        in_specs=[pl.BlockSpec((MOD_ROWS, D), lambda j: (0, 0)),
                  pl.BlockSpec((D, D), lambda j: (0, j)),
                  pl.BlockSpec((1, D), lambda j: (0, j))],
        out_specs=pl.BlockSpec((MOD_ROWS, D), lambda j: (0, j)),
        out_shape=jax.ShapeDtypeStruct((MOD_ROWS, 6 * D), F32),
        compiler_params=pltpu.CompilerParams(dimension_semantics=("arbitrary",)),
        name="adaln_mod", interpret=interpret,
    )(cs, w_mod, b_mod.reshape(1, 6 * D))
    mod = out.reshape(MOD_ROWS, 6, D)
    return jnp.stack([jnp.broadcast_to(mod[B][None], (B, 6, D)), mod[:B]], axis=1)


IN_COLS = (D_LRU, D_LRU, MLA_Q_RANK, MLA_KV_RANK, 3 * D_HY, 2 * MLA_ROPE)


def _in_proj_kernel(s_ref, m_ref, g_ref, w_ref, *out_refs):
    m = m_ref[0, 0]
    h = _rms(s_ref[0]) * g_ref[...] * (1.0 + m[1:2]) + m[0:1]
    u = jnp.dot(h.astype(w_ref.dtype), w_ref[...], preferred_element_type=F32)
    off = 0
    for ref, n in zip(out_refs, IN_COLS):
        ref[0] = u[:, off:off + n]
        off += n


def in_projection(s, mod, norm_g, w_in, cdt=BF16, interpret=False):
    B, R, D = s.shape
    o = (0,) + tuple(IN_SPLITS) + (D_IN,)
    kr = w_in[:, o[4]:o[5]]
    w = jnp.concatenate([w_in[:, :o[4]], w_in[:, o[5]:], kr, kr[:, ROPE_SWAP]], axis=1).astype(cdt)
    return pl.pallas_call(
        _in_proj_kernel,
        grid=(B, R // ROW_T),
        in_specs=[pl.BlockSpec((1, ROW_T, D), lambda b, i: (b, i, 0)),
                  pl.BlockSpec((1, 1, 6, D), lambda b, i: (b, jnp.minimum(i, 1), 0, 0)),
                  pl.BlockSpec((1, D), lambda b, i: (0, 0)),
                  pl.BlockSpec(w.shape, lambda b, i: (0, 0))],
        out_specs=[pl.BlockSpec((1, ROW_T, n), lambda b, i: (b, i, 0)) for n in IN_COLS],
        out_shape=[jax.ShapeDtypeStruct((B, R, n), F32) for n in IN_COLS],
        compiler_params=pltpu.CompilerParams(
            dimension_semantics=("arbitrary", "arbitrary"), vmem_limit_bytes=VMEM_LIMIT),
        name="in_proj", interpret=interpret,
    )(s, mod, norm_g.reshape(1, D), w)


def _mla_proj_kernel(cq_ref, ckv_ref, kr_ref, rc_ref, rs_ref, qg_ref, wq_ref, wqs_ref, kg_ref, wkv_ref,
                     q_ref, k_ref, v_ref):
    cdt = wq_ref.dtype
    cqn = (_rms(cq_ref[0]) * qg_ref[...]).astype(cdt)
    qa = jnp.dot(cqn, wq_ref[...], preferred_element_type=F32)
    qs = jnp.dot(cqn, wqs_ref[...], preferred_element_type=F32)
    ckvn = (_rms(ckv_ref[0]) * kg_ref[...]).astype(cdt)
    kv = jnp.dot(ckvn, wkv_ref[...], preferred_element_type=F32)
    rc, rs = rc_ref[...], rs_ref[...]
    kr = kr_ref[0]
    k_rope = (kr[:, :MLA_ROPE] * rc[:, MLA_NOPE:MLA_NOPE + MLA_ROPE]
              + kr[:, MLA_ROPE:] * rs[:, MLA_NOPE:MLA_NOPE + MLA_ROPE])
    pad = jnp.zeros((kr.shape[0], MLA_DK - MLA_NOPE - MLA_ROPE), F32)
    for h in range(MLA_HEADS):
        sl = slice(h * MLA_DK, (h + 1) * MLA_DK)
        q_ref[0, h] = ((qa[:, sl] * rc + qs[:, sl] * rs) * MLA_SCALE).astype(q_ref.dtype)
        kvh = kv[:, h * (MLA_NOPE + MLA_V):(h + 1) * (MLA_NOPE + MLA_V)]
        k_ref[0, h] = jnp.concatenate([kvh[:, :MLA_NOPE], k_rope, pad], axis=1).astype(k_ref.dtype)
        v_ref[0, h] = jnp.concatenate([kvh[:, MLA_NOPE:], jnp.ones((kvh.shape[0], MLA_V), F32)],
                                      axis=1).astype(v_ref.dtype)


def _rope_tables():
    grid_rows = SEQ // GRID_W
    row = jnp.repeat(jnp.arange(grid_rows, dtype=F32), GRID_W)
    col = jnp.tile(jnp.arange(GRID_W, dtype=F32), grid_rows)
    inv_freq = ROPE_THETA ** (-jnp.arange(ROPE_PAIRS, dtype=F32) / ROPE_PAIRS)
    ang_r, ang_c = row[:, None] * inv_freq, col[:, None] * inv_freq
    cos = jnp.concatenate([jnp.cos(ang_r), jnp.cos(ang_r), jnp.cos(ang_c), jnp.cos(ang_c)], axis=1)
    sin = jnp.concatenate([-jnp.sin(ang_r), jnp.sin(ang_r), -jnp.sin(ang_c), jnp.sin(ang_c)], axis=1)
    npad = MLA_DK - MLA_NOPE - MLA_ROPE
    cos = jnp.concatenate([jnp.ones((SEQ, MLA_NOPE), F32), cos, jnp.zeros((SEQ, npad), F32)], axis=1)
    sin = jnp.concatenate([jnp.zeros((SEQ, MLA_NOPE), F32), sin, jnp.zeros((SEQ, npad), F32)], axis=1)
    ctx_cos = jnp.concatenate([jnp.ones((CTX_LEN, MLA_NOPE + MLA_ROPE), F32), jnp.zeros((CTX_LEN, npad), F32)], axis=1)
    return (jnp.concatenate([ctx_cos, cos], axis=0),
            jnp.concatenate([jnp.zeros((CTX_LEN, MLA_DK), F32), sin], axis=0))


def mla_projection(cq, ckv, kr2, q_g, wqb, kv_g, wkvb, cdt=BF16, interpret=False):
    B, R, _ = cq.shape
    dq = MLA_NOPE + MLA_ROPE
    wq3 = wqb.reshape(MLA_Q_RANK, MLA_HEADS, dq)
    zq = jnp.zeros((MLA_Q_RANK, MLA_HEADS, MLA_DK - dq), wqb.dtype)
    wq = jnp.concatenate([wq3, zq], axis=2).reshape(MLA_Q_RANK, MLA_HEADS * MLA_DK).astype(cdt)
    wqs = jnp.concatenate([jnp.zeros((MLA_Q_RANK, MLA_HEADS, MLA_NOPE), wqb.dtype),
                           wq3[:, :, MLA_NOPE:][:, :, ROPE_SWAP], zq], axis=2)
    wqs = wqs.reshape(MLA_Q_RANK, MLA_HEADS * MLA_DK).astype(cdt)
    rc, rs = _rope_tables()
    row = lambda n: pl.BlockSpec((1, ROW_T, n), lambda b, i: (b, i, 0))
    tab = pl.BlockSpec((ROW_T, MLA_DK), lambda b, i: (i, 0))
    full = lambda shape: pl.BlockSpec(shape, lambda b, i: (0,) * len(shape))
    head = lambda n: pl.BlockSpec((1, MLA_HEADS, ROW_T, n), lambda b, i: (b, 0, i, 0))
    return pl.pallas_call(
        _mla_proj_kernel,
        grid=(B, R // ROW_T),
        in_specs=[row(MLA_Q_RANK), row(MLA_KV_RANK), row(2 * MLA_ROPE), tab, tab,
                  full((1, MLA_Q_RANK)), full(wq.shape), full(wqs.shape),
                  full((1, MLA_KV_RANK)), full(wkvb.shape)],
        out_specs=[head(MLA_DK), head(MLA_DK), head(2 * MLA_V)],
        out_shape=[jax.ShapeDtypeStruct((B, MLA_HEADS, R, MLA_DK), cdt),
                   jax.ShapeDtypeStruct((B, MLA_HEADS, R, MLA_DK), cdt),
                   jax.ShapeDtypeStruct((B, MLA_HEADS, R, 2 * MLA_V), cdt)],
        compiler_params=pltpu.CompilerParams(dimension_semantics=("arbitrary", "arbitrary")),
        name="mla_proj", interpret=interpret,
    )(cq, ckv, kr2, rc, rs, q_g.reshape(1, -1), wq, wqs, kv_g.reshape(1, -1), wkvb.astype(cdt))


def _out_proj_kernel(s_ref, m_ref, g_ref, ya_ref, ybx_ref, ybc_ref, yc_ref, w_ref, snew_ref, hT_ref):
    cdt = w_ref.dtype
    yb = jnp.where(pl.program_id(1) == 0, ybc_ref[0], ybx_ref[0])
    mix = (jnp.dot(ya_ref[0].astype(cdt), w_ref[0:D_LRU, :], preferred_element_type=F32)
           + jnp.dot(yb.astype(cdt), w_ref[D_LRU:D_LRU + MLA_HEADS * MLA_V, :], preferred_element_type=F32)
           + jnp.dot(yc_ref[0].astype(cdt), w_ref[D_LRU + MLA_HEADS * MLA_V:, :], preferred_element_type=F32))
    m = m_ref[0, 0]
    s_new = s_ref[0] + m[2:3] * mix
    snew_ref[0] = s_new
    h2 = _rms(s_new) * g_ref[...] * (1.0 + m[4:5]) + m[3:4]
    hT_ref[...] = h2.T.astype(hT_ref.dtype)


def out_projection(s, mod, norm2_g, ya, yb_x, yb_c, yc, w_out, cdt=BF16, interpret=False):
    B, R, D = s.shape
    nt = R // ROW_T
    row = lambda n: pl.BlockSpec((1, ROW_T, n), lambda b, i: (b, i, 0))
    return pl.pallas_call(
        _out_proj_kernel,
        grid=(B, nt),
        in_specs=[row(D),
                  pl.BlockSpec((1, 1, 6, D), lambda b, i: (b, jnp.minimum(i, 1), 0, 0)),
                  pl.BlockSpec((1, D), lambda b, i: (0, 0)),
                  row(D_LRU),
                  pl.BlockSpec((1, ROW_T, MLA_HEADS * MLA_V), lambda b, i: (b, jnp.maximum(i - 1, 0), 0)),
                  pl.BlockSpec((1, ROW_T, MLA_HEADS * MLA_V), lambda b, i: (b, 0, 0)),
                  row(D_HY),
                  pl.BlockSpec((D_MIX, D), lambda b, i: (0, 0))],
        out_specs=[row(D), pl.BlockSpec((D, ROW_T), lambda b, i: (0, b * nt + i))],
        out_shape=[jax.ShapeDtypeStruct((B, R, D), F32), jax.ShapeDtypeStruct((D, B * R), cdt)],
        compiler_params=pltpu.CompilerParams(
            dimension_semantics=("arbitrary", "arbitrary"), vmem_limit_bytes=VMEM_LIMIT),
        name="out_proj", interpret=interpret,
    )(s, mod, norm2_g.reshape(1, D), ya, yb_x, yb_c, yc, w_out.astype(cdt))


PEER_TM = 768
PEER_TE = 1024
PEER_KP = 512
NEG_BIG = -3.0e38
PACKED_ROWS = 2 * SUBLANES


def _topk_desc(s, k):
    rank = jnp.full(s.shape, float(PEER_NKEYS), F32)
    vals = []
    for r in range(k):
        m = jnp.max(s, axis=0, keepdims=True)
        hit = s == m
        rank = jnp.where(hit, float(r), rank)
        s = jnp.where(hit, NEG_BIG, s)
        vals.append(m)
    return jnp.concatenate(vals, axis=0), rank


def _route_chunk(s1, s2):
    K = PEER_TOPK
    v1, rank1 = _topk_desc(s1, K)
    v2, rank2 = _topk_desc(s2, K)
    half = K // 2
    cands = ([v1[0:1] + v2] + [v1[i:i + 1] + v2[:half] for i in range(1, half)]
             + [v1[i:i + 1] + v2[0:1] for i in range(half, K)])
    cmax = v1[0:1] + v2[0:1]
    c = jnp.concatenate(cands[:half] + [v1[half:] + v2[0:1]], axis=0)
    z = jnp.zeros_like(cmax)
    m = cmax
    for r in range(K):
        m = jnp.max(c, axis=0, keepdims=True)
        z = z + jnp.exp(m - cmax)
        if r + 1 < K:
            c = jnp.where(c == m, NEG_BIG, c)
    thr = m
    n = jnp.zeros_like(s1)
    for i in range(K):
        n_i = jnp.sum(jnp.where(cands[i] >= thr, 1.0, 0.0), axis=0, keepdims=True)
        n = jnp.where(rank1 == float(i), n_i, n)
    w = jnp.exp(s1 - v1[0:1]) / z
    e2 = jnp.exp(s2 - v2[0:1])
    return rank2, e2, n, w


def _peer_route_kernel(hT_ref, wqT_ref, keys_ref, rank2_ref, e2_ref, n_ref, w_ref, q_scr, s1_scr, s2_scr):
    h = pl.program_id(1)

    @pl.when(h == 0)
    def _():
        q_scr[...] = jnp.dot(wqT_ref[...], hT_ref[...], preferred_element_type=F32)

    cdt = keys_ref.dtype
    half = PEER_DQ // 2
    row = pl.multiple_of(h * PEER_DQ, PEER_DQ)
    q1 = q_scr[pl.ds(row, half), :].astype(cdt)
    q2 = q_scr[pl.ds(row + half, half), :].astype(cdt)
    s1_scr[...] = jnp.dot(keys_ref[0], q1, preferred_element_type=F32)
    s2_scr[...] = jnp.dot(keys_ref[1], q2, preferred_element_type=F32)

    def chunk(c, carry):
        sl = pl.ds(pl.multiple_of(c * LANES, LANES), LANES)
        rank2, e2, n, w = _route_chunk(s1_scr[:, sl], s2_scr[:, sl])
        rank2_ref[0, :, sl] = rank2.astype(rank2_ref.dtype)
        e2_ref[0, :, sl] = e2.astype(e2_ref.dtype)
        n_ref[0, :, sl] = n
        w_ref[0, :, sl] = w
        return carry

    lax.fori_loop(0, s1_scr.shape[1] // LANES, chunk, 0)


def _peer_dense_kernel(hT_ref, rank2_ref, e2_ref, n_ref, w_ref, u_ref, vT_ref, s_ref, g_ref, o_ref, acc_ref, pre_ref):
    j = pl.program_id(1)
    n_chunks = pl.num_programs(1) - 1

    @pl.when(j == 0)
    def _():
        acc_ref[...] = jnp.zeros_like(acc_ref)
        pre_ref[1] = jnp.zeros(pre_ref.shape[1:], pre_ref.dtype)

    te, tm = u_ref.shape[0], hT_ref.shape[1]
    gdt = rank2_ref.dtype

    def step(rd, wr):
        pre_ref[wr] = jnp.dot(u_ref[...], hT_ref[...], preferred_element_type=F32)
        out = None
        for k0 in range(0, te, PEER_KP):
            acts = []
            for al in range(k0 // PEER_NKEYS, (k0 + PEER_KP) // PEER_NKEYS):
                g = None
                for h in range(PEER_HEADS):
                    n16 = jnp.broadcast_to(n_ref[h, 0, al:al + 1, :], (PACKED_ROWS, tm)).astype(gdt)
                    w16 = jnp.broadcast_to(w_ref[h, 0, al:al + 1, :], (PACKED_ROWS, tm)).astype(gdt)
                    nb = jnp.concatenate([n16] * (PEER_NKEYS // PACKED_ROWS), axis=0)
                    wb = jnp.concatenate([w16] * (PEER_NKEYS // PACKED_ROWS), axis=0)
                    t = jnp.where(rank2_ref[h] < nb, e2_ref[h] * wb, jnp.zeros_like(wb))
                    g = t if g is None else g + t
                blk = slice(al * PEER_NKEYS, (al + 1) * PEER_NKEYS)
                acts.append((g * _gelu_tanh(pre_ref[rd, blk, :]).astype(gdt)).astype(vT_ref.dtype))
            part = jnp.dot(vT_ref[0, :, k0:k0 + PEER_KP], jnp.concatenate(acts, axis=0), preferred_element_type=F32)
            out = part if out is None else out + part
        acc_ref[...] += out

    for parity in range(2):
        pl.when(j % 2 == parity)(functools.partial(step, 1 - parity, parity))

    @pl.when(j == n_chunks)
    def _():
        y = acc_ref[...].T
        for r in range(g_ref.shape[0]):
            rows = slice(r * ROW_T, (r + 1) * ROW_T)
            o_ref[rows, :] = s_ref[rows, :] + g_ref[r] * y[rows, :]


def peer_tokens(hT, s, gate, wq, keys, u_tab, v_tab, interpret=False):
    D, T = hT.shape
    cdt = hT.dtype
    tm, te = PEER_TM, PEER_TE
    assert T % tm == 0 and PEER_EXPERTS % te == 0 and tm % ROW_T == 0
    wqT = wq.T.astype(cdt)
    route_shape = jax.ShapeDtypeStruct((PEER_HEADS, PEER_NKEYS, T), F32)
    packed_shape = jax.ShapeDtypeStruct((PEER_HEADS, PEER_NKEYS, T), cdt)
    route_spec = pl.BlockSpec((1, PEER_NKEYS, tm), lambda i, hh: (hh, 0, i))
    rank2, e2, n, w = pl.pallas_call(
        _peer_route_kernel,
        grid=(T // tm, PEER_HEADS),
        in_specs=[pl.BlockSpec((D, tm), lambda i, hh: (0, i)),
                  pl.BlockSpec((PEER_HEADS * PEER_DQ, D), lambda i, hh: (0, 0)),
                  pl.BlockSpec((2, PEER_NKEYS, PEER_DQ // 2), lambda i, hh: (0, 0, 0))],
        out_specs=[route_spec] * 4,
        out_shape=[packed_shape, packed_shape, route_shape, route_shape],
        scratch_shapes=[pltpu.VMEM((PEER_HEADS * PEER_DQ, tm), F32),
                        pltpu.VMEM((PEER_NKEYS, tm), F32),
                        pltpu.VMEM((PEER_NKEYS, tm), F32)],
        compiler_params=pltpu.CompilerParams(
            dimension_semantics=("arbitrary", "arbitrary"), vmem_limit_bytes=VMEM_LIMIT),
        name="peer_route", interpret=interpret,
    )(hT, wqT, keys.astype(cdt))

    head_spec = pl.BlockSpec((PEER_HEADS, PEER_NKEYS, tm), lambda i, j: (0, 0, i))
    nj = PEER_EXPERTS // te
    groups = te // PEER_NKEYS
    row_spec = pl.BlockSpec((PEER_HEADS, 1, groups, tm), lambda i, j: (0, jnp.maximum(j - 1, 0), 0, i))
    n = n.reshape(PEER_HEADS, nj, groups, T)
    w = w.reshape(PEER_HEADS, nj, groups, T)
    vT = v_tab.astype(cdt).reshape(nj, te, D).transpose(0, 2, 1)
    return pl.pallas_call(
        _peer_dense_kernel,
        grid=(T // tm, nj + 1),
        in_specs=[pl.BlockSpec((D, tm), lambda i, j: (0, i)),
                  head_spec, head_spec, row_spec, row_spec,
                  pl.BlockSpec((te, D), lambda i, j: (jnp.minimum(j, nj - 1), 0)),
                  pl.BlockSpec((1, D, te), lambda i, j: (jnp.maximum(j - 1, 0), 0, 0)),
                  pl.BlockSpec((tm, D), lambda i, j: (i, 0)),
                  pl.BlockSpec((tm // ROW_T, 1, D), lambda i, j: (i, 0, 0))],
        out_specs=pl.BlockSpec((tm, D), lambda i, j: (i, 0)),
        out_shape=jax.ShapeDtypeStruct((T, D), F32),
        scratch_shapes=[pltpu.VMEM((D, tm), F32), pltpu.VMEM((2, te, tm), F32)],
        compiler_params=pltpu.CompilerParams(
            dimension_semantics=("arbitrary", "arbitrary"), vmem_limit_bytes=VMEM_LIMIT),
        name="peer_dense", interpret=interpret,
    )(hT, rank2, e2, n, w, u_tab.astype(cdt), vT, s, gate)


def _final_norm_kernel(x_ref, g_ref, o_ref):
    o_ref[0] = _rms(x_ref[0]) * g_ref[...]


def final_norm(s, g):
    B, R, D = s.shape
    return pl.pallas_call(
        _final_norm_kernel,
        grid=(B, SEQ // ROW_T),
        in_specs=[pl.BlockSpec((1, ROW_T, D), lambda b, i: (b, i + CTX_LEN // ROW_T, 0)),
                  pl.BlockSpec((1, D), lambda b, i: (0, 0))],
        out_specs=pl.BlockSpec((1, ROW_T, D), lambda b, i: (b, i, 0)),
        out_shape=jax.ShapeDtypeStruct((B, SEQ, D), F32),
        compiler_params=pltpu.CompilerParams(dimension_semantics=("arbitrary", "arbitrary")),
        name="final_norm",
    )(s, g.reshape(1, D))


def kernel(x, c, ctx, c_ctx, w_mod, b_mod, norm1_g, norm2_g, w_in, w_out,
           lru_conv_w, lru_conv_b, lru_wr, lru_br, lru_wi, lru_bi, lru_lambda,
           mla_q_norm_g, mla_wqb, mla_kv_norm_g, mla_wkvb,
           hy_conv_w, hy_conv_b, hy_f_w1, hy_f_b1, hy_f_w2, hy_f_b2, hy_f_w3,
           hy_f_freq, hy_decay, hy_skip,
           peer_wq, peer_keys, peer_u, peer_v, final_g):
    B = x.shape[0]
    s = jnp.concatenate([ctx, x], axis=1)
    for l in range(DEPTH):
        need_ctx = l < DEPTH - 1
        mod = adaln_modulation(c, c_ctx, w_mod[l], b_mod[l])
        u_lru, u_gate, cq, ckv, u_hy, kr2 = in_projection(s, mod, norm1_g[l], w_in[l])

        ya = rglru_mixer(u_lru, u_gate, lru_conv_w[l], lru_conv_b[l], lru_wr[l], lru_br[l], lru_wi[l], lru_bi[l],
                         lru_lambda[l])

        q, k, v = mla_projection(cq, ckv, kr2, mla_q_norm_g[l], mla_wqb[l], mla_kv_norm_g[l], mla_wkvb[l])
        yb_x = flash_attention(q, k, v, q_row0=CTX_LEN, n_q=SEQ, n_k=SEQ_ALL, tk=SEQ_ALL // 3, n_sub=1)
        if need_ctx:
            yb_c = flash_attention(q, k, v, q_row0=0, n_q=CTX_LEN, n_k=CTX_LEN, tk=CTX_LEN, n_sub=1)
        else:
            yb_c = jnp.zeros((B, CTX_LEN, MLA_HEADS * MLA_V), F32)

        hy_args = (hy_conv_w[l], hy_conv_b[l], hy_f_w1[l], hy_f_b1[l], hy_f_w2[l], hy_f_b2[l], hy_f_w3[l],
                   hy_f_freq[l], hy_decay[l], hy_skip[l])
        yc_x = hyena_pallas(u_hy[:, CTX_LEN:], *hy_args)
        yc_c = hyena_pallas(u_hy[:, :CTX_LEN], *hy_args) if need_ctx else jnp.zeros((B, CTX_LEN, D_HY), F32)
        yc = jnp.concatenate([yc_c, yc_x], axis=1)

        s, hT = out_projection(s, mod, norm2_g[l], ya, yb_x, yb_c, yc, w_out[l])
        gate2 = jnp.repeat(mod[:, :, 5], jnp.array([1, ROW_NT - 1]), axis=1, total_repeat_length=ROW_NT)
        s = peer_tokens(hT, s.reshape(B * SEQ_ALL, D_MODEL), gate2.reshape(B * ROW_NT, 1, D_MODEL),
                        peer_wq[l], peer_keys[l], peer_u[l], peer_v[l]).reshape(B, SEQ_ALL, D_MODEL)
    return final_norm(s, final_g)
```

```python
import functools
import math

import jax
import jax.numpy as jnp
import numpy as np
from jax import lax
from jax.experimental import pallas as pl
from jax.experimental.pallas import tpu as pltpu

D_MODEL = 1024
BATCH = 2
SEQ = 8192
DEPTH = 2

GRID_W = 64
CTX_LEN = 256
EPS = 1e-6

D_LRU = 384
LRU_BLOCKS = 6
LRU_BLOCK_DIM = D_LRU // LRU_BLOCKS
LRU_CONV = 4
LRU_C = 8.0

MLA_HEADS = 6
MLA_NOPE = 64
MLA_ROPE = 32
MLA_V = 64
MLA_Q_RANK = 384
MLA_KV_RANK = 256
MLA_SCALE = (MLA_NOPE + MLA_ROPE) ** -0.5
ROPE_PAIRS = MLA_ROPE // 4
ROPE_THETA = 10000.0
ATTN_BLOCK = 128

D_HY = 256
HY_CONV = 3
HY_ORDER = 2
HY_BANDS = 8
HY_EMB = 1 + 2 * HY_BANDS
HY_FFN = 64

D_MIX = D_LRU + MLA_HEADS * MLA_V + D_HY
D_IN = 2 * D_LRU + MLA_Q_RANK + MLA_KV_RANK + MLA_ROPE + 3 * D_HY
IN_SPLITS = [D_LRU, 2 * D_LRU, 2 * D_LRU + MLA_Q_RANK,
             2 * D_LRU + MLA_Q_RANK + MLA_KV_RANK,
             2 * D_LRU + MLA_Q_RANK + MLA_KV_RANK + MLA_ROPE]

PEER_HEADS = 8
PEER_NKEYS = 128
PEER_EXPERTS = PEER_NKEYS * PEER_NKEYS
PEER_DQ = 256
PEER_TOPK = 16
PEER_BLOCK = 128

F32 = jnp.float32


BF16 = jnp.bfloat16
LANES = 128
SUBLANES = 8
VMEM_LIMIT = 56 * 1024 * 1024
SEQ_ALL = CTX_LEN + SEQ
LRU_TT = CTX_LEN
LRU_NT = SEQ_ALL // LRU_TT


def _gelu_tanh(x):
    return 0.5 * x * (1.0 + jnp.tanh(0.7978845608028654 * (x + 0.044715 * (x * x * x))))


def _lru_coeff_kernel(u_ref, up_ref, un_ref, cw_ref, cb_ref, wg_ref, bg_ref, c_ref,
                      af_ref, bf_ref, ab_ref, bb_ref):
    i = pl.program_id(1)
    x = u_ref[0]
    tt = x.shape[0]
    use_prev = i >= 2
    use_next = jnp.logical_and(i >= 1, i < LRU_NT - 1)
    p2 = jnp.where(use_prev, up_ref[0][SUBLANES - 2:SUBLANES], 0.0)
    n1 = jnp.where(use_next, un_ref[0][0:1], 0.0)
    ext = jnp.concatenate([p2, x, n1], axis=0)
    xc = cb_ref[...] + sum(cw_ref[k:k + 1, :] * ext[k:k + tt] for k in range(LRU_CONV))
    g = jnp.dot(xc.astype(wg_ref.dtype), wg_ref[...], preferred_element_type=F32) + bg_ref[...]
    for d, (a_ref, b_ref) in enumerate(((af_ref, bf_ref), (ab_ref, bb_ref))):
        r = jax.nn.sigmoid(g[:, (2 * d) * D_LRU:(2 * d + 1) * D_LRU])
        gi = jax.nn.sigmoid(g[:, (2 * d + 1) * D_LRU:(2 * d + 2) * D_LRU])
        log_a = c_ref[d:d + 1, :] * r
        a_ref[0] = jnp.exp(log_a)
        b_ref[0] = jnp.sqrt(1.0 - jnp.exp(2.0 * log_a)) * (gi * xc)


def _lru_scan_rows(a_ref, b_ref, h_scr, emit, reverse):
    tt = a_ref.shape[1]
    nb = a_ref.shape[0]

    def body(s, hs):
        t = (tt - 1 - s) if reverse else s
        out = []
        for b in range(nb):
            h = a_ref[b, pl.ds(t, 1), :] * hs[b] + b_ref[b, pl.ds(t, 1), :]
            emit(b, t, h)
            out.append(h)
        return tuple(out)

    hs = lax.fori_loop(0, tt, body, tuple(h_scr[b:b + 1, :] for b in range(nb)), unroll=8)
    for b in range(nb):
        h_scr[b:b + 1, :] = hs[b]


def _lru_fwd_kernel(a_ref, b_ref, hf_ref, h_scr):
    @pl.when(pl.program_id(0) == 0)
    def _():
        h_scr[...] = jnp.zeros_like(h_scr)

    def emit(b, t, h):
        hf_ref[b, pl.ds(t, 1), :] = h

    _lru_scan_rows(a_ref, b_ref, h_scr, emit, reverse=False)


def _lru_bwd_kernel(a_ref, b_ref, hf_ref, gate_ref, y_ref, h_scr):
    @pl.when(pl.program_id(0) == 0)
    def _():
        h_scr[...] = jnp.zeros_like(h_scr)

    def emit(b, t, h):
        y_ref[b, pl.ds(t, 1), :] = h

    _lru_scan_rows(a_ref, b_ref, h_scr, emit, reverse=True)
    y_ref[...] = (y_ref[...] + hf_ref[...]) * _gelu_tanh(gate_ref[...])


def rglru_mixer(u_lru, u_gate, conv_w, conv_b, wr, br, wi, bi, lam, cdt=BF16, interpret=False):
    B = u_lru.shape[0]
    tt, nt = LRU_TT, LRU_NT
    hb = tt // SUBLANES

    def blockdiag(w):
        eye = jnp.eye(LRU_BLOCKS, dtype=w.dtype)
        return jnp.einsum('nde,nm->ndme', w, eye).reshape(D_LRU, D_LRU)

    wg = jnp.concatenate([blockdiag(wr[0]), blockdiag(wi[0]), blockdiag(wr[1]), blockdiag(wi[1])], axis=1).astype(cdt)
    bg = jnp.concatenate([br[0], bi[0], br[1], bi[1]]).reshape(1, 4 * D_LRU).astype(F32)
    cdec = -LRU_C * jax.nn.softplus(-lam.astype(F32))

    tile = pl.BlockSpec((1, tt, D_LRU), lambda b, i: (b, i, 0))
    full = lambda shape: pl.BlockSpec(shape, lambda b, i: (0,) * len(shape))
    coeff_shape = jax.ShapeDtypeStruct((B, SEQ_ALL, D_LRU), F32)
    a_f, b_f, a_b, b_b = pl.pallas_call(
        _lru_coeff_kernel,
        grid=(B, nt),
        in_specs=[tile,
                  pl.BlockSpec((1, SUBLANES, D_LRU), lambda b, i: (b, jnp.maximum(i * hb - 1, 0), 0)),
                  pl.BlockSpec((1, SUBLANES, D_LRU), lambda b, i: (b, jnp.minimum((i + 1) * hb, nt * hb - 1), 0)),
                  full((LRU_CONV, D_LRU)), full((1, D_LRU)), full((D_LRU, 4 * D_LRU)), full((1, 4 * D_LRU)),
                  full((2, D_LRU))],
        out_specs=[tile] * 4,
        out_shape=[coeff_shape] * 4,
        compiler_params=pltpu.CompilerParams(dimension_semantics=("arbitrary", "arbitrary")),
        name="lru_coeff", interpret=interpret,
    )(u_lru, u_lru, u_lru, conv_w.astype(F32), conv_b.reshape(1, D_LRU).astype(F32), wg, bg, cdec)

    seq = pl.BlockSpec((B, tt, D_LRU), lambda s: (0, s, 0))
    h_f = pl.pallas_call(
        _lru_fwd_kernel,
        grid=(nt,),
        in_specs=[seq, seq],
        out_specs=seq,
        out_shape=coeff_shape,
        scratch_shapes=[pltpu.VMEM((B, D_LRU), F32)],
        compiler_params=pltpu.CompilerParams(dimension_semantics=("arbitrary",)),
        name="lru_fwd", interpret=interpret,
    )(a_f, b_f)

    rseq = pl.BlockSpec((B, tt, D_LRU), lambda s: (0, jnp.where(s == 0, 0, nt - s), 0))
    return pl.pallas_call(
        _lru_bwd_kernel,
        grid=(nt,),
        in_specs=[rseq, rseq, rseq, rseq],
        out_specs=rseq,
        out_shape=coeff_shape,
        scratch_shapes=[pltpu.VMEM((B, D_LRU), F32)],
        compiler_params=pltpu.CompilerParams(dimension_semantics=("arbitrary",)),
        name="lru_bwd", interpret=interpret,
    )(a_b, b_b, h_f, u_gate)


ATT_TQ = 256


ATT_HG = 2


def _flash_kernel(*refs, tk):
    q_refs, (k_ref, v_ref, o_ref) = refs[:-3], refs[-3:]
    hg, tq = q_refs[0].shape[1], q_refs[0].shape[2]
    dv = v_ref.shape[3] // 2
    n_chunks = k_ref.shape[2] // tk
    chains = [(q_ref[0, h], h) for q_ref in q_refs for h in range(hg)]

    def body(c, carry):
        off = pl.multiple_of(c * tk, LANES)
        out = []
        for (q, h), (m, acc) in zip(chains, carry):
            s = lax.dot_general(q, k_ref[0, h, pl.ds(off, tk), :], (((1,), (1,)), ((), ())),
                                preferred_element_type=F32)
            m_new = jnp.maximum(m, jnp.max(s, axis=1, keepdims=True))
            p = jnp.exp(s - m_new).astype(v_ref.dtype)
            pv = jnp.dot(p, v_ref[0, h, pl.ds(off, tk), :], preferred_element_type=F32)
            out.append((m_new, jnp.exp(m - m_new) * acc + pv))
        return tuple(out)

    init = tuple((jnp.full((tq, 1), NEG_BIG, F32), jnp.zeros((tq, 2 * dv), F32)) for _ in chains)
    res = lax.fori_loop(0, n_chunks, body, init, unroll=True)
    for t in range(len(q_refs)):
        o_ref[0, t * tq:(t + 1) * tq, :] = jnp.concatenate(
            [acc[:, :dv] / acc[:, dv:] for _, acc in res[t * hg:(t + 1) * hg]], axis=1)


def flash_attention(q, k, v, q_row0, n_q, n_k, tk, n_sub, interpret=False):
    B, H, R, dk = q.shape
    dv = v.shape[3] // 2
    tq = ATT_TQ
    assert n_q % (n_sub * tq) == 0 and q_row0 % tq == 0 and n_k % tk == 0 and tk % LANES == 0 and R % n_k == 0
    q0 = q_row0 // tq
    q_spec = lambda t: pl.BlockSpec((1, ATT_HG, tq, dk), lambda b, g, i: (b, g, n_sub * i + t + q0, 0))
    return pl.pallas_call(
        functools.partial(_flash_kernel, tk=tk),
        grid=(B, H // ATT_HG, n_q // (n_sub * tq)),
        in_specs=[q_spec(t) for t in range(n_sub)]
                 + [pl.BlockSpec((1, ATT_HG, n_k, dk), lambda b, g, i: (b, g, 0, 0)),
                    pl.BlockSpec((1, ATT_HG, n_k, 2 * dv), lambda b, g, i: (b, g, 0, 0))],
        out_specs=pl.BlockSpec((1, n_sub * tq, ATT_HG * dv), lambda b, g, i: (b, i, g)),
        out_shape=jax.ShapeDtypeStruct((B, n_q, H * dv), F32),
        compiler_params=pltpu.CompilerParams(
            dimension_semantics=("arbitrary", "arbitrary", "arbitrary"), vmem_limit_bytes=VMEM_LIMIT),
        name="flash_attention", interpret=interpret,
    )(*([q] * n_sub), k, v)


HY_CB = 32
HY_FR = 128
HY_UNROLL = 8


def _dft_tables(n1, n_in, n_out, cdt):
    n = n1 * LANES
    k1 = np.arange(n1)[:, None]
    f1 = np.exp(-2j * np.pi * k1 * np.arange(n_in)[None, :] / n1)
    tw = np.exp(-2j * np.pi * k1 * np.arange(LANES)[None, :] / n)
    f2 = np.exp(-2j * np.pi * np.outer(np.arange(LANES), np.arange(LANES)) / LANES)
    g1 = np.conj(np.exp(-2j * np.pi * k1 * np.arange(n_out)[None, :] / n1)).T / n
    blk = lambda m: np.block([[m.real, -m.imag], [m.imag, m.real]])
    return dict(
        ma=jnp.asarray(blk(f1), cdt),
        ma_re=jnp.asarray(np.concatenate([f1.real, f1.imag], axis=0), cdt),
        tr=jnp.asarray(tw.real, F32), ti=jnp.asarray(tw.imag, F32),
        f2=jnp.asarray(np.concatenate([f2.real, f2.imag], axis=1), cdt),
        f2c=jnp.asarray(np.concatenate([f2.real, -f2.imag], axis=1), cdt),
        mi=jnp.asarray(blk(g1), cdt),
    )


def _cmul_right(re, im, f_ref):
    n = re.shape[0]
    p = jnp.dot(jnp.concatenate([re, im], axis=0).astype(f_ref.dtype), f_ref[...], preferred_element_type=F32)
    return p[:n, :LANES] - p[n:, LANES:], p[:n, LANES:] + p[n:, :LANES]


def _fft_filter_kernel(k_ref, ma_ref, tr_ref, ti_ref, f2_ref, xr_ref, xi_ref):
    n1 = tr_ref.shape[0]

    def body(c, carry):
        o = jnp.dot(ma_ref[...], k_ref[c].astype(ma_ref.dtype), preferred_element_type=F32)
        ar, ai = o[:n1], o[n1:]
        xr, xi = _cmul_right(ar * tr_ref[...] - ai * ti_ref[...], ar * ti_ref[...] + ai * tr_ref[...], f2_ref)
        xr_ref[c] = xr
        xi_ref[c] = xi
        return carry

    lax.fori_loop(0, k_ref.shape[0], body, 0, unroll=HY_UNROLL)


def _fft_conv_kernel(z_ref, gate_ref, kfr_ref, kfi_ref, kbr_ref, kbi_ref, skip_ref,
                     ma_ref, tr_ref, ti_ref, f2_ref, f2c_ref, mi_ref, o_ref):
    n1 = tr_ref.shape[0]
    n_out = o_ref.shape[2]

    def body(c, carry):
        z0, z1 = z_ref[0, c], z_ref[1, c]
        o = jnp.dot(ma_ref[...], jnp.concatenate([z0, z1], axis=0).astype(ma_ref.dtype), preferred_element_type=F32)
        ar, ai = o[:n1], o[n1:]
        tr, ti = tr_ref[...], ti_ref[...]
        xr, xi = _cmul_right(ar * tr - ai * ti, ar * ti + ai * tr, f2_ref)
        hr = kfr_ref[c] + kbr_ref[c]
        hi = kfi_ref[c] - kbi_ref[c]
        br, bi = _cmul_right(xr * hr - xi * hi, xr * hi + xi * hr, f2c_ref)
        b2 = jnp.concatenate([br * tr + bi * ti, bi * tr - br * ti], axis=0).astype(mi_ref.dtype)
        y = jnp.dot(mi_ref[...], b2, preferred_element_type=F32)
        sk = skip_ref[c]
        o_ref[0, c] = gate_ref[0, c] * (y[:n_out] + sk * z0[:n_out])
        o_ref[1, c] = gate_ref[1, c] * (y[n_out:] + sk * z1[:n_out])
        return carry

    lax.fori_loop(0, z_ref.shape[1], body, 0, unroll=HY_UNROLL)


def fft_filter_spectra(k_rows, n1, cdt=BF16, interpret=False):
    R, n_in, _ = k_rows.shape
    t = _dft_tables(n1, n_in, n_in, cdt)
    full = lambda a: pl.BlockSpec(a.shape, lambda i: (0,) * a.ndim)
    spec = jax.ShapeDtypeStruct((R, n1, LANES), F32)
    consts = (t["ma_re"], t["tr"], t["ti"], t["f2"])
    return pl.pallas_call(
        _fft_filter_kernel,
        grid=(R // HY_CB,),
        in_specs=[pl.BlockSpec((HY_CB, n_in, LANES), lambda i: (i, 0, 0))] + [full(a) for a in consts],
        out_specs=[pl.BlockSpec((HY_CB, n1, LANES), lambda i: (i, 0, 0))] * 2,
        out_shape=[spec, spec],
        compiler_params=pltpu.CompilerParams(dimension_semantics=("arbitrary",)),
        name="fft_filter", interpret=interpret,
    )(k_rows, *consts)


def fft_long_conv(z, gate, kf, kb, skip, n1, cdt=BF16, interpret=False):
    B, C, n_in, _ = z.shape
    assert B == 2 and C % HY_CB == 0
    t = _dft_tables(n1, n_in, n_in, cdt)
    full = lambda a: pl.BlockSpec(a.shape, lambda i: (0,) * a.ndim)
    sig = pl.BlockSpec((B, HY_CB, n_in, LANES), lambda i: (0, i, 0, 0))
    spc = pl.BlockSpec((HY_CB, n1, LANES), lambda i: (i, 0, 0))
    consts = (t["ma"], t["tr"], t["ti"], t["f2"], t["f2c"], t["mi"])
    skip_b = jnp.broadcast_to(skip.astype(F32)[:, None, None], (C, 1, LANES))
    return pl.pallas_call(
        _fft_conv_kernel,
        grid=(C // HY_CB,),
        in_specs=[sig, sig, spc, spc, spc, spc, pl.BlockSpec((HY_CB, 1, LANES), lambda i: (i, 0, 0))]
                 + [full(a) for a in consts],
        out_specs=sig,
        out_shape=jax.ShapeDtypeStruct(z.shape, F32),
        compiler_params=pltpu.CompilerParams(dimension_semantics=("arbitrary",)),
        name="fft_conv", interpret=interpret,
    )(z, gate, kf[0], kf[1], kb[0], kb[1], skip_b, *consts)


def _hy_filter_kernel(z_ref, w1_ref, b1_ref, w2_ref, b2_ref, fr_ref, w3_ref, dec_ref, k_ref, h_scr):
    hp = lax.Precision.HIGHEST

    @pl.when(pl.program_id(0) == 0)
    def _():
        h = jnp.sin(fr_ref[...] * (jnp.dot(w1_ref[...], z_ref[...], precision=hp, preferred_element_type=F32) + b1_ref[...]))
        h_scr[...] = jnp.sin(fr_ref[...] * (jnp.dot(w2_ref[...], h, precision=hp, preferred_element_type=F32) + b2_ref[...]))

    t = z_ref[0:1, :]
    k = jnp.dot(w3_ref[...], h_scr[...], precision=hp, preferred_element_type=F32) * jnp.exp(-t * dec_ref[...])
    k_ref[...] = k * lax.rsqrt(jnp.sum(k * k, axis=1, keepdims=True) + EPS)


def hyena_filters(L, w1, b1, w2, b2, w3, freq, decay, interpret=False):
    t = jnp.arange(L, dtype=F32) / L
    bands = jnp.linspace(1e-4, HY_BANDS - 1, HY_BANDS, dtype=F32)
    wpos = (2.0 * math.pi) * t[None, :] * bands[:, None]
    emb = 24
    z = jnp.concatenate([t[None, :], jnp.cos(wpos), jnp.sin(wpos), jnp.zeros((emb - HY_EMB, L), F32)], axis=0)
    w1p = jnp.concatenate([w1.T, jnp.zeros((HY_FFN, emb - HY_EMB), F32)], axis=1)
    R = 2 * HY_ORDER * D_HY
    col = lambda v: v.reshape(-1, 1).astype(F32)
    full = lambda shape: pl.BlockSpec(shape, lambda i: (0,) * len(shape))
    return pl.pallas_call(
        _hy_filter_kernel,
        grid=(R // HY_FR,),
        in_specs=[full((emb, L)), full((HY_FFN, emb)), full((HY_FFN, 1)), full((HY_FFN, HY_FFN)), full((HY_FFN, 1)),
                  full((HY_FFN, 1)), pl.BlockSpec((HY_FR, HY_FFN), lambda i: (i, 0)),
                  pl.BlockSpec((HY_FR, 1), lambda i: (i, 0))],
        out_specs=pl.BlockSpec((HY_FR, L), lambda i: (i, 0)),
        out_shape=jax.ShapeDtypeStruct((R, L), F32),
        scratch_shapes=[pltpu.VMEM((HY_FFN, L), F32)],
        compiler_params=pltpu.CompilerParams(dimension_semantics=("arbitrary",)),
        name="hy_filter", interpret=interpret,
    )(z, w1p, col(b1), w2.T.astype(F32), col(b2), col(freq), w3.T.astype(F32), col(decay))


def _hy_shortconv_kernel(u_ref, up_ref, un_ref, w_ref, b_ref, o_ref):
    i = pl.program_id(1)
    n = pl.num_programs(1)
    x = u_ref[0]
    tt = x.shape[0]
    p1 = jnp.where(i > 0, up_ref[0][SUBLANES - 1:SUBLANES], 0.0)
    n1 = jnp.where(i < n - 1, un_ref[0][0:1], 0.0)
    ext = jnp.concatenate([p1, x, n1], axis=0)
    o_ref[0] = b_ref[...] + sum(w_ref[k:k + 1, :] * ext[k:k + tt] for k in range(HY_CONV))


def hyena_shortconv(u, w, b, interpret=False):
    B, L, C = u.shape
    tt = min(256, L)
    hb = tt // SUBLANES
    nt = L // tt
    tile = pl.BlockSpec((1, tt, C), lambda bb, i: (bb, i, 0))
    return pl.pallas_call(
        _hy_shortconv_kernel,
        grid=(B, nt),
        in_specs=[tile,
                  pl.BlockSpec((1, SUBLANES, C), lambda bb, i: (bb, jnp.maximum(i * hb - 1, 0), 0)),
                  pl.BlockSpec((1, SUBLANES, C), lambda bb, i: (bb, jnp.minimum((i + 1) * hb, nt * hb - 1), 0)),
                  pl.BlockSpec((HY_CONV, C), lambda bb, i: (0, 0)), pl.BlockSpec((1, C), lambda bb, i: (0, 0))],
        out_specs=tile,
        out_shape=jax.ShapeDtypeStruct(u.shape, F32),
        compiler_params=pltpu.CompilerParams(dimension_semantics=("arbitrary", "arbitrary")),
        name="hy_shortconv", interpret=interpret,
    )(u, u, u, w.astype(F32), b.reshape(1, C).astype(F32))


def hyena_pallas(u, conv_w, conv_b, w1, b1, w2, b2, w3, freq, decay, skip, cdt=BF16, interpret=False):
    B, L, _ = u.shape
    rows = L // LANES
    n1 = max(2 * rows, SUBLANES)
    n_in = n1 // 2 if n1 == 2 * rows else n1
    uc = hyena_shortconv(u, conv_w, conv_b, interpret=interpret)
    ucm = uc.transpose(0, 2, 1).reshape(B, 3 * D_HY, rows, LANES)
    if n_in != rows:
        ucm = jnp.pad(ucm, ((0, 0), (0, 0), (0, n_in - rows), (0, 0)))
    v, x1, x2 = ucm[:, :D_HY], ucm[:, D_HY:2 * D_HY], ucm[:, 2 * D_HY:]

    k = hyena_filters(L, w1, b1, w2, b2, w3, freq, decay, interpret=interpret)
    k = k.reshape(HY_ORDER, 2, D_HY, L)
    k = k.at[:, 1, :, 0].set(0.0)
    k = k.reshape(2 * HY_ORDER * D_HY, rows, LANES)
    if n_in != rows:
        k = jnp.pad(k, ((0, 0), (0, n_in - rows), (0, 0)))
    sr, si = fft_filter_spectra(k, n1, cdt=cdt, interpret=interpret)
    sr = sr.reshape(HY_ORDER, 2, D_HY, n1, LANES)
    si = si.reshape(HY_ORDER, 2, D_HY, n1, LANES)

    y = v
    for o, gate in enumerate((x1, x2)):
        y = fft_long_conv(y, gate, (sr[o, 0], si[o, 0]), (sr[o, 1], si[o, 1]), skip[o], n1, cdt=cdt, interpret=interpret)
    return y[:, :, :rows].reshape(B, D_HY, L).transpose(0, 2, 1)


ROW_T = CTX_LEN
ROW_NT = SEQ_ALL // ROW_T
MOD_ROWS = SUBLANES
MLA_DK = LANES
ROPE_SWAP = np.arange(MLA_ROPE).reshape(2, 2, MLA_ROPE // 4)[:, ::-1].reshape(-1)


def _rms(x):
    return x * lax.rsqrt(jnp.mean(x * x, axis=-1, keepdims=True) + EPS)


def _mod_kernel(c_ref, w_ref, b_ref, o_ref):
    c = c_ref[...]
    s = (c * jax.nn.sigmoid(c)).astype(BF16)
    o_ref[...] = jnp.dot(s, w_ref[...].astype(BF16), preferred_element_type=F32) + b_ref[...]


def adaln_modulation(c, c_ctx, w_mod, b_mod, interpret=False):
    B, D = c.shape
    cs = jnp.concatenate([c, c_ctx[None], jnp.zeros((MOD_ROWS - B - 1, D), F32)], axis=0)
    out = pl.pallas_call(
        _mod_kernel,
        grid=(6,),
        in_specs=[pl.BlockSpec((MOD_ROWS, D), lambda j: (0, 0)),
                  pl.BlockSpec((D, D), lambda j: (0, j)),
                  pl.BlockSpec((1, D), lambda j: (0, j))],
        out_specs=pl.BlockSpec((MOD_ROWS, D), lambda j: (0, j)),
        out_shape=jax.ShapeDtypeStruct((MOD_ROWS, 6 * D), F32),
        compiler_params=pltpu.CompilerParams(dimension_semantics=("arbitrary",)),
        name="adaln_mod", interpret=interpret,
    )(cs, w_mod, b_mod.reshape(1, 6 * D))
    mod = out.reshape(MOD_ROWS, 6, D)
    return jnp.stack([jnp.broadcast_to(mod[B][None], (B, 6, D)), mod[:B]], axis=1)


IN_COLS = (D_LRU, D_LRU, MLA_Q_RANK, MLA_KV_RANK, 3 * D_HY, 2 * MLA_ROPE)


def _in_proj_kernel(s_ref, m_ref, g_ref, w_ref, *out_refs):
    m = m_ref[0, 0]
    h = _rms(s_ref[0]) * g_ref[...] * (1.0 + m[1:2]) + m[0:1]
    u = jnp.dot(h.astype(w_ref.dtype), w_ref[...], preferred_element_type=F32)
    off = 0
    for ref, n in zip(out_refs, IN_COLS):
        ref[0] = u[:, off:off + n]
        off += n


def in_projection(s, mod, norm_g, w_in, cdt=BF16, interpret=False):
    B, R, D = s.shape
    o = (0,) + tuple(IN_SPLITS) + (D_IN,)
    kr = w_in[:, o[4]:o[5]]
    w = jnp.concatenate([w_in[:, :o[4]], w_in[:, o[5]:], kr, kr[:, ROPE_SWAP]], axis=1).astype(cdt)
    return pl.pallas_call(
        _in_proj_kernel,
        grid=(B, R // ROW_T),
        in_specs=[pl.BlockSpec((1, ROW_T, D), lambda b, i: (b, i, 0)),
                  pl.BlockSpec((1, 1, 6, D), lambda b, i: (b, jnp.minimum(i, 1), 0, 0)),
                  pl.BlockSpec((1, D), lambda b, i: (0, 0)),
                  pl.BlockSpec(w.shape, lambda b, i: (0, 0))],
        out_specs=[pl.BlockSpec((1, ROW_T, n), lambda b, i: (b, i, 0)) for n in IN_COLS],
        out_shape=[jax.ShapeDtypeStruct((B, R, n), F32) for n in IN_COLS],
        compiler_params=pltpu.CompilerParams(
            dimension_semantics=("arbitrary", "arbitrary"), vmem_limit_bytes=VMEM_LIMIT),
        name="in_proj", interpret=interpret,
    )(s, mod, norm_g.reshape(1, D), w)


def _mla_proj_kernel(cq_ref, ckv_ref, kr_ref, rc_ref, rs_ref, qg_ref, wq_ref, wqs_ref, kg_ref, wkv_ref,
                     q_ref, k_ref, v_ref):
    cdt = wq_ref.dtype
    cqn = (_rms(cq_ref[0]) * qg_ref[...]).astype(cdt)
    qa = jnp.dot(cqn, wq_ref[...], preferred_element_type=F32)
    qs = jnp.dot(cqn, wqs_ref[...], preferred_element_type=F32)
    ckvn = (_rms(ckv_ref[0]) * kg_ref[...]).astype(cdt)
    kv = jnp.dot(ckvn, wkv_ref[...], preferred_element_type=F32)
    rc, rs = rc_ref[...], rs_ref[...]
    kr = kr_ref[0]
    k_rope = (kr[:, :MLA_ROPE] * rc[:, MLA_NOPE:MLA_NOPE + MLA_ROPE]
              + kr[:, MLA_ROPE:] * rs[:, MLA_NOPE:MLA_NOPE + MLA_ROPE])
    pad = jnp.zeros((kr.shape[0], MLA_DK - MLA_NOPE - MLA_ROPE), F32)
    for h in range(MLA_HEADS):
        sl = slice(h * MLA_DK, (h + 1) * MLA_DK)
        q_ref[0, h] = ((qa[:, sl] * rc + qs[:, sl] * rs) * MLA_SCALE).astype(q_ref.dtype)
        kvh = kv[:, h * (MLA_NOPE + MLA_V):(h + 1) * (MLA_NOPE + MLA_V)]
        k_ref[0, h] = jnp.concatenate([kvh[:, :MLA_NOPE], k_rope, pad], axis=1).astype(k_ref.dtype)
        v_ref[0, h] = jnp.concatenate([kvh[:, MLA_NOPE:], jnp.ones((kvh.shape[0], MLA_V), F32)],
                                      axis=1).astype(v_ref.dtype)


def _rope_tables():
    grid_rows = SEQ // GRID_W
    row = jnp.repeat(jnp.arange(grid_rows, dtype=F32), GRID_W)
    col = jnp.tile(jnp.arange(GRID_W, dtype=F32), grid_rows)
    inv_freq = ROPE_THETA ** (-jnp.arange(ROPE_PAIRS, dtype=F32) / ROPE_PAIRS)
    ang_r, ang_c = row[:, None] * inv_freq, col[:, None] * inv_freq
    cos = jnp.concatenate([jnp.cos(ang_r), jnp.cos(ang_r), jnp.cos(ang_c), jnp.cos(ang_c)], axis=1)
    sin = jnp.concatenate([-jnp.sin(ang_r), jnp.sin(ang_r), -jnp.sin(ang_c), jnp.sin(ang_c)], axis=1)
    npad = MLA_DK - MLA_NOPE - MLA_ROPE
    cos = jnp.concatenate([jnp.ones((SEQ, MLA_NOPE), F32), cos, jnp.zeros((SEQ, npad), F32)], axis=1)
    sin = jnp.concatenate([jnp.zeros((SEQ, MLA_NOPE), F32), sin, jnp.zeros((SEQ, npad), F32)], axis=1)
    ctx_cos = jnp.concatenate([jnp.ones((CTX_LEN, MLA_NOPE + MLA_ROPE), F32), jnp.zeros((CTX_LEN, npad), F32)], axis=1)
    return (jnp.concatenate([ctx_cos, cos], axis=0),
            jnp.concatenate([jnp.zeros((CTX_LEN, MLA_DK), F32), sin], axis=0))


def mla_projection(cq, ckv, kr2, q_g, wqb, kv_g, wkvb, cdt=BF16, interpret=False):
    B, R, _ = cq.shape
    dq = MLA_NOPE + MLA_ROPE
    wq3 = wqb.reshape(MLA_Q_RANK, MLA_HEADS, dq)
    zq = jnp.zeros((MLA_Q_RANK, MLA_HEADS, MLA_DK - dq), wqb.dtype)
    wq = jnp.concatenate([wq3, zq], axis=2).reshape(MLA_Q_RANK, MLA_HEADS * MLA_DK).astype(cdt)
    wqs = jnp.concatenate([jnp.zeros((MLA_Q_RANK, MLA_HEADS, MLA_NOPE), wqb.dtype),
                           wq3[:, :, MLA_NOPE:][:, :, ROPE_SWAP], zq], axis=2)
    wqs = wqs.reshape(MLA_Q_RANK, MLA_HEADS * MLA_DK).astype(cdt)
    rc, rs = _rope_tables()
    row = lambda n: pl.BlockSpec((1, ROW_T, n), lambda b, i: (b, i, 0))
    tab = pl.BlockSpec((ROW_T, MLA_DK), lambda b, i: (i, 0))
    full = lambda shape: pl.BlockSpec(shape, lambda b, i: (0,) * len(shape))
    head = lambda n: pl.BlockSpec((1, MLA_HEADS, ROW_T, n), lambda b, i: (b, 0, i, 0))
    return pl.pallas_call(
        _mla_proj_kernel,
        grid=(B, R // ROW_T),
        in_specs=[row(MLA_Q_RANK), row(MLA_KV_RANK), row(2 * MLA_ROPE), tab, tab,
                  full((1, MLA_Q_RANK)), full(wq.shape), full(wqs.shape),
                  full((1, MLA_KV_RANK)), full(wkvb.shape)],
        out_specs=[head(MLA_DK), head(MLA_DK), head(2 * MLA_V)],
        out_shape=[jax.ShapeDtypeStruct((B, MLA_HEADS, R, MLA_DK), cdt),
                   jax.ShapeDtypeStruct((B, MLA_HEADS, R, MLA_DK), cdt),
                   jax.ShapeDtypeStruct((B, MLA_HEADS, R, 2 * MLA_V), cdt)],
        compiler_params=pltpu.CompilerParams(dimension_semantics=("arbitrary", "arbitrary")),
        name="mla_proj", interpret=interpret,
    )(cq, ckv, kr2, rc, rs, q_g.reshape(1, -1), wq, wqs, kv_g.reshape(1, -1), wkvb.astype(cdt))


def _out_proj_kernel(s_ref, m_ref, g_ref, ya_ref, ybx_ref, ybc_ref, yc_ref, w_ref, snew_ref, hT_ref):
    cdt = w_ref.dtype
    yb = jnp.where(pl.program_id(1) == 0, ybc_ref[0], ybx_ref[0])
    mix = (jnp.dot(ya_ref[0].astype(cdt), w_ref[0:D_LRU, :], preferred_element_type=F32)
           + jnp.dot(yb.astype(cdt), w_ref[D_LRU:D_LRU + MLA_HEADS * MLA_V, :], preferred_element_type=F32)
           + jnp.dot(yc_ref[0].astype(cdt), w_ref[D_LRU + MLA_HEADS * MLA_V:, :], preferred_element_type=F32))
    m = m_ref[0, 0]
    s_new = s_ref[0] + m[2:3] * mix
    snew_ref[0] = s_new
    h2 = _rms(s_new) * g_ref[...] * (1.0 + m[4:5]) + m[3:4]
    hT_ref[...] = h2.T.astype(hT_ref.dtype)


def out_projection(s, mod, norm2_g, ya, yb_x, yb_c, yc, w_out, cdt=BF16, interpret=False):
    B, R, D = s.shape
    nt = R // ROW_T
    row = lambda n: pl.BlockSpec((1, ROW_T, n), lambda b, i: (b, i, 0))
    return pl.pallas_call(
        _out_proj_kernel,
        grid=(B, nt),
        in_specs=[row(D),
                  pl.BlockSpec((1, 1, 6, D), lambda b, i: (b, jnp.minimum(i, 1), 0, 0)),
                  pl.BlockSpec((1, D), lambda b, i: (0, 0)),
                  row(D_LRU),
                  pl.BlockSpec((1, ROW_T, MLA_HEADS * MLA_V), lambda b, i: (b, jnp.maximum(i - 1, 0), 0)),
                  pl.BlockSpec((1, ROW_T, MLA_HEADS * MLA_V), lambda b, i: (b, 0, 0)),
                  row(D_HY),
                  pl.BlockSpec((D_MIX, D), lambda b, i: (0, 0))],
        out_specs=[row(D), pl.BlockSpec((D, ROW_T), lambda b, i: (0, b * nt + i))],
        out_shape=[jax.ShapeDtypeStruct((B, R, D), F32), jax.ShapeDtypeStruct((D, B * R), cdt)],
        compiler_params=pltpu.CompilerParams(
            dimension_semantics=("arbitrary", "arbitrary"), vmem_limit_bytes=VMEM_LIMIT),
        name="out_proj", interpret=interpret,
    )(s, mod, norm2_g.reshape(1, D), ya, yb_x, yb_c, yc, w_out.astype(cdt))


PEER_TM = 768
PEER_TE = 1024
PEER_KP = 512
NEG_BIG = -3.0e38
PACKED_ROWS = 2 * SUBLANES


def _topk_desc(s, k):
    rank = jnp.full(s.shape, float(PEER_NKEYS), F32)
    vals = []
    for r in range(k):
        m = jnp.max(s, axis=0, keepdims=True)
        hit = s == m
        rank = jnp.where(hit, float(r), rank)
        s = jnp.where(hit, NEG_BIG, s)
        vals.append(m)
    return jnp.concatenate(vals, axis=0), rank


def _route_chunk(s1, s2):
    K = PEER_TOPK
    v1, rank1 = _topk_desc(s1, K)
    v2, rank2 = _topk_desc(s2, K)
    half = K // 2
    cands = ([v1[0:1] + v2] + [v1[i:i + 1] + v2[:half] for i in range(1, half)]
             + [v1[i:i + 1] + v2[0:1] for i in range(half, K)])
    cmax = v1[0:1] + v2[0:1]
    c = jnp.concatenate(cands[:half] + [v1[half:] + v2[0:1]], axis=0)
    z = jnp.zeros_like(cmax)
    m = cmax
    for r in range(K):
        m = jnp.max(c, axis=0, keepdims=True)
        z = z + jnp.exp(m - cmax)
        if r + 1 < K:
            c = jnp.where(c == m, NEG_BIG, c)
    thr = m
    n = jnp.zeros_like(s1)
    for i in range(K):
        n_i = jnp.sum(jnp.where(cands[i] >= thr, 1.0, 0.0), axis=0, keepdims=True)
        n = jnp.where(rank1 == float(i), n_i, n)
    w = jnp.exp(s1 - v1[0:1]) / z
    e2 = jnp.exp(s2 - v2[0:1])
    return rank2, e2, n, w


def _peer_route_kernel(hT_ref, wqT_ref, keys_ref, rank2_ref, e2_ref, n_ref, w_ref, q_scr, s1_scr, s2_scr):
    h = pl.program_id(1)

    @pl.when(h == 0)
    def _():
        q_scr[...] = jnp.dot(wqT_ref[...], hT_ref[...], preferred_element_type=F32)

    cdt = keys_ref.dtype
    half = PEER_DQ // 2
    row = pl.multiple_of(h * PEER_DQ, PEER_DQ)
    q1 = q_scr[pl.ds(row, half), :].astype(cdt)
    q2 = q_scr[pl.ds(row + half, half), :].astype(cdt)
    s1_scr[...] = jnp.dot(keys_ref[0], q1, preferred_element_type=F32)
    s2_scr[...] = jnp.dot(keys_ref[1], q2, preferred_element_type=F32)

    def chunk(c, carry):
        sl = pl.ds(pl.multiple_of(c * LANES, LANES), LANES)
        rank2, e2, n, w = _route_chunk(s1_scr[:, sl], s2_scr[:, sl])
        rank2_ref[0, :, sl] = rank2.astype(rank2_ref.dtype)
        e2_ref[0, :, sl] = e2.astype(e2_ref.dtype)
        n_ref[0, :, sl] = n
        w_ref[0, :, sl] = w
        return carry

    lax.fori_loop(0, s1_scr.shape[1] // LANES, chunk, 0, unroll=2)


def _peer_dense_kernel(hT_ref, rank2_ref, e2_ref, n_ref, w_ref, u_ref, vT_ref, s_ref, g_ref, o_ref, acc_ref, pre_ref):
    j = pl.program_id(1)
    n_chunks = pl.num_programs(1) - 1

    @pl.when(j == 0)
    def _():
        acc_ref[...] = jnp.zeros_like(acc_ref)
        pre_ref[1] = jnp.zeros(pre_ref.shape[1:], pre_ref.dtype)

    te, tm = u_ref.shape[0], hT_ref.shape[1]
    gdt = rank2_ref.dtype

    def step(rd, wr):
        pre_ref[wr] = jnp.dot(u_ref[...], hT_ref[...], preferred_element_type=F32)
        out = None
        for k0 in range(0, te, PEER_KP):
            acts = []
            for al in range(k0 // PEER_NKEYS, (k0 + PEER_KP) // PEER_NKEYS):
                g = None
                for h in range(PEER_HEADS):
                    n16 = jnp.broadcast_to(n_ref[h, 0, al:al + 1, :], (PACKED_ROWS, tm)).astype(gdt)
                    w16 = jnp.broadcast_to(w_ref[h, 0, al:al + 1, :], (PACKED_ROWS, tm)).astype(gdt)
                    nb = jnp.concatenate([n16] * (PEER_NKEYS // PACKED_ROWS), axis=0)
                    wb = jnp.concatenate([w16] * (PEER_NKEYS // PACKED_ROWS), axis=0)
                    t = jnp.where(rank2_ref[h] < nb, e2_ref[h] * wb, jnp.zeros_like(wb))
                    g = t if g is None else g + t
                blk = slice(al * PEER_NKEYS, (al + 1) * PEER_NKEYS)
                acts.append((g * _gelu_tanh(pre_ref[rd, blk, :].astype(gdt))).astype(vT_ref.dtype))
            part = jnp.dot(vT_ref[0, :, k0:k0 + PEER_KP], jnp.concatenate(acts, axis=0), preferred_element_type=F32)
            out = part if out is None else out + part
        acc_ref[...] += out

    for parity in range(2):
        pl.when(j % 2 == parity)(functools.partial(step, 1 - parity, parity))

    @pl.when(j == n_chunks)
    def _():
        y = acc_ref[...].T
        for r in range(g_ref.shape[0]):
            rows = slice(r * ROW_T, (r + 1) * ROW_T)
            o_ref[rows, :] = s_ref[rows, :] + g_ref[r] * y[rows, :]


def peer_tokens(hT, s, gate, wq, keys, u_tab, v_tab, interpret=False):
    D, T = hT.shape
    cdt = hT.dtype
    tm, te = PEER_TM, PEER_TE
    assert T % tm == 0 and PEER_EXPERTS % te == 0 and tm % ROW_T == 0
    wqT = wq.T.astype(cdt)
    route_shape = jax.ShapeDtypeStruct((PEER_HEADS, PEER_NKEYS, T), F32)
    packed_shape = jax.ShapeDtypeStruct((PEER_HEADS, PEER_NKEYS, T), cdt)
    route_spec = pl.BlockSpec((1, PEER_NKEYS, tm), lambda i, hh: (hh, 0, i))
    rank2, e2, n, w = pl.pallas_call(
        _peer_route_kernel,
        grid=(T // tm, PEER_HEADS),
        in_specs=[pl.BlockSpec((D, tm), lambda i, hh: (0, i)),
                  pl.BlockSpec((PEER_HEADS * PEER_DQ, D), lambda i, hh: (0, 0)),
                  pl.BlockSpec((2, PEER_NKEYS, PEER_DQ // 2), lambda i, hh: (0, 0, 0))],
        out_specs=[route_spec] * 4,
        out_shape=[packed_shape, packed_shape, route_shape, route_shape],
        scratch_shapes=[pltpu.VMEM((PEER_HEADS * PEER_DQ, tm), F32),
                        pltpu.VMEM((PEER_NKEYS, tm), F32),
                        pltpu.VMEM((PEER_NKEYS, tm), F32)],
        compiler_params=pltpu.CompilerParams(
            dimension_semantics=("arbitrary", "arbitrary"), vmem_limit_bytes=VMEM_LIMIT),
        name="peer_route", interpret=interpret,
    )(hT, wqT, keys.astype(cdt))

    head_spec = pl.BlockSpec((PEER_HEADS, PEER_NKEYS, tm), lambda i, j: (0, 0, i))
    nj = PEER_EXPERTS // te
    groups = te // PEER_NKEYS
    row_spec = pl.BlockSpec((PEER_HEADS, 1, groups, tm), lambda i, j: (0, jnp.maximum(j - 1, 0), 0, i))
    n = n.reshape(PEER_HEADS, nj, groups, T)
    w = w.reshape(PEER_HEADS, nj, groups, T)
    vT = v_tab.astype(cdt).reshape(nj, te, D).transpose(0, 2, 1)
    return pl.pallas_call(
        _peer_dense_kernel,
        grid=(T // tm, nj + 1),
        in_specs=[pl.BlockSpec((D, tm), lambda i, j: (0, i)),
                  head_spec, head_spec, row_spec, row_spec,
                  pl.BlockSpec((te, D), lambda i, j: (jnp.minimum(j, nj - 1), 0)),
                  pl.BlockSpec((1, D, te), lambda i, j: (jnp.maximum(j - 1, 0), 0, 0)),
                  pl.BlockSpec((tm, D), lambda i, j: (i, 0)),
                  pl.BlockSpec((tm // ROW_T, 1, D), lambda i, j: (i, 0, 0))],
        out_specs=pl.BlockSpec((tm, D), lambda i, j: (i, 0)),
        out_shape=jax.ShapeDtypeStruct((T, D), F32),
        scratch_shapes=[pltpu.VMEM((D, tm), F32), pltpu.VMEM((2, te, tm), F32)],
        compiler_params=pltpu.CompilerParams(
            dimension_semantics=("arbitrary", "arbitrary"), vmem_limit_bytes=VMEM_LIMIT),
        name="peer_dense", interpret=interpret,
    )(hT, rank2, e2, n, w, u_tab.astype(cdt), vT, s, gate)


def _final_norm_kernel(x_ref, g_ref, o_ref):
    o_ref[0] = _rms(x_ref[0]) * g_ref[...]


def final_norm(s, g):
    B, R, D = s.shape
    return pl.pallas_call(
        _final_norm_kernel,
        grid=(B, SEQ // ROW_T),
        in_specs=[pl.BlockSpec((1, ROW_T, D), lambda b, i: (b, i + CTX_LEN // ROW_T, 0)),
                  pl.BlockSpec((1, D), lambda b, i: (0, 0))],
        out_specs=pl.BlockSpec((1, ROW_T, D), lambda b, i: (b, i, 0)),
        out_shape=jax.ShapeDtypeStruct((B, SEQ, D), F32),
        compiler_params=pltpu.CompilerParams(dimension_semantics=("arbitrary", "arbitrary")),
        name="final_norm",
    )(s, g.reshape(1, D))


def kernel(x, c, ctx, c_ctx, w_mod, b_mod, norm1_g, norm2_g, w_in, w_out,
           lru_conv_w, lru_conv_b, lru_wr, lru_br, lru_wi, lru_bi, lru_lambda,
           mla_q_norm_g, mla_wqb, mla_kv_norm_g, mla_wkvb,
           hy_conv_w, hy_conv_b, hy_f_w1, hy_f_b1, hy_f_w2, hy_f_b2, hy_f_w3,
           hy_f_freq, hy_decay, hy_skip,
           peer_wq, peer_keys, peer_u, peer_v, final_g):
    B = x.shape[0]
    s = jnp.concatenate([ctx, x], axis=1)
    for l in range(DEPTH):
        need_ctx = l < DEPTH - 1
        mod = adaln_modulation(c, c_ctx, w_mod[l], b_mod[l])
        u_lru, u_gate, cq, ckv, u_hy, kr2 = in_projection(s, mod, norm1_g[l], w_in[l])

        ya = rglru_mixer(u_lru, u_gate, lru_conv_w[l], lru_conv_b[l], lru_wr[l], lru_br[l], lru_wi[l], lru_bi[l],
                         lru_lambda[l])

        q, k, v = mla_projection(cq, ckv, kr2, mla_q_norm_g[l], mla_wqb[l], mla_kv_norm_g[l], mla_wkvb[l])
        yb_x = flash_attention(q, k, v, q_row0=CTX_LEN, n_q=SEQ, n_k=SEQ_ALL, tk=SEQ_ALL // 3, n_sub=1)
        if need_ctx:
            yb_c = flash_attention(q, k, v, q_row0=0, n_q=CTX_LEN, n_k=CTX_LEN, tk=CTX_LEN, n_sub=1)
        else:
            yb_c = jnp.zeros((B, CTX_LEN, MLA_HEADS * MLA_V), F32)

        hy_args = (hy_conv_w[l], hy_conv_b[l], hy_f_w1[l], hy_f_b1[l], hy_f_w2[l], hy_f_b2[l], hy_f_w3[l],
                   hy_f_freq[l], hy_decay[l], hy_skip[l])
        yc_x = hyena_pallas(u_hy[:, CTX_LEN:], *hy_args)
        yc_c = hyena_pallas(u_hy[:, :CTX_LEN], *hy_args) if need_ctx else jnp.zeros((B, CTX_LEN, D_HY), F32)
        yc = jnp.concatenate([yc_c, yc_x], axis=1)

        s, hT = out_projection(s, mod, norm2_g[l], ya, yb_x, yb_c, yc, w_out[l])
        gate2 = jnp.repeat(mod[:, :, 5], jnp.array([1, ROW_NT - 1]), axis=1, total_repeat_length=ROW_NT)
        s = peer_tokens(hT, s.reshape(B * SEQ_ALL, D_MODEL), gate2.reshape(B * ROW_NT, 1, D_MODEL),
                        peer_wq[l], peer_keys[l], peer_u[l], peer_v[l]).reshape(B, SEQ_ALL, D_MODEL)
    return final_norm(s, final_g)
```

```python
import functools
import math

import jax
import jax.numpy as jnp
import numpy as np
from jax import lax
from jax.experimental import pallas as pl
from jax.experimental.pallas import tpu as pltpu

D_MODEL = 1024
BATCH = 2
SEQ = 8192
DEPTH = 2

GRID_W = 64
CTX_LEN = 256
EPS = 1e-6

D_LRU = 384
LRU_BLOCKS = 6
LRU_BLOCK_DIM = D_LRU // LRU_BLOCKS
LRU_CONV = 4
LRU_C = 8.0

MLA_HEADS = 6
MLA_NOPE = 64
MLA_ROPE = 32
MLA_V = 64
MLA_Q_RANK = 384
MLA_KV_RANK = 256
MLA_SCALE = (MLA_NOPE + MLA_ROPE) ** -0.5
ROPE_PAIRS = MLA_ROPE // 4
ROPE_THETA = 10000.0
ATTN_BLOCK = 128

D_HY = 256
HY_CONV = 3
HY_ORDER = 2
HY_BANDS = 8
HY_EMB = 1 + 2 * HY_BANDS
HY_FFN = 64

D_MIX = D_LRU + MLA_HEADS * MLA_V + D_HY
D_IN = 2 * D_LRU + MLA_Q_RANK + MLA_KV_RANK + MLA_ROPE + 3 * D_HY
IN_SPLITS = [D_LRU, 2 * D_LRU, 2 * D_LRU + MLA_Q_RANK,
             2 * D_LRU + MLA_Q_RANK + MLA_KV_RANK,
             2 * D_LRU + MLA_Q_RANK + MLA_KV_RANK + MLA_ROPE]

PEER_HEADS = 8
PEER_NKEYS = 128
PEER_EXPERTS = PEER_NKEYS * PEER_NKEYS
PEER_DQ = 256
PEER_TOPK = 16
PEER_BLOCK = 128

F32 = jnp.float32


BF16 = jnp.bfloat16
LANES = 128
SUBLANES = 8
VMEM_LIMIT = 56 * 1024 * 1024
SEQ_ALL = CTX_LEN + SEQ
LRU_TT = CTX_LEN
LRU_NT = SEQ_ALL // LRU_TT


def _gelu_tanh(x):
    return 0.5 * x * (1.0 + jnp.tanh(0.7978845608028654 * (x + 0.044715 * (x * x * x))))


def _lru_coeff_kernel(u_ref, up_ref, un_ref, cw_ref, cb_ref, wg_ref, bg_ref, c_ref,
                      af_ref, bf_ref, ab_ref, bb_ref):
    i = pl.program_id(1)
    x = u_ref[0]
    tt = x.shape[0]
    use_prev = i >= 2
    use_next = jnp.logical_and(i >= 1, i < LRU_NT - 1)
    p2 = jnp.where(use_prev, up_ref[0][SUBLANES - 2:SUBLANES], 0.0)
    n1 = jnp.where(use_next, un_ref[0][0:1], 0.0)
    ext = jnp.concatenate([p2, x, n1], axis=0)
    xc = cb_ref[...] + sum(cw_ref[k:k + 1, :] * ext[k:k + tt] for k in range(LRU_CONV))
    g = jnp.dot(xc.astype(wg_ref.dtype), wg_ref[...], preferred_element_type=F32) + bg_ref[...]
    for d, (a_ref, b_ref) in enumerate(((af_ref, bf_ref), (ab_ref, bb_ref))):
        r = jax.nn.sigmoid(g[:, (2 * d) * D_LRU:(2 * d + 1) * D_LRU])
        gi = jax.nn.sigmoid(g[:, (2 * d + 1) * D_LRU:(2 * d + 2) * D_LRU])
        log_a = c_ref[d:d + 1, :] * r
        a_ref[0] = jnp.exp(log_a)
        b_ref[0] = jnp.sqrt(1.0 - jnp.exp(2.0 * log_a)) * (gi * xc)


def _lru_scan_rows(a_ref, b_ref, h_scr, emit, reverse):
    tt = a_ref.shape[1]
    nb = a_ref.shape[0]

    def body(s, hs):
        t = (tt - 1 - s) if reverse else s
        out = []
        for b in range(nb):
            h = a_ref[b, pl.ds(t, 1), :] * hs[b] + b_ref[b, pl.ds(t, 1), :]
            emit(b, t, h)
            out.append(h)
        return tuple(out)

    hs = lax.fori_loop(0, tt, body, tuple(h_scr[b:b + 1, :] for b in range(nb)), unroll=8)
    for b in range(nb):
        h_scr[b:b + 1, :] = hs[b]


def _lru_fwd_kernel(a_ref, b_ref, hf_ref, h_scr):
    @pl.when(pl.program_id(0) == 0)
    def _():
        h_scr[...] = jnp.zeros_like(h_scr)

    def emit(b, t, h):
        hf_ref[b, pl.ds(t, 1), :] = h

    _lru_scan_rows(a_ref, b_ref, h_scr, emit, reverse=False)


def _lru_bwd_kernel(a_ref, b_ref, hf_ref, gate_ref, y_ref, h_scr):
    @pl.when(pl.program_id(0) == 0)
    def _():
        h_scr[...] = jnp.zeros_like(h_scr)

    def emit(b, t, h):
        y_ref[b, pl.ds(t, 1), :] = h

    _lru_scan_rows(a_ref, b_ref, h_scr, emit, reverse=True)
    y_ref[...] = (y_ref[...] + hf_ref[...]) * _gelu_tanh(gate_ref[...])


def rglru_mixer(u_lru, u_gate, conv_w, conv_b, wr, br, wi, bi, lam, cdt=BF16, interpret=False):
    B = u_lru.shape[0]
    tt, nt = LRU_TT, LRU_NT
    hb = tt // SUBLANES

    def blockdiag(w):
        eye = jnp.eye(LRU_BLOCKS, dtype=w.dtype)
        return jnp.einsum('nde,nm->ndme', w, eye).reshape(D_LRU, D_LRU)

    wg = jnp.concatenate([blockdiag(wr[0]), blockdiag(wi[0]), blockdiag(wr[1]), blockdiag(wi[1])], axis=1).astype(cdt)
    bg = jnp.concatenate([br[0], bi[0], br[1], bi[1]]).reshape(1, 4 * D_LRU).astype(F32)
    cdec = -LRU_C * jax.nn.softplus(-lam.astype(F32))

    tile = pl.BlockSpec((1, tt, D_LRU), lambda b, i: (b, i, 0))
    full = lambda shape: pl.BlockSpec(shape, lambda b, i: (0,) * len(shape))
    coeff_shape = jax.ShapeDtypeStruct((B, SEQ_ALL, D_LRU), F32)
    a_f, b_f, a_b, b_b = pl.pallas_call(
        _lru_coeff_kernel,
        grid=(B, nt),
        in_specs=[tile,
                  pl.BlockSpec((1, SUBLANES, D_LRU), lambda b, i: (b, jnp.maximum(i * hb - 1, 0), 0)),
                  pl.BlockSpec((1, SUBLANES, D_LRU), lambda b, i: (b, jnp.minimum((i + 1) * hb, nt * hb - 1), 0)),
                  full((LRU_CONV, D_LRU)), full((1, D_LRU)), full((D_LRU, 4 * D_LRU)), full((1, 4 * D_LRU)),
                  full((2, D_LRU))],
        out_specs=[tile] * 4,
        out_shape=[coeff_shape] * 4,
        compiler_params=pltpu.CompilerParams(dimension_semantics=("arbitrary", "arbitrary")),
        name="lru_coeff", interpret=interpret,
    )(u_lru, u_lru, u_lru, conv_w.astype(F32), conv_b.reshape(1, D_LRU).astype(F32), wg, bg, cdec)

    seq = pl.BlockSpec((B, tt, D_LRU), lambda s: (0, s, 0))
    h_f = pl.pallas_call(
        _lru_fwd_kernel,
        grid=(nt,),
        in_specs=[seq, seq],
        out_specs=seq,
        out_shape=coeff_shape,
        scratch_shapes=[pltpu.VMEM((B, D_LRU), F32)],
        compiler_params=pltpu.CompilerParams(dimension_semantics=("arbitrary",)),
        name="lru_fwd", interpret=interpret,
    )(a_f, b_f)

    rseq = pl.BlockSpec((B, tt, D_LRU), lambda s: (0, jnp.where(s == 0, 0, nt - s), 0))
    return pl.pallas_call(
        _lru_bwd_kernel,
        grid=(nt,),
        in_specs=[rseq, rseq, rseq, rseq],
        out_specs=rseq,
        out_shape=coeff_shape,
        scratch_shapes=[pltpu.VMEM((B, D_LRU), F32)],
        compiler_params=pltpu.CompilerParams(dimension_semantics=("arbitrary",)),
        name="lru_bwd", interpret=interpret,
    )(a_b, b_b, h_f, u_gate)


ATT_TQ = 256


ATT_HG = 2


def _flash_kernel(*refs, tk):
    q_refs, (k_ref, v_ref, o_ref) = refs[:-3], refs[-3:]
    hg, tq = q_refs[0].shape[1], q_refs[0].shape[2]
    dv = v_ref.shape[3] // 2
    n_chunks = k_ref.shape[2] // tk
    chains = [(q_ref[0, h], h) for q_ref in q_refs for h in range(hg)]

    def body(c, carry):
        off = pl.multiple_of(c * tk, LANES)
        out = []
        for (q, h), (m, acc) in zip(chains, carry):
            s = lax.dot_general(q, k_ref[0, h, pl.ds(off, tk), :], (((1,), (1,)), ((), ())),
                                preferred_element_type=F32)
            m_new = jnp.maximum(m, jnp.max(s, axis=1, keepdims=True))
            p = jnp.exp(s - m_new).astype(v_ref.dtype)
            pv = jnp.dot(p, v_ref[0, h, pl.ds(off, tk), :], preferred_element_type=F32)
            out.append((m_new, jnp.exp(m - m_new) * acc + pv))
        return tuple(out)

    init = tuple((jnp.full((tq, 1), NEG_BIG, F32), jnp.zeros((tq, 2 * dv), F32)) for _ in chains)
    res = lax.fori_loop(0, n_chunks, body, init, unroll=True)
    for t in range(len(q_refs)):
        o_ref[0, t * tq:(t + 1) * tq, :] = jnp.concatenate(
            [acc[:, :dv] / acc[:, dv:] for _, acc in res[t * hg:(t + 1) * hg]], axis=1)


def flash_attention(q, k, v, q_row0, n_q, n_k, tk, n_sub, interpret=False):
    B, H, R, dk = q.shape
    dv = v.shape[3] // 2
    tq = ATT_TQ
    assert n_q % (n_sub * tq) == 0 and q_row0 % tq == 0 and n_k % tk == 0 and tk % LANES == 0 and R % n_k == 0
    q0 = q_row0 // tq
    q_spec = lambda t: pl.BlockSpec((1, ATT_HG, tq, dk), lambda b, g, i: (b, g, n_sub * i + t + q0, 0))
    return pl.pallas_call(
        functools.partial(_flash_kernel, tk=tk),
        grid=(B, H // ATT_HG, n_q // (n_sub * tq)),
        in_specs=[q_spec(t) for t in range(n_sub)]
                 + [pl.BlockSpec((1, ATT_HG, n_k, dk), lambda b, g, i: (b, g, 0, 0)),
                    pl.BlockSpec((1, ATT_HG, n_k, 2 * dv), lambda b, g, i: (b, g, 0, 0))],
        out_specs=pl.BlockSpec((1, n_sub * tq, ATT_HG * dv), lambda b, g, i: (b, i, g)),
        out_shape=jax.ShapeDtypeStruct((B, n_q, H * dv), F32),
        compiler_params=pltpu.CompilerParams(
            dimension_semantics=("arbitrary", "arbitrary", "arbitrary"), vmem_limit_bytes=VMEM_LIMIT),
        name="flash_attention", interpret=interpret,
    )(*([q] * n_sub), k, v)


HY_CB = 16
HY_FR = 128
HY_UNROLL = 16


def _dft_tables(n1, n_in, n_out, cdt):
    n = n1 * LANES
    k1 = np.arange(n1)[:, None]
    f1 = np.exp(-2j * np.pi * k1 * np.arange(n_in)[None, :] / n1)
    tw = np.exp(-2j * np.pi * k1 * np.arange(LANES)[None, :] / n)
    f2 = np.exp(-2j * np.pi * np.outer(np.arange(LANES), np.arange(LANES)) / LANES)
    g1 = np.conj(np.exp(-2j * np.pi * k1 * np.arange(n_out)[None, :] / n1)).T / n
    blk = lambda m: np.block([[m.real, -m.imag], [m.imag, m.real]])
    return dict(
        ma=jnp.asarray(blk(f1), cdt),
        ma_re=jnp.asarray(np.concatenate([f1.real, f1.imag], axis=0), cdt),
        tr=jnp.asarray(tw.real, F32), ti=jnp.asarray(tw.imag, F32),
        f2=jnp.asarray(np.concatenate([f2.real, f2.imag], axis=1), cdt),
        f2c=jnp.asarray(np.concatenate([f2.real, -f2.imag], axis=1), cdt),
        mi=jnp.asarray(blk(g1), cdt),
    )


def _dft_rows_stage(src, ma_ref, tr_ref, ti_ref, planar_scr):
    n1 = tr_ref.shape[0]

    def body(c, carry):
        o = jnp.dot(ma_ref[...], src(c).astype(ma_ref.dtype), preferred_element_type=F32)
        ar, ai = o[:n1], o[n1:]
        tr, ti = tr_ref[...], ti_ref[...]
        planar_scr[c, :n1] = (ar * tr - ai * ti).astype(planar_scr.dtype)
        planar_scr[c, n1:] = (ar * ti + ai * tr).astype(planar_scr.dtype)
        return carry

    lax.fori_loop(0, planar_scr.shape[0], body, 0, unroll=HY_UNROLL)


def _dft_lanes_stage(planar_scr, f_ref):
    cb, rows, _ = planar_scr.shape
    n1 = rows // 2
    p = jnp.dot(planar_scr[...].reshape(cb * rows, LANES), f_ref[...], preferred_element_type=F32)
    p = p.reshape(cb, rows, 2 * LANES)
    return p[:, :n1, :LANES] - p[:, n1:, LANES:], p[:, :n1, LANES:] + p[:, n1:, :LANES]


def _fft_filter_kernel(k_ref, ma_ref, tr_ref, ti_ref, f2_ref, xr_ref, xi_ref, a_scr):
    _dft_rows_stage(lambda c: k_ref[c], ma_ref, tr_ref, ti_ref, a_scr)
    xr, xi = _dft_lanes_stage(a_scr, f2_ref)
    xr_ref[...] = xr
    xi_ref[...] = xi


def _fft_conv_kernel(z_ref, gate_ref, kfr_ref, kfi_ref, kbr_ref, kbi_ref, skip_ref,
                     ma_ref, tr_ref, ti_ref, f2_ref, f2c_ref, mi_ref, o_ref, a_scr, b_scr):
    n1 = tr_ref.shape[0]
    n_out = o_ref.shape[2]
    sdt = a_scr.dtype
    _dft_rows_stage(lambda c: jnp.concatenate([z_ref[0, c], z_ref[1, c]], axis=0), ma_ref, tr_ref, ti_ref, a_scr)
    xr, xi = _dft_lanes_stage(a_scr, f2_ref)
    hr = kfr_ref[...] + kbr_ref[...]
    hi = kfi_ref[...] - kbi_ref[...]
    b_scr[:, :n1] = (xr * hr - xi * hi).astype(sdt)
    b_scr[:, n1:] = (xr * hi + xi * hr).astype(sdt)
    br, bi = _dft_lanes_stage(b_scr, f2c_ref)
    tr, ti = tr_ref[...][None], ti_ref[...][None]
    a_scr[:, :n1] = (br * tr + bi * ti).astype(sdt)
    a_scr[:, n1:] = (bi * tr - br * ti).astype(sdt)

    def body(c, carry):
        y = jnp.dot(mi_ref[...], a_scr[c], preferred_element_type=F32)
        sk = skip_ref[c]
        o_ref[0, c] = gate_ref[0, c] * (y[:n_out] + sk * z_ref[0, c][:n_out])
        o_ref[1, c] = gate_ref[1, c] * (y[n_out:] + sk * z_ref[1, c][:n_out])
        return carry

    lax.fori_loop(0, z_ref.shape[1], body, 0, unroll=HY_UNROLL)


def fft_filter_spectra(k_rows, n1, cdt=BF16, interpret=False):
    R, n_in, _ = k_rows.shape
    t = _dft_tables(n1, n_in, n_in, cdt)
    full = lambda a: pl.BlockSpec(a.shape, lambda i: (0,) * a.ndim)
    spec = jax.ShapeDtypeStruct((R, n1, LANES), F32)
    consts = (t["ma_re"], t["tr"], t["ti"], t["f2"])
    return pl.pallas_call(
        _fft_filter_kernel,
        grid=(R // HY_CB,),
        in_specs=[pl.BlockSpec((HY_CB, n_in, LANES), lambda i: (i, 0, 0))] + [full(a) for a in consts],
        out_specs=[pl.BlockSpec((HY_CB, n1, LANES), lambda i: (i, 0, 0))] * 2,
        out_shape=[spec, spec],
        scratch_shapes=[pltpu.VMEM((HY_CB, 2 * n1, LANES), cdt)],
        compiler_params=pltpu.CompilerParams(dimension_semantics=("arbitrary",), vmem_limit_bytes=VMEM_LIMIT),
        name="fft_filter", interpret=interpret,
    )(k_rows, *consts)


def fft_long_conv(z, gate, kf, kb, skip, n1, cdt=BF16, interpret=False):
    B, C, n_in, _ = z.shape
    assert B == 2 and C % HY_CB == 0
    t = _dft_tables(n1, n_in, n_in, cdt)
    full = lambda a: pl.BlockSpec(a.shape, lambda i: (0,) * a.ndim)
    sig = pl.BlockSpec((B, HY_CB, n_in, LANES), lambda i: (0, i, 0, 0))
    spc = pl.BlockSpec((HY_CB, n1, LANES), lambda i: (i, 0, 0))
    consts = (t["ma"], t["tr"], t["ti"], t["f2"], t["f2c"], t["mi"])
    skip_b = jnp.broadcast_to(skip.astype(F32)[:, None, None], (C, 1, LANES))
    return pl.pallas_call(
        _fft_conv_kernel,
        grid=(C // HY_CB,),
        in_specs=[sig, sig, spc, spc, spc, spc, pl.BlockSpec((HY_CB, 1, LANES), lambda i: (i, 0, 0))]
                 + [full(a) for a in consts],
        out_specs=sig,
        out_shape=jax.ShapeDtypeStruct(z.shape, F32),
        scratch_shapes=[pltpu.VMEM((HY_CB, 2 * n1, LANES), cdt), pltpu.VMEM((HY_CB, 2 * n1, LANES), cdt)],
        compiler_params=pltpu.CompilerParams(dimension_semantics=("arbitrary",), vmem_limit_bytes=VMEM_LIMIT),
        name="fft_conv", interpret=interpret,
    )(z, gate, kf[0], kf[1], kb[0], kb[1], skip_b, *consts)


def _hy_filter_kernel(z_ref, w1_ref, b1_ref, w2_ref, b2_ref, fr_ref, w3_ref, dec_ref, k_ref, h_scr):
    hp = lax.Precision.HIGHEST

    @pl.when(pl.program_id(0) == 0)
    def _():
        h = jnp.sin(fr_ref[...] * (jnp.dot(w1_ref[...], z_ref[...], precision=hp, preferred_element_type=F32) + b1_ref[...]))
        h_scr[...] = jnp.sin(fr_ref[...] * (jnp.dot(w2_ref[...], h, precision=hp, preferred_element_type=F32) + b2_ref[...]))

    t = z_ref[0:1, :]
    k = jnp.dot(w3_ref[...], h_scr[...], precision=hp, preferred_element_type=F32) * jnp.exp(-t * dec_ref[...])
    k_ref[...] = k * lax.rsqrt(jnp.sum(k * k, axis=1, keepdims=True) + EPS)


def hyena_filters(L, w1, b1, w2, b2, w3, freq, decay, interpret=False):
    t = jnp.arange(L, dtype=F32) / L
    bands = jnp.linspace(1e-4, HY_BANDS - 1, HY_BANDS, dtype=F32)
    wpos = (2.0 * math.pi) * t[None, :] * bands[:, None]
    emb = 24
    z = jnp.concatenate([t[None, :], jnp.cos(wpos), jnp.sin(wpos), jnp.zeros((emb - HY_EMB, L), F32)], axis=0)
    w1p = jnp.concatenate([w1.T, jnp.zeros((HY_FFN, emb - HY_EMB), F32)], axis=1)
    R = 2 * HY_ORDER * D_HY
    col = lambda v: v.reshape(-1, 1).astype(F32)
    full = lambda shape: pl.BlockSpec(shape, lambda i: (0,) * len(shape))
    return pl.pallas_call(
        _hy_filter_kernel,
        grid=(R // HY_FR,),
        in_specs=[full((emb, L)), full((HY_FFN, emb)), full((HY_FFN, 1)), full((HY_FFN, HY_FFN)), full((HY_FFN, 1)),
                  full((HY_FFN, 1)), pl.BlockSpec((HY_FR, HY_FFN), lambda i: (i, 0)),
                  pl.BlockSpec((HY_FR, 1), lambda i: (i, 0))],
        out_specs=pl.BlockSpec((HY_FR, L), lambda i: (i, 0)),
        out_shape=jax.ShapeDtypeStruct((R, L), F32),
        scratch_shapes=[pltpu.VMEM((HY_FFN, L), F32)],
        compiler_params=pltpu.CompilerParams(dimension_semantics=("arbitrary",)),
        name="hy_filter", interpret=interpret,
    )(z, w1p, col(b1), w2.T.astype(F32), col(b2), col(freq), w3.T.astype(F32), col(decay))


def _hy_shortconv_kernel(u_ref, up_ref, un_ref, w_ref, b_ref, o_ref):
    i = pl.program_id(1)
    n = pl.num_programs(1)
    x = u_ref[0]
    tt = x.shape[0]
    p1 = jnp.where(i > 0, up_ref[0][SUBLANES - 1:SUBLANES], 0.0)
    n1 = jnp.where(i < n - 1, un_ref[0][0:1], 0.0)
    ext = jnp.concatenate([p1, x, n1], axis=0)
    o_ref[0] = b_ref[...] + sum(w_ref[k:k + 1, :] * ext[k:k + tt] for k in range(HY_CONV))


def hyena_shortconv(u, w, b, interpret=False):
    B, L, C = u.shape
    tt = min(256, L)
    hb = tt // SUBLANES
    nt = L // tt
    tile = pl.BlockSpec((1, tt, C), lambda bb, i: (bb, i, 0))
    return pl.pallas_call(
        _hy_shortconv_kernel,
        grid=(B, nt),
        in_specs=[tile,
                  pl.BlockSpec((1, SUBLANES, C), lambda bb, i: (bb, jnp.maximum(i * hb - 1, 0), 0)),
                  pl.BlockSpec((1, SUBLANES, C), lambda bb, i: (bb, jnp.minimum((i + 1) * hb, nt * hb - 1), 0)),
                  pl.BlockSpec((HY_CONV, C), lambda bb, i: (0, 0)), pl.BlockSpec((1, C), lambda bb, i: (0, 0))],
        out_specs=tile,
        out_shape=jax.ShapeDtypeStruct(u.shape, F32),
        compiler_params=pltpu.CompilerParams(dimension_semantics=("arbitrary", "arbitrary")),
        name="hy_shortconv", interpret=interpret,
    )(u, u, u, w.astype(F32), b.reshape(1, C).astype(F32))


def hyena_pallas(u, conv_w, conv_b, w1, b1, w2, b2, w3, freq, decay, skip, cdt=BF16, interpret=False):
    B, L, _ = u.shape
    rows = L // LANES
    n1 = max(2 * rows, SUBLANES)
    n_in = n1 // 2 if n1 == 2 * rows else n1
    uc = hyena_shortconv(u, conv_w, conv_b, interpret=interpret)
    ucm = uc.transpose(0, 2, 1).reshape(B, 3 * D_HY, rows, LANES)
    if n_in != rows:
        ucm = jnp.pad(ucm, ((0, 0), (0, 0), (0, n_in - rows), (0, 0)))
    v, x1, x2 = ucm[:, :D_HY], ucm[:, D_HY:2 * D_HY], ucm[:, 2 * D_HY:]

    k = hyena_filters(L, w1, b1, w2, b2, w3, freq, decay, interpret=interpret)
    k = k.reshape(HY_ORDER, 2, D_HY, L)
    k = k.at[:, 1, :, 0].set(0.0)
    k = k.reshape(2 * HY_ORDER * D_HY, rows, LANES)
    if n_in != rows:
        k = jnp.pad(k, ((0, 0), (0, n_in - rows), (0, 0)))
    sr, si = fft_filter_spectra(k, n1, cdt=cdt, interpret=interpret)
    sr = sr.reshape(HY_ORDER, 2, D_HY, n1, LANES)
    si = si.reshape(HY_ORDER, 2, D_HY, n1, LANES)

    y = v
    for o, gate in enumerate((x1, x2)):
        y = fft_long_conv(y, gate, (sr[o, 0], si[o, 0]), (sr[o, 1], si[o, 1]), skip[o], n1, cdt=cdt, interpret=interpret)
    return y[:, :, :rows].reshape(B, D_HY, L).transpose(0, 2, 1)


ROW_T = CTX_LEN
ROW_NT = SEQ_ALL // ROW_T
MOD_ROWS = SUBLANES
MLA_DK = LANES
ROPE_SWAP = np.arange(MLA_ROPE).reshape(2, 2, MLA_ROPE // 4)[:, ::-1].reshape(-1)


def _rms(x):
    return x * lax.rsqrt(jnp.mean(x * x, axis=-1, keepdims=True) + EPS)


def _mod_kernel(c_ref, w_ref, b_ref, o_ref):
    c = c_ref[...]
    s = (c * jax.nn.sigmoid(c)).astype(BF16)
    o_ref[...] = jnp.dot(s, w_ref[...].astype(BF16), preferred_element_type=F32) + b_ref[...]


def adaln_modulation(c, c_ctx, w_mod, b_mod, interpret=False):
    B, D = c.shape
    cs = jnp.concatenate([c, c_ctx[None], jnp.zeros((MOD_ROWS - B - 1, D), F32)], axis=0)
    out = pl.pallas_call(
        _mod_kernel,
        grid=(6,),
        in_specs=[pl.BlockSpec((MOD_ROWS, D), lambda j: (0, 0)),
                  pl.BlockSpec((D, D), lambda j: (0, j)),
                  pl.BlockSpec((1, D), lambda j: (0, j))],
        out_specs=pl.BlockSpec((MOD_ROWS, D), lambda j: (0, j)),
        out_shape=jax.ShapeDtypeStruct((MOD_ROWS, 6 * D), F32),
        compiler_params=pltpu.CompilerParams(dimension_semantics=("arbitrary",)),
        name="adaln_mod", interpret=interpret,
    )(cs, w_mod, b_mod.reshape(1, 6 * D))
    mod = out.reshape(MOD_ROWS, 6, D)
    return jnp.stack([jnp.broadcast_to(mod[B][None], (B, 6, D)), mod[:B]], axis=1)


IN_COLS = (D_LRU, D_LRU, MLA_Q_RANK, MLA_KV_RANK, 3 * D_HY, 2 * MLA_ROPE)


def _in_proj_kernel(s_ref, m_ref, g_ref, w_ref, *out_refs):
    m = m_ref[0, 0]
    h = _rms(s_ref[0]) * g_ref[...] * (1.0 + m[1:2]) + m[0:1]
    u = jnp.dot(h.astype(w_ref.dtype), w_ref[...], preferred_element_type=F32)
    off = 0
    for ref, n in zip(out_refs, IN_COLS):
        ref[0] = u[:, off:off + n]
        off += n


def in_projection(s, mod, norm_g, w_in, cdt=BF16, interpret=False):
    B, R, D = s.shape
    o = (0,) + tuple(IN_SPLITS) + (D_IN,)
    kr = w_in[:, o[4]:o[5]]
    w = jnp.concatenate([w_in[:, :o[4]], w_in[:, o[5]:], kr, kr[:, ROPE_SWAP]], axis=1).astype(cdt)
    return pl.pallas_call(
        _in_proj_kernel,
        grid=(B, R // ROW_T),
        in_specs=[pl.BlockSpec((1, ROW_T, D), lambda b, i: (b, i, 0)),
                  pl.BlockSpec((1, 1, 6, D), lambda b, i: (b, jnp.minimum(i, 1), 0, 0)),
                  pl.BlockSpec((1, D), lambda b, i: (0, 0)),
                  pl.BlockSpec(w.shape, lambda b, i: (0, 0))],
        out_specs=[pl.BlockSpec((1, ROW_T, n), lambda b, i: (b, i, 0)) for n in IN_COLS],
        out_shape=[jax.ShapeDtypeStruct((B, R, n), F32) for n in IN_COLS],
        compiler_params=pltpu.CompilerParams(
            dimension_semantics=("arbitrary", "arbitrary"), vmem_limit_bytes=VMEM_LIMIT),
        name="in_proj", interpret=interpret,
    )(s, mod, norm_g.reshape(1, D), w)


def _mla_proj_kernel(cq_ref, ckv_ref, kr_ref, rc_ref, rs_ref, qg_ref, wq_ref, wqs_ref, kg_ref, wkv_ref,
                     q_ref, k_ref, v_ref):
    cdt = wq_ref.dtype
    cqn = (_rms(cq_ref[0]) * qg_ref[...]).astype(cdt)
    qa = jnp.dot(cqn, wq_ref[...], preferred_element_type=F32)
    qs = jnp.dot(cqn, wqs_ref[...], preferred_element_type=F32)
    ckvn = (_rms(ckv_ref[0]) * kg_ref[...]).astype(cdt)
    kv = jnp.dot(ckvn, wkv_ref[...], preferred_element_type=F32)
    rc, rs = rc_ref[...], rs_ref[...]
    kr = kr_ref[0]
    k_rope = (kr[:, :MLA_ROPE] * rc[:, MLA_NOPE:MLA_NOPE + MLA_ROPE]
              + kr[:, MLA_ROPE:] * rs[:, MLA_NOPE:MLA_NOPE + MLA_ROPE])
    pad = jnp.zeros((kr.shape[0], MLA_DK - MLA_NOPE - MLA_ROPE), F32)
    for h in range(MLA_HEADS):
        sl = slice(h * MLA_DK, (h + 1) * MLA_DK)
        q_ref[0, h] = ((qa[:, sl] * rc + qs[:, sl] * rs) * MLA_SCALE).astype(q_ref.dtype)
        kvh = kv[:, h * (MLA_NOPE + MLA_V):(h + 1) * (MLA_NOPE + MLA_V)]
        k_ref[0, h] = jnp.concatenate([kvh[:, :MLA_NOPE], k_rope, pad], axis=1).astype(k_ref.dtype)
        v_ref[0, h] = jnp.concatenate([kvh[:, MLA_NOPE:], jnp.ones((kvh.shape[0], MLA_V), F32)],
                                      axis=1).astype(v_ref.dtype)


def _rope_tables():
    grid_rows = SEQ // GRID_W
    row = jnp.repeat(jnp.arange(grid_rows, dtype=F32), GRID_W)
    col = jnp.tile(jnp.arange(GRID_W, dtype=F32), grid_rows)
    inv_freq = ROPE_THETA ** (-jnp.arange(ROPE_PAIRS, dtype=F32) / ROPE_PAIRS)
    ang_r, ang_c = row[:, None] * inv_freq, col[:, None] * inv_freq
    cos = jnp.concatenate([jnp.cos(ang_r), jnp.cos(ang_r), jnp.cos(ang_c), jnp.cos(ang_c)], axis=1)
    sin = jnp.concatenate([-jnp.sin(ang_r), jnp.sin(ang_r), -jnp.sin(ang_c), jnp.sin(ang_c)], axis=1)
    npad = MLA_DK - MLA_NOPE - MLA_ROPE
    cos = jnp.concatenate([jnp.ones((SEQ, MLA_NOPE), F32), cos, jnp.zeros((SEQ, npad), F32)], axis=1)
    sin = jnp.concatenate([jnp.zeros((SEQ, MLA_NOPE), F32), sin, jnp.zeros((SEQ, npad), F32)], axis=1)
    ctx_cos = jnp.concatenate([jnp.ones((CTX_LEN, MLA_NOPE + MLA_ROPE), F32), jnp.zeros((CTX_LEN, npad), F32)], axis=1)
    return (jnp.concatenate([ctx_cos, cos], axis=0),
            jnp.concatenate([jnp.zeros((CTX_LEN, MLA_DK), F32), sin], axis=0))


def mla_projection(cq, ckv, kr2, q_g, wqb, kv_g, wkvb, cdt=BF16, interpret=False):
    B, R, _ = cq.shape
    dq = MLA_NOPE + MLA_ROPE
    wq3 = wqb.reshape(MLA_Q_RANK, MLA_HEADS, dq)
    zq = jnp.zeros((MLA_Q_RANK, MLA_HEADS, MLA_DK - dq), wqb.dtype)
    wq = jnp.concatenate([wq3, zq], axis=2).reshape(MLA_Q_RANK, MLA_HEADS * MLA_DK).astype(cdt)
    wqs = jnp.concatenate([jnp.zeros((MLA_Q_RANK, MLA_HEADS, MLA_NOPE), wqb.dtype),
                           wq3[:, :, MLA_NOPE:][:, :, ROPE_SWAP], zq], axis=2)
    wqs = wqs.reshape(MLA_Q_RANK, MLA_HEADS * MLA_DK).astype(cdt)
    rc, rs = _rope_tables()
    row = lambda n: pl.BlockSpec((1, ROW_T, n), lambda b, i: (b, i, 0))
    tab = pl.BlockSpec((ROW_T, MLA_DK), lambda b, i: (i, 0))
    full = lambda shape: pl.BlockSpec(shape, lambda b, i: (0,) * len(shape))
    head = lambda n: pl.BlockSpec((1, MLA_HEADS, ROW_T, n), lambda b, i: (b, 0, i, 0))
    return pl.pallas_call(
        _mla_proj_kernel,
        grid=(B, R // ROW_T),
        in_specs=[row(MLA_Q_RANK), row(MLA_KV_RANK), row(2 * MLA_ROPE), tab, tab,
                  full((1, MLA_Q_RANK)), full(wq.shape), full(wqs.shape),
                  full((1, MLA_KV_RANK)), full(wkvb.shape)],
        out_specs=[head(MLA_DK), head(MLA_DK), head(2 * MLA_V)],
        out_shape=[jax.ShapeDtypeStruct((B, MLA_HEADS, R, MLA_DK), cdt),
                   jax.ShapeDtypeStruct((B, MLA_HEADS, R, MLA_DK), cdt),
                   jax.ShapeDtypeStruct((B, MLA_HEADS, R, 2 * MLA_V), cdt)],
        compiler_params=pltpu.CompilerParams(dimension_semantics=("arbitrary", "arbitrary")),
        name="mla_proj", interpret=interpret,
    )(cq, ckv, kr2, rc, rs, q_g.reshape(1, -1), wq, wqs, kv_g.reshape(1, -1), wkvb.astype(cdt))


def _out_proj_kernel(s_ref, m_ref, g_ref, ya_ref, ybx_ref, ybc_ref, yc_ref, w_ref, snew_ref, hT_ref):
    cdt = w_ref.dtype
    yb = jnp.where(pl.program_id(1) == 0, ybc_ref[0], ybx_ref[0])
    mix = (jnp.dot(ya_ref[0].astype(cdt), w_ref[0:D_LRU, :], preferred_element_type=F32)
           + jnp.dot(yb.astype(cdt), w_ref[D_LRU:D_LRU + MLA_HEADS * MLA_V, :], preferred_element_type=F32)
           + jnp.dot(yc_ref[0].astype(cdt), w_ref[D_LRU + MLA_HEADS * MLA_V:, :], preferred_element_type=F32))
    m = m_ref[0, 0]
    s_new = s_ref[0] + m[2:3] * mix
    snew_ref[0] = s_new
    h2 = _rms(s_new) * g_ref[...] * (1.0 + m[4:5]) + m[3:4]
    hT_ref[...] = h2.T.astype(hT_ref.dtype)


def out_projection(s, mod, norm2_g, ya, yb_x, yb_c, yc, w_out, cdt=BF16, interpret=False):
    B, R, D = s.shape
    nt = R // ROW_T
    row = lambda n: pl.BlockSpec((1, ROW_T, n), lambda b, i: (b, i, 0))
    return pl.pallas_call(
        _out_proj_kernel,
        grid=(B, nt),
        in_specs=[row(D),
                  pl.BlockSpec((1, 1, 6, D), lambda b, i: (b, jnp.minimum(i, 1), 0, 0)),
                  pl.BlockSpec((1, D), lambda b, i: (0, 0)),
                  row(D_LRU),
                  pl.BlockSpec((1, ROW_T, MLA_HEADS * MLA_V), lambda b, i: (b, jnp.maximum(i - 1, 0), 0)),
                  pl.BlockSpec((1, ROW_T, MLA_HEADS * MLA_V), lambda b, i: (b, 0, 0)),
                  row(D_HY),
                  pl.BlockSpec((D_MIX, D), lambda b, i: (0, 0))],
        out_specs=[row(D), pl.BlockSpec((D, ROW_T), lambda b, i: (0, b * nt + i))],
        out_shape=[jax.ShapeDtypeStruct((B, R, D), F32), jax.ShapeDtypeStruct((D, B * R), cdt)],
        compiler_params=pltpu.CompilerParams(
            dimension_semantics=("arbitrary", "arbitrary"), vmem_limit_bytes=VMEM_LIMIT),
        name="out_proj", interpret=interpret,
    )(s, mod, norm2_g.reshape(1, D), ya, yb_x, yb_c, yc, w_out.astype(cdt))


PEER_TM = 768
PEER_TE = 1024
PEER_KP = 512
NEG_BIG = -3.0e38
PACKED_ROWS = 2 * SUBLANES


def _topk_desc(s, k):
    rank = jnp.full(s.shape, float(PEER_NKEYS), F32)
    vals = []
    for r in range(k):
        m = jnp.max(s, axis=0, keepdims=True)
        hit = s == m
        rank = jnp.where(hit, float(r), rank)
        s = jnp.where(hit, NEG_BIG, s)
        vals.append(m)
    return jnp.concatenate(vals, axis=0), rank


def _route_chunk(s1, s2):
    K = PEER_TOPK
    v1, rank1 = _topk_desc(s1, K)
    v2, rank2 = _topk_desc(s2, K)
    half = K // 2
    cands = ([v1[0:1] + v2] + [v1[i:i + 1] + v2[:half] for i in range(1, half)]
             + [v1[i:i + 1] + v2[0:1] for i in range(half, K)])
    cmax = v1[0:1] + v2[0:1]
    c = jnp.concatenate(cands[:half] + [v1[half:] + v2[0:1]], axis=0)
    z = jnp.zeros_like(cmax)
    m = cmax
    for r in range(K):
        m = jnp.max(c, axis=0, keepdims=True)
        z = z + jnp.exp(m - cmax)
        if r + 1 < K:
            c = jnp.where(c == m, NEG_BIG, c)
    thr = m
    n = jnp.zeros_like(s1)
    for i in range(K):
        n_i = jnp.sum(jnp.where(cands[i] >= thr, 1.0, 0.0), axis=0, keepdims=True)
        n = jnp.where(rank1 == float(i), n_i, n)
    w = jnp.exp(s1 - v1[0:1]) / z
    e2 = jnp.exp(s2 - v2[0:1])
    return rank2, e2, n, w


def _peer_route_kernel(hT_ref, wqT_ref, keys_ref, rank2_ref, e2_ref, n_ref, w_ref, q_scr, s1_scr, s2_scr):
    h = pl.program_id(1)

    @pl.when(h == 0)
    def _():
        q_scr[...] = jnp.dot(wqT_ref[...], hT_ref[...], preferred_element_type=F32)

    cdt = keys_ref.dtype
    half = PEER_DQ // 2
    row = pl.multiple_of(h * PEER_DQ, PEER_DQ)
    q1 = q_scr[pl.ds(row, half), :].astype(cdt)
    q2 = q_scr[pl.ds(row + half, half), :].astype(cdt)
    s1_scr[...] = jnp.dot(keys_ref[0], q1, preferred_element_type=F32)
    s2_scr[...] = jnp.dot(keys_ref[1], q2, preferred_element_type=F32)

    def chunk(c, carry):
        sl = pl.ds(pl.multiple_of(c * LANES, LANES), LANES)
        rank2, e2, n, w = _route_chunk(s1_scr[:, sl], s2_scr[:, sl])
        rank2_ref[0, :, sl] = rank2.astype(rank2_ref.dtype)
        e2_ref[0, :, sl] = e2.astype(e2_ref.dtype)
        n_ref[0, :, sl] = n
        w_ref[0, :, sl] = w
        return carry

    lax.fori_loop(0, s1_scr.shape[1] // LANES, chunk, 0, unroll=2)


def _peer_dense_kernel(hT_ref, rank2_ref, e2_ref, n_ref, w_ref, u_ref, vT_ref, s_ref, g_ref, o_ref, acc_ref, pre_ref):
    j = pl.program_id(1)
    n_chunks = pl.num_programs(1) - 1

    @pl.when(j == 0)
    def _():
        acc_ref[...] = jnp.zeros_like(acc_ref)
        pre_ref[1] = jnp.zeros(pre_ref.shape[1:], pre_ref.dtype)

    te, tm = u_ref.shape[0], hT_ref.shape[1]
    gdt = rank2_ref.dtype

    def step(rd, wr):
        pre_ref[wr] = jnp.dot(u_ref[...], hT_ref[...], preferred_element_type=F32)
        out = None
        for k0 in range(0, te, PEER_KP):
            acts = []
            for al in range(k0 // PEER_NKEYS, (k0 + PEER_KP) // PEER_NKEYS):
                g = None
                for h in range(PEER_HEADS):
                    n16 = jnp.broadcast_to(n_ref[h, 0, al:al + 1, :], (PACKED_ROWS, tm)).astype(gdt)
                    w16 = jnp.broadcast_to(w_ref[h, 0, al:al + 1, :], (PACKED_ROWS, tm)).astype(gdt)
                    nb = jnp.concatenate([n16] * (PEER_NKEYS // PACKED_ROWS), axis=0)
                    wb = jnp.concatenate([w16] * (PEER_NKEYS // PACKED_ROWS), axis=0)
                    t = jnp.where(rank2_ref[h] < nb, e2_ref[h] * wb, jnp.zeros_like(wb))
                    g = t if g is None else g + t
                blk = slice(al * PEER_NKEYS, (al + 1) * PEER_NKEYS)
                acts.append((g * _gelu_tanh(pre_ref[rd, blk, :].astype(gdt))).astype(vT_ref.dtype))
            part = jnp.dot(vT_ref[0, :, k0:k0 + PEER_KP], jnp.concatenate(acts, axis=0), preferred_element_type=F32)
            out = part if out is None else out + part
        acc_ref[...] += out

    for parity in range(2):
        pl.when(j % 2 == parity)(functools.partial(step, 1 - parity, parity))

    @pl.when(j == n_chunks)
    def _():
        y = acc_ref[...].T
        for r in range(g_ref.shape[0]):
            rows = slice(r * ROW_T, (r + 1) * ROW_T)
            o_ref[rows, :] = s_ref[rows, :] + g_ref[r] * y[rows, :]


def peer_tokens(hT, s, gate, wq, keys, u_tab, v_tab, interpret=False):
    D, T = hT.shape
    cdt = hT.dtype
    tm, te = PEER_TM, PEER_TE
    assert T % tm == 0 and PEER_EXPERTS % te == 0 and tm % ROW_T == 0
    wqT = wq.T.astype(cdt)
    route_shape = jax.ShapeDtypeStruct((PEER_HEADS, PEER_NKEYS, T), F32)
    packed_shape = jax.ShapeDtypeStruct((PEER_HEADS, PEER_NKEYS, T), cdt)
    route_spec = pl.BlockSpec((1, PEER_NKEYS, tm), lambda i, hh: (hh, 0, i))
    rank2, e2, n, w = pl.pallas_call(
        _peer_route_kernel,
        grid=(T // tm, PEER_HEADS),
        in_specs=[pl.BlockSpec((D, tm), lambda i, hh: (0, i)),
                  pl.BlockSpec((PEER_HEADS * PEER_DQ, D), lambda i, hh: (0, 0)),
                  pl.BlockSpec((2, PEER_NKEYS, PEER_DQ // 2), lambda i, hh: (0, 0, 0))],
        out_specs=[route_spec] * 4,
        out_shape=[packed_shape, packed_shape, route_shape, route_shape],
        scratch_shapes=[pltpu.VMEM((PEER_HEADS * PEER_DQ, tm), F32),
                        pltpu.VMEM((PEER_NKEYS, tm), F32),
                        pltpu.VMEM((PEER_NKEYS, tm), F32)],
        compiler_params=pltpu.CompilerParams(
            dimension_semantics=("arbitrary", "arbitrary"), vmem_limit_bytes=VMEM_LIMIT),
        name="peer_route", interpret=interpret,
    )(hT, wqT, keys.astype(cdt))

    head_spec = pl.BlockSpec((PEER_HEADS, PEER_NKEYS, tm), lambda i, j: (0, 0, i))
    nj = PEER_EXPERTS // te
    groups = te // PEER_NKEYS
    row_spec = pl.BlockSpec((PEER_HEADS, 1, groups, tm), lambda i, j: (0, jnp.maximum(j - 1, 0), 0, i))
    n = n.reshape(PEER_HEADS, nj, groups, T)
    w = w.reshape(PEER_HEADS, nj, groups, T)
    vT = v_tab.astype(cdt).reshape(nj, te, D).transpose(0, 2, 1)
    return pl.pallas_call(
        _peer_dense_kernel,
        grid=(T // tm, nj + 1),
        in_specs=[pl.BlockSpec((D, tm), lambda i, j: (0, i)),
                  head_spec, head_spec, row_spec, row_spec,
                  pl.BlockSpec((te, D), lambda i, j: (jnp.minimum(j, nj - 1), 0)),
                  pl.BlockSpec((1, D, te), lambda i, j: (jnp.maximum(j - 1, 0), 0, 0)),
                  pl.BlockSpec((tm, D), lambda i, j: (i, 0)),
                  pl.BlockSpec((tm // ROW_T, 1, D), lambda i, j: (i, 0, 0))],
        out_specs=pl.BlockSpec((tm, D), lambda i, j: (i, 0)),
        out_shape=jax.ShapeDtypeStruct((T, D), F32),
        scratch_shapes=[pltpu.VMEM((D, tm), F32), pltpu.VMEM((2, te, tm), F32)],
        compiler_params=pltpu.CompilerParams(
            dimension_semantics=("arbitrary", "arbitrary"), vmem_limit_bytes=VMEM_LIMIT),
        name="peer_dense", interpret=interpret,
    )(hT, rank2, e2, n, w, u_tab.astype(cdt), vT, s, gate)


def _final_norm_kernel(x_ref, g_ref, o_ref):
    o_ref[0] = _rms(x_ref[0]) * g_ref[...]


def final_norm(s, g):
    B, R, D = s.shape
    return pl.pallas_call(
        _final_norm_kernel,
        grid=(B, SEQ // ROW_T),
        in_specs=[pl.BlockSpec((1, ROW_T, D), lambda b, i: (b, i + CTX_LEN // ROW_T, 0)),
                  pl.BlockSpec((1, D), lambda b, i: (0, 0))],
        out_specs=pl.BlockSpec((1, ROW_T, D), lambda b, i: (b, i, 0)),
        out_shape=jax.ShapeDtypeStruct((B, SEQ, D), F32),
        compiler_params=pltpu.CompilerParams(dimension_semantics=("arbitrary", "arbitrary")),
        name="final_norm",
    )(s, g.reshape(1, D))


def kernel(x, c, ctx, c_ctx, w_mod, b_mod, norm1_g, norm2_g, w_in, w_out,
           lru_conv_w, lru_conv_b, lru_wr, lru_br, lru_wi, lru_bi, lru_lambda,
           mla_q_norm_g, mla_wqb, mla_kv_norm_g, mla_wkvb,
           hy_conv_w, hy_conv_b, hy_f_w1, hy_f_b1, hy_f_w2, hy_f_b2, hy_f_w3,
           hy_f_freq, hy_decay, hy_skip,
           peer_wq, peer_keys, peer_u, peer_v, final_g):
    B = x.shape[0]
    s = jnp.concatenate([ctx, x], axis=1)
    for l in range(DEPTH):
        need_ctx = l < DEPTH - 1
        mod = adaln_modulation(c, c_ctx, w_mod[l], b_mod[l])
        u_lru, u_gate, cq, ckv, u_hy, kr2 = in_projection(s, mod, norm1_g[l], w_in[l])

        ya = rglru_mixer(u_lru, u_gate, lru_conv_w[l], lru_conv_b[l], lru_wr[l], lru_br[l], lru_wi[l], lru_bi[l],
                         lru_lambda[l])

        q, k, v = mla_projection(cq, ckv, kr2, mla_q_norm_g[l], mla_wqb[l], mla_kv_norm_g[l], mla_wkvb[l])
        yb_x = flash_attention(q, k, v, q_row0=CTX_LEN, n_q=SEQ, n_k=SEQ_ALL, tk=SEQ_ALL // 3, n_sub=1)
        if need_ctx:
            yb_c = flash_attention(q, k, v, q_row0=0, n_q=CTX_LEN, n_k=CTX_LEN, tk=CTX_LEN, n_sub=1)
        else:
            yb_c = jnp.zeros((B, CTX_LEN, MLA_HEADS * MLA_V), F32)

        hy_args = (hy_conv_w[l], hy_conv_b[l], hy_f_w1[l], hy_f_b1[l], hy_f_w2[l], hy_f_b2[l], hy_f_w3[l],
                   hy_f_freq[l], hy_decay[l], hy_skip[l])
        yc_x = hyena_pallas(u_hy[:, CTX_LEN:], *hy_args)
        yc_c = hyena_pallas(u_hy[:, :CTX_LEN], *hy_args) if need_ctx else jnp.zeros((B, CTX_LEN, D_HY), F32)
        yc = jnp.concatenate([yc_c, yc_x], axis=1)

        s, hT = out_projection(s, mod, norm2_g[l], ya, yb_x, yb_c, yc, w_out[l])
        gate2 = jnp.repeat(mod[:, :, 5], jnp.array([1, ROW_NT - 1]), axis=1, total_repeat_length=ROW_NT)
        s = peer_tokens(hT, s.reshape(B * SEQ_ALL, D_MODEL), gate2.reshape(B * ROW_NT, 1, D_MODEL),
                        peer_wq[l], peer_keys[l], peer_u[l], peer_v[l]).reshape(B, SEQ_ALL, D_MODEL)
    return final_norm(s, final_g)
```

```python
import functools
import math

import jax
import jax.numpy as jnp
import numpy as np
from jax import lax
from jax.experimental import pallas as pl
from jax.experimental.pallas import tpu as pltpu

D_MODEL = 1024
BATCH = 2
SEQ = 8192
DEPTH = 2

GRID_W = 64
CTX_LEN = 256
EPS = 1e-6

D_LRU = 384
LRU_BLOCKS = 6
LRU_BLOCK_DIM = D_LRU // LRU_BLOCKS
LRU_CONV = 4
LRU_C = 8.0

MLA_HEADS = 6
MLA_NOPE = 64
MLA_ROPE = 32
MLA_V = 64
MLA_Q_RANK = 384
MLA_KV_RANK = 256
MLA_SCALE = (MLA_NOPE + MLA_ROPE) ** -0.5
ROPE_PAIRS = MLA_ROPE // 4
ROPE_THETA = 10000.0
ATTN_BLOCK = 128

D_HY = 256
HY_CONV = 3
HY_ORDER = 2
HY_BANDS = 8
HY_EMB = 1 + 2 * HY_BANDS
HY_FFN = 64

D_MIX = D_LRU + MLA_HEADS * MLA_V + D_HY
D_IN = 2 * D_LRU + MLA_Q_RANK + MLA_KV_RANK + MLA_ROPE + 3 * D_HY
IN_SPLITS = [D_LRU, 2 * D_LRU, 2 * D_LRU + MLA_Q_RANK,
             2 * D_LRU + MLA_Q_RANK + MLA_KV_RANK,
             2 * D_LRU + MLA_Q_RANK + MLA_KV_RANK + MLA_ROPE]

PEER_HEADS = 8
PEER_NKEYS = 128
PEER_EXPERTS = PEER_NKEYS * PEER_NKEYS
PEER_DQ = 256
PEER_TOPK = 16
PEER_BLOCK = 128

F32 = jnp.float32


BF16 = jnp.bfloat16
LANES = 128
SUBLANES = 8
VMEM_LIMIT = 56 * 1024 * 1024
SEQ_ALL = CTX_LEN + SEQ
LRU_TT = CTX_LEN
LRU_NT = SEQ_ALL // LRU_TT


def _gelu_tanh(x):
    return 0.5 * x * (1.0 + jnp.tanh(0.7978845608028654 * (x + 0.044715 * (x * x * x))))


def _lru_coeff_kernel(u_ref, up_ref, un_ref, cw_ref, cb_ref, wg_ref, bg_ref, c_ref,
                      af_ref, bf_ref, ab_ref, bb_ref):
    i = pl.program_id(1)
    x = u_ref[0]
    tt = x.shape[0]
    use_prev = i >= 2
    use_next = jnp.logical_and(i >= 1, i < LRU_NT - 1)
    p2 = jnp.where(use_prev, up_ref[0][SUBLANES - 2:SUBLANES], 0.0)
    n1 = jnp.where(use_next, un_ref[0][0:1], 0.0)
    ext = jnp.concatenate([p2, x, n1], axis=0)
    xc = cb_ref[...] + sum(cw_ref[k:k + 1, :] * ext[k:k + tt] for k in range(LRU_CONV))
    g = jnp.dot(xc.astype(wg_ref.dtype), wg_ref[...], preferred_element_type=F32) + bg_ref[...]
    for d, (a_ref, b_ref) in enumerate(((af_ref, bf_ref), (ab_ref, bb_ref))):
        r = jax.nn.sigmoid(g[:, (2 * d) * D_LRU:(2 * d + 1) * D_LRU])
        gi = jax.nn.sigmoid(g[:, (2 * d + 1) * D_LRU:(2 * d + 2) * D_LRU])
        log_a = c_ref[d:d + 1, :] * r
        a_ref[0] = jnp.exp(log_a)
        b_ref[0] = jnp.sqrt(1.0 - jnp.exp(2.0 * log_a)) * (gi * xc)


def _lru_scan_rows(a_ref, b_ref, h_scr, emit, reverse):
    tt = a_ref.shape[1]
    nb = a_ref.shape[0]

    def body(s, hs):
        t = (tt - 1 - s) if reverse else s
        out = []
        for b in range(nb):
            h = a_ref[b, pl.ds(t, 1), :] * hs[b] + b_ref[b, pl.ds(t, 1), :]
            emit(b, t, h)
            out.append(h)
        return tuple(out)

    hs = lax.fori_loop(0, tt, body, tuple(h_scr[b:b + 1, :] for b in range(nb)), unroll=8)
    for b in range(nb):
        h_scr[b:b + 1, :] = hs[b]


def _lru_fwd_kernel(a_ref, b_ref, hf_ref, h_scr):
    @pl.when(pl.program_id(0) == 0)
    def _():
        h_scr[...] = jnp.zeros_like(h_scr)

    def emit(b, t, h):
        hf_ref[b, pl.ds(t, 1), :] = h

    _lru_scan_rows(a_ref, b_ref, h_scr, emit, reverse=False)


def _lru_bwd_kernel(a_ref, b_ref, hf_ref, gate_ref, y_ref, h_scr):
    @pl.when(pl.program_id(0) == 0)
    def _():
        h_scr[...] = jnp.zeros_like(h_scr)

    def emit(b, t, h):
        y_ref[b, pl.ds(t, 1), :] = h

    _lru_scan_rows(a_ref, b_ref, h_scr, emit, reverse=True)
    y_ref[...] = (y_ref[...] + hf_ref[...]) * _gelu_tanh(gate_ref[...])


def rglru_mixer(u_lru, u_gate, conv_w, conv_b, wr, br, wi, bi, lam, cdt=BF16, interpret=False):
    B = u_lru.shape[0]
    tt, nt = LRU_TT, LRU_NT
    hb = tt // SUBLANES

    def blockdiag(w):
        eye = jnp.eye(LRU_BLOCKS, dtype=w.dtype)
        return jnp.einsum('nde,nm->ndme', w, eye).reshape(D_LRU, D_LRU)

    wg = jnp.concatenate([blockdiag(wr[0]), blockdiag(wi[0]), blockdiag(wr[1]), blockdiag(wi[1])], axis=1).astype(cdt)
    bg = jnp.concatenate([br[0], bi[0], br[1], bi[1]]).reshape(1, 4 * D_LRU).astype(F32)
    cdec = -LRU_C * jax.nn.softplus(-lam.astype(F32))

    tile = pl.BlockSpec((1, tt, D_LRU), lambda b, i: (b, i, 0))
    full = lambda shape: pl.BlockSpec(shape, lambda b, i: (0,) * len(shape))
    coeff_shape = jax.ShapeDtypeStruct((B, SEQ_ALL, D_LRU), F32)
    a_f, b_f, a_b, b_b = pl.pallas_call(
        _lru_coeff_kernel,
        grid=(B, nt),
        in_specs=[tile,
                  pl.BlockSpec((1, SUBLANES, D_LRU), lambda b, i: (b, jnp.maximum(i * hb - 1, 0), 0)),
                  pl.BlockSpec((1, SUBLANES, D_LRU), lambda b, i: (b, jnp.minimum((i + 1) * hb, nt * hb - 1), 0)),
                  full((LRU_CONV, D_LRU)), full((1, D_LRU)), full((D_LRU, 4 * D_LRU)), full((1, 4 * D_LRU)),
                  full((2, D_LRU))],
        out_specs=[tile] * 4,
        out_shape=[coeff_shape] * 4,
        compiler_params=pltpu.CompilerParams(dimension_semantics=("arbitrary", "arbitrary")),
        name="lru_coeff", interpret=interpret,
    )(u_lru, u_lru, u_lru, conv_w.astype(F32), conv_b.reshape(1, D_LRU).astype(F32), wg, bg, cdec)

    seq = pl.BlockSpec((B, tt, D_LRU), lambda s: (0, s, 0))
    h_f = pl.pallas_call(
        _lru_fwd_kernel,
        grid=(nt,),
        in_specs=[seq, seq],
        out_specs=seq,
        out_shape=coeff_shape,
        scratch_shapes=[pltpu.VMEM((B, D_LRU), F32)],
        compiler_params=pltpu.CompilerParams(dimension_semantics=("arbitrary",)),
        name="lru_fwd", interpret=interpret,
    )(a_f, b_f)

    rseq = pl.BlockSpec((B, tt, D_LRU), lambda s: (0, jnp.where(s == 0, 0, nt - s), 0))
    return pl.pallas_call(
        _lru_bwd_kernel,
        grid=(nt,),
        in_specs=[rseq, rseq, rseq, rseq],
        out_specs=rseq,
        out_shape=coeff_shape,
        scratch_shapes=[pltpu.VMEM((B, D_LRU), F32)],
        compiler_params=pltpu.CompilerParams(dimension_semantics=("arbitrary",)),
        name="lru_bwd", interpret=interpret,
    )(a_b, b_b, h_f, u_gate)


ATT_TQ = 256


ATT_HG = 2


def _flash_kernel(*refs, tk):
    q_refs, (k_ref, v_ref, o_ref) = refs[:-3], refs[-3:]
    hg, tq = q_refs[0].shape[1], q_refs[0].shape[2]
    dv = v_ref.shape[3] // 2
    n_chunks = k_ref.shape[2] // tk
    chains = [(q_ref[0, h], h) for q_ref in q_refs for h in range(hg)]

    def body(c, carry):
        off = pl.multiple_of(c * tk, LANES)
        out = []
        for (q, h), (m, acc) in zip(chains, carry):
            s = lax.dot_general(q, k_ref[0, h, pl.ds(off, tk), :], (((1,), (1,)), ((), ())),
                                preferred_element_type=F32)
            m_new = jnp.maximum(m, jnp.max(s, axis=1, keepdims=True))
            p = jnp.exp(s - m_new).astype(v_ref.dtype)
            pv = jnp.dot(p, v_ref[0, h, pl.ds(off, tk), :], preferred_element_type=F32)
            out.append((m_new, jnp.exp(m - m_new) * acc + pv))
        return tuple(out)

    init = tuple((jnp.full((tq, 1), NEG_BIG, F32), jnp.zeros((tq, 2 * dv), F32)) for _ in chains)
    res = lax.fori_loop(0, n_chunks, body, init, unroll=True)
    for t in range(len(q_refs)):
        o_ref[0, t * tq:(t + 1) * tq, :] = jnp.concatenate(
            [acc[:, :dv] / acc[:, dv:] for _, acc in res[t * hg:(t + 1) * hg]], axis=1)


def flash_attention(q, k, v, q_row0, n_q, n_k, tk, n_sub, interpret=False):
    B, H, R, dk = q.shape
    dv = v.shape[3] // 2
    tq = ATT_TQ
    assert n_q % (n_sub * tq) == 0 and q_row0 % tq == 0 and n_k % tk == 0 and tk % LANES == 0 and R % n_k == 0
    q0 = q_row0 // tq
    q_spec = lambda t: pl.BlockSpec((1, ATT_HG, tq, dk), lambda b, g, i: (b, g, n_sub * i + t + q0, 0))
    return pl.pallas_call(
        functools.partial(_flash_kernel, tk=tk),
        grid=(B, H // ATT_HG, n_q // (n_sub * tq)),
        in_specs=[q_spec(t) for t in range(n_sub)]
                 + [pl.BlockSpec((1, ATT_HG, n_k, dk), lambda b, g, i: (b, g, 0, 0)),
                    pl.BlockSpec((1, ATT_HG, n_k, 2 * dv), lambda b, g, i: (b, g, 0, 0))],
        out_specs=pl.BlockSpec((1, n_sub * tq, ATT_HG * dv), lambda b, g, i: (b, i, g)),
        out_shape=jax.ShapeDtypeStruct((B, n_q, H * dv), F32),
        compiler_params=pltpu.CompilerParams(
            dimension_semantics=("arbitrary", "arbitrary", "arbitrary"), vmem_limit_bytes=VMEM_LIMIT),
        name="flash_attention", interpret=interpret,
    )(*([q] * n_sub), k, v)


HY_CB = 16
HY_FR = 128
HY_UNROLL = 16


def _dft_tables(n1, n_in, n_out, cdt):
    n = n1 * LANES
    k1 = np.arange(n1)[:, None]
    f1 = np.exp(-2j * np.pi * k1 * np.arange(n_in)[None, :] / n1)
    tw = np.exp(-2j * np.pi * k1 * np.arange(LANES)[None, :] / n)
    f2 = np.exp(-2j * np.pi * np.outer(np.arange(LANES), np.arange(LANES)) / LANES)
    g1 = np.conj(np.exp(-2j * np.pi * k1 * np.arange(n_out)[None, :] / n1)).T / n
    blk = lambda m: np.block([[m.real, -m.imag], [m.imag, m.real]])
    return dict(
        ma=jnp.asarray(blk(f1), cdt),
        ma_re=jnp.asarray(np.concatenate([f1.real, f1.imag], axis=0), cdt),
        tr=jnp.asarray(tw.real, F32), ti=jnp.asarray(tw.imag, F32),
        f2=jnp.asarray(np.concatenate([f2.real, f2.imag], axis=1), cdt),
        f2c=jnp.asarray(np.concatenate([f2.real, -f2.imag], axis=1), cdt),
        mi=jnp.asarray(blk(g1), cdt),
    )


def _dft_rows_stage(src, ma_ref, tr_ref, ti_ref, planar_scr):
    n1 = tr_ref.shape[0]

    def body(c, carry):
        o = jnp.dot(ma_ref[...], src(c).astype(ma_ref.dtype), preferred_element_type=F32)
        ar, ai = o[:n1], o[n1:]
        tr, ti = tr_ref[...], ti_ref[...]
        planar_scr[c, :n1] = (ar * tr - ai * ti).astype(planar_scr.dtype)
        planar_scr[c, n1:] = (ar * ti + ai * tr).astype(planar_scr.dtype)
        return carry

    lax.fori_loop(0, planar_scr.shape[0], body, 0, unroll=HY_UNROLL)


def _dft_lanes_stage(planar_scr, f_ref):
    cb, rows, _ = planar_scr.shape
    n1 = rows // 2
    p = jnp.dot(planar_scr[...].reshape(cb * rows, LANES), f_ref[...], preferred_element_type=F32)
    p = p.reshape(cb, rows, 2 * LANES)
    return p[:, :n1, :LANES] - p[:, n1:, LANES:], p[:, :n1, LANES:] + p[:, n1:, :LANES]


def _fft_filter_kernel(k_ref, ma_ref, tr_ref, ti_ref, f2_ref, xr_ref, xi_ref, a_scr):
    _dft_rows_stage(lambda c: k_ref[c], ma_ref, tr_ref, ti_ref, a_scr)
    xr, xi = _dft_lanes_stage(a_scr, f2_ref)
    xr_ref[...] = xr
    xi_ref[...] = xi


def _fft_conv_kernel(z_ref, gate_ref, kfr_ref, kfi_ref, kbr_ref, kbi_ref, skip_ref,
                     ma_ref, tr_ref, ti_ref, f2_ref, f2c_ref, mi_ref, o_ref, a_scr, b_scr):
    n1 = tr_ref.shape[0]
    n_out = o_ref.shape[2]
    sdt = a_scr.dtype
    _dft_rows_stage(lambda c: jnp.concatenate([z_ref[0, c], z_ref[1, c]], axis=0), ma_ref, tr_ref, ti_ref, a_scr)
    xr, xi = _dft_lanes_stage(a_scr, f2_ref)
    hr = kfr_ref[...] + kbr_ref[...]
    hi = kfi_ref[...] - kbi_ref[...]
    b_scr[:, :n1] = (xr * hr - xi * hi).astype(sdt)
    b_scr[:, n1:] = (xr * hi + xi * hr).astype(sdt)
    br, bi = _dft_lanes_stage(b_scr, f2c_ref)
    tr, ti = tr_ref[...][None], ti_ref[...][None]
    a_scr[:, :n1] = (br * tr + bi * ti).astype(sdt)
    a_scr[:, n1:] = (bi * tr - br * ti).astype(sdt)

    def body(c, carry):
        y = jnp.dot(mi_ref[...], a_scr[c], preferred_element_type=F32)
        sk = skip_ref[c]
        o_ref[0, c] = gate_ref[0, c] * (y[:n_out] + sk * z_ref[0, c][:n_out])
        o_ref[1, c] = gate_ref[1, c] * (y[n_out:] + sk * z_ref[1, c][:n_out])
        return carry

    lax.fori_loop(0, z_ref.shape[1], body, 0, unroll=HY_UNROLL)


def fft_filter_spectra(k_rows, n1, cdt=BF16, interpret=False):
    R, n_in, _ = k_rows.shape
    t = _dft_tables(n1, n_in, n_in, cdt)
    full = lambda a: pl.BlockSpec(a.shape, lambda i: (0,) * a.ndim)
    spec = jax.ShapeDtypeStruct((R, n1, LANES), F32)
    consts = (t["ma_re"], t["tr"], t["ti"], t["f2"])
    return pl.pallas_call(
        _fft_filter_kernel,
        grid=(R // HY_CB,),
        in_specs=[pl.BlockSpec((HY_CB, n_in, LANES), lambda i: (i, 0, 0))] + [full(a) for a in consts],
        out_specs=[pl.BlockSpec((HY_CB, n1, LANES), lambda i: (i, 0, 0))] * 2,
        out_shape=[spec, spec],
        scratch_shapes=[pltpu.VMEM((HY_CB, 2 * n1, LANES), cdt)],
        compiler_params=pltpu.CompilerParams(dimension_semantics=("arbitrary",), vmem_limit_bytes=VMEM_LIMIT),
        name="fft_filter", interpret=interpret,
    )(k_rows, *consts)


def fft_long_conv(z, z_ch0, gate, gate_ch0, spec_re, spec_im, kf_row0, kb_row0, skip, n1, cdt=BF16, interpret=False):
    B, _, n_in, _ = z.shape
    C = D_HY
    assert B == 2 and all(v % HY_CB == 0 for v in (C, z_ch0, gate_ch0, kf_row0, kb_row0))
    t = _dft_tables(n1, n_in, n_in, cdt)
    full = lambda a: pl.BlockSpec(a.shape, lambda i: (0,) * a.ndim)
    sig = lambda ch0: pl.BlockSpec((B, HY_CB, n_in, LANES), lambda i: (0, i + ch0 // HY_CB, 0, 0))
    spc = lambda row0: pl.BlockSpec((HY_CB, n1, LANES), lambda i: (i + row0 // HY_CB, 0, 0))
    consts = (t["ma"], t["tr"], t["ti"], t["f2"], t["f2c"], t["mi"])
    skip_b = jnp.broadcast_to(skip.astype(F32)[:, None, None], (C, 1, LANES))
    return pl.pallas_call(
        _fft_conv_kernel,
        grid=(C // HY_CB,),
        in_specs=[sig(z_ch0), sig(gate_ch0), spc(kf_row0), spc(kf_row0), spc(kb_row0), spc(kb_row0),
                  pl.BlockSpec((HY_CB, 1, LANES), lambda i: (i, 0, 0))] + [full(a) for a in consts],
        out_specs=sig(0),
        out_shape=jax.ShapeDtypeStruct((B, C, n_in, LANES), F32),
        scratch_shapes=[pltpu.VMEM((HY_CB, 2 * n1, LANES), cdt), pltpu.VMEM((HY_CB, 2 * n1, LANES), cdt)],
        compiler_params=pltpu.CompilerParams(dimension_semantics=("arbitrary",), vmem_limit_bytes=VMEM_LIMIT),
        name="fft_conv", interpret=interpret,
    )(z, gate, spec_re, spec_im, spec_re, spec_im, skip_b, *consts)


def _hy_filter_kernel(z_ref, w1_ref, b1_ref, w2_ref, b2_ref, fr_ref, w3_ref, dec_ref, k_ref, h_scr):
    hp = lax.Precision.HIGHEST

    @pl.when(pl.program_id(0) == 0)
    def _():
        h = jnp.sin(fr_ref[...] * (jnp.dot(w1_ref[...], z_ref[...], precision=hp, preferred_element_type=F32) + b1_ref[...]))
        h_scr[...] = jnp.sin(fr_ref[...] * (jnp.dot(w2_ref[...], h, precision=hp, preferred_element_type=F32) + b2_ref[...]))

    t = z_ref[0:1, :]
    k = jnp.dot(w3_ref[...], h_scr[...], precision=hp, preferred_element_type=F32) * jnp.exp(-t * dec_ref[...])
    k = k * lax.rsqrt(jnp.sum(k * k, axis=1, keepdims=True) + EPS)
    backward = (pl.program_id(0) // (D_HY // HY_FR)) % 2 == 1
    lag0 = lax.broadcasted_iota(jnp.int32, k.shape, 1) == 0
    k_ref[...] = jnp.where(jnp.logical_and(backward, lag0), 0.0, k)


def hyena_filters(L, w1, b1, w2, b2, w3, freq, decay, interpret=False):
    t = jnp.arange(L, dtype=F32) / L
    bands = jnp.linspace(1e-4, HY_BANDS - 1, HY_BANDS, dtype=F32)
    wpos = (2.0 * math.pi) * t[None, :] * bands[:, None]
    emb = 24
    z = jnp.concatenate([t[None, :], jnp.cos(wpos), jnp.sin(wpos), jnp.zeros((emb - HY_EMB, L), F32)], axis=0)
    w1p = jnp.concatenate([w1.T, jnp.zeros((HY_FFN, emb - HY_EMB), F32)], axis=1)
    R = 2 * HY_ORDER * D_HY
    col = lambda v: v.reshape(-1, 1).astype(F32)
    full = lambda shape: pl.BlockSpec(shape, lambda i: (0,) * len(shape))
    return pl.pallas_call(
        _hy_filter_kernel,
        grid=(R // HY_FR,),
        in_specs=[full((emb, L)), full((HY_FFN, emb)), full((HY_FFN, 1)), full((HY_FFN, HY_FFN)), full((HY_FFN, 1)),
                  full((HY_FFN, 1)), pl.BlockSpec((HY_FR, HY_FFN), lambda i: (i, 0)),
                  pl.BlockSpec((HY_FR, 1), lambda i: (i, 0))],
        out_specs=pl.BlockSpec((HY_FR, L), lambda i: (i, 0)),
        out_shape=jax.ShapeDtypeStruct((R, L), F32),
        scratch_shapes=[pltpu.VMEM((HY_FFN, L), F32)],
        compiler_params=pltpu.CompilerParams(dimension_semantics=("arbitrary",)),
        name="hy_filter", interpret=interpret,
    )(z, w1p, col(b1), w2.T.astype(F32), col(b2), col(freq), w3.T.astype(F32), col(decay))


def _hy_shortconv_kernel(u_ref, up_ref, un_ref, w_ref, b_ref, o_ref):
    i = pl.program_id(1)
    n = pl.num_programs(1)
    x = u_ref[0]
    tt = x.shape[0]
    p1 = jnp.where(i > 0, up_ref[0][SUBLANES - 1:SUBLANES], 0.0)
    n1 = jnp.where(i < n - 1, un_ref[0][0:1], 0.0)
    ext = jnp.concatenate([p1, x, n1], axis=0)
    o_ref[0] = b_ref[...] + sum(w_ref[k:k + 1, :] * ext[k:k + tt] for k in range(HY_CONV))


def hyena_shortconv(u, row0, L, w, b, interpret=False):
    B, _, C = u.shape
    tt = min(256, L)
    hb = tt // SUBLANES
    nt = L // tt
    assert row0 % tt == 0
    t0, h0 = row0 // tt, row0 // SUBLANES
    return pl.pallas_call(
        _hy_shortconv_kernel,
        grid=(B, nt),
        in_specs=[pl.BlockSpec((1, tt, C), lambda bb, i: (bb, i + t0, 0)),
                  pl.BlockSpec((1, SUBLANES, C), lambda bb, i: (bb, h0 + jnp.maximum(i * hb - 1, 0), 0)),
                  pl.BlockSpec((1, SUBLANES, C), lambda bb, i: (bb, h0 + jnp.minimum((i + 1) * hb, nt * hb - 1), 0)),
                  pl.BlockSpec((HY_CONV, C), lambda bb, i: (0, 0)), pl.BlockSpec((1, C), lambda bb, i: (0, 0))],
        out_specs=pl.BlockSpec((1, tt, C), lambda bb, i: (bb, i, 0)),
        out_shape=jax.ShapeDtypeStruct((B, L, C), F32),
        compiler_params=pltpu.CompilerParams(dimension_semantics=("arbitrary", "arbitrary")),
        name="hy_shortconv", interpret=interpret,
    )(u, u, u, w.astype(F32), b.reshape(1, C).astype(F32))


def hyena_pallas(u, row0, L, conv_w, conv_b, w1, b1, w2, b2, w3, freq, decay, skip, cdt=BF16, interpret=False):
    B = u.shape[0]
    rows = L // LANES
    n1 = max(2 * rows, SUBLANES)
    n_in = n1 // 2 if n1 == 2 * rows else n1
    uc = hyena_shortconv(u, row0, L, conv_w, conv_b, interpret=interpret)
    ucm = uc.transpose(0, 2, 1).reshape(B, 3 * D_HY, rows, LANES)
    if n_in != rows:
        ucm = jnp.pad(ucm, ((0, 0), (0, 0), (0, n_in - rows), (0, 0)))

    k = hyena_filters(L, w1, b1, w2, b2, w3, freq, decay, interpret=interpret)
    k = k.reshape(2 * HY_ORDER * D_HY, rows, LANES)
    if n_in != rows:
        k = jnp.pad(k, ((0, 0), (0, n_in - rows), (0, 0)))
    sr, si = fft_filter_spectra(k, n1, cdt=cdt, interpret=interpret)

    y, y_ch0 = ucm, 0
    for o in range(HY_ORDER):
        y = fft_long_conv(y, y_ch0, ucm, (o + 1) * D_HY, sr, si, (2 * o) * D_HY, (2 * o + 1) * D_HY, skip[o], n1,
                          cdt=cdt, interpret=interpret)
    return y[:, :, :rows].reshape(B, D_HY, L).transpose(0, 2, 1)


ROW_T = CTX_LEN
ROW_NT = SEQ_ALL // ROW_T
MOD_ROWS = SUBLANES
MLA_DK = LANES
ROPE_SWAP = np.arange(MLA_ROPE).reshape(2, 2, MLA_ROPE // 4)[:, ::-1].reshape(-1)


def _rms(x):
    return x * lax.rsqrt(jnp.mean(x * x, axis=-1, keepdims=True) + EPS)


def _mod_kernel(c_ref, w_ref, b_ref, o_ref):
    c = c_ref[...]
    s = (c * jax.nn.sigmoid(c)).astype(BF16)
    o_ref[...] = jnp.dot(s, w_ref[...].astype(BF16), preferred_element_type=F32) + b_ref[...]


def adaln_modulation(c, c_ctx, w_mod, b_mod, interpret=False):
    B, D = c.shape
    cs = jnp.concatenate([c, c_ctx[None], jnp.zeros((MOD_ROWS - B - 1, D), F32)], axis=0)
    out = pl.pallas_call(
        _mod_kernel,
        grid=(6,),
        in_specs=[pl.BlockSpec((MOD_ROWS, D), lambda j: (0, 0)),
                  pl.BlockSpec((D, D), lambda j: (0, j)),
                  pl.BlockSpec((1, D), lambda j: (0, j))],
        out_specs=pl.BlockSpec((MOD_ROWS, D), lambda j: (0, j)),
        out_shape=jax.ShapeDtypeStruct((MOD_ROWS, 6 * D), F32),
        compiler_params=pltpu.CompilerParams(dimension_semantics=("arbitrary",)),
        name="adaln_mod", interpret=interpret,
    )(cs, w_mod, b_mod.reshape(1, 6 * D))
    mod = out.reshape(MOD_ROWS, 6, D)
    return jnp.stack([jnp.broadcast_to(mod[B][None], (B, 6, D)), mod[:B]], axis=1)


IN_COLS = (D_LRU, D_LRU, MLA_Q_RANK, MLA_KV_RANK, 3 * D_HY, 2 * MLA_ROPE)


def _in_proj_kernel(s_ref, m_ref, g_ref, w_ref, *out_refs):
    m = m_ref[0, 0]
    h = _rms(s_ref[0]) * g_ref[...] * (1.0 + m[1:2]) + m[0:1]
    u = jnp.dot(h.astype(w_ref.dtype), w_ref[...], preferred_element_type=F32)
    off = 0
    for ref, n in zip(out_refs, IN_COLS):
        ref[0] = u[:, off:off + n]
        off += n


def in_projection(s, mod, norm_g, w_in, cdt=BF16, interpret=False):
    B, R, D = s.shape
    o = (0,) + tuple(IN_SPLITS) + (D_IN,)
    kr = w_in[:, o[4]:o[5]]
    w = jnp.concatenate([w_in[:, :o[4]], w_in[:, o[5]:], kr, kr[:, ROPE_SWAP]], axis=1).astype(cdt)
    return pl.pallas_call(
        _in_proj_kernel,
        grid=(B, R // ROW_T),
        in_specs=[pl.BlockSpec((1, ROW_T, D), lambda b, i: (b, i, 0)),
                  pl.BlockSpec((1, 1, 6, D), lambda b, i: (b, jnp.minimum(i, 1), 0, 0)),
                  pl.BlockSpec((1, D), lambda b, i: (0, 0)),
                  pl.BlockSpec(w.shape, lambda b, i: (0, 0))],
        out_specs=[pl.BlockSpec((1, ROW_T, n), lambda b, i: (b, i, 0)) for n in IN_COLS],
        out_shape=[jax.ShapeDtypeStruct((B, R, n), F32) for n in IN_COLS],
        compiler_params=pltpu.CompilerParams(
            dimension_semantics=("arbitrary", "arbitrary"), vmem_limit_bytes=VMEM_LIMIT),
        name="in_proj", interpret=interpret,
    )(s, mod, norm_g.reshape(1, D), w)


def _mla_proj_kernel(cq_ref, ckv_ref, kr_ref, rc_ref, rs_ref, qg_ref, wq_ref, wqs_ref, kg_ref, wkv_ref,
                     q_ref, k_ref, v_ref):
    cdt = wq_ref.dtype
    cqn = (_rms(cq_ref[0]) * qg_ref[...]).astype(cdt)
    qa = jnp.dot(cqn, wq_ref[...], preferred_element_type=F32)
    qs = jnp.dot(cqn, wqs_ref[...], preferred_element_type=F32)
    ckvn = (_rms(ckv_ref[0]) * kg_ref[...]).astype(cdt)
    kv = jnp.dot(ckvn, wkv_ref[...], preferred_element_type=F32)
    rc, rs = rc_ref[...], rs_ref[...]
    kr = kr_ref[0]
    k_rope = (kr[:, :MLA_ROPE] * rc[:, MLA_NOPE:MLA_NOPE + MLA_ROPE]
              + kr[:, MLA_ROPE:] * rs[:, MLA_NOPE:MLA_NOPE + MLA_ROPE])
    pad = jnp.zeros((kr.shape[0], MLA_DK - MLA_NOPE - MLA_ROPE), F32)
    for h in range(MLA_HEADS):
        sl = slice(h * MLA_DK, (h + 1) * MLA_DK)
        q_ref[0, h] = ((qa[:, sl] * rc + qs[:, sl] * rs) * MLA_SCALE).astype(q_ref.dtype)
        kvh = kv[:, h * (MLA_NOPE + MLA_V):(h + 1) * (MLA_NOPE + MLA_V)]
        k_ref[0, h] = jnp.concatenate([kvh[:, :MLA_NOPE], k_rope, pad], axis=1).astype(k_ref.dtype)
        v_ref[0, h] = jnp.concatenate([kvh[:, MLA_NOPE:], jnp.ones((kvh.shape[0], MLA_V), F32)],
                                      axis=1).astype(v_ref.dtype)


def _rope_tables():
    grid_rows = SEQ // GRID_W
    row = jnp.repeat(jnp.arange(grid_rows, dtype=F32), GRID_W)
    col = jnp.tile(jnp.arange(GRID_W, dtype=F32), grid_rows)
    inv_freq = ROPE_THETA ** (-jnp.arange(ROPE_PAIRS, dtype=F32) / ROPE_PAIRS)
    ang_r, ang_c = row[:, None] * inv_freq, col[:, None] * inv_freq
    cos = jnp.concatenate([jnp.cos(ang_r), jnp.cos(ang_r), jnp.cos(ang_c), jnp.cos(ang_c)], axis=1)
    sin = jnp.concatenate([-jnp.sin(ang_r), jnp.sin(ang_r), -jnp.sin(ang_c), jnp.sin(ang_c)], axis=1)
    npad = MLA_DK - MLA_NOPE - MLA_ROPE
    cos = jnp.concatenate([jnp.ones((SEQ, MLA_NOPE), F32), cos, jnp.zeros((SEQ, npad), F32)], axis=1)
    sin = jnp.concatenate([jnp.zeros((SEQ, MLA_NOPE), F32), sin, jnp.zeros((SEQ, npad), F32)], axis=1)
    ctx_cos = jnp.concatenate([jnp.ones((CTX_LEN, MLA_NOPE + MLA_ROPE), F32), jnp.zeros((CTX_LEN, npad), F32)], axis=1)
    return (jnp.concatenate([ctx_cos, cos], axis=0),
            jnp.concatenate([jnp.zeros((CTX_LEN, MLA_DK), F32), sin], axis=0))


def mla_projection(cq, ckv, kr2, q_g, wqb, kv_g, wkvb, cdt=BF16, interpret=False):
    B, R, _ = cq.shape
    dq = MLA_NOPE + MLA_ROPE
    wq3 = wqb.reshape(MLA_Q_RANK, MLA_HEADS, dq)
    zq = jnp.zeros((MLA_Q_RANK, MLA_HEADS, MLA_DK - dq), wqb.dtype)
    wq = jnp.concatenate([wq3, zq], axis=2).reshape(MLA_Q_RANK, MLA_HEADS * MLA_DK).astype(cdt)
    wqs = jnp.concatenate([jnp.zeros((MLA_Q_RANK, MLA_HEADS, MLA_NOPE), wqb.dtype),
                           wq3[:, :, MLA_NOPE:][:, :, ROPE_SWAP], zq], axis=2)
    wqs = wqs.reshape(MLA_Q_RANK, MLA_HEADS * MLA_DK).astype(cdt)
    rc, rs = _rope_tables()
    row = lambda n: pl.BlockSpec((1, ROW_T, n), lambda b, i: (b, i, 0))
    tab = pl.BlockSpec((ROW_T, MLA_DK), lambda b, i: (i, 0))
    full = lambda shape: pl.BlockSpec(shape, lambda b, i: (0,) * len(shape))
    head = lambda n: pl.BlockSpec((1, MLA_HEADS, ROW_T, n), lambda b, i: (b, 0, i, 0))
    return pl.pallas_call(
        _mla_proj_kernel,
        grid=(B, R // ROW_T),
        in_specs=[row(MLA_Q_RANK), row(MLA_KV_RANK), row(2 * MLA_ROPE), tab, tab,
                  full((1, MLA_Q_RANK)), full(wq.shape), full(wqs.shape),
                  full((1, MLA_KV_RANK)), full(wkvb.shape)],
        out_specs=[head(MLA_DK), head(MLA_DK), head(2 * MLA_V)],
        out_shape=[jax.ShapeDtypeStruct((B, MLA_HEADS, R, MLA_DK), cdt),
                   jax.ShapeDtypeStruct((B, MLA_HEADS, R, MLA_DK), cdt),
                   jax.ShapeDtypeStruct((B, MLA_HEADS, R, 2 * MLA_V), cdt)],
        compiler_params=pltpu.CompilerParams(dimension_semantics=("arbitrary", "arbitrary")),
        name="mla_proj", interpret=interpret,
    )(cq, ckv, kr2, rc, rs, q_g.reshape(1, -1), wq, wqs, kv_g.reshape(1, -1), wkvb.astype(cdt))


def _out_proj_kernel(s_ref, m_ref, g_ref, ya_ref, ybx_ref, ybc_ref, ycx_ref, ycc_ref, w_ref, snew_ref, hT_ref):
    cdt = w_ref.dtype
    is_ctx = pl.program_id(1) == 0
    yb = jnp.where(is_ctx, ybc_ref[0], ybx_ref[0])
    yc = jnp.where(is_ctx, ycc_ref[0], ycx_ref[0])
    mix = (jnp.dot(ya_ref[0].astype(cdt), w_ref[0:D_LRU, :], preferred_element_type=F32)
           + jnp.dot(yb.astype(cdt), w_ref[D_LRU:D_LRU + MLA_HEADS * MLA_V, :], preferred_element_type=F32)
           + jnp.dot(yc.astype(cdt), w_ref[D_LRU + MLA_HEADS * MLA_V:, :], preferred_element_type=F32))
    m = m_ref[0, 0]
    s_new = s_ref[0] + m[2:3] * mix
    snew_ref[0] = s_new
    h2 = _rms(s_new) * g_ref[...] * (1.0 + m[4:5]) + m[3:4]
    hT_ref[...] = h2.T.astype(hT_ref.dtype)


def out_projection(s, mod, norm2_g, ya, yb_x, yb_c, yc_x, yc_c, w_out, cdt=BF16, interpret=False):
    B, R, D = s.shape
    nt = R // ROW_T
    row = lambda n: pl.BlockSpec((1, ROW_T, n), lambda b, i: (b, i, 0))
    lat = lambda n: pl.BlockSpec((1, ROW_T, n), lambda b, i: (b, jnp.maximum(i - 1, 0), 0))
    ctx = lambda n: pl.BlockSpec((1, ROW_T, n), lambda b, i: (b, 0, 0))
    return pl.pallas_call(
        _out_proj_kernel,
        grid=(B, nt),
        in_specs=[row(D),
                  pl.BlockSpec((1, 1, 6, D), lambda b, i: (b, jnp.minimum(i, 1), 0, 0)),
                  pl.BlockSpec((1, D), lambda b, i: (0, 0)),
                  row(D_LRU),
                  lat(MLA_HEADS * MLA_V), ctx(MLA_HEADS * MLA_V), lat(D_HY), ctx(D_HY),
                  pl.BlockSpec((D_MIX, D), lambda b, i: (0, 0))],
        out_specs=[row(D), pl.BlockSpec((D, ROW_T), lambda b, i: (0, b * nt + i))],
        out_shape=[jax.ShapeDtypeStruct((B, R, D), F32), jax.ShapeDtypeStruct((D, B * R), cdt)],
        compiler_params=pltpu.CompilerParams(
            dimension_semantics=("arbitrary", "arbitrary"), vmem_limit_bytes=VMEM_LIMIT),
        name="out_proj", interpret=interpret,
    )(s, mod, norm2_g.reshape(1, D), ya, yb_x, yb_c, yc_x, yc_c, w_out.astype(cdt))


PEER_TM = 768
PEER_TE = 1024
PEER_KP = 512
NEG_BIG = -3.0e38
PACKED_ROWS = 2 * SUBLANES


def _topk_desc(s, k):
    rank = jnp.full(s.shape, float(PEER_NKEYS), F32)
    vals = []
    for r in range(k):
        m = jnp.max(s, axis=0, keepdims=True)
        hit = s == m
        rank = jnp.where(hit, float(r), rank)
        s = jnp.where(hit, NEG_BIG, s)
        vals.append(m)
    return jnp.concatenate(vals, axis=0), rank


def _route_chunk(s1, s2):
    K = PEER_TOPK
    v1, rank1 = _topk_desc(s1, K)
    v2, rank2 = _topk_desc(s2, K)
    half = K // 2
    cands = ([v1[0:1] + v2] + [v1[i:i + 1] + v2[:half] for i in range(1, half)]
             + [v1[i:i + 1] + v2[0:1] for i in range(half, K)])
    cmax = v1[0:1] + v2[0:1]
    c = jnp.concatenate(cands[:half] + [v1[half:] + v2[0:1]], axis=0)
    z = jnp.zeros_like(cmax)
    m = cmax
    for r in range(K):
        m = jnp.max(c, axis=0, keepdims=True)
        z = z + jnp.exp(m - cmax)
        if r + 1 < K:
            c = jnp.where(c == m, NEG_BIG, c)
    thr = m
    n = jnp.zeros_like(s1)
    for i in range(K):
        n_i = jnp.sum(jnp.where(cands[i] >= thr, 1.0, 0.0), axis=0, keepdims=True)
        n = jnp.where(rank1 == float(i), n_i, n)
    w = jnp.exp(s1 - v1[0:1]) / z
    e2 = jnp.exp(s2 - v2[0:1])
    return rank2, e2, n, w


def _peer_route_kernel(hT_ref, wqT_ref, keys_ref, rank2_ref, e2_ref, n_ref, w_ref, q_scr, s1_scr, s2_scr):
    h = pl.program_id(1)

    @pl.when(h == 0)
    def _():
        q_scr[...] = jnp.dot(wqT_ref[...], hT_ref[...], preferred_element_type=F32)

    cdt = keys_ref.dtype
    half = PEER_DQ // 2
    row = pl.multiple_of(h * PEER_DQ, PEER_DQ)
    q1 = q_scr[pl.ds(row, half), :].astype(cdt)
    q2 = q_scr[pl.ds(row + half, half), :].astype(cdt)
    s1_scr[...] = jnp.dot(keys_ref[0], q1, preferred_element_type=F32)
    s2_scr[...] = jnp.dot(keys_ref[1], q2, preferred_element_type=F32)

    def chunk(c, carry):
        sl = pl.ds(pl.multiple_of(c * LANES, LANES), LANES)
        rank2, e2, n, w = _route_chunk(s1_scr[:, sl], s2_scr[:, sl])
        rank2_ref[0, :, sl] = rank2.astype(rank2_ref.dtype)
        e2_ref[0, :, sl] = e2.astype(e2_ref.dtype)
        n_ref[0, :, sl] = n
        w_ref[0, :, sl] = w
        return carry

    lax.fori_loop(0, s1_scr.shape[1] // LANES, chunk, 0, unroll=2)


def _peer_dense_kernel(hT_ref, rank2_ref, e2_ref, n_ref, w_ref, u_ref, vT_ref, s_ref, g_ref, o_ref, acc_ref, pre_ref,
                       *, n_chunks):
    j = pl.program_id(1)

    te, tm = u_ref.shape[0], hT_ref.shape[1]
    gdt = rank2_ref.dtype

    def first_matmul(wr):
        pre_ref[wr] = jnp.dot(u_ref[...], hT_ref[...], preferred_element_type=F32)

    def second_matmul(rd):
        out = None
        for k0 in range(0, te, PEER_KP):
            acts = []
            for al in range(k0 // PEER_NKEYS, (k0 + PEER_KP) // PEER_NKEYS):
                g = None
                for h in range(PEER_HEADS):
                    n16 = jnp.broadcast_to(n_ref[h, 0, al:al + 1, :], (PACKED_ROWS, tm)).astype(gdt)
                    w16 = jnp.broadcast_to(w_ref[h, 0, al:al + 1, :], (PACKED_ROWS, tm)).astype(gdt)
                    nb = jnp.concatenate([n16] * (PEER_NKEYS // PACKED_ROWS), axis=0)
                    wb = jnp.concatenate([w16] * (PEER_NKEYS // PACKED_ROWS), axis=0)
                    t = jnp.where(rank2_ref[h] < nb, e2_ref[h] * wb, jnp.zeros_like(wb))
                    g = t if g is None else g + t
                blk = slice(al * PEER_NKEYS, (al + 1) * PEER_NKEYS)
                acts.append((g * _gelu_tanh(pre_ref[rd, blk, :].astype(gdt))).astype(vT_ref.dtype))
            part = jnp.dot(vT_ref[0, :, k0:k0 + PEER_KP], jnp.concatenate(acts, axis=0), preferred_element_type=F32)
            out = part if out is None else out + part
        acc_ref[...] += out

    def step(rd, wr):
        first_matmul(wr)
        second_matmul(rd)

    @pl.when(j == 0)
    def _():
        acc_ref[...] = jnp.zeros_like(acc_ref)
        first_matmul(0)

    for parity in range(2):
        pl.when((j % 2 == parity) & (j > 0) & (j < n_chunks))(functools.partial(step, 1 - parity, parity))

    @pl.when(j == n_chunks)
    def _():
        second_matmul((n_chunks - 1) % 2)
        y = acc_ref[...].T
        for r in range(g_ref.shape[0]):
            rows = slice(r * ROW_T, (r + 1) * ROW_T)
            o_ref[rows, :] = s_ref[rows, :] + g_ref[r] * y[rows, :]


def peer_tokens(hT, s, gate, wq, keys, u_tab, v_tab, interpret=False):
    D, T = hT.shape
    cdt = hT.dtype
    tm, te = PEER_TM, PEER_TE
    assert T % tm == 0 and PEER_EXPERTS % te == 0 and tm % ROW_T == 0
    wqT = wq.T.astype(cdt)
    route_shape = jax.ShapeDtypeStruct((PEER_HEADS, PEER_NKEYS, T), F32)
    packed_shape = jax.ShapeDtypeStruct((PEER_HEADS, PEER_NKEYS, T), cdt)
    route_spec = pl.BlockSpec((1, PEER_NKEYS, tm), lambda i, hh: (hh, 0, i))
    rank2, e2, n, w = pl.pallas_call(
        _peer_route_kernel,
        grid=(T // tm, PEER_HEADS),
        in_specs=[pl.BlockSpec((D, tm), lambda i, hh: (0, i)),
                  pl.BlockSpec((PEER_HEADS * PEER_DQ, D), lambda i, hh: (0, 0)),
                  pl.BlockSpec((2, PEER_NKEYS, PEER_DQ // 2), lambda i, hh: (0, 0, 0))],
        out_specs=[route_spec] * 4,
        out_shape=[packed_shape, packed_shape, route_shape, route_shape],
        scratch_shapes=[pltpu.VMEM((PEER_HEADS * PEER_DQ, tm), F32),
                        pltpu.VMEM((PEER_NKEYS, tm), F32),
                        pltpu.VMEM((PEER_NKEYS, tm), F32)],
        compiler_params=pltpu.CompilerParams(
            dimension_semantics=("arbitrary", "arbitrary"), vmem_limit_bytes=VMEM_LIMIT),
        name="peer_route", interpret=interpret,
    )(hT, wqT, keys.astype(cdt))

    head_spec = pl.BlockSpec((PEER_HEADS, PEER_NKEYS, tm), lambda i, j: (0, 0, i))
    nj = PEER_EXPERTS // te
    groups = te // PEER_NKEYS
    row_spec = pl.BlockSpec((PEER_HEADS, 1, groups, tm), lambda i, j: (0, jnp.maximum(j - 1, 0), 0, i))
    n = n.reshape(PEER_HEADS, nj, groups, T)
    w = w.reshape(PEER_HEADS, nj, groups, T)
    vT = v_tab.astype(cdt).reshape(nj, te, D).transpose(0, 2, 1)
    return pl.pallas_call(
        functools.partial(_peer_dense_kernel, n_chunks=nj),
        grid=(T // tm, nj + 1),
        in_specs=[pl.BlockSpec((D, tm), lambda i, j: (0, i)),
                  head_spec, head_spec, row_spec, row_spec,
                  pl.BlockSpec((te, D), lambda i, j: (jnp.minimum(j, nj - 1), 0)),
                  pl.BlockSpec((1, D, te), lambda i, j: (jnp.maximum(j - 1, 0), 0, 0)),
                  pl.BlockSpec((tm, D), lambda i, j: (i, 0)),
                  pl.BlockSpec((tm // ROW_T, 1, D), lambda i, j: (i, 0, 0))],
        out_specs=pl.BlockSpec((tm, D), lambda i, j: (i, 0)),
        out_shape=jax.ShapeDtypeStruct((T, D), F32),
        scratch_shapes=[pltpu.VMEM((D, tm), F32), pltpu.VMEM((2, te, tm), F32)],
        compiler_params=pltpu.CompilerParams(
            dimension_semantics=("arbitrary", "arbitrary"), vmem_limit_bytes=VMEM_LIMIT),
        name="peer_dense", interpret=interpret,
    )(hT, rank2, e2, n, w, u_tab.astype(cdt), vT, s, gate)


def _final_norm_kernel(x_ref, g_ref, o_ref):
    o_ref[0] = _rms(x_ref[0]) * g_ref[...]


def final_norm(s, g):
    B, R, D = s.shape
    return pl.pallas_call(
        _final_norm_kernel,
        grid=(B, SEQ // ROW_T),
        in_specs=[pl.BlockSpec((1, ROW_T, D), lambda b, i: (b, i + CTX_LEN // ROW_T, 0)),
                  pl.BlockSpec((1, D), lambda b, i: (0, 0))],
        out_specs=pl.BlockSpec((1, ROW_T, D), lambda b, i: (b, i, 0)),
        out_shape=jax.ShapeDtypeStruct((B, SEQ, D), F32),
        compiler_params=pltpu.CompilerParams(dimension_semantics=("arbitrary", "arbitrary")),
        name="final_norm",
    )(s, g.reshape(1, D))


def kernel(x, c, ctx, c_ctx, w_mod, b_mod, norm1_g, norm2_g, w_in, w_out,
           lru_conv_w, lru_conv_b, lru_wr, lru_br, lru_wi, lru_bi, lru_lambda,
           mla_q_norm_g, mla_wqb, mla_kv_norm_g, mla_wkvb,
           hy_conv_w, hy_conv_b, hy_f_w1, hy_f_b1, hy_f_w2, hy_f_b2, hy_f_w3,
           hy_f_freq, hy_decay, hy_skip,
           peer_wq, peer_keys, peer_u, peer_v, final_g):
    B = x.shape[0]
    s = jnp.concatenate([ctx, x], axis=1)
    for l in range(DEPTH):
        need_ctx = l < DEPTH - 1
        mod = adaln_modulation(c, c_ctx, w_mod[l], b_mod[l])
        u_lru, u_gate, cq, ckv, u_hy, kr2 = in_projection(s, mod, norm1_g[l], w_in[l])

        ya = rglru_mixer(u_lru, u_gate, lru_conv_w[l], lru_conv_b[l], lru_wr[l], lru_br[l], lru_wi[l], lru_bi[l],
                         lru_lambda[l])

        q, k, v = mla_projection(cq, ckv, kr2, mla_q_norm_g[l], mla_wqb[l], mla_kv_norm_g[l], mla_wkvb[l])
        yb_x = flash_attention(q, k, v, q_row0=CTX_LEN, n_q=SEQ, n_k=SEQ_ALL, tk=SEQ_ALL // 3, n_sub=1)
        if need_ctx:
            yb_c = flash_attention(q, k, v, q_row0=0, n_q=CTX_LEN, n_k=CTX_LEN, tk=CTX_LEN, n_sub=1)
        else:
            yb_c = jnp.zeros((B, CTX_LEN, MLA_HEADS * MLA_V), F32)

        hy_args = (hy_conv_w[l], hy_conv_b[l], hy_f_w1[l], hy_f_b1[l], hy_f_w2[l], hy_f_b2[l], hy_f_w3[l],
                   hy_f_freq[l], hy_decay[l], hy_skip[l])
        yc_x = hyena_pallas(u_hy, CTX_LEN, SEQ, *hy_args)
        yc_c = hyena_pallas(u_hy, 0, CTX_LEN, *hy_args) if need_ctx else jnp.zeros((B, CTX_LEN, D_HY), F32)

        s, hT = out_projection(s, mod, norm2_g[l], ya, yb_x, yb_c, yc_x, yc_c, w_out[l])
        gate2 = jnp.repeat(mod[:, :, 5], jnp.array([1, ROW_NT - 1]), axis=1, total_repeat_length=ROW_NT)
        s = peer_tokens(hT, s.reshape(B * SEQ_ALL, D_MODEL), gate2.reshape(B * ROW_NT, 1, D_MODEL),
                        peer_wq[l], peer_keys[l], peer_u[l], peer_v[l]).reshape(B, SEQ_ALL, D_MODEL)
    return final_norm(s, final_g)
```

```python
import functools
import math

import jax
import jax.numpy as jnp
import numpy as np
from jax import lax
from jax.experimental import pallas as pl
from jax.experimental.pallas import tpu as pltpu

D_MODEL = 1024
BATCH = 2
SEQ = 8192
DEPTH = 2

GRID_W = 64
CTX_LEN = 256
EPS = 1e-6

D_LRU = 384
LRU_BLOCKS = 6
LRU_BLOCK_DIM = D_LRU // LRU_BLOCKS
LRU_CONV = 4
LRU_C = 8.0

MLA_HEADS = 6
MLA_NOPE = 64
MLA_ROPE = 32
MLA_V = 64
MLA_Q_RANK = 384
MLA_KV_RANK = 256
MLA_SCALE = (MLA_NOPE + MLA_ROPE) ** -0.5
ROPE_PAIRS = MLA_ROPE // 4
ROPE_THETA = 10000.0
ATTN_BLOCK = 128

D_HY = 256
HY_CONV = 3
HY_ORDER = 2
HY_BANDS = 8
HY_EMB = 1 + 2 * HY_BANDS
HY_FFN = 64

D_MIX = D_LRU + MLA_HEADS * MLA_V + D_HY
D_IN = 2 * D_LRU + MLA_Q_RANK + MLA_KV_RANK + MLA_ROPE + 3 * D_HY
IN_SPLITS = [D_LRU, 2 * D_LRU, 2 * D_LRU + MLA_Q_RANK,
             2 * D_LRU + MLA_Q_RANK + MLA_KV_RANK,
             2 * D_LRU + MLA_Q_RANK + MLA_KV_RANK + MLA_ROPE]

PEER_HEADS = 8
PEER_NKEYS = 128
PEER_EXPERTS = PEER_NKEYS * PEER_NKEYS
PEER_DQ = 256
PEER_TOPK = 16
PEER_BLOCK = 128

F32 = jnp.float32


BF16 = jnp.bfloat16
LANES = 128
SUBLANES = 8
VMEM_LIMIT = 56 * 1024 * 1024
SEQ_ALL = CTX_LEN + SEQ
LRU_TT = CTX_LEN
LRU_NT = SEQ_ALL // LRU_TT


def _gelu_tanh(x):
    return 0.5 * x * (1.0 + jnp.tanh(0.7978845608028654 * (x + 0.044715 * (x * x * x))))


def _lru_coeff_kernel(u_ref, up_ref, un_ref, cw_ref, cb_ref, wg_ref, bg_ref, c_ref,
                      af_ref, bf_ref, ab_ref, bb_ref):
    i = pl.program_id(1)
    x = u_ref[0]
    tt = x.shape[0]
    use_prev = i >= 2
    use_next = jnp.logical_and(i >= 1, i < LRU_NT - 1)
    p2 = jnp.where(use_prev, up_ref[0][SUBLANES - 2:SUBLANES], 0.0)
    n1 = jnp.where(use_next, un_ref[0][0:1], 0.0)
    ext = jnp.concatenate([p2, x, n1], axis=0)
    xc = cb_ref[...] + sum(cw_ref[k:k + 1, :] * ext[k:k + tt] for k in range(LRU_CONV))
    g = jnp.dot(xc.astype(wg_ref.dtype), wg_ref[...], preferred_element_type=F32) + bg_ref[...]
    for d, (a_ref, b_ref) in enumerate(((af_ref, bf_ref), (ab_ref, bb_ref))):
        r = jax.nn.sigmoid(g[:, (2 * d) * D_LRU:(2 * d + 1) * D_LRU])
        gi = jax.nn.sigmoid(g[:, (2 * d + 1) * D_LRU:(2 * d + 2) * D_LRU])
        log_a = c_ref[d:d + 1, :] * r
        a_ref[0] = jnp.exp(log_a)
        b_ref[0] = jnp.sqrt(1.0 - jnp.exp(2.0 * log_a)) * (gi * xc)


def _lru_scan_rows(a_ref, b_ref, h_scr, emit, reverse):
    tt = a_ref.shape[1]
    nb = a_ref.shape[0]

    def body(s, hs):
        t = (tt - 1 - s) if reverse else s
        out = []
        for b in range(nb):
            h = a_ref[b, pl.ds(t, 1), :] * hs[b] + b_ref[b, pl.ds(t, 1), :]
            emit(b, t, h)
            out.append(h)
        return tuple(out)

    hs = lax.fori_loop(0, tt, body, tuple(h_scr[b:b + 1, :] for b in range(nb)), unroll=8)
    for b in range(nb):
        h_scr[b:b + 1, :] = hs[b]


def _lru_fwd_kernel(a_ref, b_ref, hf_ref, h_scr):
    @pl.when(pl.program_id(0) == 0)
    def _():
        h_scr[...] = jnp.zeros_like(h_scr)

    def emit(b, t, h):
        hf_ref[b, pl.ds(t, 1), :] = h

    _lru_scan_rows(a_ref, b_ref, h_scr, emit, reverse=False)


def _lru_bwd_kernel(a_ref, b_ref, hf_ref, gate_ref, y_ref, h_scr):
    @pl.when(pl.program_id(0) == 0)
    def _():
        h_scr[...] = jnp.zeros_like(h_scr)

    def emit(b, t, h):
        y_ref[b, pl.ds(t, 1), :] = h

    _lru_scan_rows(a_ref, b_ref, h_scr, emit, reverse=True)
    y_ref[...] = (y_ref[...] + hf_ref[...]) * _gelu_tanh(gate_ref[...])


def rglru_mixer(u_lru, u_gate, conv_w, conv_b, wr, br, wi, bi, lam, cdt=BF16, interpret=False):
    B = u_lru.shape[0]
    tt, nt = LRU_TT, LRU_NT
    hb = tt // SUBLANES

    def blockdiag(w):
        eye = jnp.eye(LRU_BLOCKS, dtype=w.dtype)
        return jnp.einsum('nde,nm->ndme', w, eye).reshape(D_LRU, D_LRU)

    wg = jnp.concatenate([blockdiag(wr[0]), blockdiag(wi[0]), blockdiag(wr[1]), blockdiag(wi[1])], axis=1).astype(cdt)
    bg = jnp.concatenate([br[0], bi[0], br[1], bi[1]]).reshape(1, 4 * D_LRU).astype(F32)
    cdec = -LRU_C * jax.nn.softplus(-lam.astype(F32))

    tile = pl.BlockSpec((1, tt, D_LRU), lambda b, i: (b, i, 0))
    full = lambda shape: pl.BlockSpec(shape, lambda b, i: (0,) * len(shape))
    coeff_shape = jax.ShapeDtypeStruct((B, SEQ_ALL, D_LRU), F32)
    a_f, b_f, a_b, b_b = pl.pallas_call(
        _lru_coeff_kernel,
        grid=(B, nt),
        in_specs=[tile,
                  pl.BlockSpec((1, SUBLANES, D_LRU), lambda b, i: (b, jnp.maximum(i * hb - 1, 0), 0)),
                  pl.BlockSpec((1, SUBLANES, D_LRU), lambda b, i: (b, jnp.minimum((i + 1) * hb, nt * hb - 1), 0)),
                  full((LRU_CONV, D_LRU)), full((1, D_LRU)), full((D_LRU, 4 * D_LRU)), full((1, 4 * D_LRU)),
                  full((2, D_LRU))],
        out_specs=[tile] * 4,
        out_shape=[coeff_shape] * 4,
        compiler_params=pltpu.CompilerParams(dimension_semantics=("arbitrary", "arbitrary")),
        name="lru_coeff", interpret=interpret,
    )(u_lru, u_lru, u_lru, conv_w.astype(F32), conv_b.reshape(1, D_LRU).astype(F32), wg, bg, cdec)

    seq = pl.BlockSpec((B, tt, D_LRU), lambda s: (0, s, 0))
    h_f = pl.pallas_call(
        _lru_fwd_kernel,
        grid=(nt,),
        in_specs=[seq, seq],
        out_specs=seq,
        out_shape=coeff_shape,
        scratch_shapes=[pltpu.VMEM((B, D_LRU), F32)],
        compiler_params=pltpu.CompilerParams(dimension_semantics=("arbitrary",)),
        name="lru_fwd", interpret=interpret,
    )(a_f, b_f)

    rseq = pl.BlockSpec((B, tt, D_LRU), lambda s: (0, jnp.where(s == 0, 0, nt - s), 0))
    return pl.pallas_call(
        _lru_bwd_kernel,
        grid=(nt,),
        in_specs=[rseq, rseq, rseq, rseq],
        out_specs=rseq,
        out_shape=coeff_shape,
        scratch_shapes=[pltpu.VMEM((B, D_LRU), F32)],
        compiler_params=pltpu.CompilerParams(dimension_semantics=("arbitrary",)),
        name="lru_bwd", interpret=interpret,
    )(a_b, b_b, h_f, u_gate)


ATT_TQ = 256


ATT_HG = 2


def _flash_kernel(*refs, tk):
    q_refs, (k_ref, v_ref, o_ref) = refs[:-3], refs[-3:]
    hg, tq = q_refs[0].shape[1], q_refs[0].shape[2]
    dv = v_ref.shape[3] // 2
    n_chunks = k_ref.shape[2] // tk
    chains = [(q_ref[0, h], h) for q_ref in q_refs for h in range(hg)]

    def body(c, carry):
        off = pl.multiple_of(c * tk, LANES)
        out = []
        for (q, h), (m, acc) in zip(chains, carry):
            s = lax.dot_general(q, k_ref[0, h, pl.ds(off, tk), :], (((1,), (1,)), ((), ())),
                                preferred_element_type=F32)
            m_new = jnp.maximum(m, jnp.max(s, axis=1, keepdims=True))
            p = jnp.exp(s - m_new).astype(v_ref.dtype)
            pv = jnp.dot(p, v_ref[0, h, pl.ds(off, tk), :], preferred_element_type=F32)
            out.append((m_new, jnp.exp(m - m_new) * acc + pv))
        return tuple(out)

    init = tuple((jnp.full((tq, 1), NEG_BIG, F32), jnp.zeros((tq, 2 * dv), F32)) for _ in chains)
    res = lax.fori_loop(0, n_chunks, body, init, unroll=True)
    for t in range(len(q_refs)):
        o_ref[0, t * tq:(t + 1) * tq, :] = jnp.concatenate(
            [acc[:, :dv] / acc[:, dv:] for _, acc in res[t * hg:(t + 1) * hg]], axis=1)


def flash_attention(q, k, v, q_row0, n_q, n_k, tk, n_sub, interpret=False):
    B, H, R, dk = q.shape
    dv = v.shape[3] // 2
    tq = ATT_TQ
    assert n_q % (n_sub * tq) == 0 and q_row0 % tq == 0 and n_k % tk == 0 and tk % LANES == 0 and R % n_k == 0
    q0 = q_row0 // tq
    q_spec = lambda t: pl.BlockSpec((1, ATT_HG, tq, dk), lambda b, g, i: (b, g, n_sub * i + t + q0, 0))
    return pl.pallas_call(
        functools.partial(_flash_kernel, tk=tk),
        grid=(B, H // ATT_HG, n_q // (n_sub * tq)),
        in_specs=[q_spec(t) for t in range(n_sub)]
                 + [pl.BlockSpec((1, ATT_HG, n_k, dk), lambda b, g, i: (b, g, 0, 0)),
                    pl.BlockSpec((1, ATT_HG, n_k, 2 * dv), lambda b, g, i: (b, g, 0, 0))],
        out_specs=pl.BlockSpec((1, n_sub * tq, ATT_HG * dv), lambda b, g, i: (b, i, g)),
        out_shape=jax.ShapeDtypeStruct((B, n_q, H * dv), F32),
        compiler_params=pltpu.CompilerParams(
            dimension_semantics=("arbitrary", "arbitrary", "arbitrary"), vmem_limit_bytes=VMEM_LIMIT),
        name="flash_attention", interpret=interpret,
    )(*([q] * n_sub), k, v)


HY_CB = 16
HY_FR = 128
HY_UNROLL = 16


def _dft_tables(n1, n_in, n_out, cdt):
    n = n1 * LANES
    k1 = np.arange(n1)[:, None]
    f1 = np.exp(-2j * np.pi * k1 * np.arange(n_in)[None, :] / n1)
    tw = np.exp(-2j * np.pi * k1 * np.arange(LANES)[None, :] / n)
    f2 = np.exp(-2j * np.pi * np.outer(np.arange(LANES), np.arange(LANES)) / LANES)
    g1 = np.conj(np.exp(-2j * np.pi * k1 * np.arange(n_out)[None, :] / n1)).T / n
    blk = lambda m: np.block([[m.real, -m.imag], [m.imag, m.real]])
    return dict(
        ma=jnp.asarray(blk(f1), cdt),
        ma_re=jnp.asarray(np.concatenate([f1.real, f1.imag], axis=0), cdt),
        tr=jnp.asarray(tw.real, F32), ti=jnp.asarray(tw.imag, F32),
        f2=jnp.asarray(np.concatenate([f2.real, f2.imag], axis=1), cdt),
        f2c=jnp.asarray(np.concatenate([f2.real, -f2.imag], axis=1), cdt),
        mi=jnp.asarray(blk(g1), cdt),
    )


def _dft_rows_stage(src, ma_ref, tr_ref, ti_ref, planar_scr):
    n1 = tr_ref.shape[0]

    def body(c, carry):
        o = jnp.dot(ma_ref[...], src(c).astype(ma_ref.dtype), preferred_element_type=F32)
        ar, ai = o[:n1], o[n1:]
        tr, ti = tr_ref[...], ti_ref[...]
        planar_scr[c, :n1] = (ar * tr - ai * ti).astype(planar_scr.dtype)
        planar_scr[c, n1:] = (ar * ti + ai * tr).astype(planar_scr.dtype)
        return carry

    lax.fori_loop(0, planar_scr.shape[0], body, 0, unroll=HY_UNROLL)


def _dft_lanes_stage(planar_scr, f_ref):
    cb, rows, _ = planar_scr.shape
    n1 = rows // 2
    p = jnp.dot(planar_scr[...].reshape(cb * rows, LANES), f_ref[...], preferred_element_type=F32)
    p = p.reshape(cb, rows, 2 * LANES)
    return p[:, :n1, :LANES] - p[:, n1:, LANES:], p[:, :n1, LANES:] + p[:, n1:, :LANES]


def _fft_filter_kernel(k_ref, ma_ref, tr_ref, ti_ref, f2_ref, xr_ref, xi_ref, a_scr):
    _dft_rows_stage(lambda c: k_ref[c], ma_ref, tr_ref, ti_ref, a_scr)
    xr, xi = _dft_lanes_stage(a_scr, f2_ref)
    xr_ref[...] = xr
    xi_ref[...] = xi


def _fft_conv_kernel(z_ref, gate_ref, kfr_ref, kfi_ref, kbr_ref, kbi_ref, skip_ref,
                     ma_ref, tr_ref, ti_ref, f2_ref, f2c_ref, mi_ref, o_ref, a_scr, b_scr):
    n1 = tr_ref.shape[0]
    n_out = o_ref.shape[2]
    sdt = a_scr.dtype
    _dft_rows_stage(lambda c: jnp.concatenate([z_ref[0, c], z_ref[1, c]], axis=0), ma_ref, tr_ref, ti_ref, a_scr)
    xr, xi = _dft_lanes_stage(a_scr, f2_ref)
    hr = kfr_ref[...] + kbr_ref[...]
    hi = kfi_ref[...] - kbi_ref[...]
    b_scr[:, :n1] = (xr * hr - xi * hi).astype(sdt)
    b_scr[:, n1:] = (xr * hi + xi * hr).astype(sdt)
    br, bi = _dft_lanes_stage(b_scr, f2c_ref)
    tr, ti = tr_ref[...][None], ti_ref[...][None]
    a_scr[:, :n1] = (br * tr + bi * ti).astype(sdt)
    a_scr[:, n1:] = (bi * tr - br * ti).astype(sdt)

    def body(c, carry):
        y = jnp.dot(mi_ref[...], a_scr[c], preferred_element_type=F32)
        sk = skip_ref[c]
        o_ref[0, c] = gate_ref[0, c] * (y[:n_out] + sk * z_ref[0, c][:n_out])
        o_ref[1, c] = gate_ref[1, c] * (y[n_out:] + sk * z_ref[1, c][:n_out])
        return carry

    lax.fori_loop(0, z_ref.shape[1], body, 0, unroll=HY_UNROLL)


def fft_filter_spectra(k_rows, n1, cdt=BF16, interpret=False):
    R, n_in, _ = k_rows.shape
    t = _dft_tables(n1, n_in, n_in, cdt)
    full = lambda a: pl.BlockSpec(a.shape, lambda i: (0,) * a.ndim)
    spec = jax.ShapeDtypeStruct((R, n1, LANES), F32)
    consts = (t["ma_re"], t["tr"], t["ti"], t["f2"])
    return pl.pallas_call(
        _fft_filter_kernel,
        grid=(R // HY_CB,),
        in_specs=[pl.BlockSpec((HY_CB, n_in, LANES), lambda i: (i, 0, 0))] + [full(a) for a in consts],
        out_specs=[pl.BlockSpec((HY_CB, n1, LANES), lambda i: (i, 0, 0))] * 2,
        out_shape=[spec, spec],
        scratch_shapes=[pltpu.VMEM((HY_CB, 2 * n1, LANES), cdt)],
        compiler_params=pltpu.CompilerParams(dimension_semantics=("arbitrary",), vmem_limit_bytes=VMEM_LIMIT),
        name="fft_filter", interpret=interpret,
    )(k_rows, *consts)


def fft_long_conv(z, z_ch0, gate, gate_ch0, spec_re, spec_im, kf_row0, kb_row0, skip, n1, cdt=BF16, interpret=False):
    B, _, n_in, _ = z.shape
    C = D_HY
    assert B == 2 and all(v % HY_CB == 0 for v in (C, z_ch0, gate_ch0, kf_row0, kb_row0))
    t = _dft_tables(n1, n_in, n_in, cdt)
    full = lambda a: pl.BlockSpec(a.shape, lambda i: (0,) * a.ndim)
    sig = lambda ch0: pl.BlockSpec((B, HY_CB, n_in, LANES), lambda i: (0, i + ch0 // HY_CB, 0, 0))
    spc = lambda row0: pl.BlockSpec((HY_CB, n1, LANES), lambda i: (i + row0 // HY_CB, 0, 0))
    consts = (t["ma"], t["tr"], t["ti"], t["f2"], t["f2c"], t["mi"])
    skip_b = jnp.broadcast_to(skip.astype(F32)[:, None, None], (C, 1, LANES))
    return pl.pallas_call(
        _fft_conv_kernel,
        grid=(C // HY_CB,),
        in_specs=[sig(z_ch0), sig(gate_ch0), spc(kf_row0), spc(kf_row0), spc(kb_row0), spc(kb_row0),
                  pl.BlockSpec((HY_CB, 1, LANES), lambda i: (i, 0, 0))] + [full(a) for a in consts],
        out_specs=sig(0),
        out_shape=jax.ShapeDtypeStruct((B, C, n_in, LANES), F32),
        scratch_shapes=[pltpu.VMEM((HY_CB, 2 * n1, LANES), cdt), pltpu.VMEM((HY_CB, 2 * n1, LANES), cdt)],
        compiler_params=pltpu.CompilerParams(dimension_semantics=("arbitrary",), vmem_limit_bytes=VMEM_LIMIT),
        name="fft_conv", interpret=interpret,
    )(z, gate, spec_re, spec_im, spec_re, spec_im, skip_b, *consts)


def _hy_filter_kernel(z_ref, w1_ref, b1_ref, w2_ref, b2_ref, fr_ref, w3_ref, dec_ref, k_ref, h_scr):
    hp = lax.Precision.HIGHEST

    @pl.when(pl.program_id(0) == 0)
    def _():
        h = jnp.sin(fr_ref[...] * (jnp.dot(w1_ref[...], z_ref[...], precision=hp, preferred_element_type=F32) + b1_ref[...]))
        h_scr[...] = jnp.sin(fr_ref[...] * (jnp.dot(w2_ref[...], h, precision=hp, preferred_element_type=F32) + b2_ref[...]))

    t = z_ref[0:1, :]
    k = jnp.dot(w3_ref[...], h_scr[...], precision=hp, preferred_element_type=F32) * jnp.exp(-t * dec_ref[...])
    k = k * lax.rsqrt(jnp.sum(k * k, axis=1, keepdims=True) + EPS)
    backward = (pl.program_id(0) // (D_HY // HY_FR)) % 2 == 1
    lag0 = lax.broadcasted_iota(jnp.int32, k.shape, 1) == 0
    k_ref[...] = jnp.where(jnp.logical_and(backward, lag0), 0.0, k)


def hyena_filters(L, w1, b1, w2, b2, w3, freq, decay, interpret=False):
    f32 = np.float32
    t = np.arange(L, dtype=f32) / f32(L)
    bands = np.linspace(1e-4, HY_BANDS - 1, HY_BANDS, dtype=f32)
    wpos = f32(2.0 * math.pi) * t[None, :] * bands[:, None]
    emb = 24
    z = jnp.asarray(np.concatenate([t[None, :], np.cos(wpos), np.sin(wpos), np.zeros((emb - HY_EMB, L), f32)], axis=0), F32)
    w1p = jnp.concatenate([w1.T, jnp.zeros((HY_FFN, emb - HY_EMB), F32)], axis=1)
    R = 2 * HY_ORDER * D_HY
    col = lambda v: v.reshape(-1, 1).astype(F32)
    full = lambda shape: pl.BlockSpec(shape, lambda i: (0,) * len(shape))
    return pl.pallas_call(
        _hy_filter_kernel,
        grid=(R // HY_FR,),
        in_specs=[full((emb, L)), full((HY_FFN, emb)), full((HY_FFN, 1)), full((HY_FFN, HY_FFN)), full((HY_FFN, 1)),
                  full((HY_FFN, 1)), pl.BlockSpec((HY_FR, HY_FFN), lambda i: (i, 0)),
                  pl.BlockSpec((HY_FR, 1), lambda i: (i, 0))],
        out_specs=pl.BlockSpec((HY_FR, L), lambda i: (i, 0)),
        out_shape=jax.ShapeDtypeStruct((R, L), F32),
        scratch_shapes=[pltpu.VMEM((HY_FFN, L), F32)],
        compiler_params=pltpu.CompilerParams(dimension_semantics=("arbitrary",)),
        name="hy_filter", interpret=interpret,
    )(z, w1p, col(b1), w2.T.astype(F32), col(b2), col(freq), w3.T.astype(F32), col(decay))


def _hy_shortconv_kernel(u_ref, up_ref, un_ref, w_ref, b_ref, o_ref):
    i = pl.program_id(1)
    n = pl.num_programs(1)
    x = u_ref[0]
    tt = x.shape[0]
    p1 = jnp.where(i > 0, up_ref[0][SUBLANES - 1:SUBLANES], 0.0)
    n1 = jnp.where(i < n - 1, un_ref[0][0:1], 0.0)
    ext = jnp.concatenate([p1, x, n1], axis=0)
    o_ref[0] = b_ref[...] + sum(w_ref[k:k + 1, :] * ext[k:k + tt] for k in range(HY_CONV))


def hyena_shortconv(u, row0, L, w, b, interpret=False):
    B, _, C = u.shape
    tt = min(256, L)
    hb = tt // SUBLANES
    nt = L // tt
    assert row0 % tt == 0
    t0, h0 = row0 // tt, row0 // SUBLANES
    return pl.pallas_call(
        _hy_shortconv_kernel,
        grid=(B, nt),
        in_specs=[pl.BlockSpec((1, tt, C), lambda bb, i: (bb, i + t0, 0)),
                  pl.BlockSpec((1, SUBLANES, C), lambda bb, i: (bb, h0 + jnp.maximum(i * hb - 1, 0), 0)),
                  pl.BlockSpec((1, SUBLANES, C), lambda bb, i: (bb, h0 + jnp.minimum((i + 1) * hb, nt * hb - 1), 0)),
                  pl.BlockSpec((HY_CONV, C), lambda bb, i: (0, 0)), pl.BlockSpec((1, C), lambda bb, i: (0, 0))],
        out_specs=pl.BlockSpec((1, tt, C), lambda bb, i: (bb, i, 0)),
        out_shape=jax.ShapeDtypeStruct((B, L, C), F32),
        compiler_params=pltpu.CompilerParams(dimension_semantics=("arbitrary", "arbitrary")),
        name="hy_shortconv", interpret=interpret,
    )(u, u, u, w.astype(F32), b.reshape(1, C).astype(F32))


def hyena_pallas(u, row0, L, conv_w, conv_b, w1, b1, w2, b2, w3, freq, decay, skip, cdt=BF16, interpret=False):
    B = u.shape[0]
    rows = L // LANES
    n1 = max(2 * rows, SUBLANES)
    n_in = n1 // 2 if n1 == 2 * rows else n1
    uc = hyena_shortconv(u, row0, L, conv_w, conv_b, interpret=interpret)
    ucm = uc.transpose(0, 2, 1).reshape(B, 3 * D_HY, rows, LANES)
    if n_in != rows:
        ucm = jnp.pad(ucm, ((0, 0), (0, 0), (0, n_in - rows), (0, 0)))

    k = hyena_filters(L, w1, b1, w2, b2, w3, freq, decay, interpret=interpret)
    k = k.reshape(2 * HY_ORDER * D_HY, rows, LANES)
    if n_in != rows:
        k = jnp.pad(k, ((0, 0), (0, n_in - rows), (0, 0)))
    sr, si = fft_filter_spectra(k, n1, cdt=cdt, interpret=interpret)

    y, y_ch0 = ucm, 0
    for o in range(HY_ORDER):
        y = fft_long_conv(y, y_ch0, ucm, (o + 1) * D_HY, sr, si, (2 * o) * D_HY, (2 * o + 1) * D_HY, skip[o], n1,
                          cdt=cdt, interpret=interpret)
    return y[:, :, :rows].reshape(B, D_HY, L).transpose(0, 2, 1)


ROW_T = CTX_LEN
ROW_NT = SEQ_ALL // ROW_T
MOD_ROWS = SUBLANES
MLA_DK = LANES
ROPE_SWAP = np.arange(MLA_ROPE).reshape(2, 2, MLA_ROPE // 4)[:, ::-1].reshape(-1)


def _rms(x):
    return x * lax.rsqrt(jnp.mean(x * x, axis=-1, keepdims=True) + EPS)


def _mod_kernel(c_ref, w_ref, b_ref, o_ref):
    c = c_ref[...]
    s = (c * jax.nn.sigmoid(c)).astype(BF16)
    o_ref[...] = jnp.dot(s, w_ref[...].astype(BF16), preferred_element_type=F32) + b_ref[...]


def adaln_modulation(c, c_ctx, w_mod, b_mod, interpret=False):
    B, D = c.shape
    cs = jnp.concatenate([c, c_ctx[None], jnp.zeros((MOD_ROWS - B - 1, D), F32)], axis=0)
    out = pl.pallas_call(
        _mod_kernel,
        grid=(6,),
        in_specs=[pl.BlockSpec((MOD_ROWS, D), lambda j: (0, 0)),
                  pl.BlockSpec((D, D), lambda j: (0, j)),
                  pl.BlockSpec((1, D), lambda j: (0, j))],
        out_specs=pl.BlockSpec((MOD_ROWS, D), lambda j: (0, j)),
        out_shape=jax.ShapeDtypeStruct((MOD_ROWS, 6 * D), F32),
        compiler_params=pltpu.CompilerParams(dimension_semantics=("arbitrary",)),
        name="adaln_mod", interpret=interpret,
    )(cs, w_mod, b_mod.reshape(1, 6 * D))
    mod = out.reshape(MOD_ROWS, 6, D)
    return jnp.stack([jnp.broadcast_to(mod[B][None], (B, 6, D)), mod[:B]], axis=1)


IN_COLS = (D_LRU, D_LRU, MLA_Q_RANK, MLA_KV_RANK, 3 * D_HY, 2 * MLA_ROPE)


def _in_proj_kernel(s_ref, m_ref, g_ref, w_ref, *out_refs):
    m = m_ref[0, 0]
    h = _rms(s_ref[0]) * g_ref[...] * (1.0 + m[1:2]) + m[0:1]
    u = jnp.dot(h.astype(w_ref.dtype), w_ref[...], preferred_element_type=F32)
    off = 0
    for ref, n in zip(out_refs, IN_COLS):
        ref[0] = u[:, off:off + n]
        off += n


def in_projection(s, mod, norm_g, w_in, cdt=BF16, interpret=False):
    B, R, D = s.shape
    o = (0,) + tuple(IN_SPLITS) + (D_IN,)
    kr = w_in[:, o[4]:o[5]]
    w = jnp.concatenate([w_in[:, :o[4]], w_in[:, o[5]:], kr, kr[:, ROPE_SWAP]], axis=1).astype(cdt)
    return pl.pallas_call(
        _in_proj_kernel,
        grid=(B, R // ROW_T),
        in_specs=[pl.BlockSpec((1, ROW_T, D), lambda b, i: (b, i, 0)),
                  pl.BlockSpec((1, 1, 6, D), lambda b, i: (b, jnp.minimum(i, 1), 0, 0)),
                  pl.BlockSpec((1, D), lambda b, i: (0, 0)),
                  pl.BlockSpec(w.shape, lambda b, i: (0, 0))],
        out_specs=[pl.BlockSpec((1, ROW_T, n), lambda b, i: (b, i, 0)) for n in IN_COLS],
        out_shape=[jax.ShapeDtypeStruct((B, R, n), F32) for n in IN_COLS],
        compiler_params=pltpu.CompilerParams(
            dimension_semantics=("arbitrary", "arbitrary"), vmem_limit_bytes=VMEM_LIMIT),
        name="in_proj", interpret=interpret,
    )(s, mod, norm_g.reshape(1, D), w)


def _mla_proj_kernel(cq_ref, ckv_ref, kr_ref, rc_ref, rs_ref, qg_ref, wq_ref, wqs_ref, kg_ref, wkv_ref,
                     q_ref, k_ref, v_ref):
    cdt = wq_ref.dtype
    cqn = (_rms(cq_ref[0]) * qg_ref[...]).astype(cdt)
    qa = jnp.dot(cqn, wq_ref[...], preferred_element_type=F32)
    qs = jnp.dot(cqn, wqs_ref[...], preferred_element_type=F32)
    ckvn = (_rms(ckv_ref[0]) * kg_ref[...]).astype(cdt)
    kv = jnp.dot(ckvn, wkv_ref[...], preferred_element_type=F32)
    rc, rs = rc_ref[...], rs_ref[...]
    kr = kr_ref[0]
    k_rope = (kr[:, :MLA_ROPE] * rc[:, MLA_NOPE:MLA_NOPE + MLA_ROPE]
              + kr[:, MLA_ROPE:] * rs[:, MLA_NOPE:MLA_NOPE + MLA_ROPE])
    pad = jnp.zeros((kr.shape[0], MLA_DK - MLA_NOPE - MLA_ROPE), F32)
    for h in range(MLA_HEADS):
        sl = slice(h * MLA_DK, (h + 1) * MLA_DK)
        q_ref[0, h] = ((qa[:, sl] * rc + qs[:, sl] * rs) * MLA_SCALE).astype(q_ref.dtype)
        kvh = kv[:, h * (MLA_NOPE + MLA_V):(h + 1) * (MLA_NOPE + MLA_V)]
        k_ref[0, h] = jnp.concatenate([kvh[:, :MLA_NOPE], k_rope, pad], axis=1).astype(k_ref.dtype)
        v_ref[0, h] = jnp.concatenate([kvh[:, MLA_NOPE:], jnp.ones((kvh.shape[0], MLA_V), F32)],
                                      axis=1).astype(v_ref.dtype)


def _rope_tables():
    f32 = np.float32
    grid_rows = SEQ // GRID_W
    row = np.repeat(np.arange(grid_rows, dtype=f32), GRID_W)
    col = np.tile(np.arange(GRID_W, dtype=f32), grid_rows)
    inv_freq = (f32(ROPE_THETA) ** (-np.arange(ROPE_PAIRS, dtype=f32) / f32(ROPE_PAIRS))).astype(f32)
    ang_r, ang_c = row[:, None] * inv_freq, col[:, None] * inv_freq
    cos = np.concatenate([np.cos(ang_r), np.cos(ang_r), np.cos(ang_c), np.cos(ang_c)], axis=1)
    sin = np.concatenate([-np.sin(ang_r), np.sin(ang_r), -np.sin(ang_c), np.sin(ang_c)], axis=1)
    npad = MLA_DK - MLA_NOPE - MLA_ROPE
    cos = np.concatenate([np.ones((SEQ, MLA_NOPE), f32), cos, np.zeros((SEQ, npad), f32)], axis=1)
    sin = np.concatenate([np.zeros((SEQ, MLA_NOPE), f32), sin, np.zeros((SEQ, npad), f32)], axis=1)
    ctx_cos = np.concatenate([np.ones((CTX_LEN, MLA_NOPE + MLA_ROPE), f32), np.zeros((CTX_LEN, npad), f32)], axis=1)
    return (jnp.asarray(np.concatenate([ctx_cos, cos], axis=0), F32),
            jnp.asarray(np.concatenate([np.zeros((CTX_LEN, MLA_DK), f32), sin], axis=0), F32))


def mla_projection(cq, ckv, kr2, q_g, wqb, kv_g, wkvb, cdt=BF16, interpret=False):
    B, R, _ = cq.shape
    dq = MLA_NOPE + MLA_ROPE
    wq3 = wqb.reshape(MLA_Q_RANK, MLA_HEADS, dq)
    zq = jnp.zeros((MLA_Q_RANK, MLA_HEADS, MLA_DK - dq), wqb.dtype)
    wq = jnp.concatenate([wq3, zq], axis=2).reshape(MLA_Q_RANK, MLA_HEADS * MLA_DK).astype(cdt)
    wqs = jnp.concatenate([jnp.zeros((MLA_Q_RANK, MLA_HEADS, MLA_NOPE), wqb.dtype),
                           wq3[:, :, MLA_NOPE:][:, :, ROPE_SWAP], zq], axis=2)
    wqs = wqs.reshape(MLA_Q_RANK, MLA_HEADS * MLA_DK).astype(cdt)
    rc, rs = _rope_tables()
    row = lambda n: pl.BlockSpec((1, ROW_T, n), lambda b, i: (b, i, 0))
    tab = pl.BlockSpec((ROW_T, MLA_DK), lambda b, i: (i, 0))
    full = lambda shape: pl.BlockSpec(shape, lambda b, i: (0,) * len(shape))
    head = lambda n: pl.BlockSpec((1, MLA_HEADS, ROW_T, n), lambda b, i: (b, 0, i, 0))
    return pl.pallas_call(
        _mla_proj_kernel,
        grid=(B, R // ROW_T),
        in_specs=[row(MLA_Q_RANK), row(MLA_KV_RANK), row(2 * MLA_ROPE), tab, tab,
                  full((1, MLA_Q_RANK)), full(wq.shape), full(wqs.shape),
                  full((1, MLA_KV_RANK)), full(wkvb.shape)],
        out_specs=[head(MLA_DK), head(MLA_DK), head(2 * MLA_V)],
        out_shape=[jax.ShapeDtypeStruct((B, MLA_HEADS, R, MLA_DK), cdt),
                   jax.ShapeDtypeStruct((B, MLA_HEADS, R, MLA_DK), cdt),
                   jax.ShapeDtypeStruct((B, MLA_HEADS, R, 2 * MLA_V), cdt)],
        compiler_params=pltpu.CompilerParams(dimension_semantics=("arbitrary", "arbitrary")),
        name="mla_proj", interpret=interpret,
    )(cq, ckv, kr2, rc, rs, q_g.reshape(1, -1), wq, wqs, kv_g.reshape(1, -1), wkvb.astype(cdt))


def _out_proj_kernel(s_ref, m_ref, g_ref, ya_ref, ybx_ref, ybc_ref, ycx_ref, ycc_ref, w_ref, snew_ref, hT_ref):
    cdt = w_ref.dtype
    is_ctx = pl.program_id(1) == 0
    yb = jnp.where(is_ctx, ybc_ref[0], ybx_ref[0])
    yc = jnp.where(is_ctx, ycc_ref[0], ycx_ref[0])
    mix = (jnp.dot(ya_ref[0].astype(cdt), w_ref[0:D_LRU, :], preferred_element_type=F32)
           + jnp.dot(yb.astype(cdt), w_ref[D_LRU:D_LRU + MLA_HEADS * MLA_V, :], preferred_element_type=F32)
           + jnp.dot(yc.astype(cdt), w_ref[D_LRU + MLA_HEADS * MLA_V:, :], preferred_element_type=F32))
    m = m_ref[0, 0]
    s_new = s_ref[0] + m[2:3] * mix
    snew_ref[0] = s_new
    h2 = _rms(s_new) * g_ref[...] * (1.0 + m[4:5]) + m[3:4]
    hT_ref[...] = h2.T.astype(hT_ref.dtype)


def out_projection(s, mod, norm2_g, ya, yb_x, yb_c, yc_x, yc_c, w_out, cdt=BF16, interpret=False):
    B, R, D = s.shape
    nt = R // ROW_T
    row = lambda n: pl.BlockSpec((1, ROW_T, n), lambda b, i: (b, i, 0))
    lat = lambda n: pl.BlockSpec((1, ROW_T, n), lambda b, i: (b, jnp.maximum(i - 1, 0), 0))
    ctx = lambda n: pl.BlockSpec((1, ROW_T, n), lambda b, i: (b, 0, 0))
    return pl.pallas_call(
        _out_proj_kernel,
        grid=(B, nt),
        in_specs=[row(D),
                  pl.BlockSpec((1, 1, 6, D), lambda b, i: (b, jnp.minimum(i, 1), 0, 0)),
                  pl.BlockSpec((1, D), lambda b, i: (0, 0)),
                  row(D_LRU),
                  lat(MLA_HEADS * MLA_V), ctx(MLA_HEADS * MLA_V), lat(D_HY), ctx(D_HY),
                  pl.BlockSpec((D_MIX, D), lambda b, i: (0, 0))],
        out_specs=[row(D), pl.BlockSpec((D, ROW_T), lambda b, i: (0, b * nt + i))],
        out_shape=[jax.ShapeDtypeStruct((B, R, D), F32), jax.ShapeDtypeStruct((D, B * R), cdt)],
        compiler_params=pltpu.CompilerParams(
            dimension_semantics=("arbitrary", "arbitrary"), vmem_limit_bytes=VMEM_LIMIT),
        name="out_proj", interpret=interpret,
    )(s, mod, norm2_g.reshape(1, D), ya, yb_x, yb_c, yc_x, yc_c, w_out.astype(cdt))


PEER_TM = 768
PEER_TE = 1024
PEER_KP = 256
NEG_BIG = -3.0e38
PACKED_ROWS = 2 * SUBLANES


def _topk_desc(s, k):
    rank = jnp.full(s.shape, float(PEER_NKEYS), F32)
    vals = []
    for r in range(k):
        m = jnp.max(s, axis=0, keepdims=True)
        hit = s == m
        rank = jnp.where(hit, float(r), rank)
        s = jnp.where(hit, NEG_BIG, s)
        vals.append(m)
    return jnp.concatenate(vals, axis=0), rank


def _route_chunk(s1, s2):
    K = PEER_TOPK
    v1, rank1 = _topk_desc(s1, K)
    v2, rank2 = _topk_desc(s2, K)
    half = K // 2
    cands = ([v1[0:1] + v2] + [v1[i:i + 1] + v2[:half] for i in range(1, half)]
             + [v1[i:i + 1] + v2[0:1] for i in range(half, K)])
    cmax = v1[0:1] + v2[0:1]
    c = jnp.concatenate(cands[:half] + [v1[half:] + v2[0:1]], axis=0)
    z = jnp.zeros_like(cmax)
    m = cmax
    for r in range(K):
        m = jnp.max(c, axis=0, keepdims=True)
        z = z + jnp.exp(m - cmax)
        if r + 1 < K:
            c = jnp.where(c == m, NEG_BIG, c)
    thr = m
    n = jnp.zeros_like(s1)
    for i in range(K):
        n_i = jnp.sum(jnp.where(cands[i] >= thr, 1.0, 0.0), axis=0, keepdims=True)
        n = jnp.where(rank1 == float(i), n_i, n)
    w = jnp.exp(s1 - v1[0:1]) / z
    e2 = jnp.exp(s2 - v2[0:1])
    return rank2, e2, n, w


def _peer_route_kernel(hT_ref, wqT_ref, keys_ref, rank2_ref, e2_ref, n_ref, w_ref, q_scr, s1_scr, s2_scr):
    h = pl.program_id(1)

    @pl.when(h == 0)
    def _():
        q_scr[...] = jnp.dot(wqT_ref[...], hT_ref[...], preferred_element_type=F32)

    cdt = keys_ref.dtype
    half = PEER_DQ // 2
    row = pl.multiple_of(h * PEER_DQ, PEER_DQ)
    q1 = q_scr[pl.ds(row, half), :].astype(cdt)
    q2 = q_scr[pl.ds(row + half, half), :].astype(cdt)
    s1_scr[...] = jnp.dot(keys_ref[0], q1, preferred_element_type=F32)
    s2_scr[...] = jnp.dot(keys_ref[1], q2, preferred_element_type=F32)

    def chunk(c, carry):
        sl = pl.ds(pl.multiple_of(c * LANES, LANES), LANES)
        rank2, e2, n, w = _route_chunk(s1_scr[:, sl], s2_scr[:, sl])
        rank2_ref[0, :, sl] = rank2.astype(rank2_ref.dtype)
        e2_ref[0, :, sl] = e2.astype(e2_ref.dtype)
        n_ref[0, :, sl] = n
        w_ref[0, :, sl] = w
        return carry

    lax.fori_loop(0, s1_scr.shape[1] // LANES, chunk, 0, unroll=2)


def _peer_dense_kernel(hT_ref, rank2_ref, e2_ref, n_ref, w_ref, u_ref, vT_ref, s_ref, g_ref, o_ref, acc_ref, pre_ref,
                       *, n_chunks):
    j = pl.program_id(1)

    te, tm = u_ref.shape[0], hT_ref.shape[1]
    gdt = rank2_ref.dtype

    def first_matmul(wr):
        pre_ref[wr] = jnp.dot(u_ref[...], hT_ref[...], preferred_element_type=F32)

    def second_matmul(rd):
        out = None
        for k0 in range(0, te, PEER_KP):
            acts = []
            for al in range(k0 // PEER_NKEYS, (k0 + PEER_KP) // PEER_NKEYS):
                g = None
                for h in range(PEER_HEADS):
                    n16 = jnp.broadcast_to(n_ref[h, 0, al:al + 1, :], (PACKED_ROWS, tm)).astype(gdt)
                    w16 = jnp.broadcast_to(w_ref[h, 0, al:al + 1, :], (PACKED_ROWS, tm)).astype(gdt)
                    nb = jnp.concatenate([n16] * (PEER_NKEYS // PACKED_ROWS), axis=0)
                    wb = jnp.concatenate([w16] * (PEER_NKEYS // PACKED_ROWS), axis=0)
                    t = jnp.where(rank2_ref[h] < nb, e2_ref[h] * wb, jnp.zeros_like(wb))
                    g = t if g is None else g + t
                blk = slice(al * PEER_NKEYS, (al + 1) * PEER_NKEYS)
                acts.append((g * _gelu_tanh(pre_ref[rd, blk, :].astype(gdt))).astype(vT_ref.dtype))
            part = jnp.dot(vT_ref[0, :, k0:k0 + PEER_KP], jnp.concatenate(acts, axis=0), preferred_element_type=F32)
            out = part if out is None else out + part
        acc_ref[...] += out

    def step(rd, wr):
        first_matmul(wr)
        second_matmul(rd)

    @pl.when(j == 0)
    def _():
        acc_ref[...] = jnp.zeros_like(acc_ref)
        first_matmul(0)

    for parity in range(2):
        pl.when((j % 2 == parity) & (j > 0) & (j < n_chunks))(functools.partial(step, 1 - parity, parity))

    @pl.when(j == n_chunks)
    def _():
        second_matmul((n_chunks - 1) % 2)
        y = acc_ref[...].T
        for r in range(g_ref.shape[0]):
            rows = slice(r * ROW_T, (r + 1) * ROW_T)
            o_ref[rows, :] = s_ref[rows, :] + g_ref[r] * y[rows, :]


def peer_tokens(hT, s, gate, wq, keys, u_tab, v_tab, interpret=False):
    D, T = hT.shape
    cdt = hT.dtype
    tm, te = PEER_TM, PEER_TE
    assert T % tm == 0 and PEER_EXPERTS % te == 0 and tm % ROW_T == 0
    wqT = wq.T.astype(cdt)
    route_shape = jax.ShapeDtypeStruct((PEER_HEADS, PEER_NKEYS, T), F32)
    packed_shape = jax.ShapeDtypeStruct((PEER_HEADS, PEER_NKEYS, T), cdt)
    route_spec = pl.BlockSpec((1, PEER_NKEYS, tm), lambda i, hh: (hh, 0, i))
    rank2, e2, n, w = pl.pallas_call(
        _peer_route_kernel,
        grid=(T // tm, PEER_HEADS),
        in_specs=[pl.BlockSpec((D, tm), lambda i, hh: (0, i)),
                  pl.BlockSpec((PEER_HEADS * PEER_DQ, D), lambda i, hh: (0, 0)),
                  pl.BlockSpec((2, PEER_NKEYS, PEER_DQ // 2), lambda i, hh: (0, 0, 0))],
        out_specs=[route_spec] * 4,
        out_shape=[packed_shape, packed_shape, route_shape, route_shape],
        scratch_shapes=[pltpu.VMEM((PEER_HEADS * PEER_DQ, tm), F32),
                        pltpu.VMEM((PEER_NKEYS, tm), F32),
                        pltpu.VMEM((PEER_NKEYS, tm), F32)],
        compiler_params=pltpu.CompilerParams(
            dimension_semantics=("arbitrary", "arbitrary"), vmem_limit_bytes=VMEM_LIMIT),
        name="peer_route", interpret=interpret,
    )(hT, wqT, keys.astype(cdt))

    head_spec = pl.BlockSpec((PEER_HEADS, PEER_NKEYS, tm), lambda i, j: (0, 0, i))
    nj = PEER_EXPERTS // te
    groups = te // PEER_NKEYS
    row_spec = pl.BlockSpec((PEER_HEADS, 1, groups, tm), lambda i, j: (0, jnp.maximum(j - 1, 0), 0, i))
    n = n.reshape(PEER_HEADS, nj, groups, T)
    w = w.reshape(PEER_HEADS, nj, groups, T)
    vT = v_tab.astype(cdt).reshape(nj, te, D).transpose(0, 2, 1)
    return pl.pallas_call(
        functools.partial(_peer_dense_kernel, n_chunks=nj),
        grid=(T // tm, nj + 1),
        in_specs=[pl.BlockSpec((D, tm), lambda i, j: (0, i)),
                  head_spec, head_spec, row_spec, row_spec,
                  pl.BlockSpec((te, D), lambda i, j: (jnp.minimum(j, nj - 1), 0)),
                  pl.BlockSpec((1, D, te), lambda i, j: (jnp.maximum(j - 1, 0), 0, 0)),
                  pl.BlockSpec((tm, D), lambda i, j: (i, 0)),
                  pl.BlockSpec((tm // ROW_T, 1, D), lambda i, j: (i, 0, 0))],
        out_specs=pl.BlockSpec((tm, D), lambda i, j: (i, 0)),
        out_shape=jax.ShapeDtypeStruct((T, D), F32),
        scratch_shapes=[pltpu.VMEM((D, tm), F32), pltpu.VMEM((2, te, tm), F32)],
        compiler_params=pltpu.CompilerParams(
            dimension_semantics=("arbitrary", "arbitrary"), vmem_limit_bytes=VMEM_LIMIT),
        name="peer_dense", interpret=interpret,
    )(hT, rank2, e2, n, w, u_tab.astype(cdt), vT, s, gate)


def _final_norm_kernel(x_ref, g_ref, o_ref):
    o_ref[0] = _rms(x_ref[0]) * g_ref[...]


def final_norm(s, g):
    B, R, D = s.shape
    return pl.pallas_call(
        _final_norm_kernel,
        grid=(B, SEQ // ROW_T),
        in_specs=[pl.BlockSpec((1, ROW_T, D), lambda b, i: (b, i + CTX_LEN // ROW_T, 0)),
                  pl.BlockSpec((1, D), lambda b, i: (0, 0))],
        out_specs=pl.BlockSpec((1, ROW_T, D), lambda b, i: (b, i, 0)),
        out_shape=jax.ShapeDtypeStruct((B, SEQ, D), F32),
        compiler_params=pltpu.CompilerParams(dimension_semantics=("arbitrary", "arbitrary")),
        name="final_norm",
    )(s, g.reshape(1, D))


def kernel(x, c, ctx, c_ctx, w_mod, b_mod, norm1_g, norm2_g, w_in, w_out,
           lru_conv_w, lru_conv_b, lru_wr, lru_br, lru_wi, lru_bi, lru_lambda,
           mla_q_norm_g, mla_wqb, mla_kv_norm_g, mla_wkvb,
           hy_conv_w, hy_conv_b, hy_f_w1, hy_f_b1, hy_f_w2, hy_f_b2, hy_f_w3,
           hy_f_freq, hy_decay, hy_skip,
           peer_wq, peer_keys, peer_u, peer_v, final_g):
    B = x.shape[0]
    s = jnp.concatenate([ctx, x], axis=1)
    for l in range(DEPTH):
        need_ctx = l < DEPTH - 1
        mod = adaln_modulation(c, c_ctx, w_mod[l], b_mod[l])
        u_lru, u_gate, cq, ckv, u_hy, kr2 = in_projection(s, mod, norm1_g[l], w_in[l])

        ya = rglru_mixer(u_lru, u_gate, lru_conv_w[l], lru_conv_b[l], lru_wr[l], lru_br[l], lru_wi[l], lru_bi[l],
                         lru_lambda[l])

        q, k, v = mla_projection(cq, ckv, kr2, mla_q_norm_g[l], mla_wqb[l], mla_kv_norm_g[l], mla_wkvb[l])
        yb_x = flash_attention(q, k, v, q_row0=CTX_LEN, n_q=SEQ, n_k=SEQ_ALL, tk=SEQ_ALL // 3, n_sub=1)
        if need_ctx:
            yb_c = flash_attention(q, k, v, q_row0=0, n_q=CTX_LEN, n_k=CTX_LEN, tk=CTX_LEN, n_sub=1)
        else:
            yb_c = jnp.zeros((B, CTX_LEN, MLA_HEADS * MLA_V), F32)

        hy_args = (hy_conv_w[l], hy_conv_b[l], hy_f_w1[l], hy_f_b1[l], hy_f_w2[l], hy_f_b2[l], hy_f_w3[l],
                   hy_f_freq[l], hy_decay[l], hy_skip[l])
        yc_x = hyena_pallas(u_hy, CTX_LEN, SEQ, *hy_args)
        yc_c = hyena_pallas(u_hy, 0, CTX_LEN, *hy_args) if need_ctx else jnp.zeros((B, CTX_LEN, D_HY), F32)

        s, hT = out_projection(s, mod, norm2_g[l], ya, yb_x, yb_c, yc_x, yc_c, w_out[l])
        gate2 = jnp.repeat(mod[:, :, 5], jnp.array([1, ROW_NT - 1]), axis=1, total_repeat_length=ROW_NT)
        s = peer_tokens(hT, s.reshape(B * SEQ_ALL, D_MODEL), gate2.reshape(B * ROW_NT, 1, D_MODEL),
                        peer_wq[l], peer_keys[l], peer_u[l], peer_v[l]).reshape(B, SEQ_ALL, D_MODEL)
    return final_norm(s, final_g)
```

```python
import functools
import math

import jax
import jax.numpy as jnp
import numpy as np
from jax import lax
from jax.experimental import pallas as pl
from jax.experimental.pallas import tpu as pltpu

D_MODEL = 1024
BATCH = 2
SEQ = 8192
DEPTH = 2

GRID_W = 64
CTX_LEN = 256
EPS = 1e-6

D_LRU = 384
LRU_BLOCKS = 6
LRU_BLOCK_DIM = D_LRU // LRU_BLOCKS
LRU_CONV = 4
LRU_C = 8.0

MLA_HEADS = 6
MLA_NOPE = 64
MLA_ROPE = 32
MLA_V = 64
MLA_Q_RANK = 384
MLA_KV_RANK = 256
MLA_SCALE = (MLA_NOPE + MLA_ROPE) ** -0.5
ROPE_PAIRS = MLA_ROPE // 4
ROPE_THETA = 10000.0

D_HY = 256
HY_CONV = 3
HY_ORDER = 2
HY_BANDS = 8
HY_EMB = 1 + 2 * HY_BANDS
HY_FFN = 64

D_MIX = D_LRU + MLA_HEADS * MLA_V + D_HY
D_IN = 2 * D_LRU + MLA_Q_RANK + MLA_KV_RANK + MLA_ROPE + 3 * D_HY
IN_SPLITS = [D_LRU, 2 * D_LRU, 2 * D_LRU + MLA_Q_RANK,
             2 * D_LRU + MLA_Q_RANK + MLA_KV_RANK,
             2 * D_LRU + MLA_Q_RANK + MLA_KV_RANK + MLA_ROPE]

PEER_HEADS = 8
PEER_NKEYS = 128
PEER_EXPERTS = PEER_NKEYS * PEER_NKEYS
PEER_DQ = 256
PEER_TOPK = 16

F32 = jnp.float32


BF16 = jnp.bfloat16
LANES = 128
SUBLANES = 8
VMEM_LIMIT = 56 * 1024 * 1024
SEQ_ALL = CTX_LEN + SEQ
LRU_TT = CTX_LEN
LRU_NT = SEQ_ALL // LRU_TT


def _gelu_tanh(x):
    return 0.5 * x * (1.0 + jnp.tanh(0.7978845608028654 * (x + 0.044715 * (x * x * x))))


def _lru_coeff_kernel(u_ref, up_ref, un_ref, cw_ref, cb_ref, wg_ref, bg_ref, c_ref,
                      af_ref, bf_ref, ab_ref, bb_ref):
    i = pl.program_id(1)
    x = u_ref[0]
    tt = x.shape[0]
    use_prev = i >= 2
    use_next = jnp.logical_and(i >= 1, i < LRU_NT - 1)
    p2 = jnp.where(use_prev, up_ref[0][SUBLANES - 2:SUBLANES], 0.0)
    n1 = jnp.where(use_next, un_ref[0][0:1], 0.0)
    ext = jnp.concatenate([p2, x, n1], axis=0)
    xc = cb_ref[...] + sum(cw_ref[k:k + 1, :] * ext[k:k + tt] for k in range(LRU_CONV))
    g = jnp.dot(xc.astype(wg_ref.dtype), wg_ref[...], preferred_element_type=F32) + bg_ref[...]
    for d, (a_ref, b_ref) in enumerate(((af_ref, bf_ref), (ab_ref, bb_ref))):
        r = jax.nn.sigmoid(g[:, (2 * d) * D_LRU:(2 * d + 1) * D_LRU])
        gi = jax.nn.sigmoid(g[:, (2 * d + 1) * D_LRU:(2 * d + 2) * D_LRU])
        log_a = c_ref[d:d + 1, :] * r
        a_ref[0] = jnp.exp(log_a)
        b_ref[0] = jnp.sqrt(1.0 - jnp.exp(2.0 * log_a)) * (gi * xc)


def _lru_scan_rows(a_ref, b_ref, h_scr, emit, reverse):
    tt = a_ref.shape[1]
    nb = a_ref.shape[0]

    def body(s, hs):
        t = (tt - 1 - s) if reverse else s
        out = []
        for b in range(nb):
            h = a_ref[b, pl.ds(t, 1), :] * hs[b] + b_ref[b, pl.ds(t, 1), :]
            emit(b, t, h)
            out.append(h)
        return tuple(out)

    hs = lax.fori_loop(0, tt, body, tuple(h_scr[b:b + 1, :] for b in range(nb)), unroll=8)
    for b in range(nb):
        h_scr[b:b + 1, :] = hs[b]


def _lru_fwd_kernel(a_ref, b_ref, hf_ref, h_scr):
    @pl.when(pl.program_id(0) == 0)
    def _():
        h_scr[...] = jnp.zeros_like(h_scr)

    def emit(b, t, h):
        hf_ref[b, pl.ds(t, 1), :] = h

    _lru_scan_rows(a_ref, b_ref, h_scr, emit, reverse=False)


def _lru_bwd_kernel(a_ref, b_ref, hf_ref, gate_ref, y_ref, h_scr):
    @pl.when(pl.program_id(0) == 0)
    def _():
        h_scr[...] = jnp.zeros_like(h_scr)

    def emit(b, t, h):
        y_ref[b, pl.ds(t, 1), :] = h

    _lru_scan_rows(a_ref, b_ref, h_scr, emit, reverse=True)
    y_ref[...] = (y_ref[...] + hf_ref[...]) * _gelu_tanh(gate_ref[...])


def rglru_mixer(u_lru, u_gate, conv_w, conv_b, wr, br, wi, bi, lam, cdt=BF16, interpret=False):
    B = u_lru.shape[0]
    tt, nt = LRU_TT, LRU_NT
    hb = tt // SUBLANES

    def blockdiag(w):
        eye = jnp.eye(LRU_BLOCKS, dtype=w.dtype)
        return jnp.einsum('nde,nm->ndme', w, eye).reshape(D_LRU, D_LRU)

    wg = jnp.concatenate([blockdiag(wr[0]), blockdiag(wi[0]), blockdiag(wr[1]), blockdiag(wi[1])], axis=1).astype(cdt)
    bg = jnp.concatenate([br[0], bi[0], br[1], bi[1]]).reshape(1, 4 * D_LRU).astype(F32)
    cdec = -LRU_C * jax.nn.softplus(-lam.astype(F32))

    tile = pl.BlockSpec((1, tt, D_LRU), lambda b, i: (b, i, 0))
    full = lambda shape: pl.BlockSpec(shape, lambda b, i: (0,) * len(shape))
    coeff_shape = jax.ShapeDtypeStruct((B, SEQ_ALL, D_LRU), F32)
    a_f, b_f, a_b, b_b = pl.pallas_call(
        _lru_coeff_kernel,
        grid=(B, nt),
        in_specs=[tile,
                  pl.BlockSpec((1, SUBLANES, D_LRU), lambda b, i: (b, jnp.maximum(i * hb - 1, 0), 0)),
                  pl.BlockSpec((1, SUBLANES, D_LRU), lambda b, i: (b, jnp.minimum((i + 1) * hb, nt * hb - 1), 0)),
                  full((LRU_CONV, D_LRU)), full((1, D_LRU)), full((D_LRU, 4 * D_LRU)), full((1, 4 * D_LRU)),
                  full((2, D_LRU))],
        out_specs=[tile] * 4,
        out_shape=[coeff_shape] * 4,
        compiler_params=pltpu.CompilerParams(dimension_semantics=("arbitrary", "arbitrary")),
        name="lru_coeff", interpret=interpret,
    )(u_lru, u_lru, u_lru, conv_w.astype(F32), conv_b.reshape(1, D_LRU).astype(F32), wg, bg, cdec)

    seq = pl.BlockSpec((B, tt, D_LRU), lambda s: (0, s, 0))
    h_f = pl.pallas_call(
        _lru_fwd_kernel,
        grid=(nt,),
        in_specs=[seq, seq],
        out_specs=seq,
        out_shape=coeff_shape,
        scratch_shapes=[pltpu.VMEM((B, D_LRU), F32)],
        compiler_params=pltpu.CompilerParams(dimension_semantics=("arbitrary",)),
        name="lru_fwd", interpret=interpret,
    )(a_f, b_f)

    rseq = pl.BlockSpec((B, tt, D_LRU), lambda s: (0, jnp.where(s == 0, 0, nt - s), 0))
    return pl.pallas_call(
        _lru_bwd_kernel,
        grid=(nt,),
        in_specs=[rseq, rseq, rseq, rseq],
        out_specs=rseq,
        out_shape=coeff_shape,
        scratch_shapes=[pltpu.VMEM((B, D_LRU), F32)],
        compiler_params=pltpu.CompilerParams(dimension_semantics=("arbitrary",)),
        name="lru_bwd", interpret=interpret,
    )(a_b, b_b, h_f, u_gate)


ATT_TQ = 256


ATT_HG = 2


def _flash_kernel(*refs, tk):
    q_refs, (k_ref, v_ref, o_ref) = refs[:-3], refs[-3:]
    hg, tq = q_refs[0].shape[1], q_refs[0].shape[2]
    dv = v_ref.shape[3] // 2
    n_chunks = k_ref.shape[2] // tk
    chains = [(q_ref[0, h], h) for q_ref in q_refs for h in range(hg)]

    def body(c, carry):
        off = pl.multiple_of(c * tk, LANES)
        out = []
        for (q, h), (m, acc) in zip(chains, carry):
            s = lax.dot_general(q, k_ref[0, h, pl.ds(off, tk), :], (((1,), (1,)), ((), ())),
                                preferred_element_type=F32)
            m_new = jnp.maximum(m, jnp.max(s, axis=1, keepdims=True))
            p = jnp.exp(s - m_new).astype(v_ref.dtype)
            pv = jnp.dot(p, v_ref[0, h, pl.ds(off, tk), :], preferred_element_type=F32)
            out.append((m_new, jnp.exp(m - m_new) * acc + pv))
        return tuple(out)

    init = tuple((jnp.full((tq, 1), NEG_BIG, F32), jnp.zeros((tq, 2 * dv), F32)) for _ in chains)
    res = lax.fori_loop(0, n_chunks, body, init, unroll=True)
    for t in range(len(q_refs)):
        o_ref[0, t * tq:(t + 1) * tq, :] = jnp.concatenate(
            [acc[:, :dv] / acc[:, dv:] for _, acc in res[t * hg:(t + 1) * hg]], axis=1)


def flash_attention(q, k, v, q_row0, n_q, n_k, tk, n_sub, interpret=False):
    B, H, R, dk = q.shape
    dv = v.shape[3] // 2
    tq = ATT_TQ
    assert n_q % (n_sub * tq) == 0 and q_row0 % tq == 0 and n_k % tk == 0 and tk % LANES == 0 and R % n_k == 0
    q0 = q_row0 // tq
    q_spec = lambda t: pl.BlockSpec((1, ATT_HG, tq, dk), lambda b, g, i: (b, g, n_sub * i + t + q0, 0))
    return pl.pallas_call(
        functools.partial(_flash_kernel, tk=tk),
        grid=(B, H // ATT_HG, n_q // (n_sub * tq)),
        in_specs=[q_spec(t) for t in range(n_sub)]
                 + [pl.BlockSpec((1, ATT_HG, n_k, dk), lambda b, g, i: (b, g, 0, 0)),
                    pl.BlockSpec((1, ATT_HG, n_k, 2 * dv), lambda b, g, i: (b, g, 0, 0))],
        out_specs=pl.BlockSpec((1, n_sub * tq, ATT_HG * dv), lambda b, g, i: (b, i, g)),
        out_shape=jax.ShapeDtypeStruct((B, n_q, H * dv), F32),
        compiler_params=pltpu.CompilerParams(
            dimension_semantics=("arbitrary", "arbitrary", "arbitrary"), vmem_limit_bytes=VMEM_LIMIT),
        name="flash_attention", interpret=interpret,
    )(*([q] * n_sub), k, v)


HY_CB = 16
HY_FR = 128
HY_UNROLL = 16


def _dft_tables(n1, n_in, n_out, cdt):
    n = n1 * LANES
    k1 = np.arange(n1)[:, None]
    f1 = np.exp(-2j * np.pi * k1 * np.arange(n_in)[None, :] / n1)
    tw = np.exp(-2j * np.pi * k1 * np.arange(LANES)[None, :] / n)
    f2 = np.exp(-2j * np.pi * np.outer(np.arange(LANES), np.arange(LANES)) / LANES)
    g1 = np.conj(np.exp(-2j * np.pi * k1 * np.arange(n_out)[None, :] / n1)).T / n
    blk = lambda m: np.block([[m.real, -m.imag], [m.imag, m.real]])
    return dict(
        ma=jnp.asarray(blk(f1), cdt),
        ma_re=jnp.asarray(np.concatenate([f1.real, f1.imag], axis=0), cdt),
        tr=jnp.asarray(tw.real, F32), ti=jnp.asarray(tw.imag, F32),
        f2=jnp.asarray(np.concatenate([f2.real, f2.imag], axis=1), cdt),
        f2c=jnp.asarray(np.concatenate([f2.real, -f2.imag], axis=1), cdt),
        mi=jnp.asarray(blk(g1), cdt),
    )


def _dft_rows_stage(src, ma_ref, tr_ref, ti_ref, planar_scr):
    n1 = tr_ref.shape[0]

    def body(c, carry):
        o = jnp.dot(ma_ref[...], src(c).astype(ma_ref.dtype), preferred_element_type=F32)
        ar, ai = o[:n1], o[n1:]
        tr, ti = tr_ref[...], ti_ref[...]
        planar_scr[c, :n1] = (ar * tr - ai * ti).astype(planar_scr.dtype)
        planar_scr[c, n1:] = (ar * ti + ai * tr).astype(planar_scr.dtype)
        return carry

    lax.fori_loop(0, planar_scr.shape[0], body, 0, unroll=HY_UNROLL)


def _dft_lanes_stage(planar_scr, f_ref):
    cb, rows, _ = planar_scr.shape
    n1 = rows // 2
    p = jnp.dot(planar_scr[...].reshape(cb * rows, LANES), f_ref[...], preferred_element_type=F32)
    p = p.reshape(cb, rows, 2 * LANES)
    return p[:, :n1, :LANES] - p[:, n1:, LANES:], p[:, :n1, LANES:] + p[:, n1:, :LANES]


def _fft_filter_kernel(k_ref, ma_ref, tr_ref, ti_ref, f2_ref, xr_ref, xi_ref, a_scr):
    _dft_rows_stage(lambda c: k_ref[c], ma_ref, tr_ref, ti_ref, a_scr)
    xr, xi = _dft_lanes_stage(a_scr, f2_ref)
    xr_ref[...] = xr
    xi_ref[...] = xi


def _fft_conv_kernel(z_ref, gate_ref, kfr_ref, kfi_ref, kbr_ref, kbi_ref, skip_ref,
                     ma_ref, tr_ref, ti_ref, f2_ref, f2c_ref, mi_ref, o_ref, a_scr, b_scr):
    n1 = tr_ref.shape[0]
    n_out = o_ref.shape[2]
    sdt = a_scr.dtype
    _dft_rows_stage(lambda c: jnp.concatenate([z_ref[0, c], z_ref[1, c]], axis=0), ma_ref, tr_ref, ti_ref, a_scr)
    xr, xi = _dft_lanes_stage(a_scr, f2_ref)
    hr = kfr_ref[...] + kbr_ref[...]
    hi = kfi_ref[...] - kbi_ref[...]
    b_scr[:, :n1] = (xr * hr - xi * hi).astype(sdt)
    b_scr[:, n1:] = (xr * hi + xi * hr).astype(sdt)
    br, bi = _dft_lanes_stage(b_scr, f2c_ref)
    tr, ti = tr_ref[...][None], ti_ref[...][None]
    a_scr[:, :n1] = (br * tr + bi * ti).astype(sdt)
    a_scr[:, n1:] = (bi * tr - br * ti).astype(sdt)

    def body(c, carry):
        y = jnp.dot(mi_ref[...], a_scr[c], preferred_element_type=F32)
        sk = skip_ref[c]
        o_ref[0, c] = gate_ref[0, c] * (y[:n_out] + sk * z_ref[0, c][:n_out])
        o_ref[1, c] = gate_ref[1, c] * (y[n_out:] + sk * z_ref[1, c][:n_out])
        return carry

    lax.fori_loop(0, z_ref.shape[1], body, 0, unroll=HY_UNROLL)


def fft_filter_spectra(k_rows, n1, cdt=BF16, interpret=False):
    R, n_in, _ = k_rows.shape
    t = _dft_tables(n1, n_in, n_in, cdt)
    full = lambda a: pl.BlockSpec(a.shape, lambda i: (0,) * a.ndim)
    spec = jax.ShapeDtypeStruct((R, n1, LANES), F32)
    consts = (t["ma_re"], t["tr"], t["ti"], t["f2"])
    return pl.pallas_call(
        _fft_filter_kernel,
        grid=(R // HY_CB,),
        in_specs=[pl.BlockSpec((HY_CB, n_in, LANES), lambda i: (i, 0, 0))] + [full(a) for a in consts],
        out_specs=[pl.BlockSpec((HY_CB, n1, LANES), lambda i: (i, 0, 0))] * 2,
        out_shape=[spec, spec],
        scratch_shapes=[pltpu.VMEM((HY_CB, 2 * n1, LANES), cdt)],
        compiler_params=pltpu.CompilerParams(dimension_semantics=("arbitrary",), vmem_limit_bytes=VMEM_LIMIT),
        name="fft_filter", interpret=interpret,
    )(k_rows, *consts)


def fft_long_conv(z, z_ch0, gate, gate_ch0, spec_re, spec_im, kf_row0, kb_row0, skip, n1, cdt=BF16, interpret=False):
    B, _, n_in, _ = z.shape
    C = D_HY
    assert B == 2 and all(v % HY_CB == 0 for v in (C, z_ch0, gate_ch0, kf_row0, kb_row0))
    t = _dft_tables(n1, n_in, n_in, cdt)
    full = lambda a: pl.BlockSpec(a.shape, lambda i: (0,) * a.ndim)
    sig = lambda ch0: pl.BlockSpec((B, HY_CB, n_in, LANES), lambda i: (0, i + ch0 // HY_CB, 0, 0))
    spc = lambda row0: pl.BlockSpec((HY_CB, n1, LANES), lambda i: (i + row0 // HY_CB, 0, 0))
    consts = (t["ma"], t["tr"], t["ti"], t["f2"], t["f2c"], t["mi"])
    skip_b = jnp.broadcast_to(skip.astype(F32)[:, None, None], (C, 1, LANES))
    return pl.pallas_call(
        _fft_conv_kernel,
        grid=(C // HY_CB,),
        in_specs=[sig(z_ch0), sig(gate_ch0), spc(kf_row0), spc(kf_row0), spc(kb_row0), spc(kb_row0),
                  pl.BlockSpec((HY_CB, 1, LANES), lambda i: (i, 0, 0))] + [full(a) for a in consts],
        out_specs=sig(0),
        out_shape=jax.ShapeDtypeStruct((B, C, n_in, LANES), F32),
        scratch_shapes=[pltpu.VMEM((HY_CB, 2 * n1, LANES), cdt), pltpu.VMEM((HY_CB, 2 * n1, LANES), cdt)],
        compiler_params=pltpu.CompilerParams(dimension_semantics=("arbitrary",), vmem_limit_bytes=VMEM_LIMIT),
        name="fft_conv", interpret=interpret,
    )(z, gate, spec_re, spec_im, spec_re, spec_im, skip_b, *consts)


def _hy_filter_kernel(z_ref, w1_ref, b1_ref, w2_ref, b2_ref, fr_ref, w3_ref, dec_ref, k_ref, h_scr):
    hp = lax.Precision.HIGHEST

    @pl.when(pl.program_id(0) == 0)
    def _():
        h = jnp.sin(fr_ref[...] * (jnp.dot(w1_ref[...], z_ref[...], precision=hp, preferred_element_type=F32) + b1_ref[...]))
        h_scr[...] = jnp.sin(fr_ref[...] * (jnp.dot(w2_ref[...], h, precision=hp, preferred_element_type=F32) + b2_ref[...]))

    t = z_ref[0:1, :]
    k = jnp.dot(w3_ref[...], h_scr[...], precision=hp, preferred_element_type=F32) * jnp.exp(-t * dec_ref[...])
    k = k * lax.rsqrt(jnp.sum(k * k, axis=1, keepdims=True) + EPS)
    backward = (pl.program_id(0) // (D_HY // HY_FR)) % 2 == 1
    lag0 = lax.broadcasted_iota(jnp.int32, k.shape, 1) == 0
    k_ref[...] = jnp.where(jnp.logical_and(backward, lag0), 0.0, k)


def hyena_filters(L, w1, b1, w2, b2, w3, freq, decay, interpret=False):
    f32 = np.float32
    t = np.arange(L, dtype=f32) / f32(L)
    bands = np.linspace(1e-4, HY_BANDS - 1, HY_BANDS, dtype=f32)
    wpos = f32(2.0 * math.pi) * t[None, :] * bands[:, None]
    emb = 24
    z = jnp.asarray(np.concatenate([t[None, :], np.cos(wpos), np.sin(wpos), np.zeros((emb - HY_EMB, L), f32)], axis=0), F32)
    w1p = jnp.concatenate([w1.T, jnp.zeros((HY_FFN, emb - HY_EMB), F32)], axis=1)
    R = 2 * HY_ORDER * D_HY
    col = lambda v: v.reshape(-1, 1).astype(F32)
    full = lambda shape: pl.BlockSpec(shape, lambda i: (0,) * len(shape))
    return pl.pallas_call(
        _hy_filter_kernel,
        grid=(R // HY_FR,),
        in_specs=[full((emb, L)), full((HY_FFN, emb)), full((HY_FFN, 1)), full((HY_FFN, HY_FFN)), full((HY_FFN, 1)),
                  full((HY_FFN, 1)), pl.BlockSpec((HY_FR, HY_FFN), lambda i: (i, 0)),
                  pl.BlockSpec((HY_FR, 1), lambda i: (i, 0))],
        out_specs=pl.BlockSpec((HY_FR, L), lambda i: (i, 0)),
        out_shape=jax.ShapeDtypeStruct((R, L), F32),
        scratch_shapes=[pltpu.VMEM((HY_FFN, L), F32)],
        compiler_params=pltpu.CompilerParams(dimension_semantics=("arbitrary",)),
        name="hy_filter", interpret=interpret,
    )(z, w1p, col(b1), w2.T.astype(F32), col(b2), col(freq), w3.T.astype(F32), col(decay))


def _hy_shortconv_kernel(u_ref, up_ref, un_ref, w_ref, b_ref, o_ref):
    i = pl.program_id(1)
    n = pl.num_programs(1)
    x = u_ref[0]
    tt = x.shape[0]
    p1 = jnp.where(i > 0, up_ref[0][SUBLANES - 1:SUBLANES], 0.0)
    n1 = jnp.where(i < n - 1, un_ref[0][0:1], 0.0)
    ext = jnp.concatenate([p1, x, n1], axis=0)
    o_ref[0] = b_ref[...] + sum(w_ref[k:k + 1, :] * ext[k:k + tt] for k in range(HY_CONV))


def hyena_shortconv(u, row0, L, w, b, interpret=False):
    B, _, C = u.shape
    tt = min(256, L)
    hb = tt // SUBLANES
    nt = L // tt
    assert row0 % tt == 0
    t0, h0 = row0 // tt, row0 // SUBLANES
    return pl.pallas_call(
        _hy_shortconv_kernel,
        grid=(B, nt),
        in_specs=[pl.BlockSpec((1, tt, C), lambda bb, i: (bb, i + t0, 0)),
                  pl.BlockSpec((1, SUBLANES, C), lambda bb, i: (bb, h0 + jnp.maximum(i * hb - 1, 0), 0)),
                  pl.BlockSpec((1, SUBLANES, C), lambda bb, i: (bb, h0 + jnp.minimum((i + 1) * hb, nt * hb - 1), 0)),
                  pl.BlockSpec((HY_CONV, C), lambda bb, i: (0, 0)), pl.BlockSpec((1, C), lambda bb, i: (0, 0))],
        out_specs=pl.BlockSpec((1, tt, C), lambda bb, i: (bb, i, 0)),
        out_shape=jax.ShapeDtypeStruct((B, L, C), F32),
        compiler_params=pltpu.CompilerParams(dimension_semantics=("arbitrary", "arbitrary")),
        name="hy_shortconv", interpret=interpret,
    )(u, u, u, w.astype(F32), b.reshape(1, C).astype(F32))


def hyena_pallas(u, row0, L, conv_w, conv_b, w1, b1, w2, b2, w3, freq, decay, skip, cdt=BF16, interpret=False):
    B = u.shape[0]
    rows = L // LANES
    n1 = max(2 * rows, SUBLANES)
    n_in = n1 // 2 if n1 == 2 * rows else n1
    uc = hyena_shortconv(u, row0, L, conv_w, conv_b, interpret=interpret)
    ucm = uc.transpose(0, 2, 1).reshape(B, 3 * D_HY, rows, LANES)
    if n_in != rows:
        ucm = jnp.pad(ucm, ((0, 0), (0, 0), (0, n_in - rows), (0, 0)))

    k = hyena_filters(L, w1, b1, w2, b2, w3, freq, decay, interpret=interpret)
    k = k.reshape(2 * HY_ORDER * D_HY, rows, LANES)
    if n_in != rows:
        k = jnp.pad(k, ((0, 0), (0, n_in - rows), (0, 0)))
    sr, si = fft_filter_spectra(k, n1, cdt=cdt, interpret=interpret)

    y, y_ch0 = ucm, 0
    for o in range(HY_ORDER):
        y = fft_long_conv(y, y_ch0, ucm, (o + 1) * D_HY, sr, si, (2 * o) * D_HY, (2 * o + 1) * D_HY, skip[o], n1,
                          cdt=cdt, interpret=interpret)
    return y[:, :, :rows].reshape(B, D_HY, L).transpose(0, 2, 1)


ROW_T = CTX_LEN
ROW_NT = SEQ_ALL // ROW_T
MOD_ROWS = SUBLANES
MLA_DK = LANES
ROPE_SWAP = np.arange(MLA_ROPE).reshape(2, 2, MLA_ROPE // 4)[:, ::-1].reshape(-1)


def _rms(x):
    return x * lax.rsqrt(jnp.mean(x * x, axis=-1, keepdims=True) + EPS)


def _mod_kernel(c_ref, w_ref, b_ref, o_ref):
    c = c_ref[...]
    s = (c * jax.nn.sigmoid(c)).astype(BF16)
    o_ref[...] = jnp.dot(s, w_ref[...].astype(BF16), preferred_element_type=F32) + b_ref[...]


def adaln_modulation(c, c_ctx, w_mod, b_mod, interpret=False):
    B, D = c.shape
    cs = jnp.concatenate([c, c_ctx[None], jnp.zeros((MOD_ROWS - B - 1, D), F32)], axis=0)
    out = pl.pallas_call(
        _mod_kernel,
        grid=(6,),
        in_specs=[pl.BlockSpec((MOD_ROWS, D), lambda j: (0, 0)),
                  pl.BlockSpec((D, D), lambda j: (0, j)),
                  pl.BlockSpec((1, D), lambda j: (0, j))],
        out_specs=pl.BlockSpec((MOD_ROWS, D), lambda j: (0, j)),
        out_shape=jax.ShapeDtypeStruct((MOD_ROWS, 6 * D), F32),
        compiler_params=pltpu.CompilerParams(dimension_semantics=("arbitrary",)),
        name="adaln_mod", interpret=interpret,
    )(cs, w_mod, b_mod.reshape(1, 6 * D))
    mod = out.reshape(MOD_ROWS, 6, D)
    return jnp.stack([jnp.broadcast_to(mod[B][None], (B, 6, D)), mod[:B]], axis=1)


IN_COLS = (D_LRU, D_LRU, MLA_Q_RANK, MLA_KV_RANK, 3 * D_HY, 2 * MLA_ROPE)


def _in_proj_kernel(s_ref, m_ref, g_ref, w_ref, *out_refs):
    m = m_ref[0, 0]
    h = _rms(s_ref[0]) * g_ref[...] * (1.0 + m[1:2]) + m[0:1]
    u = jnp.dot(h.astype(w_ref.dtype), w_ref[...], preferred_element_type=F32)
    off = 0
    for ref, n in zip(out_refs, IN_COLS):
        ref[0] = u[:, off:off + n]
        off += n


def in_projection(s, mod, norm_g, w_in, cdt=BF16, interpret=False):
    B, R, D = s.shape
    o = (0,) + tuple(IN_SPLITS) + (D_IN,)
    kr = w_in[:, o[4]:o[5]]
    w = jnp.concatenate([w_in[:, :o[4]], w_in[:, o[5]:], kr, kr[:, ROPE_SWAP]], axis=1).astype(cdt)
    return pl.pallas_call(
        _in_proj_kernel,
        grid=(B, R // ROW_T),
        in_specs=[pl.BlockSpec((1, ROW_T, D), lambda b, i: (b, i, 0)),
                  pl.BlockSpec((1, 1, 6, D), lambda b, i: (b, jnp.minimum(i, 1), 0, 0)),
                  pl.BlockSpec((1, D), lambda b, i: (0, 0)),
                  pl.BlockSpec(w.shape, lambda b, i: (0, 0))],
        out_specs=[pl.BlockSpec((1, ROW_T, n), lambda b, i: (b, i, 0)) for n in IN_COLS],
        out_shape=[jax.ShapeDtypeStruct((B, R, n), F32) for n in IN_COLS],
        compiler_params=pltpu.CompilerParams(
            dimension_semantics=("arbitrary", "arbitrary"), vmem_limit_bytes=VMEM_LIMIT),
        name="in_proj", interpret=interpret,
    )(s, mod, norm_g.reshape(1, D), w)


def _mla_proj_kernel(cq_ref, ckv_ref, kr_ref, rc_ref, rs_ref, qg_ref, wq_ref, wqs_ref, kg_ref, wkv_ref,
                     q_ref, k_ref, v_ref):
    cdt = wq_ref.dtype
    cqn = (_rms(cq_ref[0]) * qg_ref[...]).astype(cdt)
    qa = jnp.dot(cqn, wq_ref[...], preferred_element_type=F32)
    qs = jnp.dot(cqn, wqs_ref[...], preferred_element_type=F32)
    ckvn = (_rms(ckv_ref[0]) * kg_ref[...]).astype(cdt)
    kv = jnp.dot(ckvn, wkv_ref[...], preferred_element_type=F32)
    rc, rs = rc_ref[...], rs_ref[...]
    kr = kr_ref[0]
    k_rope = (kr[:, :MLA_ROPE] * rc[:, MLA_NOPE:MLA_NOPE + MLA_ROPE]
              + kr[:, MLA_ROPE:] * rs[:, MLA_NOPE:MLA_NOPE + MLA_ROPE])
    pad = jnp.zeros((kr.shape[0], MLA_DK - MLA_NOPE - MLA_ROPE), F32)
    for h in range(MLA_HEADS):
        sl = slice(h * MLA_DK, (h + 1) * MLA_DK)
        q_ref[0, h] = ((qa[:, sl] * rc + qs[:, sl] * rs) * MLA_SCALE).astype(q_ref.dtype)
        kvh = kv[:, h * (MLA_NOPE + MLA_V):(h + 1) * (MLA_NOPE + MLA_V)]
        k_ref[0, h] = jnp.concatenate([kvh[:, :MLA_NOPE], k_rope, pad], axis=1).astype(k_ref.dtype)
        v_ref[0, h] = jnp.concatenate([kvh[:, MLA_NOPE:], jnp.ones((kvh.shape[0], MLA_V), F32)],
                                      axis=1).astype(v_ref.dtype)


def _rope_tables():
    f32 = np.float32
    grid_rows = SEQ // GRID_W
    row = np.repeat(np.arange(grid_rows, dtype=f32), GRID_W)
    col = np.tile(np.arange(GRID_W, dtype=f32), grid_rows)
    inv_freq = (f32(ROPE_THETA) ** (-np.arange(ROPE_PAIRS, dtype=f32) / f32(ROPE_PAIRS))).astype(f32)
    ang_r, ang_c = row[:, None] * inv_freq, col[:, None] * inv_freq
    cos = np.concatenate([np.cos(ang_r), np.cos(ang_r), np.cos(ang_c), np.cos(ang_c)], axis=1)
    sin = np.concatenate([-np.sin(ang_r), np.sin(ang_r), -np.sin(ang_c), np.sin(ang_c)], axis=1)
    npad = MLA_DK - MLA_NOPE - MLA_ROPE
    cos = np.concatenate([np.ones((SEQ, MLA_NOPE), f32), cos, np.zeros((SEQ, npad), f32)], axis=1)
    sin = np.concatenate([np.zeros((SEQ, MLA_NOPE), f32), sin, np.zeros((SEQ, npad), f32)], axis=1)
    ctx_cos = np.concatenate([np.ones((CTX_LEN, MLA_NOPE + MLA_ROPE), f32), np.zeros((CTX_LEN, npad), f32)], axis=1)
    return (jnp.asarray(np.concatenate([ctx_cos, cos], axis=0), F32),
            jnp.asarray(np.concatenate([np.zeros((CTX_LEN, MLA_DK), f32), sin], axis=0), F32))


def mla_projection(cq, ckv, kr2, q_g, wqb, kv_g, wkvb, cdt=BF16, interpret=False):
    B, R, _ = cq.shape
    dq = MLA_NOPE + MLA_ROPE
    wq3 = wqb.reshape(MLA_Q_RANK, MLA_HEADS, dq)
    zq = jnp.zeros((MLA_Q_RANK, MLA_HEADS, MLA_DK - dq), wqb.dtype)
    wq = jnp.concatenate([wq3, zq], axis=2).reshape(MLA_Q_RANK, MLA_HEADS * MLA_DK).astype(cdt)
    wqs = jnp.concatenate([jnp.zeros((MLA_Q_RANK, MLA_HEADS, MLA_NOPE), wqb.dtype),
                           wq3[:, :, MLA_NOPE:][:, :, ROPE_SWAP], zq], axis=2)
    wqs = wqs.reshape(MLA_Q_RANK, MLA_HEADS * MLA_DK).astype(cdt)
    rc, rs = _rope_tables()
    row = lambda n: pl.BlockSpec((1, ROW_T, n), lambda b, i: (b, i, 0))
    tab = pl.BlockSpec((ROW_T, MLA_DK), lambda b, i: (i, 0))
    full = lambda shape: pl.BlockSpec(shape, lambda b, i: (0,) * len(shape))
    head = lambda n: pl.BlockSpec((1, MLA_HEADS, ROW_T, n), lambda b, i: (b, 0, i, 0))
    return pl.pallas_call(
        _mla_proj_kernel,
        grid=(B, R // ROW_T),
        in_specs=[row(MLA_Q_RANK), row(MLA_KV_RANK), row(2 * MLA_ROPE), tab, tab,
                  full((1, MLA_Q_RANK)), full(wq.shape), full(wqs.shape),
                  full((1, MLA_KV_RANK)), full(wkvb.shape)],
        out_specs=[head(MLA_DK), head(MLA_DK), head(2 * MLA_V)],
        out_shape=[jax.ShapeDtypeStruct((B, MLA_HEADS, R, MLA_DK), cdt),
                   jax.ShapeDtypeStruct((B, MLA_HEADS, R, MLA_DK), cdt),
                   jax.ShapeDtypeStruct((B, MLA_HEADS, R, 2 * MLA_V), cdt)],
        compiler_params=pltpu.CompilerParams(dimension_semantics=("arbitrary", "arbitrary")),
        name="mla_proj", interpret=interpret,
    )(cq, ckv, kr2, rc, rs, q_g.reshape(1, -1), wq, wqs, kv_g.reshape(1, -1), wkvb.astype(cdt))


def _out_proj_kernel(s_ref, m_ref, g_ref, ya_ref, ybx_ref, ybc_ref, ycx_ref, ycc_ref, w_ref, snew_ref, hT_ref):
    cdt = w_ref.dtype
    is_ctx = pl.program_id(1) == 0
    yb = jnp.where(is_ctx, ybc_ref[0], ybx_ref[0])
    yc = jnp.where(is_ctx, ycc_ref[0], ycx_ref[0])
    mix = (jnp.dot(ya_ref[0].astype(cdt), w_ref[0:D_LRU, :], preferred_element_type=F32)
           + jnp.dot(yb.astype(cdt), w_ref[D_LRU:D_LRU + MLA_HEADS * MLA_V, :], preferred_element_type=F32)
           + jnp.dot(yc.astype(cdt), w_ref[D_LRU + MLA_HEADS * MLA_V:, :], preferred_element_type=F32))
    m = m_ref[0, 0]
    s_new = s_ref[0] + m[2:3] * mix
    snew_ref[0] = s_new
    h2 = _rms(s_new) * g_ref[...] * (1.0 + m[4:5]) + m[3:4]
    hT_ref[...] = h2.T.astype(hT_ref.dtype)


def out_projection(s, mod, norm2_g, ya, yb_x, yb_c, yc_x, yc_c, w_out, cdt=BF16, interpret=False):
    B, R, D = s.shape
    nt = R // ROW_T
    row = lambda n: pl.BlockSpec((1, ROW_T, n), lambda b, i: (b, i, 0))
    lat = lambda n: pl.BlockSpec((1, ROW_T, n), lambda b, i: (b, jnp.maximum(i - 1, 0), 0))
    ctx = lambda n: pl.BlockSpec((1, ROW_T, n), lambda b, i: (b, 0, 0))
    return pl.pallas_call(
        _out_proj_kernel,
        grid=(B, nt),
        in_specs=[row(D),
                  pl.BlockSpec((1, 1, 6, D), lambda b, i: (b, jnp.minimum(i, 1), 0, 0)),
                  pl.BlockSpec((1, D), lambda b, i: (0, 0)),
                  row(D_LRU),
                  lat(MLA_HEADS * MLA_V), ctx(MLA_HEADS * MLA_V), lat(D_HY), ctx(D_HY),
                  pl.BlockSpec((D_MIX, D), lambda b, i: (0, 0))],
        out_specs=[row(D), pl.BlockSpec((D, ROW_T), lambda b, i: (0, b * nt + i))],
        out_shape=[jax.ShapeDtypeStruct((B, R, D), F32), jax.ShapeDtypeStruct((D, B * R), cdt)],
        compiler_params=pltpu.CompilerParams(
            dimension_semantics=("arbitrary", "arbitrary"), vmem_limit_bytes=VMEM_LIMIT),
        name="out_proj", interpret=interpret,
    )(s, mod, norm2_g.reshape(1, D), ya, yb_x, yb_c, yc_x, yc_c, w_out.astype(cdt))


PEER_TM = 768
PEER_TE = 1024
PEER_KP = 256
NEG_BIG = -3.0e38
PACKED_ROWS = 2 * SUBLANES


def _bitonic_clean(vals):
    vals = list(vals)
    d = len(vals) // 2
    while d >= 1:
        for i in range(len(vals)):
            if i & d == 0:
                vals[i], vals[i + d] = jnp.maximum(vals[i], vals[i + d]), jnp.minimum(vals[i], vals[i + d])
        d //= 2
    return vals


def _topk_sorted(s, k):
    assert s.shape[0] == k * SUBLANES and k & (k - 1) == 0
    vals = [s[i * SUBLANES:(i + 1) * SUBLANES, :] for i in range(k)]
    size = 2
    while size <= k:
        for lo in range(0, k, size):
            blk = vals[lo:lo + size]
            vals[lo:lo + size] = _bitonic_clean(blk[:size // 2] + blk[size // 2:][::-1])
        size *= 2
    shift = SUBLANES // 2
    while shift >= 1:
        other = [pltpu.roll(v, shift, axis=0) for v in vals]
        vals = _bitonic_clean([jnp.maximum(vals[i], other[k - 1 - i]) for i in range(k)])
        shift //= 2
    return vals


def _rows_of(tiles):
    row = lax.broadcasted_iota(jnp.int32, tiles[0].shape, 0)
    out = []
    for lo in range(0, len(tiles), SUBLANES):
        acc = tiles[lo]
        for r in range(1, SUBLANES):
            acc = jnp.where(row == r, tiles[lo + r], acc)
        out.append(acc)
    return jnp.concatenate(out, axis=0)


def _route_chunk(s1, s2):
    K = PEER_TOPK
    t1 = _topk_sorted(s1, K)
    t2 = _topk_sorted(s2, K)
    v1, v2 = _rows_of(t1), _rows_of(t2)
    tile_rows = lambda x: [x[i * SUBLANES:(i + 1) * SUBLANES, :] for i in range(x.shape[0] // SUBLANES)]
    rank2 = []
    for x in tile_rows(s2):
        r = jnp.full(x.shape, float(PEER_NKEYS), F32)
        for i in range(K - 1, -1, -1):
            r = jnp.where(x == t2[i], float(i), r)
        rank2.append(r)
    rank2 = jnp.concatenate(rank2, axis=0)
    half = K // 2
    cands = ([v1[0:1] + v2] + [v1[i:i + 1] + v2[:half] for i in range(1, half)]
             + [v1[i:i + 1] + v2[0:1] for i in range(half, K)])
    cmax = v1[0:1] + v2[0:1]
    c = jnp.concatenate(cands[:half] + [v1[half:] + v2[0:1]], axis=0)
    z = jnp.zeros_like(cmax)
    m = cmax
    for r in range(K):
        m = jnp.max(c, axis=0, keepdims=True)
        z = z + jnp.exp(m - cmax)
        if r + 1 < K:
            c = jnp.where(c == m, NEG_BIG, c)
    thr = m
    n_i = [jnp.sum(jnp.where(cands[i] >= thr, 1.0, 0.0), axis=0, keepdims=True) for i in range(K)]
    n = []
    for x in tile_rows(s1):
        acc = jnp.zeros(x.shape, F32)
        for i in range(K):
            acc = jnp.where(x == t1[i], n_i[i], acc)
        n.append(acc)
    n = jnp.concatenate(n, axis=0)
    w = jnp.exp(s1 - v1[0:1]) / z
    e2 = jnp.exp(s2 - v2[0:1])
    return rank2, e2, n, w


def _peer_route_kernel(hT_ref, wqT_ref, keys_ref, rank2_ref, e2_ref, n_ref, w_ref, q_scr, s1_scr, s2_scr):
    h = pl.program_id(1)

    @pl.when(h == 0)
    def _():
        q_scr[...] = jnp.dot(wqT_ref[...], hT_ref[...], preferred_element_type=F32)

    cdt = keys_ref.dtype
    half = PEER_DQ // 2
    row = pl.multiple_of(h * PEER_DQ, PEER_DQ)
    q1 = q_scr[pl.ds(row, half), :].astype(cdt)
    q2 = q_scr[pl.ds(row + half, half), :].astype(cdt)
    s1_scr[...] = jnp.dot(keys_ref[0], q1, preferred_element_type=F32)
    s2_scr[...] = jnp.dot(keys_ref[1], q2, preferred_element_type=F32)

    def chunk(c, carry):
        sl = pl.ds(pl.multiple_of(c * LANES, LANES), LANES)
        rank2, e2, n, w = _route_chunk(s1_scr[:, sl], s2_scr[:, sl])
        rank2_ref[0, :, sl] = rank2.astype(rank2_ref.dtype)
        e2_ref[0, :, sl] = e2.astype(e2_ref.dtype)
        n_ref[0, :, sl] = n
        w_ref[0, :, sl] = w
        return carry

    lax.fori_loop(0, s1_scr.shape[1] // LANES, chunk, 0, unroll=2)


def _peer_dense_kernel(hT_ref, rank2_ref, e2_ref, n_ref, w_ref, u_ref, vT_ref, s_ref, g_ref, o_ref, acc_ref, pre_ref,
                       *, n_chunks):
    j = pl.program_id(1)

    te, tm = u_ref.shape[0], hT_ref.shape[1]
    gdt = rank2_ref.dtype

    def first_matmul(wr):
        pre_ref[wr] = jnp.dot(u_ref[...], hT_ref[...], preferred_element_type=F32)

    def second_matmul(rd):
        out = None
        for k0 in range(0, te, PEER_KP):
            acts = []
            for al in range(k0 // PEER_NKEYS, (k0 + PEER_KP) // PEER_NKEYS):
                g = None
                for h in range(PEER_HEADS):
                    n16 = jnp.broadcast_to(n_ref[h, 0, al:al + 1, :], (PACKED_ROWS, tm)).astype(gdt)
                    w16 = jnp.broadcast_to(w_ref[h, 0, al:al + 1, :], (PACKED_ROWS, tm)).astype(gdt)
                    nb = jnp.concatenate([n16] * (PEER_NKEYS // PACKED_ROWS), axis=0)
                    wb = jnp.concatenate([w16] * (PEER_NKEYS // PACKED_ROWS), axis=0)
                    t = jnp.where(rank2_ref[h] < nb, e2_ref[h] * wb, jnp.zeros_like(wb))
                    g = t if g is None else g + t
                blk = slice(al * PEER_NKEYS, (al + 1) * PEER_NKEYS)
                acts.append((g * _gelu_tanh(pre_ref[rd, blk, :].astype(gdt))).astype(vT_ref.dtype))
            part = jnp.dot(vT_ref[0, :, k0:k0 + PEER_KP], jnp.concatenate(acts, axis=0), preferred_element_type=F32)
            out = part if out is None else out + part
        acc_ref[...] += out

    def step(rd, wr):
        first_matmul(wr)
        second_matmul(rd)

    @pl.when(j == 0)
    def _():
        acc_ref[...] = jnp.zeros_like(acc_ref)
        first_matmul(0)

    for parity in range(2):
        pl.when((j % 2 == parity) & (j > 0) & (j < n_chunks))(functools.partial(step, 1 - parity, parity))

    @pl.when(j == n_chunks)
    def _():
        second_matmul((n_chunks - 1) % 2)
        y = acc_ref[...].T
        for r in range(g_ref.shape[0]):
            rows = slice(r * ROW_T, (r + 1) * ROW_T)
            o_ref[rows, :] = s_ref[rows, :] + g_ref[r] * y[rows, :]


def peer_tokens(hT, s, gate, wq, keys, u_tab, v_tab, interpret=False):
    D, T = hT.shape
    cdt = hT.dtype
    tm, te = PEER_TM, PEER_TE
    assert T % tm == 0 and PEER_EXPERTS % te == 0 and tm % ROW_T == 0
    wqT = wq.T.astype(cdt)
    route_shape = jax.ShapeDtypeStruct((PEER_HEADS, PEER_NKEYS, T), F32)
    packed_shape = jax.ShapeDtypeStruct((PEER_HEADS, PEER_NKEYS, T), cdt)
    route_spec = pl.BlockSpec((1, PEER_NKEYS, tm), lambda i, hh: (hh, 0, i))
    rank2, e2, n, w = pl.pallas_call(
        _peer_route_kernel,
        grid=(T // tm, PEER_HEADS),
        in_specs=[pl.BlockSpec((D, tm), lambda i, hh: (0, i)),
                  pl.BlockSpec((PEER_HEADS * PEER_DQ, D), lambda i, hh: (0, 0)),
                  pl.BlockSpec((2, PEER_NKEYS, PEER_DQ // 2), lambda i, hh: (0, 0, 0))],
        out_specs=[route_spec] * 4,
        out_shape=[packed_shape, packed_shape, route_shape, route_shape],
        scratch_shapes=[pltpu.VMEM((PEER_HEADS * PEER_DQ, tm), F32),
                        pltpu.VMEM((PEER_NKEYS, tm), F32),
                        pltpu.VMEM((PEER_NKEYS, tm), F32)],
        compiler_params=pltpu.CompilerParams(
            dimension_semantics=("arbitrary", "arbitrary"), vmem_limit_bytes=VMEM_LIMIT),
        name="peer_route", interpret=interpret,
    )(hT, wqT, keys.astype(cdt))

    head_spec = pl.BlockSpec((PEER_HEADS, PEER_NKEYS, tm), lambda i, j: (0, 0, i))
    nj = PEER_EXPERTS // te
    groups = te // PEER_NKEYS
    row_spec = pl.BlockSpec((PEER_HEADS, 1, groups, tm), lambda i, j: (0, jnp.maximum(j - 1, 0), 0, i))
    n = n.reshape(PEER_HEADS, nj, groups, T)
    w = w.reshape(PEER_HEADS, nj, groups, T)
    vT = v_tab.astype(cdt).reshape(nj, te, D).transpose(0, 2, 1)
    return pl.pallas_call(
        functools.partial(_peer_dense_kernel, n_chunks=nj),
        grid=(T // tm, nj + 1),
        in_specs=[pl.BlockSpec((D, tm), lambda i, j: (0, i)),
                  head_spec, head_spec, row_spec, row_spec,
                  pl.BlockSpec((te, D), lambda i, j: (jnp.minimum(j, nj - 1), 0)),
                  pl.BlockSpec((1, D, te), lambda i, j: (jnp.maximum(j - 1, 0), 0, 0)),
                  pl.BlockSpec((tm, D), lambda i, j: (i, 0)),
                  pl.BlockSpec((tm // ROW_T, 1, D), lambda i, j: (i, 0, 0))],
        out_specs=pl.BlockSpec((tm, D), lambda i, j: (i, 0)),
        out_shape=jax.ShapeDtypeStruct((T, D), F32),
        scratch_shapes=[pltpu.VMEM((D, tm), F32), pltpu.VMEM((2, te, tm), F32)],
        compiler_params=pltpu.CompilerParams(
            dimension_semantics=("arbitrary", "arbitrary"), vmem_limit_bytes=VMEM_LIMIT),
        name="peer_dense", interpret=interpret,
    )(hT, rank2, e2, n, w, u_tab.astype(cdt), vT, s, gate)


def _final_norm_kernel(x_ref, g_ref, o_ref):
    o_ref[0] = _rms(x_ref[0]) * g_ref[...]


def final_norm(s, g):
    B, R, D = s.shape
    return pl.pallas_call(
        _final_norm_kernel,
        grid=(B, SEQ // ROW_T),
        in_specs=[pl.BlockSpec((1, ROW_T, D), lambda b, i: (b, i + CTX_LEN // ROW_T, 0)),
                  pl.BlockSpec((1, D), lambda b, i: (0, 0))],
        out_specs=pl.BlockSpec((1, ROW_T, D), lambda b, i: (b, i, 0)),
        out_shape=jax.ShapeDtypeStruct((B, SEQ, D), F32),
        compiler_params=pltpu.CompilerParams(dimension_semantics=("arbitrary", "arbitrary")),
        name="final_norm",
    )(s, g.reshape(1, D))


def kernel(x, c, ctx, c_ctx, w_mod, b_mod, norm1_g, norm2_g, w_in, w_out,
           lru_conv_w, lru_conv_b, lru_wr, lru_br, lru_wi, lru_bi, lru_lambda,
           mla_q_norm_g, mla_wqb, mla_kv_norm_g, mla_wkvb,
           hy_conv_w, hy_conv_b, hy_f_w1, hy_f_b1, hy_f_w2, hy_f_b2, hy_f_w3,
           hy_f_freq, hy_decay, hy_skip,
           peer_wq, peer_keys, peer_u, peer_v, final_g):
    B = x.shape[0]
    s = jnp.concatenate([ctx, x], axis=1)
    for l in range(DEPTH):
        need_ctx = l < DEPTH - 1
        mod = adaln_modulation(c, c_ctx, w_mod[l], b_mod[l])
        u_lru, u_gate, cq, ckv, u_hy, kr2 = in_projection(s, mod, norm1_g[l], w_in[l])

        ya = rglru_mixer(u_lru, u_gate, lru_conv_w[l], lru_conv_b[l], lru_wr[l], lru_br[l], lru_wi[l], lru_bi[l],
                         lru_lambda[l])

        q, k, v = mla_projection(cq, ckv, kr2, mla_q_norm_g[l], mla_wqb[l], mla_kv_norm_g[l], mla_wkvb[l])
        yb_x = flash_attention(q, k, v, q_row0=CTX_LEN, n_q=SEQ, n_k=SEQ_ALL, tk=SEQ_ALL // 3, n_sub=1)
        if need_ctx:
            yb_c = flash_attention(q, k, v, q_row0=0, n_q=CTX_LEN, n_k=CTX_LEN, tk=CTX_LEN, n_sub=1)
        else:
            yb_c = jnp.zeros((B, CTX_LEN, MLA_HEADS * MLA_V), F32)

        hy_args = (hy_conv_w[l], hy_conv_b[l], hy_f_w1[l], hy_f_b1[l], hy_f_w2[l], hy_f_b2[l], hy_f_w3[l],
                   hy_f_freq[l], hy_decay[l], hy_skip[l])
        yc_x = hyena_pallas(u_hy, CTX_LEN, SEQ, *hy_args)
        yc_c = hyena_pallas(u_hy, 0, CTX_LEN, *hy_args) if need_ctx else jnp.zeros((B, CTX_LEN, D_HY), F32)

        s, hT = out_projection(s, mod, norm2_g[l], ya, yb_x, yb_c, yc_x, yc_c, w_out[l])
        gate2 = jnp.repeat(mod[:, :, 5], jnp.array([1, ROW_NT - 1]), axis=1, total_repeat_length=ROW_NT)
        s = peer_tokens(hT, s.reshape(B * SEQ_ALL, D_MODEL), gate2.reshape(B * ROW_NT, 1, D_MODEL),
                        peer_wq[l], peer_keys[l], peer_u[l], peer_v[l]).reshape(B, SEQ_ALL, D_MODEL)
    return final_norm(s, final_g)
```

```python
import functools
import math

import jax
import jax.numpy as jnp
import numpy as np
from jax import lax
from jax.experimental import pallas as pl
from jax.experimental.pallas import tpu as pltpu

D_MODEL = 1024
BATCH = 2
SEQ = 8192
DEPTH = 2

GRID_W = 64
CTX_LEN = 256
EPS = 1e-6

D_LRU = 384
LRU_BLOCKS = 6
LRU_BLOCK_DIM = D_LRU // LRU_BLOCKS
LRU_CONV = 4
LRU_C = 8.0

MLA_HEADS = 6
MLA_NOPE = 64
MLA_ROPE = 32
MLA_V = 64
MLA_Q_RANK = 384
MLA_KV_RANK = 256
MLA_SCALE = (MLA_NOPE + MLA_ROPE) ** -0.5
ROPE_PAIRS = MLA_ROPE // 4
ROPE_THETA = 10000.0

D_HY = 256
HY_CONV = 3
HY_ORDER = 2
HY_BANDS = 8
HY_EMB = 1 + 2 * HY_BANDS
HY_FFN = 64

D_MIX = D_LRU + MLA_HEADS * MLA_V + D_HY
D_IN = 2 * D_LRU + MLA_Q_RANK + MLA_KV_RANK + MLA_ROPE + 3 * D_HY
IN_SPLITS = [D_LRU, 2 * D_LRU, 2 * D_LRU + MLA_Q_RANK,
             2 * D_LRU + MLA_Q_RANK + MLA_KV_RANK,
             2 * D_LRU + MLA_Q_RANK + MLA_KV_RANK + MLA_ROPE]

PEER_HEADS = 8
PEER_NKEYS = 128
PEER_EXPERTS = PEER_NKEYS * PEER_NKEYS
PEER_DQ = 256
PEER_TOPK = 16

F32 = jnp.float32


BF16 = jnp.bfloat16
LANES = 128
SUBLANES = 8
VMEM_LIMIT = 56 * 1024 * 1024
SEQ_ALL = CTX_LEN + SEQ
LRU_TT = CTX_LEN
LRU_NT = SEQ_ALL // LRU_TT


def _gelu_tanh(x):
    return 0.5 * x * (1.0 + jnp.tanh(0.7978845608028654 * (x + 0.044715 * (x * x * x))))


def _lru_coeff_kernel(u_ref, up_ref, un_ref, cw_ref, cb_ref, wg_ref, bg_ref, c_ref,
                      af_ref, bf_ref, ab_ref, bb_ref):
    i = pl.program_id(1)
    x = u_ref[0]
    tt = x.shape[0]
    use_prev = i >= 2
    use_next = jnp.logical_and(i >= 1, i < LRU_NT - 1)
    p2 = jnp.where(use_prev, up_ref[0][SUBLANES - 2:SUBLANES], 0.0)
    n1 = jnp.where(use_next, un_ref[0][0:1], 0.0)
    ext = jnp.concatenate([p2, x, n1], axis=0)
    xc = cb_ref[...] + sum(cw_ref[k:k + 1, :] * ext[k:k + tt] for k in range(LRU_CONV))
    g = jnp.dot(xc.astype(wg_ref.dtype), wg_ref[...], preferred_element_type=F32) + bg_ref[...]
    for d, (a_ref, b_ref) in enumerate(((af_ref, bf_ref), (ab_ref, bb_ref))):
        r = jax.nn.sigmoid(g[:, (2 * d) * D_LRU:(2 * d + 1) * D_LRU])
        gi = jax.nn.sigmoid(g[:, (2 * d + 1) * D_LRU:(2 * d + 2) * D_LRU])
        log_a = c_ref[d:d + 1, :] * r
        a_ref[0] = jnp.exp(log_a)
        b_ref[0] = jnp.sqrt(1.0 - jnp.exp(2.0 * log_a)) * (gi * xc)


def _lru_scan_rows(a_ref, b_ref, h_scr, emit, reverse):
    tt = a_ref.shape[1]
    nb = a_ref.shape[0]

    def body(s, hs):
        t = (tt - 1 - s) if reverse else s
        out = []
        for b in range(nb):
            h = a_ref[b, pl.ds(t, 1), :] * hs[b] + b_ref[b, pl.ds(t, 1), :]
            emit(b, t, h)
            out.append(h)
        return tuple(out)

    hs = lax.fori_loop(0, tt, body, tuple(h_scr[b:b + 1, :] for b in range(nb)), unroll=8)
    for b in range(nb):
        h_scr[b:b + 1, :] = hs[b]


def _lru_fwd_kernel(a_ref, b_ref, hf_ref, h_scr):
    @pl.when(pl.program_id(0) == 0)
    def _():
        h_scr[...] = jnp.zeros_like(h_scr)

    def emit(b, t, h):
        hf_ref[b, pl.ds(t, 1), :] = h

    _lru_scan_rows(a_ref, b_ref, h_scr, emit, reverse=False)


def _lru_bwd_kernel(a_ref, b_ref, hf_ref, gate_ref, y_ref, h_scr):
    @pl.when(pl.program_id(0) == 0)
    def _():
        h_scr[...] = jnp.zeros_like(h_scr)

    def emit(b, t, h):
        y_ref[b, pl.ds(t, 1), :] = h

    _lru_scan_rows(a_ref, b_ref, h_scr, emit, reverse=True)
    y_ref[...] = (y_ref[...] + hf_ref[...]) * _gelu_tanh(gate_ref[...])


def rglru_mixer(u_lru, u_gate, conv_w, conv_b, wr, br, wi, bi, lam, cdt=BF16, interpret=False):
    B = u_lru.shape[0]
    tt, nt = LRU_TT, LRU_NT
    hb = tt // SUBLANES

    def blockdiag(w):
        eye = jnp.eye(LRU_BLOCKS, dtype=w.dtype)
        return jnp.einsum('nde,nm->ndme', w, eye).reshape(D_LRU, D_LRU)

    wg = jnp.concatenate([blockdiag(wr[0]), blockdiag(wi[0]), blockdiag(wr[1]), blockdiag(wi[1])], axis=1).astype(cdt)
    bg = jnp.concatenate([br[0], bi[0], br[1], bi[1]]).reshape(1, 4 * D_LRU).astype(F32)
    cdec = -LRU_C * jax.nn.softplus(-lam.astype(F32))

    tile = pl.BlockSpec((1, tt, D_LRU), lambda b, i: (b, i, 0))
    full = lambda shape: pl.BlockSpec(shape, lambda b, i: (0,) * len(shape))
    coeff_shape = jax.ShapeDtypeStruct((B, SEQ_ALL, D_LRU), F32)
    a_f, b_f, a_b, b_b = pl.pallas_call(
        _lru_coeff_kernel,
        grid=(B, nt),
        in_specs=[tile,
                  pl.BlockSpec((1, SUBLANES, D_LRU), lambda b, i: (b, jnp.maximum(i * hb - 1, 0), 0)),
                  pl.BlockSpec((1, SUBLANES, D_LRU), lambda b, i: (b, jnp.minimum((i + 1) * hb, nt * hb - 1), 0)),
                  full((LRU_CONV, D_LRU)), full((1, D_LRU)), full((D_LRU, 4 * D_LRU)), full((1, 4 * D_LRU)),
                  full((2, D_LRU))],
        out_specs=[tile] * 4,
        out_shape=[coeff_shape] * 4,
        compiler_params=pltpu.CompilerParams(dimension_semantics=("arbitrary", "arbitrary")),
        name="lru_coeff", interpret=interpret,
    )(u_lru, u_lru, u_lru, conv_w.astype(F32), conv_b.reshape(1, D_LRU).astype(F32), wg, bg, cdec)

    seq = pl.BlockSpec((B, tt, D_LRU), lambda s: (0, s, 0))
    h_f = pl.pallas_call(
        _lru_fwd_kernel,
        grid=(nt,),
        in_specs=[seq, seq],
        out_specs=seq,
        out_shape=coeff_shape,
        scratch_shapes=[pltpu.VMEM((B, D_LRU), F32)],
        compiler_params=pltpu.CompilerParams(dimension_semantics=("arbitrary",)),
        name="lru_fwd", interpret=interpret,
    )(a_f, b_f)

    rseq = pl.BlockSpec((B, tt, D_LRU), lambda s: (0, jnp.where(s == 0, 0, nt - s), 0))
    return pl.pallas_call(
        _lru_bwd_kernel,
        grid=(nt,),
        in_specs=[rseq, rseq, rseq, rseq],
        out_specs=rseq,
        out_shape=coeff_shape,
        scratch_shapes=[pltpu.VMEM((B, D_LRU), F32)],
        compiler_params=pltpu.CompilerParams(dimension_semantics=("arbitrary",)),
        name="lru_bwd", interpret=interpret,
    )(a_b, b_b, h_f, u_gate)


ATT_TQ = 256


ATT_HG = 2


def _flash_kernel(*refs, tk):
    q_refs, (k_ref, v_ref, o_ref) = refs[:-3], refs[-3:]
    hg, tq = q_refs[0].shape[1], q_refs[0].shape[2]
    dv = v_ref.shape[3] // 2
    n_chunks = k_ref.shape[2] // tk
    chains = [(q_ref[0, h], h) for q_ref in q_refs for h in range(hg)]

    def body(c, carry):
        off = pl.multiple_of(c * tk, LANES)
        out = []
        for (q, h), (m, acc) in zip(chains, carry):
            s = lax.dot_general(q, k_ref[0, h, pl.ds(off, tk), :], (((1,), (1,)), ((), ())),
                                preferred_element_type=F32)
            m_new = jnp.maximum(m, jnp.max(s, axis=1, keepdims=True))
            p = jnp.exp(s - m_new).astype(v_ref.dtype)
            pv = jnp.dot(p, v_ref[0, h, pl.ds(off, tk), :], preferred_element_type=F32)
            out.append((m_new, jnp.exp(m - m_new) * acc + pv))
        return tuple(out)

    init = tuple((jnp.full((tq, 1), NEG_BIG, F32), jnp.zeros((tq, 2 * dv), F32)) for _ in chains)
    res = lax.fori_loop(0, n_chunks, body, init, unroll=True)
    for t in range(len(q_refs)):
        o_ref[0, t * tq:(t + 1) * tq, :] = jnp.concatenate(
            [acc[:, :dv] / acc[:, dv:] for _, acc in res[t * hg:(t + 1) * hg]], axis=1)


def flash_attention(q, k, v, q_row0, n_q, n_k, tk, n_sub, interpret=False):
    B, H, R, dk = q.shape
    dv = v.shape[3] // 2
    tq = ATT_TQ
    assert n_q % (n_sub * tq) == 0 and q_row0 % tq == 0 and n_k % tk == 0 and tk % LANES == 0 and R % n_k == 0
    q0 = q_row0 // tq
    q_spec = lambda t: pl.BlockSpec((1, ATT_HG, tq, dk), lambda b, g, i: (b, g, n_sub * i + t + q0, 0))
    return pl.pallas_call(
        functools.partial(_flash_kernel, tk=tk),
        grid=(B, H // ATT_HG, n_q // (n_sub * tq)),
        in_specs=[q_spec(t) for t in range(n_sub)]
                 + [pl.BlockSpec((1, ATT_HG, n_k, dk), lambda b, g, i: (b, g, 0, 0)),
                    pl.BlockSpec((1, ATT_HG, n_k, 2 * dv), lambda b, g, i: (b, g, 0, 0))],
        out_specs=pl.BlockSpec((1, n_sub * tq, ATT_HG * dv), lambda b, g, i: (b, i, g)),
        out_shape=jax.ShapeDtypeStruct((B, n_q, H * dv), F32),
        compiler_params=pltpu.CompilerParams(
            dimension_semantics=("arbitrary", "arbitrary", "arbitrary"), vmem_limit_bytes=VMEM_LIMIT),
        name="flash_attention", interpret=interpret,
    )(*([q] * n_sub), k, v)


HY_CB = 16
HY_FR = 128
HY_UNROLL = 16


def _dft_tables(n1, n_in, n_out, cdt):
    n = n1 * LANES
    k1 = np.arange(n1)[:, None]
    f1 = np.exp(-2j * np.pi * k1 * np.arange(n_in)[None, :] / n1)
    tw = np.exp(-2j * np.pi * k1 * np.arange(LANES)[None, :] / n)
    f2 = np.exp(-2j * np.pi * np.outer(np.arange(LANES), np.arange(LANES)) / LANES)
    g1 = np.conj(np.exp(-2j * np.pi * k1 * np.arange(n_out)[None, :] / n1)).T / n
    blk = lambda m: np.block([[m.real, -m.imag], [m.imag, m.real]])
    return dict(
        ma=jnp.asarray(blk(f1), cdt),
        ma_re=jnp.asarray(np.concatenate([f1.real, f1.imag], axis=0), cdt),
        tr=jnp.asarray(tw.real, F32), ti=jnp.asarray(tw.imag, F32),
        f2=jnp.asarray(np.concatenate([f2.real, f2.imag], axis=1), cdt),
        f2c=jnp.asarray(np.concatenate([f2.real, -f2.imag], axis=1), cdt),
        mi=jnp.asarray(blk(g1), cdt),
    )


def _dft_rows_stage(src, ma_ref, tr_ref, ti_ref, planar_scr):
    n1 = tr_ref.shape[0]

    def body(c, carry):
        o = jnp.dot(ma_ref[...], src(c).astype(ma_ref.dtype), preferred_element_type=F32)
        ar, ai = o[:n1], o[n1:]
        tr, ti = tr_ref[...], ti_ref[...]
        planar_scr[c, :n1] = (ar * tr - ai * ti).astype(planar_scr.dtype)
        planar_scr[c, n1:] = (ar * ti + ai * tr).astype(planar_scr.dtype)
        return carry

    lax.fori_loop(0, planar_scr.shape[0], body, 0, unroll=HY_UNROLL)


def _dft_lanes_stage(planar_scr, f_ref):
    cb, rows, _ = planar_scr.shape
    n1 = rows // 2
    p = jnp.dot(planar_scr[...].reshape(cb * rows, LANES), f_ref[...], preferred_element_type=F32)
    p = p.reshape(cb, rows, 2 * LANES)
    return p[:, :n1, :LANES] - p[:, n1:, LANES:], p[:, :n1, LANES:] + p[:, n1:, :LANES]


def _fft_filter_kernel(k_ref, ma_ref, tr_ref, ti_ref, f2_ref, xr_ref, xi_ref, a_scr):
    _dft_rows_stage(lambda c: k_ref[c], ma_ref, tr_ref, ti_ref, a_scr)
    xr, xi = _dft_lanes_stage(a_scr, f2_ref)
    xr_ref[...] = xr
    xi_ref[...] = xi


def _fft_conv_kernel(z_ref, gate_ref, kfr_ref, kfi_ref, kbr_ref, kbi_ref, skip_ref,
                     ma_ref, tr_ref, ti_ref, f2_ref, f2c_ref, mi_ref, o_ref, a_scr, b_scr):
    n1 = tr_ref.shape[0]
    n_out = o_ref.shape[2]
    sdt = a_scr.dtype
    _dft_rows_stage(lambda c: jnp.concatenate([z_ref[0, c], z_ref[1, c]], axis=0), ma_ref, tr_ref, ti_ref, a_scr)
    xr, xi = _dft_lanes_stage(a_scr, f2_ref)
    hr = kfr_ref[...] + kbr_ref[...]
    hi = kfi_ref[...] - kbi_ref[...]
    b_scr[:, :n1] = (xr * hr - xi * hi).astype(sdt)
    b_scr[:, n1:] = (xr * hi + xi * hr).astype(sdt)
    br, bi = _dft_lanes_stage(b_scr, f2c_ref)
    tr, ti = tr_ref[...][None], ti_ref[...][None]
    a_scr[:, :n1] = (br * tr + bi * ti).astype(sdt)
    a_scr[:, n1:] = (bi * tr - br * ti).astype(sdt)

    def body(c, carry):
        y = jnp.dot(mi_ref[...], a_scr[c], preferred_element_type=F32)
        sk = skip_ref[c]
        o_ref[0, c] = gate_ref[0, c] * (y[:n_out] + sk * z_ref[0, c][:n_out])
        o_ref[1, c] = gate_ref[1, c] * (y[n_out:] + sk * z_ref[1, c][:n_out])
        return carry

    lax.fori_loop(0, z_ref.shape[1], body, 0, unroll=HY_UNROLL)


def fft_filter_spectra(k_rows, n1, cdt=BF16, interpret=False):
    R, n_in, _ = k_rows.shape
    t = _dft_tables(n1, n_in, n_in, cdt)
    full = lambda a: pl.BlockSpec(a.shape, lambda i: (0,) * a.ndim)
    spec = jax.ShapeDtypeStruct((R, n1, LANES), F32)
    consts = (t["ma_re"], t["tr"], t["ti"], t["f2"])
    return pl.pallas_call(
        _fft_filter_kernel,
        grid=(R // HY_CB,),
        in_specs=[pl.BlockSpec((HY_CB, n_in, LANES), lambda i: (i, 0, 0))] + [full(a) for a in consts],
        out_specs=[pl.BlockSpec((HY_CB, n1, LANES), lambda i: (i, 0, 0))] * 2,
        out_shape=[spec, spec],
        scratch_shapes=[pltpu.VMEM((HY_CB, 2 * n1, LANES), cdt)],
        compiler_params=pltpu.CompilerParams(dimension_semantics=("arbitrary",), vmem_limit_bytes=VMEM_LIMIT),
        name="fft_filter", interpret=interpret,
    )(k_rows, *consts)


def fft_long_conv(z, z_ch0, gate, gate_ch0, spec_re, spec_im, kf_row0, kb_row0, skip, n1, cdt=BF16, interpret=False):
    B, _, n_in, _ = z.shape
    C = D_HY
    assert B == 2 and all(v % HY_CB == 0 for v in (C, z_ch0, gate_ch0, kf_row0, kb_row0))
    t = _dft_tables(n1, n_in, n_in, cdt)
    full = lambda a: pl.BlockSpec(a.shape, lambda i: (0,) * a.ndim)
    sig = lambda ch0: pl.BlockSpec((B, HY_CB, n_in, LANES), lambda i: (0, i + ch0 // HY_CB, 0, 0))
    spc = lambda row0: pl.BlockSpec((HY_CB, n1, LANES), lambda i: (i + row0 // HY_CB, 0, 0))
    consts = (t["ma"], t["tr"], t["ti"], t["f2"], t["f2c"], t["mi"])
    skip_b = jnp.broadcast_to(skip.astype(F32)[:, None, None], (C, 1, LANES))
    return pl.pallas_call(
        _fft_conv_kernel,
        grid=(C // HY_CB,),
        in_specs=[sig(z_ch0), sig(gate_ch0), spc(kf_row0), spc(kf_row0), spc(kb_row0), spc(kb_row0),
                  pl.BlockSpec((HY_CB, 1, LANES), lambda i: (i, 0, 0))] + [full(a) for a in consts],
        out_specs=sig(0),
        out_shape=jax.ShapeDtypeStruct((B, C, n_in, LANES), F32),
        scratch_shapes=[pltpu.VMEM((HY_CB, 2 * n1, LANES), cdt), pltpu.VMEM((HY_CB, 2 * n1, LANES), cdt)],
        compiler_params=pltpu.CompilerParams(dimension_semantics=("arbitrary",), vmem_limit_bytes=VMEM_LIMIT),
        name="fft_conv", interpret=interpret,
    )(z, gate, spec_re, spec_im, spec_re, spec_im, skip_b, *consts)


def _hy_filter_kernel(z_ref, w1_ref, b1_ref, w2_ref, b2_ref, fr_ref, w3_ref, dec_ref, k_ref, h_scr):
    hp = lax.Precision.HIGHEST

    @pl.when(pl.program_id(0) == 0)
    def _():
        h = jnp.sin(fr_ref[...] * (jnp.dot(w1_ref[...], z_ref[...], precision=hp, preferred_element_type=F32) + b1_ref[...]))
        h_scr[...] = jnp.sin(fr_ref[...] * (jnp.dot(w2_ref[...], h, precision=hp, preferred_element_type=F32) + b2_ref[...]))

    t = z_ref[0:1, :]
    k = jnp.dot(w3_ref[...], h_scr[...], precision=hp, preferred_element_type=F32) * jnp.exp(-t * dec_ref[...])
    k = k * lax.rsqrt(jnp.sum(k * k, axis=1, keepdims=True) + EPS)
    backward = (pl.program_id(0) // (D_HY // HY_FR)) % 2 == 1
    lag0 = lax.broadcasted_iota(jnp.int32, k.shape, 1) == 0
    k_ref[...] = jnp.where(jnp.logical_and(backward, lag0), 0.0, k)


def hyena_filters(L, w1, b1, w2, b2, w3, freq, decay, interpret=False):
    f32 = np.float32
    t = np.arange(L, dtype=f32) / f32(L)
    bands = np.linspace(1e-4, HY_BANDS - 1, HY_BANDS, dtype=f32)
    wpos = f32(2.0 * math.pi) * t[None, :] * bands[:, None]
    emb = 24
    z = jnp.asarray(np.concatenate([t[None, :], np.cos(wpos), np.sin(wpos), np.zeros((emb - HY_EMB, L), f32)], axis=0), F32)
    w1p = jnp.concatenate([w1.T, jnp.zeros((HY_FFN, emb - HY_EMB), F32)], axis=1)
    R = 2 * HY_ORDER * D_HY
    col = lambda v: v.reshape(-1, 1).astype(F32)
    full = lambda shape: pl.BlockSpec(shape, lambda i: (0,) * len(shape))
    return pl.pallas_call(
        _hy_filter_kernel,
        grid=(R // HY_FR,),
        in_specs=[full((emb, L)), full((HY_FFN, emb)), full((HY_FFN, 1)), full((HY_FFN, HY_FFN)), full((HY_FFN, 1)),
                  full((HY_FFN, 1)), pl.BlockSpec((HY_FR, HY_FFN), lambda i: (i, 0)),
                  pl.BlockSpec((HY_FR, 1), lambda i: (i, 0))],
        out_specs=pl.BlockSpec((HY_FR, L), lambda i: (i, 0)),
        out_shape=jax.ShapeDtypeStruct((R, L), F32),
        scratch_shapes=[pltpu.VMEM((HY_FFN, L), F32)],
        compiler_params=pltpu.CompilerParams(dimension_semantics=("arbitrary",)),
        name="hy_filter", interpret=interpret,
    )(z, w1p, col(b1), w2.T.astype(F32), col(b2), col(freq), w3.T.astype(F32), col(decay))


def _hy_shortconv_kernel(u_ref, up_ref, un_ref, w_ref, b_ref, o_ref):
    i = pl.program_id(1)
    n = pl.num_programs(1)
    x = u_ref[0]
    tt = x.shape[0]
    p1 = jnp.where(i > 0, up_ref[0][SUBLANES - 1:SUBLANES], 0.0)
    n1 = jnp.where(i < n - 1, un_ref[0][0:1], 0.0)
    ext = jnp.concatenate([p1, x, n1], axis=0)
    o_ref[0] = b_ref[...] + sum(w_ref[k:k + 1, :] * ext[k:k + tt] for k in range(HY_CONV))


def hyena_shortconv(u, row0, L, w, b, interpret=False):
    B, _, C = u.shape
    tt = min(256, L)
    hb = tt // SUBLANES
    nt = L // tt
    assert row0 % tt == 0
    t0, h0 = row0 // tt, row0 // SUBLANES
    return pl.pallas_call(
        _hy_shortconv_kernel,
        grid=(B, nt),
        in_specs=[pl.BlockSpec((1, tt, C), lambda bb, i: (bb, i + t0, 0)),
                  pl.BlockSpec((1, SUBLANES, C), lambda bb, i: (bb, h0 + jnp.maximum(i * hb - 1, 0), 0)),
                  pl.BlockSpec((1, SUBLANES, C), lambda bb, i: (bb, h0 + jnp.minimum((i + 1) * hb, nt * hb - 1), 0)),
                  pl.BlockSpec((HY_CONV, C), lambda bb, i: (0, 0)), pl.BlockSpec((1, C), lambda bb, i: (0, 0))],
        out_specs=pl.BlockSpec((1, tt, C), lambda bb, i: (bb, i, 0)),
        out_shape=jax.ShapeDtypeStruct((B, L, C), F32),
        compiler_params=pltpu.CompilerParams(dimension_semantics=("arbitrary", "arbitrary")),
        name="hy_shortconv", interpret=interpret,
    )(u, u, u, w.astype(F32), b.reshape(1, C).astype(F32))


def hyena_pallas(u, row0, L, conv_w, conv_b, w1, b1, w2, b2, w3, freq, decay, skip, cdt=BF16, interpret=False):
    B = u.shape[0]
    rows = L // LANES
    n1 = max(2 * rows, SUBLANES)
    n_in = n1 // 2 if n1 == 2 * rows else n1
    uc = hyena_shortconv(u, row0, L, conv_w, conv_b, interpret=interpret)
    ucm = uc.transpose(0, 2, 1).reshape(B, 3 * D_HY, rows, LANES)
    if n_in != rows:
        ucm = jnp.pad(ucm, ((0, 0), (0, 0), (0, n_in - rows), (0, 0)))

    k = hyena_filters(L, w1, b1, w2, b2, w3, freq, decay, interpret=interpret)
    k = k.reshape(2 * HY_ORDER * D_HY, rows, LANES)
    if n_in != rows:
        k = jnp.pad(k, ((0, 0), (0, n_in - rows), (0, 0)))
    sr, si = fft_filter_spectra(k, n1, cdt=cdt, interpret=interpret)

    y, y_ch0 = ucm, 0
    for o in range(HY_ORDER):
        y = fft_long_conv(y, y_ch0, ucm, (o + 1) * D_HY, sr, si, (2 * o) * D_HY, (2 * o + 1) * D_HY, skip[o], n1,
                          cdt=cdt, interpret=interpret)
    return y[:, :, :rows].reshape(B, D_HY, L).transpose(0, 2, 1)


ROW_T = CTX_LEN
ROW_NT = SEQ_ALL // ROW_T
MOD_ROWS = SUBLANES
MLA_DK = LANES
ROPE_SWAP = np.arange(MLA_ROPE).reshape(2, 2, MLA_ROPE // 4)[:, ::-1].reshape(-1)


def _rms(x):
    return x * lax.rsqrt(jnp.mean(x * x, axis=-1, keepdims=True) + EPS)


def _mod_kernel(c_ref, w_ref, b_ref, o_ref):
    c = c_ref[...]
    s = (c * jax.nn.sigmoid(c)).astype(BF16)
    o_ref[...] = jnp.dot(s, w_ref[...].astype(BF16), preferred_element_type=F32) + b_ref[...]


def adaln_modulation(c, c_ctx, w_mod, b_mod, interpret=False):
    B, D = c.shape
    cs = jnp.concatenate([c, c_ctx[None], jnp.zeros((MOD_ROWS - B - 1, D), F32)], axis=0)
    out = pl.pallas_call(
        _mod_kernel,
        grid=(6,),
        in_specs=[pl.BlockSpec((MOD_ROWS, D), lambda j: (0, 0)),
                  pl.BlockSpec((D, D), lambda j: (0, j)),
                  pl.BlockSpec((1, D), lambda j: (0, j))],
        out_specs=pl.BlockSpec((MOD_ROWS, D), lambda j: (0, j)),
        out_shape=jax.ShapeDtypeStruct((MOD_ROWS, 6 * D), F32),
        compiler_params=pltpu.CompilerParams(dimension_semantics=("arbitrary",)),
        name="adaln_mod", interpret=interpret,
    )(cs, w_mod, b_mod.reshape(1, 6 * D))
    mod = out.reshape(MOD_ROWS, 6, D)
    return jnp.stack([jnp.broadcast_to(mod[B][None], (B, 6, D)), mod[:B]], axis=1)


IN_COLS = (D_LRU, D_LRU, MLA_Q_RANK, MLA_KV_RANK, 3 * D_HY, 2 * MLA_ROPE)


def _in_proj_kernel(s_ref, m_ref, g_ref, w_ref, *out_refs):
    m = m_ref[0, 0]
    h = _rms(s_ref[0]) * g_ref[...] * (1.0 + m[1:2]) + m[0:1]
    u = jnp.dot(h.astype(w_ref.dtype), w_ref[...], preferred_element_type=F32)
    off = 0
    for ref, n in zip(out_refs, IN_COLS):
        ref[0] = u[:, off:off + n]
        off += n


def in_projection(s, mod, norm_g, w_in, cdt=BF16, interpret=False):
    B, R, D = s.shape
    o = (0,) + tuple(IN_SPLITS) + (D_IN,)
    kr = w_in[:, o[4]:o[5]]
    w = jnp.concatenate([w_in[:, :o[4]], w_in[:, o[5]:], kr, kr[:, ROPE_SWAP]], axis=1).astype(cdt)
    return pl.pallas_call(
        _in_proj_kernel,
        grid=(B, R // ROW_T),
        in_specs=[pl.BlockSpec((1, ROW_T, D), lambda b, i: (b, i, 0)),
                  pl.BlockSpec((1, 1, 6, D), lambda b, i: (b, jnp.minimum(i, 1), 0, 0)),
                  pl.BlockSpec((1, D), lambda b, i: (0, 0)),
                  pl.BlockSpec(w.shape, lambda b, i: (0, 0))],
        out_specs=[pl.BlockSpec((1, ROW_T, n), lambda b, i: (b, i, 0)) for n in IN_COLS],
        out_shape=[jax.ShapeDtypeStruct((B, R, n), F32) for n in IN_COLS],
        compiler_params=pltpu.CompilerParams(
            dimension_semantics=("arbitrary", "arbitrary"), vmem_limit_bytes=VMEM_LIMIT),
        name="in_proj", interpret=interpret,
    )(s, mod, norm_g.reshape(1, D), w)


def _mla_proj_kernel(cq_ref, ckv_ref, kr_ref, rc_ref, rs_ref, qg_ref, wq_ref, wqs_ref, kg_ref, wkv_ref,
                     q_ref, k_ref, v_ref):
    cdt = wq_ref.dtype
    cqn = (_rms(cq_ref[0]) * qg_ref[...]).astype(cdt)
    qa = jnp.dot(cqn, wq_ref[...], preferred_element_type=F32)
    qs = jnp.dot(cqn, wqs_ref[...], preferred_element_type=F32)
    ckvn = (_rms(ckv_ref[0]) * kg_ref[...]).astype(cdt)
    kv = jnp.dot(ckvn, wkv_ref[...], preferred_element_type=F32)
    rc, rs = rc_ref[...], rs_ref[...]
    kr = kr_ref[0]
    k_rope = (kr[:, :MLA_ROPE] * rc[:, MLA_NOPE:MLA_NOPE + MLA_ROPE]
              + kr[:, MLA_ROPE:] * rs[:, MLA_NOPE:MLA_NOPE + MLA_ROPE])
    pad = jnp.zeros((kr.shape[0], MLA_DK - MLA_NOPE - MLA_ROPE), F32)
    for h in range(MLA_HEADS):
        sl = slice(h * MLA_DK, (h + 1) * MLA_DK)
        q_ref[0, h] = ((qa[:, sl] * rc + qs[:, sl] * rs) * MLA_SCALE).astype(q_ref.dtype)
        kvh = kv[:, h * (MLA_NOPE + MLA_V):(h + 1) * (MLA_NOPE + MLA_V)]
        k_ref[0, h] = jnp.concatenate([kvh[:, :MLA_NOPE], k_rope, pad], axis=1).astype(k_ref.dtype)
        v_ref[0, h] = jnp.concatenate([kvh[:, MLA_NOPE:], jnp.ones((kvh.shape[0], MLA_V), F32)],
                                      axis=1).astype(v_ref.dtype)


def _rope_tables():
    f32 = np.float32
    grid_rows = SEQ // GRID_W
    row = np.repeat(np.arange(grid_rows, dtype=f32), GRID_W)
    col = np.tile(np.arange(GRID_W, dtype=f32), grid_rows)
    inv_freq = (f32(ROPE_THETA) ** (-np.arange(ROPE_PAIRS, dtype=f32) / f32(ROPE_PAIRS))).astype(f32)
    ang_r, ang_c = row[:, None] * inv_freq, col[:, None] * inv_freq
    cos = np.concatenate([np.cos(ang_r), np.cos(ang_r), np.cos(ang_c), np.cos(ang_c)], axis=1)
    sin = np.concatenate([-np.sin(ang_r), np.sin(ang_r), -np.sin(ang_c), np.sin(ang_c)], axis=1)
    npad = MLA_DK - MLA_NOPE - MLA_ROPE
    cos = np.concatenate([np.ones((SEQ, MLA_NOPE), f32), cos, np.zeros((SEQ, npad), f32)], axis=1)
    sin = np.concatenate([np.zeros((SEQ, MLA_NOPE), f32), sin, np.zeros((SEQ, npad), f32)], axis=1)
    ctx_cos = np.concatenate([np.ones((CTX_LEN, MLA_NOPE + MLA_ROPE), f32), np.zeros((CTX_LEN, npad), f32)], axis=1)
    return (jnp.asarray(np.concatenate([ctx_cos, cos], axis=0), F32),
            jnp.asarray(np.concatenate([np.zeros((CTX_LEN, MLA_DK), f32), sin], axis=0), F32))


def mla_projection(cq, ckv, kr2, q_g, wqb, kv_g, wkvb, cdt=BF16, interpret=False):
    B, R, _ = cq.shape
    dq = MLA_NOPE + MLA_ROPE
    wq3 = wqb.reshape(MLA_Q_RANK, MLA_HEADS, dq)
    zq = jnp.zeros((MLA_Q_RANK, MLA_HEADS, MLA_DK - dq), wqb.dtype)
    wq = jnp.concatenate([wq3, zq], axis=2).reshape(MLA_Q_RANK, MLA_HEADS * MLA_DK).astype(cdt)
    wqs = jnp.concatenate([jnp.zeros((MLA_Q_RANK, MLA_HEADS, MLA_NOPE), wqb.dtype),
                           wq3[:, :, MLA_NOPE:][:, :, ROPE_SWAP], zq], axis=2)
    wqs = wqs.reshape(MLA_Q_RANK, MLA_HEADS * MLA_DK).astype(cdt)
    rc, rs = _rope_tables()
    row = lambda n: pl.BlockSpec((1, ROW_T, n), lambda b, i: (b, i, 0))
    tab = pl.BlockSpec((ROW_T, MLA_DK), lambda b, i: (i, 0))
    full = lambda shape: pl.BlockSpec(shape, lambda b, i: (0,) * len(shape))
    head = lambda n: pl.BlockSpec((1, MLA_HEADS, ROW_T, n), lambda b, i: (b, 0, i, 0))
    return pl.pallas_call(
        _mla_proj_kernel,
        grid=(B, R // ROW_T),
        in_specs=[row(MLA_Q_RANK), row(MLA_KV_RANK), row(2 * MLA_ROPE), tab, tab,
                  full((1, MLA_Q_RANK)), full(wq.shape), full(wqs.shape),
                  full((1, MLA_KV_RANK)), full(wkvb.shape)],
        out_specs=[head(MLA_DK), head(MLA_DK), head(2 * MLA_V)],
        out_shape=[jax.ShapeDtypeStruct((B, MLA_HEADS, R, MLA_DK), cdt),
                   jax.ShapeDtypeStruct((B, MLA_HEADS, R, MLA_DK), cdt),
                   jax.ShapeDtypeStruct((B, MLA_HEADS, R, 2 * MLA_V), cdt)],
        compiler_params=pltpu.CompilerParams(dimension_semantics=("arbitrary", "arbitrary")),
        name="mla_proj", interpret=interpret,
    )(cq, ckv, kr2, rc, rs, q_g.reshape(1, -1), wq, wqs, kv_g.reshape(1, -1), wkvb.astype(cdt))


def _out_proj_kernel(s_ref, m_ref, g_ref, ya_ref, ybx_ref, ybc_ref, ycx_ref, ycc_ref, w_ref, snew_ref, hT_ref):
    cdt = w_ref.dtype
    is_ctx = pl.program_id(1) == 0
    yb = jnp.where(is_ctx, ybc_ref[0], ybx_ref[0])
    yc = jnp.where(is_ctx, ycc_ref[0], ycx_ref[0])
    mix = (jnp.dot(ya_ref[0].astype(cdt), w_ref[0:D_LRU, :], preferred_element_type=F32)
           + jnp.dot(yb.astype(cdt), w_ref[D_LRU:D_LRU + MLA_HEADS * MLA_V, :], preferred_element_type=F32)
           + jnp.dot(yc.astype(cdt), w_ref[D_LRU + MLA_HEADS * MLA_V:, :], preferred_element_type=F32))
    m = m_ref[0, 0]
    s_new = s_ref[0] + m[2:3] * mix
    snew_ref[0] = s_new
    h2 = _rms(s_new) * g_ref[...] * (1.0 + m[4:5]) + m[3:4]
    hT_ref[...] = h2.T.astype(hT_ref.dtype)


def out_projection(s, mod, norm2_g, ya, yb_x, yb_c, yc_x, yc_c, w_out, cdt=BF16, interpret=False):
    B, R, D = s.shape
    nt = R // ROW_T
    row = lambda n: pl.BlockSpec((1, ROW_T, n), lambda b, i: (b, i, 0))
    lat = lambda n: pl.BlockSpec((1, ROW_T, n), lambda b, i: (b, jnp.maximum(i - 1, 0), 0))
    ctx = lambda n: pl.BlockSpec((1, ROW_T, n), lambda b, i: (b, 0, 0))
    return pl.pallas_call(
        _out_proj_kernel,
        grid=(B, nt),
        in_specs=[row(D),
                  pl.BlockSpec((1, 1, 6, D), lambda b, i: (b, jnp.minimum(i, 1), 0, 0)),
                  pl.BlockSpec((1, D), lambda b, i: (0, 0)),
                  row(D_LRU),
                  lat(MLA_HEADS * MLA_V), ctx(MLA_HEADS * MLA_V), lat(D_HY), ctx(D_HY),
                  pl.BlockSpec((D_MIX, D), lambda b, i: (0, 0))],
        out_specs=[row(D), pl.BlockSpec((D, ROW_T), lambda b, i: (0, b * nt + i))],
        out_shape=[jax.ShapeDtypeStruct((B, R, D), F32), jax.ShapeDtypeStruct((D, B * R), cdt)],
        compiler_params=pltpu.CompilerParams(
            dimension_semantics=("arbitrary", "arbitrary"), vmem_limit_bytes=VMEM_LIMIT),
        name="out_proj", interpret=interpret,
    )(s, mod, norm2_g.reshape(1, D), ya, yb_x, yb_c, yc_x, yc_c, w_out.astype(cdt))


PEER_TM = 768
PEER_TE = 1024
PEER_KP = 256
NEG_BIG = -3.0e38
PACKED_ROWS = 2 * SUBLANES


def _bitonic_clean(vals):
    vals = list(vals)
    d = len(vals) // 2
    while d >= 1:
        for i in range(len(vals)):
            if i & d == 0:
                vals[i], vals[i + d] = jnp.maximum(vals[i], vals[i + d]), jnp.minimum(vals[i], vals[i + d])
        d //= 2
    return vals


def _topk_sorted(s, k):
    assert s.shape[0] == k * SUBLANES and k & (k - 1) == 0
    vals = [s[i * SUBLANES:(i + 1) * SUBLANES, :] for i in range(k)]
    size = 2
    while size <= k:
        for lo in range(0, k, size):
            blk = vals[lo:lo + size]
            vals[lo:lo + size] = _bitonic_clean(blk[:size // 2] + blk[size // 2:][::-1])
        size *= 2
    shift = SUBLANES // 2
    while shift >= 1:
        other = [pltpu.roll(v, shift, axis=0) for v in vals]
        vals = _bitonic_clean([jnp.maximum(vals[i], other[k - 1 - i]) for i in range(k)])
        shift //= 2
    return vals


def _rows_of(tiles):
    row = lax.broadcasted_iota(jnp.int32, tiles[0].shape, 0)
    out = []
    for lo in range(0, len(tiles), SUBLANES):
        acc = tiles[lo]
        for r in range(1, SUBLANES):
            acc = jnp.where(row == r, tiles[lo + r], acc)
        out.append(acc)
    return jnp.concatenate(out, axis=0)


def _route_chunk(s1, s2):
    K = PEER_TOPK
    t1 = _topk_sorted(s1, K)
    t2 = _topk_sorted(s2, K)
    v1, v2 = _rows_of(t1), _rows_of(t2)
    tile_rows = lambda x: [x[i * SUBLANES:(i + 1) * SUBLANES, :] for i in range(x.shape[0] // SUBLANES)]
    rank2 = []
    for x in tile_rows(s2):
        r = jnp.full(x.shape, float(PEER_NKEYS), F32)
        for i in range(K - 1, -1, -1):
            r = jnp.where(x == t2[i], float(i), r)
        rank2.append(r)
    rank2 = jnp.concatenate(rank2, axis=0)
    half = K // 2
    cands = ([v1[0:1] + v2] + [v1[i:i + 1] + v2[:half] for i in range(1, half)]
             + [v1[i:i + 1] + v2[0:1] for i in range(half, K)])
    cmax = v1[0:1] + v2[0:1]
    c = jnp.concatenate(cands[:half] + [v1[half:] + v2[0:1]], axis=0)
    z = jnp.zeros_like(cmax)
    m = cmax
    for r in range(K):
        m = jnp.max(c, axis=0, keepdims=True)
        z = z + jnp.exp(m - cmax)
        if r + 1 < K:
            c = jnp.where(c == m, NEG_BIG, c)
    thr = m
    n_i = [jnp.sum(jnp.where(cands[i] >= thr, 1.0, 0.0), axis=0, keepdims=True) for i in range(K)]
    n = []
    for x in tile_rows(s1):
        acc = jnp.zeros(x.shape, F32)
        for i in range(K):
            acc = jnp.where(x == t1[i], n_i[i], acc)
        n.append(acc)
    n = jnp.concatenate(n, axis=0)
    w = jnp.exp(s1 - v1[0:1]) / z
    e2 = jnp.exp(s2 - v2[0:1])
    return rank2, e2, n, w


def _peer_route_kernel(hT_ref, wqT_ref, keys_ref, rank2_ref, e2_ref, n_ref, w_ref, q_scr, s1_scr, s2_scr):
    h = pl.program_id(1)

    @pl.when(h == 0)
    def _():
        q_scr[...] = jnp.dot(wqT_ref[...], hT_ref[...], preferred_element_type=F32)

    cdt = keys_ref.dtype
    half = PEER_DQ // 2
    row = pl.multiple_of(h * PEER_DQ, PEER_DQ)
    q1 = q_scr[pl.ds(row, half), :].astype(cdt)
    q2 = q_scr[pl.ds(row + half, half), :].astype(cdt)
    s1_scr[...] = jnp.dot(keys_ref[0], q1, preferred_element_type=F32)
    s2_scr[...] = jnp.dot(keys_ref[1], q2, preferred_element_type=F32)

    def chunk(c, carry):
        sl = pl.ds(pl.multiple_of(c * LANES, LANES), LANES)
        rank2, e2, n, w = _route_chunk(s1_scr[:, sl], s2_scr[:, sl])
        rank2_ref[0, :, sl] = rank2.astype(rank2_ref.dtype)
        e2_ref[0, :, sl] = e2.astype(e2_ref.dtype)
        n_ref[0, :, sl] = n
        w_ref[0, :, sl] = w
        return carry

    lax.fori_loop(0, s1_scr.shape[1] // LANES, chunk, 0, unroll=3)


def _peer_dense_kernel(hT_ref, rank2_ref, e2_ref, n_ref, w_ref, u_ref, vT_ref, s_ref, g_ref, o_ref, acc_ref, pre_ref,
                       *, n_chunks):
    j = pl.program_id(1)

    te, tm = u_ref.shape[0], hT_ref.shape[1]
    gdt = rank2_ref.dtype

    def first_matmul(wr):
        pre_ref[wr] = jnp.dot(u_ref[...], hT_ref[...], preferred_element_type=F32)

    def second_matmul(rd):
        out = None
        for k0 in range(0, te, PEER_KP):
            acts = []
            for al in range(k0 // PEER_NKEYS, (k0 + PEER_KP) // PEER_NKEYS):
                g = None
                for h in range(PEER_HEADS):
                    n16 = jnp.broadcast_to(n_ref[h, 0, al:al + 1, :], (PACKED_ROWS, tm)).astype(gdt)
                    w16 = jnp.broadcast_to(w_ref[h, 0, al:al + 1, :], (PACKED_ROWS, tm)).astype(gdt)
                    nb = jnp.concatenate([n16] * (PEER_NKEYS // PACKED_ROWS), axis=0)
                    wb = jnp.concatenate([w16] * (PEER_NKEYS // PACKED_ROWS), axis=0)
                    t = jnp.where(rank2_ref[h] < nb, e2_ref[h] * wb, jnp.zeros_like(wb))
                    g = t if g is None else g + t
                blk = slice(al * PEER_NKEYS, (al + 1) * PEER_NKEYS)
                acts.append((g * _gelu_tanh(pre_ref[rd, blk, :].astype(gdt))).astype(vT_ref.dtype))
            part = jnp.dot(vT_ref[0, :, k0:k0 + PEER_KP], jnp.concatenate(acts, axis=0), preferred_element_type=F32)
            out = part if out is None else out + part
        acc_ref[...] += out

    def step(rd, wr):
        first_matmul(wr)
        second_matmul(rd)

    @pl.when(j == 0)
    def _():
        acc_ref[...] = jnp.zeros_like(acc_ref)
        first_matmul(0)

    for parity in range(2):
        pl.when((j % 2 == parity) & (j > 0) & (j < n_chunks))(functools.partial(step, 1 - parity, parity))

    @pl.when(j == n_chunks)
    def _():
        second_matmul((n_chunks - 1) % 2)
        y = acc_ref[...].T
        for r in range(g_ref.shape[0]):
            rows = slice(r * ROW_T, (r + 1) * ROW_T)
            o_ref[rows, :] = s_ref[rows, :] + g_ref[r] * y[rows, :]


def peer_tokens(hT, s, gate, wq, keys, u_tab, v_tab, interpret=False):
    D, T = hT.shape
    cdt = hT.dtype
    tm, te = PEER_TM, PEER_TE
    assert T % tm == 0 and PEER_EXPERTS % te == 0 and tm % ROW_T == 0
    wqT = wq.T.astype(cdt)
    route_shape = jax.ShapeDtypeStruct((PEER_HEADS, PEER_NKEYS, T), F32)
    packed_shape = jax.ShapeDtypeStruct((PEER_HEADS, PEER_NKEYS, T), cdt)
    route_spec = pl.BlockSpec((1, PEER_NKEYS, tm), lambda i, hh: (hh, 0, i))
    rank2, e2, n, w = pl.pallas_call(
        _peer_route_kernel,
        grid=(T // tm, PEER_HEADS),
        in_specs=[pl.BlockSpec((D, tm), lambda i, hh: (0, i)),
                  pl.BlockSpec((PEER_HEADS * PEER_DQ, D), lambda i, hh: (0, 0)),
                  pl.BlockSpec((2, PEER_NKEYS, PEER_DQ // 2), lambda i, hh: (0, 0, 0))],
        out_specs=[route_spec] * 4,
        out_shape=[packed_shape, packed_shape, route_shape, route_shape],
        scratch_shapes=[pltpu.VMEM((PEER_HEADS * PEER_DQ, tm), F32),
                        pltpu.VMEM((PEER_NKEYS, tm), F32),
                        pltpu.VMEM((PEER_NKEYS, tm), F32)],
        compiler_params=pltpu.CompilerParams(
            dimension_semantics=("arbitrary", "arbitrary"), vmem_limit_bytes=VMEM_LIMIT),
        name="peer_route", interpret=interpret,
    )(hT, wqT, keys.astype(cdt))

    head_spec = pl.BlockSpec((PEER_HEADS, PEER_NKEYS, tm), lambda i, j: (0, 0, i))
    nj = PEER_EXPERTS // te
    groups = te // PEER_NKEYS
    row_spec = pl.BlockSpec((PEER_HEADS, 1, groups, tm), lambda i, j: (0, jnp.maximum(j - 1, 0), 0, i))
    n = n.reshape(PEER_HEADS, nj, groups, T)
    w = w.reshape(PEER_HEADS, nj, groups, T)
    vT = v_tab.astype(cdt).reshape(nj, te, D).transpose(0, 2, 1)
    return pl.pallas_call(
        functools.partial(_peer_dense_kernel, n_chunks=nj),
        grid=(T // tm, nj + 1),
        in_specs=[pl.BlockSpec((D, tm), lambda i, j: (0, i)),
                  head_spec, head_spec, row_spec, row_spec,
                  pl.BlockSpec((te, D), lambda i, j: (jnp.minimum(j, nj - 1), 0)),
                  pl.BlockSpec((1, D, te), lambda i, j: (jnp.maximum(j - 1, 0), 0, 0)),
                  pl.BlockSpec((tm, D), lambda i, j: (i, 0)),
                  pl.BlockSpec((tm // ROW_T, 1, D), lambda i, j: (i, 0, 0))],
        out_specs=pl.BlockSpec((tm, D), lambda i, j: (i, 0)),
        out_shape=jax.ShapeDtypeStruct((T, D), F32),
        scratch_shapes=[pltpu.VMEM((D, tm), F32), pltpu.VMEM((2, te, tm), F32)],
        compiler_params=pltpu.CompilerParams(
            dimension_semantics=("arbitrary", "arbitrary"), vmem_limit_bytes=VMEM_LIMIT),
        name="peer_dense", interpret=interpret,
    )(hT, rank2, e2, n, w, u_tab.astype(cdt), vT, s, gate)


def _final_norm_kernel(x_ref, g_ref, o_ref):
    o_ref[0] = _rms(x_ref[0]) * g_ref[...]


def final_norm(s, g):
    B, R, D = s.shape
    return pl.pallas_call(
        _final_norm_kernel,
        grid=(B, SEQ // ROW_T),
        in_specs=[pl.BlockSpec((1, ROW_T, D), lambda b, i: (b, i + CTX_LEN // ROW_T, 0)),
                  pl.BlockSpec((1, D), lambda b, i: (0, 0))],
        out_specs=pl.BlockSpec((1, ROW_T, D), lambda b, i: (b, i, 0)),
        out_shape=jax.ShapeDtypeStruct((B, SEQ, D), F32),
        compiler_params=pltpu.CompilerParams(dimension_semantics=("arbitrary", "arbitrary")),
        name="final_norm",
    )(s, g.reshape(1, D))


def kernel(x, c, ctx, c_ctx, w_mod, b_mod, norm1_g, norm2_g, w_in, w_out,
           lru_conv_w, lru_conv_b, lru_wr, lru_br, lru_wi, lru_bi, lru_lambda,
           mla_q_norm_g, mla_wqb, mla_kv_norm_g, mla_wkvb,
           hy_conv_w, hy_conv_b, hy_f_w1, hy_f_b1, hy_f_w2, hy_f_b2, hy_f_w3,
           hy_f_freq, hy_decay, hy_skip,
           peer_wq, peer_keys, peer_u, peer_v, final_g):
    B = x.shape[0]
    s = jnp.concatenate([ctx, x], axis=1)
    for l in range(DEPTH):
        need_ctx = l < DEPTH - 1
        mod = adaln_modulation(c, c_ctx, w_mod[l], b_mod[l])
        u_lru, u_gate, cq, ckv, u_hy, kr2 = in_projection(s, mod, norm1_g[l], w_in[l])

        ya = rglru_mixer(u_lru, u_gate, lru_conv_w[l], lru_conv_b[l], lru_wr[l], lru_br[l], lru_wi[l], lru_bi[l],
                         lru_lambda[l])

        q, k, v = mla_projection(cq, ckv, kr2, mla_q_norm_g[l], mla_wqb[l], mla_kv_norm_g[l], mla_wkvb[l])
        yb_x = flash_attention(q, k, v, q_row0=CTX_LEN, n_q=SEQ, n_k=SEQ_ALL, tk=SEQ_ALL // 3, n_sub=1)
        if need_ctx:
            yb_c = flash_attention(q, k, v, q_row0=0, n_q=CTX_LEN, n_k=CTX_LEN, tk=CTX_LEN, n_sub=1)
        else:
            yb_c = jnp.zeros((B, CTX_LEN, MLA_HEADS * MLA_V), F32)

        hy_args = (hy_conv_w[l], hy_conv_b[l], hy_f_w1[l], hy_f_b1[l], hy_f_w2[l], hy_f_b2[l], hy_f_w3[l],
                   hy_f_freq[l], hy_decay[l], hy_skip[l])
        yc_x = hyena_pallas(u_hy, CTX_LEN, SEQ, *hy_args)
        yc_c = hyena_pallas(u_hy, 0, CTX_LEN, *hy_args) if need_ctx else jnp.zeros((B, CTX_LEN, D_HY), F32)

        s, hT = out_projection(s, mod, norm2_g[l], ya, yb_x, yb_c, yc_x, yc_c, w_out[l])
        gate2 = jnp.repeat(mod[:, :, 5], jnp.array([1, ROW_NT - 1]), axis=1, total_repeat_length=ROW_NT)
        s = peer_tokens(hT, s.reshape(B * SEQ_ALL, D_MODEL), gate2.reshape(B * ROW_NT, 1, D_MODEL),
                        peer_wq[l], peer_keys[l], peer_u[l], peer_v[l]).reshape(B, SEQ_ALL, D_MODEL)
    return final_norm(s, final_g)
```
